```python
import math
import jax
import jax.numpy as jnp
from jax import lax
import numpy as np

D_MODEL = 1024
BATCH = 8
SEQ = 4096
DEPTH = 2

CTX_LEN = 256
GRID_W = 64
N_BRANCH = 4
BR_W = 512
EPS = 1e-6
CONV_W = 4
DT_MIN = 1e-3
DT_MAX = 1e-1

HG_HEADS = 4
HG_DK = BR_W // HG_HEADS
HG_CHUNK = 64
S5_GROUP = 16
S5_GROUPS = BR_W // S5_GROUP
S5_STATE = 64
LRU_BLOCKS = 8
LRU_BW = BR_W // LRU_BLOCKS
LRU_C = 8.0
M2_HEADDIM = 64
M2_HEADS = BR_W // M2_HEADDIM
M2_GROUPS = 2
M2_HPG = M2_HEADS // M2_GROUPS
M2_STATE = 64
M2_CHUNK = 64
M2_XBC = BR_W + 2 * M2_GROUPS * M2_STATE

IN_SIZES = (BR_W,) * 9 + (M2_XBC, 2 * M2_HEADS, BR_W)
IN_SPLITS = tuple(sum(IN_SIZES[:i + 1]) for i in range(len(IN_SIZES) - 1))
IN_COLS = sum(IN_SIZES)

kernel_name = 'hybrid_gated_recurrent_flow_block'


def _rms(x, w):
    xf = x.astype(jnp.float32)
    y = xf * lax.rsqrt(jnp.mean(xf * xf, axis=-1, keepdims=True) + EPS)
    return (y * w.astype(jnp.float32)).astype(x.dtype)


def _lin_comb(e1, e2):
    a1, b1 = e1
    a2, b2 = e2
    return a1 * a2, a2 * b1 + b2


def _cat_f(c, l):
    return jnp.concatenate([c, l], axis=1)


def _cat_b(c, l):
    return jnp.flip(jnp.concatenate([l, c], axis=1), axis=1)


def _uncat_b(y, n_ctx):
    y = jnp.flip(y, axis=1)
    n_lat = y.shape[1] - n_ctx
    return jnp.concatenate([y[:, n_lat:], y[:, :n_lat]], axis=1)


def _f_to_b(y, n_ctx):
    return _cat_b(y[:, :n_ctx], y[:, n_ctx:])


def _dwconv(x, w, b):
    n = x.shape[-2]
    lo = (CONV_W - 1) // 2
    pad = [(0, 0)] * (x.ndim - 2) + [(lo, CONV_W - 1 - lo), (0, 0)]
    xp = jnp.pad(x, pad)
    out = b
    for k in range(CONV_W):
        out = out + w[k] * xp[..., k:k + n, :]
    return out


def _short_conv(xc, xl, w, b):
    bsz, n_lat, ch = xl.shape
    rows = n_lat // GRID_W
    yl = _dwconv(xl.reshape(bsz, rows, GRID_W, ch), w, b).reshape(bsz, n_lat, ch)
    return _dwconv(xc, w, b), yl


def _gla_chunked(q, k, v, logf):
    bsz, T, H, _ = q.shape
    n = T // HG_CHUNK

    def r(a):
        return a.reshape(bsz, n, HG_CHUNK, H, a.shape[-1])
    q, k, v, logf = r(q), r(k), r(v), r(logf)
    b = jnp.cumsum(logf, axis=2)
    b_end = b[:, :, -1:]
    mid = 0.5 * b_end
    att = jnp.einsum('bnihk,bnjhk->bnhij', q * jnp.exp(b - mid), k * jnp.exp(mid - b))
    mask = jnp.tril(jnp.ones((HG_CHUNK, HG_CHUNK), bool))
    att = jnp.where(mask, att, 0.0)
    o_intra = jnp.einsum('bnhij,bnjhv->bnihv', att, v)
    chunk_kv = jnp.einsum('bnjhk,bnjhv->bnhkv', k * jnp.exp(b_end - b), v)
    decay = jnp.exp(b_end[:, :, 0])

    def step(s, inp):
        kv, dec = inp
        return dec[..., None] * s + kv, s
    s0 = jnp.zeros((bsz, H, k.shape[-1], v.shape[-1]), jnp.float32)
    _, s_prev = lax.scan(step, s0, (jnp.moveaxis(chunk_kv, 1, 0), jnp.moveaxis(decay, 1, 0)))
    s_prev = jnp.moveaxis(s_prev, 0, 1)
    o_inter = jnp.einsum('bnihk,bnhkv->bnihv', q * jnp.exp(b), s_prev)
    return (o_intra + o_inter).reshape(bsz, T, H, v.shape[-1])


def _hgrn2(ctx_in, lat_in, lb, norm_w):
    (cq, ci, cff, cfb, cz), (lq, li, lff, lfb, lz) = ctx_in, lat_in
    n_ctx = cq.shape[1]
    f32 = jnp.float32

    def heads(a):
        return a.reshape(a.shape[0], a.shape[1], HG_HEADS, HG_DK)

    def run(q, i, f_raw, lbd):
        f = lbd + (1.0 - lbd) * jax.nn.sigmoid(f_raw.astype(f32))
        return _gla_chunked(heads(jax.nn.silu(q.astype(f32))), heads(1.0 - f),
                            heads(i.astype(f32)), heads(jnp.log(f)))
    o = (run(_cat_f(cq, lq), _cat_f(ci, li), _cat_f(cff, lff), lb[0])
         + _uncat_b(run(_cat_b(cq, lq), _cat_b(ci, li), _cat_b(cfb, lfb), lb[1]), n_ctx))
    o = _rms(o, norm_w.reshape(HG_HEADS, HG_DK))
    o = o.reshape(o.shape[0], o.shape[1], BR_W)
    return o * jax.nn.silu(_cat_f(cz, lz).astype(f32))


def _s5(ctx_in, lat_in, a_re, a_im, log_step, b_re, b_im, c_re, c_im, d_skip, w_glu, b_glu):
    (cu, cz), (lu, lz) = ctx_in, lat_in
    n_ctx = cu.shape[1]
    f32 = jnp.float32
    u = _cat_f(cu, lu).astype(f32)
    bsz, T, _ = u.shape
    ug = u.reshape(bsz, T, S5_GROUPS, S5_GROUP).astype(jnp.complex64)
    b_mat = lax.complex(b_re.astype(f32), b_im.astype(f32))
    c_mat = lax.complex(c_re.astype(f32), c_im.astype(f32))
    bu = jnp.einsum('gnp,btgp->btgn', b_mat, ug)

    def run(bu_seq, d):
        lam = lax.complex(a_re[d].astype(f32), a_im[d].astype(f32))
        step = jnp.exp(log_step[d].astype(f32))[:, None]
        a_bar = jnp.exp(lam * step)
        drive = ((a_bar - 1.0) / lam) * bu_seq
        a_seq = jnp.broadcast_to(a_bar, (1, T) + a_bar.shape)
        _, s = lax.associative_scan(_lin_comb, (a_seq, drive), axis=1)
        return jnp.real(jnp.einsum('gpn,btgn->btgp', c_mat, s))
    y = run(bu, 0) + _uncat_b(run(_f_to_b(bu, n_ctx), 1), n_ctx)
    y = y.reshape(bsz, T, BR_W) + d_skip.astype(f32) * u
    g = jax.nn.gelu(y)
    y = g * jax.nn.sigmoid(g @ w_glu.astype(f32) + b_glu.astype(f32))
    return y * jax.nn.silu(_cat_f(cz, lz).astype(f32))


def _rglru(ctx_in, lat_in, conv_w, conv_b, gate_w, gate_b, lam):
    (cx, cz), (lx, lz) = ctx_in, lat_in
    n_ctx = cx.shape[1]
    f32 = jnp.float32
    cx, lx = _short_conv(cx, lx, conv_w, conv_b)

    def run(xs, d):
        xf = xs.astype(f32)
        bsz, T, _ = xf.shape
        xb = xf.reshape(bsz, T, LRU_BLOCKS, LRU_BW)
        gates = jax.nn.sigmoid(jnp.einsum('btnc,gncd->gbtnd', xb, gate_w[d].astype(f32))
                               + gate_b[d].astype(f32)[:, None, None])
        r = gates[0].reshape(bsz, T, BR_W)
        i = gates[1].reshape(bsz, T, BR_W)
        log_a = -LRU_C * r * jax.nn.softplus(-lam[d].astype(f32))
        b = jnp.sqrt(-jnp.expm1(2.0 * log_a)) * (i * xf)
        _, h = lax.associative_scan(_lin_comb, (jnp.exp(log_a), b), axis=1)
        return h
    h = run(_cat_f(cx, lx), 0) + _uncat_b(run(_cat_b(cx, lx), 1), n_ctx)
    return h * jax.nn.silu(_cat_f(cz, lz).astype(f32))


def _ssd_chunked(x, dt, a, bm, cm):
    bsz, T = x.shape[:2]
    n = T // M2_CHUNK

    def r(t):
        return t.reshape((bsz, n, M2_CHUNK) + t.shape[2:])
    x, dt, bm, cm = r(x), r(dt), r(bm), r(cm)
    cum = jnp.cumsum(dt * a, axis=2)
    seg = cum[:, :, :, None] - cum[:, :, None, :]
    mask = jnp.tril(jnp.ones((M2_CHUNK, M2_CHUNK), bool))[:, :, None, None]
    decay_ij = jnp.exp(jnp.where(mask, seg, -jnp.inf))
    scores = jnp.einsum('bcigs,bcjgs->bcijg', cm, bm)
    w = scores[..., None] * decay_ij * dt[:, :, None]
    y_intra = jnp.einsum('bcijgr,bcjgrp->bcigrp', w, x)
    cum_end = cum[:, :, -1]
    wx = (jnp.exp(cum_end[:, :, None] - cum) * dt)[..., None] * x
    chunk_state = jnp.einsum('bcjgs,bcjgrp->bcgrps', bm, wx)

    def step(s, inp):
        st, dec = inp
        return dec[..., None, None] * s + st, s
    s0 = jnp.zeros(chunk_state.shape[:1] + chunk_state.shape[2:], jnp.float32)
    _, s_prev = lax.scan(step, s0, (jnp.moveaxis(chunk_state, 1, 0), jnp.moveaxis(jnp.exp(cum_end), 1, 0)))
    s_prev = jnp.moveaxis(s_prev, 0, 1)
    y_inter = jnp.einsum('bcigs,bcgrps->bcigrp', cm, s_prev) * jnp.exp(cum)[..., None]
    return (y_intra + y_inter).reshape(bsz, T, M2_GROUPS, M2_HPG, M2_HEADDIM)


def _mamba2(ctx_in, lat_in, conv_w, conv_b, dt_bias, a_log, d_skip, norm_w):
    (cxbc, cdt, cz), (lxbc, ldt, lz) = ctx_in, lat_in
    n_ctx = cxbc.shape[1]
    f32 = jnp.float32
    cxbc, lxbc = _short_conv(cxbc, lxbc, conv_w, conv_b)
    cxbc, lxbc = jax.nn.silu(cxbc.astype(f32)), jax.nn.silu(lxbc.astype(f32))
    gn = M2_GROUPS * M2_STATE

    def run(xbc, dt_raw, d):
        bsz, T, _ = xbc.shape
        xs = xbc[..., :BR_W].reshape(bsz, T, M2_GROUPS, M2_HPG, M2_HEADDIM)
        bm = xbc[..., BR_W:BR_W + gn].reshape(bsz, T, M2_GROUPS, M2_STATE)
        cm = xbc[..., BR_W + gn:].reshape(bsz, T, M2_GROUPS, M2_STATE)
        dt = jax.nn.softplus(dt_raw.astype(f32) + dt_bias[d].astype(f32)).reshape(bsz, T, M2_GROUPS, M2_HPG)
        a = -jnp.exp(a_log[d].astype(f32)).reshape(M2_GROUPS, M2_HPG)
        return _ssd_chunked(xs, dt, a, bm, cm)
    xbc_f = _cat_f(cxbc, lxbc)
    y = (run(xbc_f, _cat_f(cdt[..., :M2_HEADS], ldt[..., :M2_HEADS]), 0)
         + _uncat_b(run(_cat_b(cxbc, lxbc), _cat_b(cdt[..., M2_HEADS:], ldt[..., M2_HEADS:]), 1), n_ctx))
    bsz, T, _ = xbc_f.shape
    skip = d_skip.astype(f32)[:, None] * xbc_f[..., :BR_W].reshape(bsz, T, M2_HEADS, M2_HEADDIM)
    y = y.reshape(bsz, T, BR_W) + skip.reshape(bsz, T, BR_W)
    return _rms(y * jax.nn.silu(_cat_f(cz, lz).astype(f32)), norm_w)


def _merge(h, ys, w_gate, b_gate, w_branch):
    out = jax.nn.sigmoid(h @ w_gate[0] + b_gate[0]) * (ys[0].astype(h.dtype) @ w_branch[0])
    for k in range(1, N_BRANCH):
        out = out + jax.nn.sigmoid(h @ w_gate[k] + b_gate[k]) * (ys[k].astype(h.dtype) @ w_branch[k])
    return out


def _fwd_setup_inputs(seed: int = 0) -> dict:
    key = jax.random.key(seed)
    ks = iter(jax.random.split(key, 48))
    f32 = jnp.float32

    def nrm(shape, s=1.0):
        return s * jax.random.normal(next(ks), shape, f32)

    def uni(shape, lo, hi):
        return jax.random.uniform(next(ks), shape, f32, lo, hi)
    L = DEPTH
    dt_m2 = jnp.exp(uni((L, 2, M2_HEADS), math.log(DT_MIN), math.log(DT_MAX)))
    lru_a = uni((L, 2, BR_W), 0.9, 0.999) ** (1.0 / LRU_C)
    return {
        'x': nrm((BATCH, SEQ, D_MODEL)),
        'c': nrm((BATCH, D_MODEL)),
        'ctx': nrm((BATCH, CTX_LEN, D_MODEL)),
        'c_ctx': nrm((D_MODEL,)),
        'norm_w': 1.0 + nrm((L, D_MODEL), 0.02),
        'w_mod': nrm((L, D_MODEL, 3 * D_MODEL), 0.5 * D_MODEL ** -0.5),
        'b_mod': nrm((L, 3 * D_MODEL), 0.01),
        'w_in': nrm((L, D_MODEL, IN_COLS), D_MODEL ** -0.5),
        'hg_lb_logits': nrm((L + 1, 2, BR_W), 0.1),
        'hg_norm': 1.0 + nrm((L, BR_W), 0.02),
        's5_a_re': -0.5 + nrm((L, 2, S5_GROUPS, S5_STATE), 0.01),
        's5_a_im': math.pi * jnp.arange(S5_STATE, dtype=f32) + nrm((L, 2, S5_GROUPS, S5_STATE), 0.01),
        's5_log_step': uni((L, 2, S5_GROUPS), math.log(DT_MIN), math.log(DT_MAX)),
        's5_b_re': nrm((L, S5_GROUPS, S5_STATE, S5_GROUP), (2 * S5_GROUP) ** -0.5),
        's5_b_im': nrm((L, S5_GROUPS, S5_STATE, S5_GROUP), (2 * S5_GROUP) ** -0.5),
        's5_c_re': nrm((L, S5_GROUPS, S5_GROUP, S5_STATE), S5_STATE ** -0.5),
        's5_c_im': nrm((L, S5_GROUPS, S5_GROUP, S5_STATE), S5_STATE ** -0.5),
        's5_d': nrm((L, BR_W)),
        's5_w_glu': nrm((L, BR_W, BR_W), BR_W ** -0.5),
        's5_b_glu': nrm((L, BR_W), 0.01),
        'lru_conv_w': nrm((L, CONV_W, BR_W), CONV_W ** -0.5),
        'lru_conv_b': nrm((L, BR_W), 0.01),
        'lru_gate_w': nrm((L, 2, 2, LRU_BLOCKS, LRU_BW, LRU_BW), LRU_BW ** -0.5),
        'lru_gate_b': nrm((L, 2, 2, LRU_BLOCKS, LRU_BW), 0.01),
        'lru_lam': jnp.log(lru_a) - jnp.log1p(-lru_a),
        'm2_conv_w': nrm((L, CONV_W, M2_XBC), CONV_W ** -0.5),
        'm2_conv_b': nrm((L, M2_XBC), 0.01),
        'm2_dt_bias': dt_m2 + jnp.log(-jnp.expm1(-dt_m2)),
        'm2_a_log': jnp.log(uni((L, 2, M2_HEADS), 1.0, 16.0)),
        'm2_d': 1.0 + nrm((L, M2_HEADS), 0.02),
        'm2_norm': 1.0 + nrm((L, BR_W), 0.02),
        'w_branch': nrm((L, N_BRANCH, BR_W, D_MODEL), BR_W ** -0.5),
        'w_gate': nrm((L, N_BRANCH, D_MODEL, D_MODEL), D_MODEL ** -0.5),
        'b_gate': nrm((L, N_BRANCH, D_MODEL), 0.01),
        'w_out': nrm((L, D_MODEL, D_MODEL), D_MODEL ** -0.5),
        'final_norm': 1.0 + nrm((D_MODEL,), 0.02),
    }


def _fwd_reference(x, c, ctx, c_ctx, norm_w, w_mod, b_mod, w_in, hg_lb_logits, hg_norm,
              s5_a_re, s5_a_im, s5_log_step, s5_b_re, s5_b_im, s5_c_re, s5_c_im, s5_d, s5_w_glu, s5_b_glu,
              lru_conv_w, lru_conv_b, lru_gate_w, lru_gate_b, lru_lam,
              m2_conv_w, m2_conv_b, m2_dt_bias, m2_a_log, m2_d, m2_norm,
              w_branch, w_gate, b_gate, w_out, final_norm):
    n_ctx = ctx.shape[1]
    lb_all = jnp.cumsum(jax.nn.softmax(hg_lb_logits.astype(jnp.float32), axis=0), axis=0)
    for l in range(DEPTH):
        mod = jax.nn.silu(c) @ w_mod[l] + b_mod[l]
        mod_c = jax.nn.silu(c_ctx) @ w_mod[l] + b_mod[l]
        sh, sc, gt = jnp.split(mod[:, None, :], 3, axis=-1)
        sh_c, sc_c, gt_c = jnp.split(mod_c, 3, axis=-1)
        h = _rms(x, norm_w[l]) * (1.0 + sc) + sh
        hc = _rms(ctx, norm_w[l]) * (1.0 + sc_c) + sh_c
        u = jnp.split(h @ w_in[l], IN_SPLITS, axis=-1)
        uc = jnp.split(hc @ w_in[l], IN_SPLITS, axis=-1)
        ys = (
            _hgrn2(uc[0:5], u[0:5], lb_all[l], hg_norm[l]),
            _s5(uc[5:7], u[5:7], s5_a_re[l], s5_a_im[l], s5_log_step[l], s5_b_re[l], s5_b_im[l],
                s5_c_re[l], s5_c_im[l], s5_d[l], s5_w_glu[l], s5_b_glu[l]),
            _rglru(uc[7:9], u[7:9], lru_conv_w[l], lru_conv_b[l], lru_gate_w[l], lru_gate_b[l], lru_lam[l]),
            _mamba2(uc[9:12], u[9:12], m2_conv_w[l], m2_conv_b[l], m2_dt_bias[l], m2_a_log[l], m2_d[l], m2_norm[l]),
        )
        x_new = x + gt * (_merge(h, [y[:, n_ctx:] for y in ys], w_gate[l], b_gate[l], w_branch[l]) @ w_out[l])
        if l < DEPTH - 1:
            ctx = ctx + gt_c * (_merge(hc, [y[:, :n_ctx] for y in ys], w_gate[l], b_gate[l], w_branch[l]) @ w_out[l])
        x = x_new
    return _rms(x, final_norm)


import jax as _jax
import jax.numpy as _jnp

TWIN_FORMAT = 'train_step'
FWD_PARAMS = ['x', 'c', 'ctx', 'c_ctx', 'norm_w', 'w_mod', 'b_mod', 'w_in', 'hg_lb_logits', 'hg_norm', 's5_a_re', 's5_a_im', 's5_log_step', 's5_b_re', 's5_b_im', 's5_c_re', 's5_c_im', 's5_d', 's5_w_glu', 's5_b_glu', 'lru_conv_w', 'lru_conv_b', 'lru_gate_w', 'lru_gate_b', 'lru_lam', 'm2_conv_w', 'm2_conv_b', 'm2_dt_bias', 'm2_a_log', 'm2_d', 'm2_norm', 'w_branch', 'w_gate', 'b_gate', 'w_out', 'final_norm']
TWIN_WEIGHTS = ['c_ctx', 'norm_w', 'w_mod', 'b_mod', 'w_in', 'hg_lb_logits', 'hg_norm', 's5_a_re', 's5_a_im', 's5_log_step', 's5_b_re', 's5_b_im', 's5_c_re', 's5_c_im', 's5_d', 's5_w_glu', 's5_b_glu', 'lru_conv_w', 'lru_conv_b', 'lru_gate_w', 'lru_gate_b', 'lru_lam', 'm2_conv_w', 'm2_conv_b', 'm2_dt_bias', 'm2_a_log', 'm2_d', 'm2_norm', 'w_branch', 'w_gate', 'b_gate', 'w_out', 'final_norm']
TWIN_DIFF_INPUT = 'x'
TWIN_INPUTS = ['x', 'c', 'ctx', 'c_ctx', 'norm_w', 'w_mod', 'b_mod', 'w_in', 'hg_lb_logits', 'hg_norm', 's5_a_re', 's5_a_im', 's5_log_step', 's5_b_re', 's5_b_im', 's5_c_re', 's5_c_im', 's5_d', 's5_w_glu', 's5_b_glu', 'lru_conv_w', 'lru_conv_b', 'lru_gate_w', 'lru_gate_b', 'lru_lam', 'm2_conv_w', 'm2_conv_b', 'm2_dt_bias', 'm2_a_log', 'm2_d', 'm2_norm', 'w_branch', 'w_gate', 'b_gate', 'w_out', 'final_norm', 'loss_target', 'm_c_ctx', 'm_norm_w', 'm_w_mod', 'm_b_mod', 'm_w_in', 'm_hg_lb_logits', 'm_hg_norm', 'm_s5_a_re', 'm_s5_a_im', 'm_s5_log_step', 'm_s5_b_re', 'm_s5_b_im', 'm_s5_c_re', 'm_s5_c_im', 'm_s5_d', 'm_s5_w_glu', 'm_s5_b_glu', 'm_lru_conv_w', 'm_lru_conv_b', 'm_lru_gate_w', 'm_lru_gate_b', 'm_lru_lam', 'm_m2_conv_w', 'm_m2_conv_b', 'm_m2_dt_bias', 'm_m2_a_log', 'm_m2_d', 'm_m2_norm', 'm_w_branch', 'm_w_gate', 'm_b_gate', 'm_w_out', 'm_final_norm', 'v_c_ctx', 'v_norm_w', 'v_w_mod', 'v_b_mod', 'v_w_in', 'v_hg_lb_logits', 'v_hg_norm', 'v_s5_a_re', 'v_s5_a_im', 'v_s5_log_step', 'v_s5_b_re', 'v_s5_b_im', 'v_s5_c_re', 'v_s5_c_im', 'v_s5_d', 'v_s5_w_glu', 'v_s5_b_glu', 'v_lru_conv_w', 'v_lru_conv_b', 'v_lru_gate_w', 'v_lru_gate_b', 'v_lru_lam', 'v_m2_conv_w', 'v_m2_conv_b', 'v_m2_dt_bias', 'v_m2_a_log', 'v_m2_d', 'v_m2_norm', 'v_w_branch', 'v_w_gate', 'v_b_gate', 'v_w_out', 'v_final_norm']
TWIN_OUTPUTS = ['loss', 'grad_x', 'grad_c_ctx', 'grad_norm_w', 'grad_w_mod', 'grad_b_mod', 'grad_w_in', 'grad_hg_lb_logits', 'grad_hg_norm', 'grad_s5_a_re', 'grad_s5_a_im', 'grad_s5_log_step', 'grad_s5_b_re', 'grad_s5_b_im', 'grad_s5_c_re', 'grad_s5_c_im', 'grad_s5_d', 'grad_s5_w_glu', 'grad_s5_b_glu', 'grad_lru_conv_w', 'grad_lru_conv_b', 'grad_lru_gate_w', 'grad_lru_gate_b', 'grad_lru_lam', 'grad_m2_conv_w', 'grad_m2_conv_b', 'grad_m2_dt_bias', 'grad_m2_a_log', 'grad_m2_d', 'grad_m2_norm', 'grad_w_branch', 'grad_w_gate', 'grad_b_gate', 'grad_w_out', 'grad_final_norm', 'delta_c_ctx', 'delta_norm_w', 'delta_w_mod', 'delta_b_mod', 'delta_w_in', 'delta_hg_lb_logits', 'delta_hg_norm', 'delta_s5_a_re', 'delta_s5_a_im', 'delta_s5_log_step', 'delta_s5_b_re', 'delta_s5_b_im', 'delta_s5_c_re', 'delta_s5_c_im', 'delta_s5_d', 'delta_s5_w_glu', 'delta_s5_b_glu', 'delta_lru_conv_w', 'delta_lru_conv_b', 'delta_lru_gate_w', 'delta_lru_gate_b', 'delta_lru_lam', 'delta_m2_conv_w', 'delta_m2_conv_b', 'delta_m2_dt_bias', 'delta_m2_a_log', 'delta_m2_d', 'delta_m2_norm', 'delta_w_branch', 'delta_w_gate', 'delta_b_gate', 'delta_w_out', 'delta_final_norm', 'new_m_c_ctx', 'new_m_norm_w', 'new_m_w_mod', 'new_m_b_mod', 'new_m_w_in', 'new_m_hg_lb_logits', 'new_m_hg_norm', 'new_m_s5_a_re', 'new_m_s5_a_im', 'new_m_s5_log_step', 'new_m_s5_b_re', 'new_m_s5_b_im', 'new_m_s5_c_re', 'new_m_s5_c_im', 'new_m_s5_d', 'new_m_s5_w_glu', 'new_m_s5_b_glu', 'new_m_lru_conv_w', 'new_m_lru_conv_b', 'new_m_lru_gate_w', 'new_m_lru_gate_b', 'new_m_lru_lam', 'new_m_m2_conv_w', 'new_m_m2_conv_b', 'new_m_m2_dt_bias', 'new_m_m2_a_log', 'new_m_m2_d', 'new_m_m2_norm', 'new_m_w_branch', 'new_m_w_gate', 'new_m_b_gate', 'new_m_w_out', 'new_m_final_norm', 'new_v_c_ctx', 'new_v_norm_w', 'new_v_w_mod', 'new_v_b_mod', 'new_v_w_in', 'new_v_hg_lb_logits', 'new_v_hg_norm', 'new_v_s5_a_re', 'new_v_s5_a_im', 'new_v_s5_log_step', 'new_v_s5_b_re', 'new_v_s5_b_im', 'new_v_s5_c_re', 'new_v_s5_c_im', 'new_v_s5_d', 'new_v_s5_w_glu', 'new_v_s5_b_glu', 'new_v_lru_conv_w', 'new_v_lru_conv_b', 'new_v_lru_gate_w', 'new_v_lru_gate_b', 'new_v_lru_lam', 'new_v_m2_conv_w', 'new_v_m2_conv_b', 'new_v_m2_dt_bias', 'new_v_m2_a_log', 'new_v_m2_d', 'new_v_m2_norm', 'new_v_w_branch', 'new_v_w_gate', 'new_v_b_gate', 'new_v_w_out', 'new_v_final_norm']
TWIN_LEAF_KINDS = {'loss': 'loss', 'grad_x': 'grad_x', 'grad_c_ctx': 'grad_w', 'grad_norm_w': 'grad_w', 'grad_w_mod': 'grad_w', 'grad_b_mod': 'grad_w', 'grad_w_in': 'grad_w', 'grad_hg_lb_logits': 'grad_w', 'grad_hg_norm': 'grad_w', 'grad_s5_a_re': 'grad_w', 'grad_s5_a_im': 'grad_w', 'grad_s5_log_step': 'grad_w', 'grad_s5_b_re': 'grad_w', 'grad_s5_b_im': 'grad_w', 'grad_s5_c_re': 'grad_w', 'grad_s5_c_im': 'grad_w', 'grad_s5_d': 'grad_w', 'grad_s5_w_glu': 'grad_w', 'grad_s5_b_glu': 'grad_w', 'grad_lru_conv_w': 'grad_w', 'grad_lru_conv_b': 'grad_w', 'grad_lru_gate_w': 'grad_w', 'grad_lru_gate_b': 'grad_w', 'grad_lru_lam': 'grad_w', 'grad_m2_conv_w': 'grad_w', 'grad_m2_conv_b': 'grad_w', 'grad_m2_dt_bias': 'grad_w', 'grad_m2_a_log': 'grad_w', 'grad_m2_d': 'grad_w', 'grad_m2_norm': 'grad_w', 'grad_w_branch': 'grad_w', 'grad_w_gate': 'grad_w', 'grad_b_gate': 'grad_w', 'grad_w_out': 'grad_w', 'grad_final_norm': 'grad_w', 'delta_c_ctx': 'delta_w', 'delta_norm_w': 'delta_w', 'delta_w_mod': 'delta_w', 'delta_b_mod': 'delta_w', 'delta_w_in': 'delta_w', 'delta_hg_lb_logits': 'delta_w', 'delta_hg_norm': 'delta_w', 'delta_s5_a_re': 'delta_w', 'delta_s5_a_im': 'delta_w', 'delta_s5_log_step': 'delta_w', 'delta_s5_b_re': 'delta_w', 'delta_s5_b_im': 'delta_w', 'delta_s5_c_re': 'delta_w', 'delta_s5_c_im': 'delta_w', 'delta_s5_d': 'delta_w', 'delta_s5_w_glu': 'delta_w', 'delta_s5_b_glu': 'delta_w', 'delta_lru_conv_w': 'delta_w', 'delta_lru_conv_b': 'delta_w', 'delta_lru_gate_w': 'delta_w', 'delta_lru_gate_b': 'delta_w', 'delta_lru_lam': 'delta_w', 'delta_m2_conv_w': 'delta_w', 'delta_m2_conv_b': 'delta_w', 'delta_m2_dt_bias': 'delta_w', 'delta_m2_a_log': 'delta_w', 'delta_m2_d': 'delta_w', 'delta_m2_norm': 'delta_w', 'delta_w_branch': 'delta_w', 'delta_w_gate': 'delta_w', 'delta_b_gate': 'delta_w', 'delta_w_out': 'delta_w', 'delta_final_norm': 'delta_w', 'new_m_c_ctx': 'new_m', 'new_m_norm_w': 'new_m', 'new_m_w_mod': 'new_m', 'new_m_b_mod': 'new_m', 'new_m_w_in': 'new_m', 'new_m_hg_lb_logits': 'new_m', 'new_m_hg_norm': 'new_m', 'new_m_s5_a_re': 'new_m', 'new_m_s5_a_im': 'new_m', 'new_m_s5_log_step': 'new_m', 'new_m_s5_b_re': 'new_m', 'new_m_s5_b_im': 'new_m', 'new_m_s5_c_re': 'new_m', 'new_m_s5_c_im': 'new_m', 'new_m_s5_d': 'new_m', 'new_m_s5_w_glu': 'new_m', 'new_m_s5_b_glu': 'new_m', 'new_m_lru_conv_w': 'new_m', 'new_m_lru_conv_b': 'new_m', 'new_m_lru_gate_w': 'new_m', 'new_m_lru_gate_b': 'new_m', 'new_m_lru_lam': 'new_m', 'new_m_m2_conv_w': 'new_m', 'new_m_m2_conv_b': 'new_m', 'new_m_m2_dt_bias': 'new_m', 'new_m_m2_a_log': 'new_m', 'new_m_m2_d': 'new_m', 'new_m_m2_norm': 'new_m', 'new_m_w_branch': 'new_m', 'new_m_w_gate': 'new_m', 'new_m_b_gate': 'new_m', 'new_m_w_out': 'new_m', 'new_m_final_norm': 'new_m', 'new_v_c_ctx': 'new_v', 'new_v_norm_w': 'new_v', 'new_v_w_mod': 'new_v', 'new_v_b_mod': 'new_v', 'new_v_w_in': 'new_v', 'new_v_hg_lb_logits': 'new_v', 'new_v_hg_norm': 'new_v', 'new_v_s5_a_re': 'new_v', 'new_v_s5_a_im': 'new_v', 'new_v_s5_log_step': 'new_v', 'new_v_s5_b_re': 'new_v', 'new_v_s5_b_im': 'new_v', 'new_v_s5_c_re': 'new_v', 'new_v_s5_c_im': 'new_v', 'new_v_s5_d': 'new_v', 'new_v_s5_w_glu': 'new_v', 'new_v_s5_b_glu': 'new_v', 'new_v_lru_conv_w': 'new_v', 'new_v_lru_conv_b': 'new_v', 'new_v_lru_gate_w': 'new_v', 'new_v_lru_gate_b': 'new_v', 'new_v_lru_lam': 'new_v', 'new_v_m2_conv_w': 'new_v', 'new_v_m2_conv_b': 'new_v', 'new_v_m2_dt_bias': 'new_v', 'new_v_m2_a_log': 'new_v', 'new_v_m2_d': 'new_v', 'new_v_m2_norm': 'new_v', 'new_v_w_branch': 'new_v', 'new_v_w_gate': 'new_v', 'new_v_b_gate': 'new_v', 'new_v_w_out': 'new_v', 'new_v_final_norm': 'new_v'}


def _forward(args):
    return _fwd_reference(*[args[k] for k in FWD_PARAMS])


def _output_shape():
    out = _jax.eval_shape(lambda: _forward(_fwd_setup_inputs(0)))
    return out.shape, out.dtype

N_MICROBATCH = 1
ADAM_LR = 0.001
ADAM_B1 = 0.9
ADAM_B2 = 0.999
ADAM_EPS = 1e-08
ADAM_WD = 0.01
ADAM_STEP = 10
PER_EXAMPLE_BATCH_AXIS = {'x': 0, 'c': 0, 'ctx': 0, 'loss_target': 0}
SHARED_INPUTS = []
_WEIGHT_DTYPES = {'c_ctx': _jnp.float32, 'norm_w': _jnp.float32, 'w_mod': _jnp.float32, 'b_mod': _jnp.float32, 'w_in': _jnp.float32, 'hg_lb_logits': _jnp.float32, 'hg_norm': _jnp.float32, 's5_a_re': _jnp.float32, 's5_a_im': _jnp.float32, 's5_log_step': _jnp.float32, 's5_b_re': _jnp.float32, 's5_b_im': _jnp.float32, 's5_c_re': _jnp.float32, 's5_c_im': _jnp.float32, 's5_d': _jnp.float32, 's5_w_glu': _jnp.float32, 's5_b_glu': _jnp.float32, 'lru_conv_w': _jnp.float32, 'lru_conv_b': _jnp.float32, 'lru_gate_w': _jnp.float32, 'lru_gate_b': _jnp.float32, 'lru_lam': _jnp.float32, 'm2_conv_w': _jnp.float32, 'm2_conv_b': _jnp.float32, 'm2_dt_bias': _jnp.float32, 'm2_a_log': _jnp.float32, 'm2_d': _jnp.float32, 'm2_norm': _jnp.float32, 'w_branch': _jnp.float32, 'w_gate': _jnp.float32, 'b_gate': _jnp.float32, 'w_out': _jnp.float32, 'final_norm': _jnp.float32}
MOMENT_SCALE = {'c_ctx': 2.895571e-02, 'norm_w': 8.642769e-02, 'w_mod': 1.365082e-01, 'b_mod': 2.291561e-01, 'w_in': 4.558836e-02, 'hg_lb_logits': 1.095207e-03, 'hg_norm': 2.551298e-02, 's5_a_re': 1.004455e-03, 's5_a_im': 1.118798e-03, 's5_log_step': 4.227703e-01, 's5_b_re': 9.993362e-04, 's5_b_im': 9.766700e-04, 's5_c_re': 1.314520e-03, 's5_c_im': 1.297197e-03, 's5_d': 1.030317e-02, 's5_w_glu': 2.983179e-03, 's5_b_glu': 4.065168e-03, 'lru_conv_w': 9.882603e-02, 'lru_conv_b': 3.325084e-01, 'lru_gate_w': 1.024920e-02, 'lru_gate_b': 1.497121e-02, 'lru_lam': 2.366971e-02, 'm2_conv_w': 3.601554e-02, 'm2_conv_b': 4.451842e-02, 'm2_dt_bias': 7.932548e-02, 'm2_a_log': 7.095120e-02, 'm2_d': 2.556256e-01, 'm2_norm': 3.670669e-02, 'w_branch': 3.808739e-02, 'w_gate': 1.525907e-02, 'b_gate': 1.468772e-02, 'w_out': 7.631262e-02, 'final_norm': 3.223789e+01}


def _to_microbatches(a, axis):
    t = _jnp.moveaxis(a, axis, 0)
    t = t.reshape((N_MICROBATCH, t.shape[0] // N_MICROBATCH) + t.shape[1:])
    return _jnp.moveaxis(t, 1, axis + 1)


def setup_inputs(seed: int = 0) -> dict:
    inp = _fwd_setup_inputs(seed)
    key = _jax.random.fold_in(_jax.random.key(seed), 7919)
    shape, _ = _output_shape()
    out = dict(inp)
    out["loss_target"] = _jax.random.normal(_jax.random.fold_in(key, 0), shape, _jnp.float32)
    for i, name in enumerate(TWIN_WEIGHTS):
        w = inp[name].astype(_jnp.float32)
        if MOMENT_SCALE is None:
            s = _jnp.sqrt(_jnp.mean(_jnp.square(w)) + 1e-30)
        else:
            s = MOMENT_SCALE[name]
        km, kv = _jax.random.split(_jax.random.fold_in(key, i + 1))
        out[name] = w
        out["m_" + name] = s * _jax.random.normal(km, w.shape, _jnp.float32)
        out["v_" + name] = (s * s) * _jax.random.uniform(kv, w.shape, _jnp.float32, 0.5, 1.5)
    if N_MICROBATCH > 1:
        for name, axis in PER_EXAMPLE_BATCH_AXIS.items():
            out[name] = _to_microbatches(out[name], axis)
    return {'x': out['x'], 'c': out['c'], 'ctx': out['ctx'], 'c_ctx': out['c_ctx'], 'norm_w': out['norm_w'], 'w_mod': out['w_mod'], 'b_mod': out['b_mod'], 'w_in': out['w_in'], 'hg_lb_logits': out['hg_lb_logits'], 'hg_norm': out['hg_norm'], 's5_a_re': out['s5_a_re'], 's5_a_im': out['s5_a_im'], 's5_log_step': out['s5_log_step'], 's5_b_re': out['s5_b_re'], 's5_b_im': out['s5_b_im'], 's5_c_re': out['s5_c_re'], 's5_c_im': out['s5_c_im'], 's5_d': out['s5_d'], 's5_w_glu': out['s5_w_glu'], 's5_b_glu': out['s5_b_glu'], 'lru_conv_w': out['lru_conv_w'], 'lru_conv_b': out['lru_conv_b'], 'lru_gate_w': out['lru_gate_w'], 'lru_gate_b': out['lru_gate_b'], 'lru_lam': out['lru_lam'], 'm2_conv_w': out['m2_conv_w'], 'm2_conv_b': out['m2_conv_b'], 'm2_dt_bias': out['m2_dt_bias'], 'm2_a_log': out['m2_a_log'], 'm2_d': out['m2_d'], 'm2_norm': out['m2_norm'], 'w_branch': out['w_branch'], 'w_gate': out['w_gate'], 'b_gate': out['b_gate'], 'w_out': out['w_out'], 'final_norm': out['final_norm'], 'loss_target': out['loss_target'], 'm_c_ctx': out['m_c_ctx'], 'm_norm_w': out['m_norm_w'], 'm_w_mod': out['m_w_mod'], 'm_b_mod': out['m_b_mod'], 'm_w_in': out['m_w_in'], 'm_hg_lb_logits': out['m_hg_lb_logits'], 'm_hg_norm': out['m_hg_norm'], 'm_s5_a_re': out['m_s5_a_re'], 'm_s5_a_im': out['m_s5_a_im'], 'm_s5_log_step': out['m_s5_log_step'], 'm_s5_b_re': out['m_s5_b_re'], 'm_s5_b_im': out['m_s5_b_im'], 'm_s5_c_re': out['m_s5_c_re'], 'm_s5_c_im': out['m_s5_c_im'], 'm_s5_d': out['m_s5_d'], 'm_s5_w_glu': out['m_s5_w_glu'], 'm_s5_b_glu': out['m_s5_b_glu'], 'm_lru_conv_w': out['m_lru_conv_w'], 'm_lru_conv_b': out['m_lru_conv_b'], 'm_lru_gate_w': out['m_lru_gate_w'], 'm_lru_gate_b': out['m_lru_gate_b'], 'm_lru_lam': out['m_lru_lam'], 'm_m2_conv_w': out['m_m2_conv_w'], 'm_m2_conv_b': out['m_m2_conv_b'], 'm_m2_dt_bias': out['m_m2_dt_bias'], 'm_m2_a_log': out['m_m2_a_log'], 'm_m2_d': out['m_m2_d'], 'm_m2_norm': out['m_m2_norm'], 'm_w_branch': out['m_w_branch'], 'm_w_gate': out['m_w_gate'], 'm_b_gate': out['m_b_gate'], 'm_w_out': out['m_w_out'], 'm_final_norm': out['m_final_norm'], 'v_c_ctx': out['v_c_ctx'], 'v_norm_w': out['v_norm_w'], 'v_w_mod': out['v_w_mod'], 'v_b_mod': out['v_b_mod'], 'v_w_in': out['v_w_in'], 'v_hg_lb_logits': out['v_hg_lb_logits'], 'v_hg_norm': out['v_hg_norm'], 'v_s5_a_re': out['v_s5_a_re'], 'v_s5_a_im': out['v_s5_a_im'], 'v_s5_log_step': out['v_s5_log_step'], 'v_s5_b_re': out['v_s5_b_re'], 'v_s5_b_im': out['v_s5_b_im'], 'v_s5_c_re': out['v_s5_c_re'], 'v_s5_c_im': out['v_s5_c_im'], 'v_s5_d': out['v_s5_d'], 'v_s5_w_glu': out['v_s5_w_glu'], 'v_s5_b_glu': out['v_s5_b_glu'], 'v_lru_conv_w': out['v_lru_conv_w'], 'v_lru_conv_b': out['v_lru_conv_b'], 'v_lru_gate_w': out['v_lru_gate_w'], 'v_lru_gate_b': out['v_lru_gate_b'], 'v_lru_lam': out['v_lru_lam'], 'v_m2_conv_w': out['v_m2_conv_w'], 'v_m2_conv_b': out['v_m2_conv_b'], 'v_m2_dt_bias': out['v_m2_dt_bias'], 'v_m2_a_log': out['v_m2_a_log'], 'v_m2_d': out['v_m2_d'], 'v_m2_norm': out['v_m2_norm'], 'v_w_branch': out['v_w_branch'], 'v_w_gate': out['v_w_gate'], 'v_b_gate': out['v_b_gate'], 'v_w_out': out['v_w_out'], 'v_final_norm': out['v_final_norm']}


def _loss(weights, diff, rest, loss_target):
    with _jax.named_scope("forward"):
        args = {**rest, TWIN_DIFF_INPUT: diff, **{k: w.astype(_WEIGHT_DTYPES[k]) for k, w in weights.items()}}
        y = _forward(args)
    with _jax.named_scope("loss_head"):
        err = _jnp.square(y.astype(_jnp.float32) - loss_target)
        return 0.5 * _jnp.sum(_jnp.mean(err, axis=-1)) if err.ndim else 0.5 * err


def _adamw(w, g, m, v):
    m = ADAM_B1 * m + (1.0 - ADAM_B1) * g
    v = ADAM_B2 * v + (1.0 - ADAM_B2) * _jnp.square(g)
    m_hat = m / (1.0 - ADAM_B1 ** ADAM_STEP)
    v_hat = v / (1.0 - ADAM_B2 ** ADAM_STEP)
    delta = -ADAM_LR * (m_hat / (_jnp.sqrt(v_hat) + ADAM_EPS) + ADAM_WD * w)
    return delta, m, v


def reference(x, c, ctx, c_ctx, norm_w, w_mod, b_mod, w_in, hg_lb_logits, hg_norm, s5_a_re, s5_a_im, s5_log_step, s5_b_re, s5_b_im, s5_c_re, s5_c_im, s5_d, s5_w_glu, s5_b_glu, lru_conv_w, lru_conv_b, lru_gate_w, lru_gate_b, lru_lam, m2_conv_w, m2_conv_b, m2_dt_bias, m2_a_log, m2_d, m2_norm, w_branch, w_gate, b_gate, w_out, final_norm, loss_target, m_c_ctx, m_norm_w, m_w_mod, m_b_mod, m_w_in, m_hg_lb_logits, m_hg_norm, m_s5_a_re, m_s5_a_im, m_s5_log_step, m_s5_b_re, m_s5_b_im, m_s5_c_re, m_s5_c_im, m_s5_d, m_s5_w_glu, m_s5_b_glu, m_lru_conv_w, m_lru_conv_b, m_lru_gate_w, m_lru_gate_b, m_lru_lam, m_m2_conv_w, m_m2_conv_b, m_m2_dt_bias, m_m2_a_log, m_m2_d, m_m2_norm, m_w_branch, m_w_gate, m_b_gate, m_w_out, m_final_norm, v_c_ctx, v_norm_w, v_w_mod, v_b_mod, v_w_in, v_hg_lb_logits, v_hg_norm, v_s5_a_re, v_s5_a_im, v_s5_log_step, v_s5_b_re, v_s5_b_im, v_s5_c_re, v_s5_c_im, v_s5_d, v_s5_w_glu, v_s5_b_glu, v_lru_conv_w, v_lru_conv_b, v_lru_gate_w, v_lru_gate_b, v_lru_lam, v_m2_conv_w, v_m2_conv_b, v_m2_dt_bias, v_m2_a_log, v_m2_d, v_m2_norm, v_w_branch, v_w_gate, v_b_gate, v_w_out, v_final_norm):
    given = dict(x=x, c=c, ctx=ctx, c_ctx=c_ctx, norm_w=norm_w, w_mod=w_mod, b_mod=b_mod, w_in=w_in, hg_lb_logits=hg_lb_logits, hg_norm=hg_norm, s5_a_re=s5_a_re, s5_a_im=s5_a_im, s5_log_step=s5_log_step, s5_b_re=s5_b_re, s5_b_im=s5_b_im, s5_c_re=s5_c_re, s5_c_im=s5_c_im, s5_d=s5_d, s5_w_glu=s5_w_glu, s5_b_glu=s5_b_glu, lru_conv_w=lru_conv_w, lru_conv_b=lru_conv_b, lru_gate_w=lru_gate_w, lru_gate_b=lru_gate_b, lru_lam=lru_lam, m2_conv_w=m2_conv_w, m2_conv_b=m2_conv_b, m2_dt_bias=m2_dt_bias, m2_a_log=m2_a_log, m2_d=m2_d, m2_norm=m2_norm, w_branch=w_branch, w_gate=w_gate, b_gate=b_gate, w_out=w_out, final_norm=final_norm, loss_target=loss_target, m_c_ctx=m_c_ctx, m_norm_w=m_norm_w, m_w_mod=m_w_mod, m_b_mod=m_b_mod, m_w_in=m_w_in, m_hg_lb_logits=m_hg_lb_logits, m_hg_norm=m_hg_norm, m_s5_a_re=m_s5_a_re, m_s5_a_im=m_s5_a_im, m_s5_log_step=m_s5_log_step, m_s5_b_re=m_s5_b_re, m_s5_b_im=m_s5_b_im, m_s5_c_re=m_s5_c_re, m_s5_c_im=m_s5_c_im, m_s5_d=m_s5_d, m_s5_w_glu=m_s5_w_glu, m_s5_b_glu=m_s5_b_glu, m_lru_conv_w=m_lru_conv_w, m_lru_conv_b=m_lru_conv_b, m_lru_gate_w=m_lru_gate_w, m_lru_gate_b=m_lru_gate_b, m_lru_lam=m_lru_lam, m_m2_conv_w=m_m2_conv_w, m_m2_conv_b=m_m2_conv_b, m_m2_dt_bias=m_m2_dt_bias, m_m2_a_log=m_m2_a_log, m_m2_d=m_m2_d, m_m2_norm=m_m2_norm, m_w_branch=m_w_branch, m_w_gate=m_w_gate, m_b_gate=m_b_gate, m_w_out=m_w_out, m_final_norm=m_final_norm, v_c_ctx=v_c_ctx, v_norm_w=v_norm_w, v_w_mod=v_w_mod, v_b_mod=v_b_mod, v_w_in=v_w_in, v_hg_lb_logits=v_hg_lb_logits, v_hg_norm=v_hg_norm, v_s5_a_re=v_s5_a_re, v_s5_a_im=v_s5_a_im, v_s5_log_step=v_s5_log_step, v_s5_b_re=v_s5_b_re, v_s5_b_im=v_s5_b_im, v_s5_c_re=v_s5_c_re, v_s5_c_im=v_s5_c_im, v_s5_d=v_s5_d, v_s5_w_glu=v_s5_w_glu, v_s5_b_glu=v_s5_b_glu, v_lru_conv_w=v_lru_conv_w, v_lru_conv_b=v_lru_conv_b, v_lru_gate_w=v_lru_gate_w, v_lru_gate_b=v_lru_gate_b, v_lru_lam=v_lru_lam, v_m2_conv_w=v_m2_conv_w, v_m2_conv_b=v_m2_conv_b, v_m2_dt_bias=v_m2_dt_bias, v_m2_a_log=v_m2_a_log, v_m2_d=v_m2_d, v_m2_norm=v_m2_norm, v_w_branch=v_w_branch, v_w_gate=v_w_gate, v_b_gate=v_b_gate, v_w_out=v_w_out, v_final_norm=v_final_norm)
    weights = {n: given[n] for n in TWIN_WEIGHTS}
    shared = {n: given[n] for n in SHARED_INPUTS}
    per_example = {n: given[n] for n in ['x', 'c', 'ctx']}
    grad_fn = _jax.value_and_grad(_loss, argnums=(0, 1))

    def one_microbatch(ex, loss_target):
        ex = dict(ex)
        diff = ex.pop(TWIN_DIFF_INPUT)
        return grad_fn(weights, diff, {**shared, **ex}, loss_target)

    if N_MICROBATCH == 1:
        loss, (grad_w, grad_x) = one_microbatch(per_example, given["loss_target"])
    else:
        def body(carry, xs):
            loss_sum, grad_sum = carry
            l_k, (gw_k, gx_k) = one_microbatch(xs[0], xs[1])
            with _jax.named_scope("update"):
                return (loss_sum + l_k, _jax.tree.map(_jnp.add, grad_sum, gw_k)), gx_k

        init = (_jnp.zeros((), _jnp.float32), _jax.tree.map(_jnp.zeros_like, weights))
        (loss, grad_w), grad_x = _jax.lax.scan(body, init, (per_example, given["loss_target"]))
    with _jax.named_scope("update"):
        delta_w, new_m, new_v = {}, {}, {}
        for n in TWIN_WEIGHTS:
            delta_w[n], new_m[n], new_v[n] = _adamw(weights[n], grad_w[n], given["m_" + n], given["v_" + n])
    return (loss, grad_x, *[grad_w[n] for n in TWIN_WEIGHTS], *[delta_w[n] for n in TWIN_WEIGHTS],
            *[new_m[n] for n in TWIN_WEIGHTS], *[new_v[n] for n in TWIN_WEIGHTS])
```

```python
import functools
import math

import jax
import jax.numpy as jnp
from jax import lax
from jax.experimental import pallas as pl
from jax.experimental.pallas import tpu as pltpu

f32 = jnp.float32
bf16 = jnp.bfloat16
_MM_DTYPE = bf16
_HI = lax.Precision.HIGHEST
_VMEM_LIMIT = 56 * 1024 * 1024
_LANE = 128
_SUB = 8

EPS = 1e-6
CONV_W = 4
CHUNK = 64
RB = 256
BR_W = 512
HG_HEADS = 4
HG_DK = 128
S5_GROUPS = 32
S5_GROUP = 16
S5_STATE = 64
LRU_BLOCKS = 8
LRU_C = 8.0
M2_HEADS = 8
M2_HEADDIM = 64
M2_GROUPS = 2
M2_STATE = 64
M2_XBC = BR_W + 2 * M2_GROUPS * M2_STATE
ADAM_LR = 0.001
ADAM_B1 = 0.9
ADAM_B2 = 0.999
ADAM_EPS = 1e-08
ADAM_WD = 0.01
ADAM_STEP = 10

_NN = (((1,), (0,)), ((), ()))
_NT = (((1,), (1,)), ((), ()))
_TN = (((0,), (0,)), ((), ()))


def _silu(x):
    return x * jax.nn.sigmoid(x)


def _softplus(x):
    return jnp.maximum(x, 0.0) + jnp.log1p(jnp.exp(-jnp.abs(x)))


def _one_minus_exp(z):
    series = -z * (1.0 + z * 0.5 * (1.0 + z * (1.0 / 3.0) * (1.0 + z * 0.25 * (1.0 + z * 0.2))))
    return jnp.where(z > -0.05, series, 1.0 - jnp.exp(z))


def _rms(x, w):
    return x * lax.rsqrt(jnp.mean(x * x, axis=-1, keepdims=True) + EPS) * w


def _dot(a, b, dn=_NN, hi=False):
    return lax.dot_general(a, b, dn, precision=_HI if hi else None, preferred_element_type=f32)


def _cparams(n_grid):
    return pltpu.CompilerParams(dimension_semantics=("arbitrary",) * n_grid, vmem_limit_bytes=_VMEM_LIMIT)


_REV = {"asc": "desc", "d1": "d1r", "desc": "asc", "d1r": "d1"}


def _blk(order, i, n):
    if order == "asc":
        return i
    if order == "desc":
        return n - 1 - i
    if order == "d1":
        return jnp.where(i == 0, 0, n - i)
    return jnp.where(i == n - 1, 0, i + 1)


def _pick(n, cap, unit):
    if n <= cap:
        return n
    best = None
    d = unit
    while d <= cap:
        if n % d == 0:
            best = d
        d += unit
    return n if best is None else best


def _mm_call(name, a, b, mode, hi, out_dtype):
    if mode == "tn":
        k, m = a.shape
        n = b.shape[1]
        tm = _pick(m, 256, _LANE)
        tn = _pick(n, 512, _LANE)
        a_spec = pl.BlockSpec((k, tm), lambda i, j: (0, i))
        b_spec = pl.BlockSpec((k, tn), lambda i, j: (0, j))
    else:
        m, k = a.shape
        tm = _pick(m, 256, _SUB)
        a_spec = pl.BlockSpec((tm, k), lambda i, j: (i, 0))
        if mode == "nn":
            n = b.shape[1]
            tn = _pick(n, max(_LANE, (4 * 1024 * 1024 // (k * 4)) // _LANE * _LANE), _LANE)
            b_spec = pl.BlockSpec((k, tn), lambda i, j: (0, j))
        else:
            n = b.shape[0]
            tn = _pick(n, max(_LANE, (4 * 1024 * 1024 // (k * 4)) // _LANE * _LANE), _LANE)
            b_spec = pl.BlockSpec((tn, k), lambda i, j: (j, 0))
    dn = {"nn": _NN, "nt": _NT, "tn": _TN}[mode]

    def body(a_ref, b_ref, o_ref):
        av = a_ref[...]
        bv = b_ref[...]
        if hi:
            av = av.astype(f32)
            bv = bv.astype(f32)
        else:
            av = av.astype(_MM_DTYPE)
            bv = bv.astype(_MM_DTYPE)
        o_ref[...] = _dot(av, bv, dn, hi).astype(o_ref.dtype)

    return pl.pallas_call(
        body, name=name, grid=(m // tm, n // tn), in_specs=[a_spec, b_spec],
        out_specs=pl.BlockSpec((tm, tn), lambda i, j: (i, j)),
        out_shape=jax.ShapeDtypeStruct((m, n), out_dtype), compiler_params=_cparams(2),
    )(a, b)


def mm(name, a, b, slot=None, hi=False):
    @jax.custom_vjp
    def op(a, b, slot):
        return _mm_call(name, a, b, "nn", hi, f32)

    def fwd(a, b, slot):
        return op(a, b, slot), (a, b)

    def bwd(res, g):
        a, b = res
        da = _mm_call(name + "_da", g, b, "nt", hi, a.dtype)
        db = _mm_call(name + "_db", a, g, "tn", hi, f32)
        if slot is None:
            return da, db.astype(b.dtype), None
        return da, jnp.zeros_like(b), db

    op.defvjp(fwd, bwd)
    return op(a, b, slot)


def _blocked_fwd(name, f, order, rb, params, xs, out_sds, carry_sds):
    t = xs[0].shape[0]
    n = t // rb
    n_p, n_x, n_o, n_c = len(params), len(xs), len(out_sds), len(carry_sds)

    def body(*refs):
        p_refs = refs[:n_p]
        x_refs = refs[n_p:n_p + n_x]
        o_refs = refs[n_p + n_x:n_p + n_x + n_o]
        st_refs = refs[n_p + n_x + n_o:n_p + n_x + n_o + n_c]
        c_refs = refs[n_p + n_x + n_o + n_c:]
        i = pl.program_id(0)
        blk = _blk(order, i, n)
        p = [r[...] for r in p_refs]
        x = [r[...] for r in x_refs]
        if n_c:
            @pl.when(i == 0)
            def _():
                for c in c_refs:
                    c[...] = jnp.zeros_like(c)
            c_in = [c[...] for c in c_refs]
            for sr, c in zip(st_refs, c_in):
                sr[0] = c
            c_out, ys = f(blk, p, c_in, x)
            for c, v in zip(c_refs, c_out):
                c[...] = v
        else:
            ys = f(blk, p, x)
        for o, y in zip(o_refs, ys):
            o[...] = y.astype(o.dtype)

    row = lambda i: (_blk(order, i, n), 0)
    in_specs = [pl.BlockSpec(p.shape, lambda i: (0, 0)) for p in params]
    in_specs += [pl.BlockSpec((rb, x.shape[1]), row) for x in xs]
    out_specs = [pl.BlockSpec((rb, c), row) for c, _ in out_sds]
    out_specs += [pl.BlockSpec((1,) + s, lambda i: (_blk(order, i, n), 0, 0)) for s in carry_sds]
    out_shape = [jax.ShapeDtypeStruct((t, c), d) for c, d in out_sds]
    out_shape += [jax.ShapeDtypeStruct((n,) + s, f32) for s in carry_sds]
    res = pl.pallas_call(
        body, name=name, grid=(n,), in_specs=in_specs, out_specs=out_specs, out_shape=out_shape,
        scratch_shapes=[pltpu.VMEM(s, f32) for s in carry_sds], compiler_params=_cparams(1),
    )(*params, *xs)
    return list(res[:n_o]), list(res[n_o:])


def _blocked_bwd(name, f, order, rb, params, xs, states, dys, carry_sds):
    t = xs[0].shape[0]
    n = t // rb
    rorder = _REV[order]
    n_p, n_x, n_o, n_c = len(params), len(xs), len(dys), len(carry_sds)

    def body(*refs):
        k = 0
        p_refs = refs[k:k + n_p]; k += n_p
        x_refs = refs[k:k + n_x]; k += n_x
        st_refs = refs[k:k + n_c]; k += n_c
        dy_refs = refs[k:k + n_o]; k += n_o
        dp_refs = refs[k:k + n_p]; k += n_p
        dx_refs = refs[k:k + n_x]; k += n_x
        dc_refs = refs[k:]
        i = pl.program_id(0)
        blk = _blk(rorder, i, n)
        p = [r[...] for r in p_refs]
        x = [r[...] for r in x_refs]
        dy = [r[...] for r in dy_refs]
        if n_c:
            @pl.when(i == 0)
            def _():
                for c in dc_refs:
                    c[...] = jnp.zeros_like(c)
            c_in = [r[0] for r in st_refs]
            dc = [c[...] for c in dc_refs]
            _, vjp = jax.vjp(lambda p_, c_, x_: f(blk, p_, c_, x_), p, c_in, x)
            dp, dcin, dx = vjp((dc, dy))
            for c, v in zip(dc_refs, dcin):
                c[...] = v
        else:
            _, vjp = jax.vjp(lambda p_, x_: f(blk, p_, x_), p, x)
            dp, dx = vjp(dy)

        @pl.when(i == 0)
        def _():
            for r, v in zip(dp_refs, dp):
                r[...] = v

        @pl.when(i > 0)
        def _():
            for r, v in zip(dp_refs, dp):
                r[...] += v
        for r, v in zip(dx_refs, dx):
            r[...] = v.astype(r.dtype)

    row = lambda i: (_blk(rorder, i, n), 0)
    in_specs = [pl.BlockSpec(p.shape, lambda i: (0, 0)) for p in params]
    in_specs += [pl.BlockSpec((rb, x.shape[1]), row) for x in xs]
    in_specs += [pl.BlockSpec((1,) + s, lambda i: (_blk(rorder, i, n), 0, 0)) for s in carry_sds]
    in_specs += [pl.BlockSpec((rb, d.shape[1]), row) for d in dys]
    out_specs = [pl.BlockSpec(p.shape, lambda i: (0, 0)) for p in params]
    out_specs += [pl.BlockSpec((rb, x.shape[1]), row) for x in xs]
    out_shape = [jax.ShapeDtypeStruct(p.shape, f32) for p in params]
    out_shape += [jax.ShapeDtypeStruct(x.shape, x.dtype) for x in xs]
    res = pl.pallas_call(
        body, name=name + "_bwd", grid=(n,), in_specs=in_specs, out_specs=out_specs, out_shape=out_shape,
        scratch_shapes=[pltpu.VMEM(s, f32) for s in carry_sds], compiler_params=_cparams(1),
    )(*params, *xs, *states, *dys)
    return list(res[:n_p]), list(res[n_p:])


def blocked_op(name, f, params, xs, out_sds, order="asc", carry_sds=(), rb=RB):
    carry_sds = tuple(carry_sds)

    @jax.custom_vjp
    def op(params, xs):
        return tuple(_blocked_fwd(name, f, order, rb, params, xs, out_sds, carry_sds)[0])

    def fwd(params, xs):
        ys, states = _blocked_fwd(name, f, order, rb, params, xs, out_sds, carry_sds)
        return tuple(ys), (params, xs, states)

    def bwd(res, dys):
        params, xs, states = res
        dp, dx = _blocked_bwd(name, f, order, rb, params, xs, states, list(dys), carry_sds)
        return list(dp), list(dx)

    op.defvjp(fwd, bwd)
    return op(list(params), list(xs))


def _cscan_call(name, order, asc, a, x, s=None):
    t = x.shape[0]
    n = t // RB
    blkshape = (RB,) + x.shape[1:]
    cshape = x.shape[1:]
    row = lambda i: (_blk(order, i, n), 0, 0, 0)
    xspec = pl.BlockSpec(blkshape, row)
    aspec = pl.BlockSpec(cshape, lambda i: (0, 0, 0))

    def rowidx(tt):
        return tt if asc else RB - 1 - tt

    if s is None:
        def body(a_ref, x_ref, s_ref, c_ref):
            i = pl.program_id(0)

            @pl.when(i == 0)
            def _():
                c_ref[...] = jnp.zeros_like(c_ref)
            ar = a_ref[0]
            ai = a_ref[1]

            def step(tt, carry):
                sr, si = carry
                r = rowidx(tt)
                nr = ar * sr - ai * si + x_ref[r, 0]
                ni = ar * si + ai * sr + x_ref[r, 1]
                s_ref[r, 0] = nr
                s_ref[r, 1] = ni
                return nr, ni
            sr, si = lax.fori_loop(0, RB, step, (c_ref[0], c_ref[1]), unroll=8)
            c_ref[0] = sr
            c_ref[1] = si

        return pl.pallas_call(
            body, name=name, grid=(n,), in_specs=[aspec, xspec], out_specs=xspec,
            out_shape=jax.ShapeDtypeStruct(x.shape, f32), scratch_shapes=[pltpu.VMEM(cshape, f32)],
            compiler_params=_cparams(1),
        )(a, x)

    def body(a_ref, x_ref, s_ref, g_ref, da_ref, c_ref):
        i = pl.program_id(0)

        @pl.when(i == 0)
        def _():
            c_ref[...] = jnp.zeros_like(c_ref)
            da_ref[...] = jnp.zeros_like(da_ref)
        ar = a_ref[0]
        ai = a_ref[1]

        def step(tt, carry):
            gr, gi, dar, dai = carry
            r = rowidx(tt)
            sr = s_ref[r, 0]
            si = s_ref[r, 1]
            dar = dar + gr * sr + gi * si
            dai = dai + gi * sr - gr * si
            nr = x_ref[r, 0] + ar * gr + ai * gi
            ni = x_ref[r, 1] + ar * gi - ai * gr
            g_ref[r, 0] = nr
            g_ref[r, 1] = ni
            return nr, ni, dar, dai
        z = jnp.zeros(cshape[1:], f32)
        gr, gi, dar, dai = lax.fori_loop(0, RB, step, (c_ref[0], c_ref[1], z, z), unroll=8)
        c_ref[0] = gr
        c_ref[1] = gi
        da_ref[0] += dar
        da_ref[1] += dai

    return pl.pallas_call(
        body, name=name, grid=(n,), in_specs=[aspec, xspec, xspec], out_specs=[xspec, aspec],
        out_shape=[jax.ShapeDtypeStruct(x.shape, f32), jax.ShapeDtypeStruct(cshape, f32)],
        scratch_shapes=[pltpu.VMEM(cshape, f32)], compiler_params=_cparams(1),
    )(a, x, s)


def cscan(name, d, a, x):
    order = "d1" if d else "asc"

    @jax.custom_vjp
    def op(a, x):
        return _cscan_call(name, order, d == 0, a, x)

    def fwd(a, x):
        s = op(a, x)
        return s, (a, s)

    def bwd(res, ds):
        a, s = res
        g, da = _cscan_call(name + "_bwd", _REV[order], d != 0, a, ds, s)
        return da, g

    op.defvjp(fwd, bwd)
    return op(a, x)


def _rscan_call(name, order, asc, a, x, hp=None):
    t = x.shape[0]
    n = t // RB
    cshape = x.shape[1:]
    xspec = pl.BlockSpec((RB,) + cshape, lambda i: (_blk(order, i, n), 0, 0))

    def rowidx(tt):
        return tt if asc else RB - 1 - tt

    if hp is None:
        def body(a_ref, x_ref, h_ref, hp_ref, c_ref):
            i = pl.program_id(0)

            @pl.when(i == 0)
            def _():
                c_ref[...] = jnp.zeros_like(c_ref)

            def step(tt, h):
                r = rowidx(tt)
                hp_ref[r] = h
                h = a_ref[r] * h + x_ref[r]
                h_ref[r] = h
                return h
            c_ref[...] = lax.fori_loop(0, RB, step, c_ref[...], unroll=8)

        return pl.pallas_call(
            body, name=name, grid=(n,), in_specs=[xspec, xspec], out_specs=[xspec, xspec],
            out_shape=[jax.ShapeDtypeStruct(x.shape, f32)] * 2, scratch_shapes=[pltpu.VMEM(cshape, f32)],
            compiler_params=_cparams(1),
        )(a, x)

    def body(a_ref, x_ref, hp_ref, da_ref, db_ref, c_ref):
        i = pl.program_id(0)

        @pl.when(i == 0)
        def _():
            c_ref[...] = jnp.zeros_like(c_ref)

        def step(tt, c):
            r = rowidx(tt)
            g = x_ref[r] + c
            db_ref[r] = g
            da_ref[r] = g * hp_ref[r]
            return a_ref[r] * g
        c_ref[...] = lax.fori_loop(0, RB, step, c_ref[...], unroll=8)

    return pl.pallas_call(
        body, name=name, grid=(n,), in_specs=[xspec, xspec, xspec], out_specs=[xspec, xspec],
        out_shape=[jax.ShapeDtypeStruct(x.shape, f32)] * 2, scratch_shapes=[pltpu.VMEM(cshape, f32)],
        compiler_params=_cparams(1),
    )(a, x, hp)


def rscan(name, d, a, x):
    order = "d1" if d else "asc"

    @jax.custom_vjp
    def op(a, x):
        return _rscan_call(name, order, d == 0, a, x)[0]

    def fwd(a, x):
        h, hp = _rscan_call(name, order, d == 0, a, x)
        return h, (a, hp)

    def bwd(res, dh):
        a, hp = res
        da, db = _rscan_call(name + "_bwd", _REV[order], d != 0, a, dh, hp)
        return da, db

    op.defvjp(fwd, bwd)
    return op(a, x)


def _mod_row(blk, mod, bm):
    return jnp.where(blk == 0, mod[1:2], mod[0:1]) + bm


def _f_silu(blk, p, x):
    return [_silu(x[0]).astype(bf16)]


def _f_normmod(blk, p, x):
    nw, mod, bm = p
    d = nw.shape[1]
    r = _mod_row(blk, mod, bm)
    return [(_rms(x[0], nw) * (1.0 + r[:, d:2 * d]) + r[:, :d]).astype(bf16)]


def _f_resid(blk, p, x):
    mod, bm = p
    d = x[0].shape[1]
    r = _mod_row(blk, mod, bm)
    return [x[0] + r[:, 2 * d:] * x[1]]


def _f_mix(blk, p, x):
    bg, = p
    gp = x[0]
    d = x[1].shape[1]
    acc = None
    for k in range(4):
        t = jax.nn.sigmoid(gp[:, k * d:(k + 1) * d] + bg[:, k * d:(k + 1) * d]) * x[1 + k]
        acc = t if acc is None else acc + t
    return [acc.astype(bf16)]


def _tri(rev):
    row = lax.broadcasted_iota(jnp.int32, (CHUNK, CHUNK), 0)
    col = lax.broadcasted_iota(jnp.int32, (CHUNK, CHUNK), 1)
    return (col >= row) if rev else (col <= row)


def _chunk_ids(rev):
    ids = list(range(RB // CHUNK))
    return ids[::-1] if rev else ids


def _f_hg(rev):
    def f(blk, p, c, x):
        lb, = p
        st, = c
        qi, fr = x
        q = _silu(qi[:, :BR_W])
        v = qi[:, BR_W:]
        fg = lb + (1.0 - lb) * jax.nn.sigmoid(fr)
        logf = jnp.log(fg)
        k = 1.0 - fg
        m = _tri(rev)
        mf = m.astype(f32)
        outs = [None] * (RB // CHUNK)
        for ci in _chunk_ids(rev):
            sl = slice(CHUNK * ci, CHUNK * ci + CHUNK)
            lf = logf[sl]
            b = _dot(mf, lf, hi=True)
            bend = jnp.sum(lf, axis=0, keepdims=True)
            mid = 0.5 * bend
            qe = q[sl] * jnp.exp(b - mid)
            ke = k[sl] * jnp.exp(mid - b)
            kd = k[sl] * jnp.exp(bend - b)
            qb = q[sl] * jnp.exp(b)
            dec = jnp.exp(bend)
            vc = v[sl]
            oh, ns = [], []
            for hh in range(HG_HEADS):
                cs = slice(HG_DK * hh, HG_DK * hh + HG_DK)
                sth = st[cs]
                att = jnp.where(m, _dot(qe[:, cs], ke[:, cs], _NT), 0.0)
                oh.append(_dot(att, vc[:, cs]) + _dot(qb[:, cs], sth, _NT))
                ns.append(sth * dec[:, cs] + _dot(vc[:, cs], kd[:, cs], _TN))
            st = jnp.concatenate(ns, axis=0)
            outs[ci] = jnp.concatenate(oh, axis=1)
        return [st], [jnp.concatenate(outs, axis=0)]
    return f


def _f_hg_final(blk, p, x):
    nw, = p
    o = x[0] + x[1]
    parts = []
    for hh in range(HG_HEADS):
        cs = slice(HG_DK * hh, HG_DK * hh + HG_DK)
        parts.append(_rms(o[:, cs], nw[:, cs]))
    return [(jnp.concatenate(parts, axis=1) * _silu(x[2])).astype(bf16)]


def _conv(x, cw, cb, blk):
    rows = x.shape[0]
    r = lax.broadcasted_iota(jnp.int32, (rows, 1), 0)
    rm = jnp.where(blk == 0, r, r % CHUNK)
    seg = jnp.where(blk == 0, rows, CHUNK)

    def vmask(o):
        return ((rm + o >= 0) & (rm + o < seg)).astype(f32)

    def shifted(o):
        @jax.custom_vjp
        def sh(x, mo, mn):
            return pltpu.roll(x, (-o) % rows, 0) * mo

        def fwd(x, mo, mn):
            return sh(x, mo, mn), (mo, mn)

        def bwd(res, g):
            mo, mn = res
            return pltpu.roll(g, o % rows, 0) * mn, jnp.zeros_like(mo), jnp.zeros_like(mn)
        sh.defvjp(fwd, bwd)
        return sh(x, vmask(o), vmask(-o))

    lo = (CONV_W - 1) // 2
    out = cb
    for k in range(CONV_W):
        o = k - lo
        out = out + cw[k:k + 1] * (x if o == 0 else shifted(o))
    return out


def _f_lru_a(blk, p, x):
    cw, cb, wg, gb, lam = p
    xc = _conv(x[0], cw, cb, blk)
    gates = jax.nn.sigmoid(_dot(xc, wg, hi=True) + gb)
    outs = []
    for d in range(2):
        r = gates[:, 2 * BR_W * d:2 * BR_W * d + BR_W]
        ig = gates[:, 2 * BR_W * d + BR_W:2 * BR_W * (d + 1)]
        log_a = -LRU_C * r * _softplus(-lam[d:d + 1])
        outs.append(jnp.exp(log_a))
        outs.append(jnp.sqrt(_one_minus_exp(2.0 * log_a)) * (ig * xc))
    return outs


def _f_lru_c(blk, p, x):
    return [((x[0] + x[1]) * _silu(x[2])).astype(bf16)]


def _f_s5_c1(blk, p, x):
    dsk, = p
    return [jax.nn.gelu(x[0] + x[1] + dsk * x[2])]


def _f_s5_c2(blk, p, x):
    bglu, = p
    return [(x[0] * jax.nn.sigmoid(x[1] + bglu) * _silu(x[2])).astype(bf16)]


def _f_m2_a(blk, p, x):
    cw, cb, dtb = p
    return [_silu(_conv(x[0], cw, cb, blk)), _softplus(x[1] + dtb)]


def _f_ssd(d):
    rev = d == 1
    hpg = M2_HEADS // M2_GROUPS

    def f(blk, p, c, x):
        alog, = p
        st, = c
        xbc, dtp = x
        a = -jnp.exp(alog[:, M2_HEADS * d:M2_HEADS * (d + 1)])
        dt = dtp[:, M2_HEADS * d:M2_HEADS * (d + 1)]
        xs = xbc[:, :BR_W]
        bm = xbc[:, BR_W:BR_W + M2_GROUPS * M2_STATE]
        cm = xbc[:, BR_W + M2_GROUPS * M2_STATE:]
        m = _tri(rev)
        mf = m.astype(f32)
        outs = [None] * (RB // CHUNK)
        for ci in _chunk_ids(rev):
            sl = slice(CHUNK * ci, CHUNK * ci + CHUNK)
            dtc = dt[sl]
            dta = dtc * a
            cum = _dot(mf, dta, hi=True)
            cend = jnp.sum(dta, axis=0, keepdims=True)
            cum_t = cum.T
            dt_t = dtc.T
            ys, ns = [], []
            for g in range(M2_GROUPS):
                bmg = bm[sl, M2_STATE * g:M2_STATE * (g + 1)]
                cmg = cm[sl, M2_STATE * g:M2_STATE * (g + 1)]
                scores = _dot(cmg, bmg, _NT)
                for r in range(hpg):
                    h = g * hpg + r
                    hs = slice(M2_HEADDIM * h, M2_HEADDIM * (h + 1))
                    ci_col = cum[:, h:h + 1]
                    decay = jnp.exp(jnp.where(m, ci_col - cum_t[h:h + 1, :], -1e30))
                    w = scores * decay * dt_t[h:h + 1, :]
                    xh = xs[sl, hs]
                    sth = st[hs]
                    ys.append(_dot(w, xh) + _dot(cmg, sth, _NT) * jnp.exp(ci_col))
                    wx = (jnp.exp(cend[:, h:h + 1] - ci_col) * dtc[:, h:h + 1]) * xh
                    ns.append(jnp.exp(cend[:, h:h + 1]) * sth + _dot(wx, bmg, _TN))
            st = jnp.concatenate(ns, axis=0)
            outs[ci] = jnp.concatenate(ys, axis=1)
        return [st], [jnp.concatenate(outs, axis=0)]
    return f


def _f_m2_c(blk, p, x):
    dsk, nw = p
    y = x[0] + x[1] + dsk * x[2][:, :BR_W]
    return [_rms(y * _silu(x[3]), nw).astype(bf16)]


def _f_loss(blk, p, x):
    fnw, = p
    err = _rms(x[0], fnw) - x[1]
    return [0.5 * jnp.mean(err * err, axis=-1, keepdims=True)]


def _blockdiag(w):
    g, a, b = w.shape
    return jnp.einsum("gab,gh->gahb", w, jnp.eye(g, dtype=w.dtype)).reshape(g * a, g * b)


def _s5_params(l, w):
    a_scan, cds = [], []
    b_re = jnp.transpose(w["s5_b_re"][l], (0, 2, 1))
    b_im = jnp.transpose(w["s5_b_im"][l], (0, 2, 1))
    bd = jnp.concatenate([_blockdiag(b_re), _blockdiag(b_im)], axis=1)
    c_re = jnp.transpose(w["s5_c_re"][l], (0, 2, 1))
    c_im = jnp.transpose(w["s5_c_im"][l], (0, 2, 1))
    for d in range(2):
        lam_re = w["s5_a_re"][l, d]
        lam_im = w["s5_a_im"][l, d]
        step = jnp.exp(w["s5_log_step"][l, d])[:, None]
        mag = jnp.exp(lam_re * step)
        ab_re = mag * jnp.cos(lam_im * step)
        ab_im = mag * jnp.sin(lam_im * step)
        den = lam_re * lam_re + lam_im * lam_im
        nr = ab_re - 1.0
        co_re = (nr * lam_re + ab_im * lam_im) / den
        co_im = (ab_im * lam_re - nr * lam_im) / den
        n_state = S5_GROUPS * S5_STATE
        a_scan.append(jnp.stack([ab_re.reshape(_SUB, n_state // _SUB), ab_im.reshape(_SUB, n_state // _SUB)]))
        cp_re = c_re * co_re[:, :, None] - c_im * co_im[:, :, None]
        cp_im = c_re * co_im[:, :, None] + c_im * co_re[:, :, None]
        cds.append(jnp.concatenate([_blockdiag(cp_re), -_blockdiag(cp_im)], axis=0))
    return a_scan, bd, cds


def _lru_gate(l, w):
    gw = w["lru_gate_w"][l]
    cols = [_blockdiag(gw[d, g]) for d in range(2) for g in range(2)]
    return jnp.concatenate(cols, axis=1), w["lru_gate_b"][l].reshape(1, -1)


def _pad_cols(a, n):
    return jnp.pad(a, ((0, 0), (0, n - a.shape[1])))


IN_SIZES = (BR_W,) * 9 + (M2_XBC, 2 * M2_HEADS, BR_W)
IN_OFFS = tuple(sum(IN_SIZES[:i]) for i in range(len(IN_SIZES) + 1))
IN_GROUPS = (("hg_qi", 0, 2, 1024), ("hg_ff", 2, 1, 512), ("hg_fb", 3, 1, 512), ("hg_z", 4, 1, 512),
             ("s5_u", 5, 1, 512), ("s5_z", 6, 1, 512), ("lru_x", 7, 1, 512), ("lru_z", 8, 1, 512),
             ("m2_xbc", 9, 1, 768), ("m2_dt", 10, 1, 128), ("m2_z", 11, 1, 512))


def _new_slots(big):
    slots = {n: jnp.zeros(w.shape, f32) for n, w in big.items() if n not in ("w_in", "w_gate")}
    n_layers, d_model = big["w_in"].shape[:2]
    slots["w_in"] = [{name: jnp.zeros((d_model, width), f32) for name, _, _, width in IN_GROUPS} for _ in range(n_layers)]
    slots["w_gate"] = jnp.zeros((n_layers, d_model, 4 * d_model), f32)
    return slots


def _slot_grads(g):
    out = dict(g)
    out["w_in"] = jnp.stack([
        jnp.concatenate([gl[name][:, :IN_OFFS[s0 + ns] - IN_OFFS[s0]] for name, s0, ns, _ in IN_GROUPS], axis=1)
        for gl in g["w_in"]])
    n_layers, d_model = g["w_gate"].shape[:2]
    out["w_gate"] = jnp.transpose(g["w_gate"].reshape(n_layers, d_model, 4, d_model), (0, 2, 1, 3))
    return out


def _forward(p, big, slots, x, ctx, c, target):
    n_layers = p["norm_w"].shape[0]
    d_model = x.shape[-1]
    xa = jnp.concatenate([ctx, x], axis=0)
    t = xa.shape[0]
    cc = jnp.concatenate([c, p["c_ctx"][None], jnp.zeros((_SUB - 2, d_model), f32)], axis=0)
    lb_all = jnp.cumsum(jax.nn.softmax(p["hg_lb_logits"], axis=0), axis=0)
    scc, = blocked_op("silu_c", _f_silu, [], [cc], [(d_model, bf16)], rb=_SUB)

    for l in range(n_layers):
        tag = "l%d_" % l
        mod = mm(tag + "mod", scc, big["w_mod"][l], slots["w_mod"][l])
        bm = p["b_mod"][l][None]
        h, = blocked_op(tag + "normmod", _f_normmod, [p["norm_w"][l][None], mod, bm], [xa], [(d_model, bf16)])
        u = {}
        for name, s0, ns, width in IN_GROUPS:
            wv = _pad_cols(big["w_in"][l][:, IN_OFFS[s0]:IN_OFFS[s0 + ns]], width)
            u[name] = mm(tag + "in_" + name, h, wv, slots["w_in"][l][name])

        o_dirs = []
        for d, fname in ((0, "hg_ff"), (1, "hg_fb")):
            o, = blocked_op(tag + "hg%d" % d, _f_hg(d == 1), [lb_all[l, d][None]], [u["hg_qi"], u[fname]],
                            [(BR_W, f32)], order="d1" if d else "asc", carry_sds=[(BR_W, HG_DK)])
            o_dirs.append(o)
        y_hg, = blocked_op(tag + "hg_fin", _f_hg_final, [p["hg_norm"][l][None]], o_dirs + [u["hg_z"]], [(BR_W, bf16)])

        a_scan, bd, cds = _s5_params(l, p)
        n_state = S5_GROUPS * S5_STATE
        bu = mm(tag + "s5_bu", u["s5_u"], bd, hi=True).reshape(t, 2, _SUB, n_state // _SUB)
        ysum = []
        for d in range(2):
            s = cscan(tag + "s5_scan%d" % d, d, a_scan[d], bu).reshape(t, 2 * n_state)
            ysum.append(mm(tag + "s5_c%d" % d, s, cds[d], hi=True))
        g5, = blocked_op(tag + "s5_c1", _f_s5_c1, [p["s5_d"][l][None]], ysum + [u["s5_u"]], [(BR_W, f32)])
        gl = mm(tag + "s5_glu", g5, big["s5_w_glu"][l], slots["s5_w_glu"][l])
        y_s5, = blocked_op(tag + "s5_c2", _f_s5_c2, [p["s5_b_glu"][l][None]], [g5, gl, u["s5_z"]], [(BR_W, bf16)])

        wg, gb = _lru_gate(l, p)
        ab = blocked_op(tag + "lru_a", _f_lru_a,
                        [p["lru_conv_w"][l], p["lru_conv_b"][l][None], wg, gb, p["lru_lam"][l]],
                        [u["lru_x"]], [(BR_W, f32)] * 4)
        hs = []
        for d in range(2):
            a3 = ab[2 * d].reshape(t, 4, BR_W // 4)
            b3 = ab[2 * d + 1].reshape(t, 4, BR_W // 4)
            hs.append(rscan(tag + "lru_scan%d" % d, d, a3, b3).reshape(t, BR_W))
        y_lru, = blocked_op(tag + "lru_c", _f_lru_c, [], hs + [u["lru_z"]], [(BR_W, bf16)])

        dtb = _pad_cols(p["m2_dt_bias"][l].reshape(1, -1), _LANE)
        xbc, dtp = blocked_op(tag + "m2_a", _f_m2_a, [p["m2_conv_w"][l], p["m2_conv_b"][l][None], dtb],
                              [u["m2_xbc"], u["m2_dt"]], [(M2_XBC, f32), (_LANE, f32)])
        alog = _pad_cols(p["m2_a_log"][l].reshape(1, -1), _LANE)
        y_dirs = []
        for d in range(2):
            y, = blocked_op(tag + "ssd%d" % d, _f_ssd(d), [alog], [xbc, dtp], [(BR_W, f32)],
                            order="d1" if d else "asc", carry_sds=[(BR_W, M2_STATE)])
            y_dirs.append(y)
        dsk = jnp.repeat(p["m2_d"][l], M2_HEADDIM)[None]
        y_m2, = blocked_op(tag + "m2_c", _f_m2_c, [dsk, p["m2_norm"][l][None]], y_dirs + [xbc, u["m2_z"]], [(BR_W, bf16)])

        wg_all = jnp.transpose(big["w_gate"][l], (1, 0, 2)).reshape(d_model, 4 * d_model)
        gp = mm(tag + "gate", h, wg_all, slots["w_gate"][l])
        bs = [mm(tag + "br%d" % k, yk, big["w_branch"][l, k], slots["w_branch"][l, k])
              for k, yk in enumerate((y_hg, y_s5, y_lru, y_m2))]
        mix, = blocked_op(tag + "mix", _f_mix, [p["b_gate"][l].reshape(1, -1)], [gp] + bs, [(d_model, bf16)])
        o = mm(tag + "out", mix, big["w_out"][l], slots["w_out"][l])
        xa, = blocked_op(tag + "resid", _f_resid, [mod, bm], [xa, o], [(d_model, f32)])

    rl, = blocked_op("loss", _f_loss, [p["final_norm"][None]], [xa[ctx.shape[0]:], target], [(1, f32)])
    return jnp.sum(rl)


_MESH = pl.DeviceIdType.MESH
_ANY = pl.BlockSpec(memory_space=pl.ANY)
W_PACK = 1024


def _place():
    x, y, c = lax.axis_index("x"), lax.axis_index("y"), lax.axis_index("c")
    chips = [(x, 1 - y), (1 - x, y), (1 - x, 1 - y)]
    return x, y, c, chips


def _rcopy(src, dst, ssem, rsem, k, to):
    return pltpu.make_async_remote_copy(src_ref=src, dst_ref=dst, send_sem=ssem.at[k], recv_sem=rsem.at[k],
                                        device_id=to, device_id_type=_MESH)


def gather_shards(xs):
    n = len(xs)

    def body(*refs):
        x_refs, o_refs = refs[:n], refs[n:2 * n]
        ssem, rsem, lsem = refs[2 * n:]
        x, y, c, chips = _place()
        j = 2 * x + y
        sib = (x, y, 1 - c)
        mine = [pltpu.make_async_copy(x_refs[a], o_refs[a].at[j], lsem.at[a]) for a in range(n)]
        for cp in mine:
            cp.start()
        first = [_rcopy(x_refs[a].at[c], o_refs[a].at[j, c], ssem, rsem, 6 * a + r, (*chips[r], c))
                 for r in range(3) for a in range(n)]
        for cp in first:
            cp.start()
        passed = []
        for r in range(3):
            jr = j ^ (r + 1)
            for a in range(n):
                _rcopy(x_refs[a].at[c], o_refs[a].at[jr, c], ssem, rsem, 6 * a + r, sib).wait_recv()
                cp = _rcopy(o_refs[a].at[jr, c], o_refs[a].at[jr, c], ssem, rsem, 6 * a + 3 + r, sib)
                cp.start()
                passed.append(cp)
        for r in range(3):
            jr = j ^ (r + 1)
            for a in range(n):
                _rcopy(x_refs[a].at[c], o_refs[a].at[jr, 1 - c], ssem, rsem, 6 * a + 3 + r, sib).wait_recv()
        for cp in first + passed:
            cp.wait_send()
        for cp in mine:
            cp.wait()

    return pl.pallas_call(
        body, name="gather_shards", out_shape=[jax.ShapeDtypeStruct((4,) + x.shape, x.dtype) for x in xs],
        in_specs=[_ANY] * n, out_specs=[_ANY] * n,
        scratch_shapes=[pltpu.SemaphoreType.DMA((6 * n,)), pltpu.SemaphoreType.DMA((6 * n,)), pltpu.SemaphoreType.DMA((n,))],
    )(*xs)


def sibling_halves(gs):
    n = len(gs)

    def body(*refs):
        g_refs, o_refs = refs[:n], refs[n:2 * n]
        ssem, rsem = refs[2 * n:]
        x, y, c, _ = _place()
        sib = (x, y, 1 - c)
        cps = [_rcopy(g_refs[a].at[k, 1 - c], o_refs[a].at[k], ssem, rsem, 4 * a + k, sib)
               for k in range(4) for a in range(n)]
        for cp in cps:
            cp.start()
        for cp in cps:
            cp.wait()

    return pl.pallas_call(
        body, name="sibling_halves", out_shape=[jax.ShapeDtypeStruct((4,) + g.shape[2:], g.dtype) for g in gs],
        in_specs=[_ANY] * n, out_specs=[_ANY] * n,
        scratch_shapes=[pltpu.SemaphoreType.DMA((4 * n,)), pltpu.SemaphoreType.DMA((4 * n,))],
    )(*gs)


def scatter_chips(ps):
    n = len(ps)

    def body(*refs):
        p_refs, o_refs = refs[:n], refs[n:2 * n]
        ssem, rsem = refs[2 * n:]
        x, y, c, chips = _place()
        j = 2 * x + y
        cps = [_rcopy(p_refs[a].at[j ^ (r + 1)], o_refs[a].at[r], ssem, rsem, 3 * a + r, (*chips[r], c))
               for r in range(3) for a in range(n)]
        for cp in cps:
            cp.start()
        for cp in cps:
            cp.wait()

    return pl.pallas_call(
        body, name="scatter_chips", out_shape=[jax.ShapeDtypeStruct((3,) + p.shape[1:], p.dtype) for p in ps],
        in_specs=[_ANY] * n, out_specs=[_ANY] * n,
        scratch_shapes=[pltpu.SemaphoreType.DMA((3 * n,)), pltpu.SemaphoreType.DMA((3 * n,))],
    )(*ps)


def join_halves(qs):
    n = len(qs)

    def body(*refs):
        q_refs, o_refs = refs[:n], refs[n:2 * n]
        ssem, rsem, lsem = refs[2 * n:]
        x, y, c, _ = _place()
        sib = (x, y, 1 - c)
        mine = [pltpu.make_async_copy(q_refs[a], o_refs[a].at[c], lsem.at[a]) for a in range(n)]
        for cp in mine:
            cp.start()
        cps = [_rcopy(q_refs[a], o_refs[a].at[c], ssem, rsem, a, sib) for a in range(n)]
        for cp in cps:
            cp.start()
        for a in range(n):
            _rcopy(q_refs[a], o_refs[a].at[1 - c], ssem, rsem, a, sib).wait_recv()
        for cp in cps:
            cp.wait_send()
        for cp in mine:
            cp.wait()

    return pl.pallas_call(
        body, name="join_halves", out_shape=[jax.ShapeDtypeStruct((2,) + q.shape, q.dtype) for q in qs],
        in_specs=[_ANY] * n, out_specs=[_ANY] * n,
        scratch_shapes=[pltpu.SemaphoreType.DMA((n,)), pltpu.SemaphoreType.DMA((n,)), pltpu.SemaphoreType.DMA((n,))],
    )(*qs)


def _rows_block(r):
    return _pick(r, 256, _SUB)


def add_sibling(tag, g, r1, place):
    _, _, rows, w = g.shape
    rb = _rows_block(rows)

    def body(pl_ref, g_ref, r_ref, o_ref):
        o_ref[...] = g_ref[0] + r_ref[...]

    return pl.pallas_call(
        body, name="add_sibling_" + tag, out_shape=jax.ShapeDtypeStruct((4, rows, w), f32),
        grid_spec=pltpu.PrefetchScalarGridSpec(
            num_scalar_prefetch=1, grid=(4, rows // rb),
            in_specs=[pl.BlockSpec((1, 1, rb, w), lambda k, i, s: (k, s[1], i, 0)),
                      pl.BlockSpec((1, rb, w), lambda k, i, s: (k, i, 0))],
            out_specs=pl.BlockSpec((1, rb, w), lambda k, i, s: (k, i, 0))),
        compiler_params=_cparams(2),
    )(place, g, r1)


def add_chips(tag, p, r2, place):
    _, rows, w = p.shape
    rb = _rows_block(rows)

    def body(pl_ref, p_ref, r_ref, o_ref):
        j = pl_ref[0]
        own = p_ref[0]
        others = [r_ref[0], r_ref[1], r_ref[2]]
        acc = None
        for k in range(4):
            rel = k ^ j
            t = jnp.where(rel == 0, own, jnp.where(rel == 1, others[0], jnp.where(rel == 2, others[1], others[2])))
            acc = t if acc is None else acc + t
        o_ref[...] = acc

    return pl.pallas_call(
        body, name="add_chips_" + tag, out_shape=jax.ShapeDtypeStruct((rows, w), f32),
        grid_spec=pltpu.PrefetchScalarGridSpec(
            num_scalar_prefetch=1, grid=(rows // rb,),
            in_specs=[pl.BlockSpec((1, rb, w), lambda i, s: (s[0], i, 0)),
                      pl.BlockSpec((3, rb, w), lambda i, s: (0, i, 0))],
            out_specs=pl.BlockSpec((rb, w), lambda i, s: (i, 0))),
        compiler_params=_cparams(1),
    )(place, p, r2)


def adamw(tag, g, w, m, v):
    rows, wd = g.shape
    rb = _rows_block(rows)

    def body(g_ref, w_ref, m_ref, v_ref, d_ref, nm_ref, nv_ref):
        gv = g_ref[...]
        nm = ADAM_B1 * m_ref[...] + (1.0 - ADAM_B1) * gv
        nv = ADAM_B2 * v_ref[...] + (1.0 - ADAM_B2) * (gv * gv)
        m_hat = nm / (1.0 - ADAM_B1 ** ADAM_STEP)
        v_hat = nv / (1.0 - ADAM_B2 ** ADAM_STEP)
        d_ref[...] = -ADAM_LR * (m_hat / (jnp.sqrt(v_hat) + ADAM_EPS) + ADAM_WD * w_ref[...])
        nm_ref[...] = nm
        nv_ref[...] = nv

    spec = pl.BlockSpec((rb, wd), lambda i: (i, 0))
    return pl.pallas_call(
        body, name="adamw_" + tag, grid=(rows // rb,), in_specs=[spec] * 4, out_specs=[spec] * 3,
        out_shape=[jax.ShapeDtypeStruct(g.shape, f32)] * 3, compiler_params=_cparams(1),
    )(g, w, m, v)


WEIGHTS = ("c_ctx", "norm_w", "w_mod", "b_mod", "w_in", "hg_lb_logits", "hg_norm", "s5_a_re", "s5_a_im", "s5_log_step",
           "s5_b_re", "s5_b_im", "s5_c_re", "s5_c_im", "s5_d", "s5_w_glu", "s5_b_glu", "lru_conv_w", "lru_conv_b",
           "lru_gate_w", "lru_gate_b", "lru_lam", "m2_conv_w", "m2_conv_b", "m2_dt_bias", "m2_a_log", "m2_d", "m2_norm",
           "w_branch", "w_gate", "b_gate", "w_out", "final_norm")
SHARD_AXIS = {"w_mod": 2, "w_in": 2, "hg_lb_logits": 2, "s5_w_glu": 1, "lru_conv_w": 2, "lru_lam": 2, "m2_conv_w": 2,
              "w_branch": 3, "w_gate": 2, "b_gate": 2, "w_out": 1}
BIG = ("w_mod", "w_in", "s5_w_glu", "w_branch", "w_gate", "w_out")
N_CHIPS = 4


def _to_rows(flat, row_unit):
    n = flat.shape[-1]
    per = 2 * row_unit * W_PACK
    total = -(-n // per) * per
    flat = jnp.pad(flat, [(0, 0)] * (flat.ndim - 1) + [(0, total - n)])
    return flat.reshape(flat.shape[:-1] + (2, total // (2 * W_PACK), W_PACK))


SMALL_SHARDED = tuple(n for n in WEIGHTS if n in SHARD_AXIS and n not in BIG)
SMALL_REPLICATED = tuple(n for n in WEIGHTS if n not in SHARD_AXIS)


def _chip_slices(a, axis):
    width = a.shape[axis] // N_CHIPS
    return jnp.stack([lax.slice_in_dim(a, k * width, (k + 1) * width, axis=axis) for k in range(N_CHIPS)])


def _gather_weights(local):
    small = jnp.concatenate([lax.bitcast_convert_type(local[n], bf16).reshape(-1) for n in SMALL_SHARDED])
    got = gather_shards([local[n].astype(bf16) for n in BIG] + [_to_rows(small, 16)])
    full = {}
    for n, g in zip(BIG, got):
        full[n] = jnp.concatenate([g[j] for j in range(N_CHIPS)], axis=SHARD_AXIS[n])
    flat, off = got[-1].reshape(N_CHIPS, -1), 0
    for n in SMALL_SHARDED:
        shp = local[n].shape
        size = 2 * math.prod(shp)
        part = lax.bitcast_convert_type(flat[:, off:off + size].reshape((N_CHIPS,) + shp + (2,)), f32)
        off += size
        full[n] = jnp.concatenate([part[j] for j in range(N_CHIPS)], axis=SHARD_AXIS[n])
    return full


def _pack_small(vals, extra):
    return jnp.concatenate([vals[n].reshape(-1) for n in SMALL_SHARDED + SMALL_REPLICATED] + [extra.reshape(1)])


def _pack_small_grads(grads, loss):
    rep = [grads[n].reshape(-1) for n in SMALL_REPLICATED] + [loss.reshape(1)]
    sh = [_chip_slices(grads[n], SHARD_AXIS[n]).reshape(N_CHIPS, -1) for n in SMALL_SHARDED]
    return jnp.concatenate(sh + [jnp.broadcast_to(r, (N_CHIPS,) + r.shape) for r in rep], axis=1)


def _unpack_small(flat, like):
    out, off = {}, 0
    for n in SMALL_SHARDED + SMALL_REPLICATED:
        size = math.prod(like[n].shape)
        out[n] = flat[off:off + size].reshape(like[n].shape)
        off += size
    return out, flat[off]


def _reduce_grads(tags, gs):
    place = jnp.stack([2 * lax.axis_index("x") + lax.axis_index("y"), lax.axis_index("c")]).astype(jnp.int32)
    pairs = [add_sibling(t, g, r, place) for t, g, r in zip(tags, gs, sibling_halves(gs))]
    quads = [add_chips(t, p, r, place) for t, p, r in zip(tags, pairs, scatter_chips(pairs))]
    return join_halves(quads)


def kernel(x, c, ctx, c_ctx, norm_w, w_mod, b_mod, w_in, hg_lb_logits, hg_norm, s5_a_re, s5_a_im, s5_log_step, s5_b_re, s5_b_im, s5_c_re, s5_c_im, s5_d, s5_w_glu, s5_b_glu, lru_conv_w, lru_conv_b, lru_gate_w, lru_gate_b, lru_lam, m2_conv_w, m2_conv_b, m2_dt_bias, m2_a_log, m2_d, m2_norm, w_branch, w_gate, b_gate, w_out, final_norm, loss_target, m_c_ctx, m_norm_w, m_w_mod, m_b_mod, m_w_in, m_hg_lb_logits, m_hg_norm, m_s5_a_re, m_s5_a_im, m_s5_log_step, m_s5_b_re, m_s5_b_im, m_s5_c_re, m_s5_c_im, m_s5_d, m_s5_w_glu, m_s5_b_glu, m_lru_conv_w, m_lru_conv_b, m_lru_gate_w, m_lru_gate_b, m_lru_lam, m_m2_conv_w, m_m2_conv_b, m_m2_dt_bias, m_m2_a_log, m_m2_d, m_m2_norm, m_w_branch, m_w_gate, m_b_gate, m_w_out, m_final_norm, v_c_ctx, v_norm_w, v_w_mod, v_b_mod, v_w_in, v_hg_lb_logits, v_hg_norm, v_s5_a_re, v_s5_a_im, v_s5_log_step, v_s5_b_re, v_s5_b_im, v_s5_c_re, v_s5_c_im, v_s5_d, v_s5_w_glu, v_s5_b_glu, v_lru_conv_w, v_lru_conv_b, v_lru_gate_w, v_lru_gate_b, v_lru_lam, v_m2_conv_w, v_m2_conv_b, v_m2_dt_bias, v_m2_a_log, v_m2_d, v_m2_norm, v_w_branch, v_w_gate, v_b_gate, v_w_out, v_final_norm):
    given = dict(locals())
    w_loc = {n: given[n] for n in WEIGHTS}
    m_loc = {n: given["m_" + n] for n in WEIGHTS}
    v_loc = {n: given["v_" + n] for n in WEIGHTS}

    full = _gather_weights(w_loc)
    params = {n: (full[n] if n in SHARD_AXIS else w_loc[n]) for n in WEIGHTS if n not in BIG}
    big = {n: full[n] for n in BIG}
    def loss_fn(p, s, xx):
        return _forward(p, big, s, xx, ctx[0], c, loss_target[0])

    loss, (g_p, g_s, g_x) = jax.value_and_grad(loss_fn, argnums=(0, 1, 2))(params, _new_slots(big), x[0])
    grads = {**g_p, **_slot_grads(g_s)}

    def rows4(a):
        return a.reshape(a.shape[:2] + (-1, a.shape[-1]))

    g_big = [rows4(_chip_slices(grads[n], SHARD_AXIS[n])) for n in BIG]
    g_small = _to_rows(_pack_small_grads(grads, loss), 64)
    summed = _reduce_grads(list(BIG) + ["small"], g_big + [g_small])

    g_out, d_out, m_out, v_out = {}, {}, {}, {}
    for n, g in zip(BIG, summed):
        shp = w_loc[n].shape
        flat2 = lambda a: a.reshape(-1, shp[-1])
        g_out[n] = g.reshape(shp)
        d, nm, nv = adamw(n, flat2(g), flat2(w_loc[n]), flat2(m_loc[n]), flat2(v_loc[n]))
        d_out[n], m_out[n], v_out[n] = d.reshape(shp), nm.reshape(shp), nv.reshape(shp)
    zero = jnp.zeros((), f32)
    flat = lambda vals: _to_rows(_pack_small(vals, zero), 64).reshape(-1, W_PACK)
    gs = summed[-1].reshape(-1, W_PACK)
    d, nm, nv = adamw("small", gs, flat(w_loc), flat(m_loc), flat(v_loc))
    gsm, loss_out = _unpack_small(gs.reshape(-1), w_loc)
    g_out.update(gsm)
    d_out.update(_unpack_small(d.reshape(-1), w_loc)[0])
    m_out.update(_unpack_small(nm.reshape(-1), w_loc)[0])
    v_out.update(_unpack_small(nv.reshape(-1), w_loc)[0])
    outs = [loss_out, g_x[None]]
    for group in (g_out, d_out, m_out, v_out):
        outs += [group[n] for n in WEIGHTS]
    return tuple(outs)
```

```python
import functools
import math

import jax
import jax.numpy as jnp
from jax import lax
from jax.experimental import pallas as pl
from jax.experimental.pallas import tpu as pltpu

f32 = jnp.float32
bf16 = jnp.bfloat16
_MM_DTYPE = bf16
_HI = lax.Precision.HIGHEST
_VMEM_LIMIT = 56 * 1024 * 1024
_LANE = 128
_SUB = 8

EPS = 1e-6
CONV_W = 4
CHUNK = 64
RB = 256
BR_W = 512
HG_HEADS = 4
HG_DK = 128
S5_GROUPS = 32
S5_GROUP = 16
S5_STATE = 64
LRU_BLOCKS = 8
LRU_C = 8.0
M2_HEADS = 8
M2_HEADDIM = 64
M2_GROUPS = 2
M2_STATE = 64
M2_XBC = BR_W + 2 * M2_GROUPS * M2_STATE
ADAM_LR = 0.001
ADAM_B1 = 0.9
ADAM_B2 = 0.999
ADAM_EPS = 1e-08
ADAM_WD = 0.01
ADAM_STEP = 10

_NN = (((1,), (0,)), ((), ()))
_NT = (((1,), (1,)), ((), ()))
_TN = (((0,), (0,)), ((), ()))


def _silu(x):
    return x * jax.nn.sigmoid(x)


def _softplus(x):
    return jnp.maximum(x, 0.0) + jnp.log1p(jnp.exp(-jnp.abs(x)))


def _one_minus_exp(z):
    series = -z * (1.0 + z * 0.5 * (1.0 + z * (1.0 / 3.0) * (1.0 + z * 0.25 * (1.0 + z * 0.2))))
    return jnp.where(z > -0.05, series, 1.0 - jnp.exp(z))


def _rms(x, w):
    return x * lax.rsqrt(jnp.mean(x * x, axis=-1, keepdims=True) + EPS) * w


def _dot(a, b, dn=_NN, hi=False):
    return lax.dot_general(a, b, dn, precision=_HI if hi else None, preferred_element_type=f32)


def _cparams(n_grid):
    return pltpu.CompilerParams(dimension_semantics=("arbitrary",) * n_grid, vmem_limit_bytes=_VMEM_LIMIT)


_REV = {"asc": "desc", "d1": "d1r", "desc": "asc", "d1r": "d1"}


def _blk(order, i, n):
    if order == "asc":
        return i
    if order == "desc":
        return n - 1 - i
    if order == "d1":
        return jnp.where(i == 0, 0, n - i)
    return jnp.where(i == n - 1, 0, i + 1)


def _pick(n, cap, unit):
    if n <= cap:
        return n
    best = None
    d = unit
    while d <= cap:
        if n % d == 0:
            best = d
        d += unit
    return n if best is None else best


def _mm_call(name, a, b, mode, hi, out_dtype):
    if mode == "tn":
        k, m = a.shape
        n = b.shape[1]
        tm = _pick(m, 256, _LANE)
        tn = _pick(n, 512, _LANE)
        a_spec = pl.BlockSpec((k, tm), lambda i, j: (0, i))
        b_spec = pl.BlockSpec((k, tn), lambda i, j: (0, j))
    else:
        m, k = a.shape
        tm = _pick(m, 256, _SUB)
        a_spec = pl.BlockSpec((tm, k), lambda i, j: (i, 0))
        if mode == "nn":
            n = b.shape[1]
            tn = _pick(n, max(_LANE, (4 * 1024 * 1024 // (k * 4)) // _LANE * _LANE), _LANE)
            b_spec = pl.BlockSpec((k, tn), lambda i, j: (0, j))
        else:
            n = b.shape[0]
            tn = _pick(n, max(_LANE, (4 * 1024 * 1024 // (k * 4)) // _LANE * _LANE), _LANE)
            b_spec = pl.BlockSpec((tn, k), lambda i, j: (j, 0))
    dn = {"nn": _NN, "nt": _NT, "tn": _TN}[mode]

    def body(a_ref, b_ref, o_ref):
        av = a_ref[...]
        bv = b_ref[...]
        if hi:
            av = av.astype(f32)
            bv = bv.astype(f32)
        else:
            av = av.astype(_MM_DTYPE)
            bv = bv.astype(_MM_DTYPE)
        o_ref[...] = _dot(av, bv, dn, hi).astype(o_ref.dtype)

    return pl.pallas_call(
        body, name=name, grid=(m // tm, n // tn), in_specs=[a_spec, b_spec],
        out_specs=pl.BlockSpec((tm, tn), lambda i, j: (i, j)),
        out_shape=jax.ShapeDtypeStruct((m, n), out_dtype), compiler_params=_cparams(2),
    )(a, b)


def mm(name, a, b, slot=None, hi=False):
    @jax.custom_vjp
    def op(a, b, slot):
        return _mm_call(name, a, b, "nn", hi, f32)

    def fwd(a, b, slot):
        return op(a, b, slot), (a, b)

    def bwd(res, g):
        a, b = res
        da = _mm_call(name + "_da", g, b, "nt", hi, a.dtype)
        db = _mm_call(name + "_db", a, g, "tn", hi, f32)
        if slot is None:
            return da, db.astype(b.dtype), None
        return da, jnp.zeros_like(b), db

    op.defvjp(fwd, bwd)
    return op(a, b, slot)


def _sum_nt_call(name, gs, ws, out_dtype):
    m = gs[0].shape[0]
    kdim = ws[0].shape[0]
    tm = _pick(m, 256, _SUB)
    n = len(gs)

    def body(*refs):
        acc = None
        for g_ref, w_ref in zip(refs[:n], refs[n:2 * n]):
            part = _dot(g_ref[...].astype(_MM_DTYPE), w_ref[...].astype(_MM_DTYPE), _NT)
            acc = part if acc is None else acc + part
        refs[2 * n][...] = acc.astype(out_dtype)

    in_specs = [pl.BlockSpec((tm, g.shape[1]), lambda i: (i, 0)) for g in gs]
    in_specs += [pl.BlockSpec(w.shape, lambda i: (0, 0)) for w in ws]
    return pl.pallas_call(
        body, name=name, grid=(m // tm,), in_specs=in_specs, out_specs=pl.BlockSpec((tm, kdim), lambda i: (i, 0)),
        out_shape=jax.ShapeDtypeStruct((m, kdim), out_dtype), compiler_params=_cparams(1),
    )(*gs, *ws)


def multi_mm(tag, names, a, ws, slots):
    @jax.custom_vjp
    def op(a, ws, slots):
        return tuple(_mm_call(tag + n, a, w, "nn", False, f32) for n, w in zip(names, ws))

    def fwd(a, ws, slots):
        return op(a, ws, slots), (a, ws)

    def bwd(res, gs):
        a, ws = res
        dws = [_mm_call(tag + n + "_db", a, g, "tn", False, f32) for n, g in zip(names, gs)]
        da = _sum_nt_call(tag + "da", list(gs), ws, a.dtype)
        return da, [jnp.zeros_like(w) for w in ws], dws

    op.defvjp(fwd, bwd)
    return op(a, list(ws), list(slots))


def _blocked_fwd(name, f, order, rb, params, xs, out_sds, carry_sds):
    t = xs[0].shape[0]
    n = t // rb
    n_p, n_x, n_o, n_c = len(params), len(xs), len(out_sds), len(carry_sds)

    def body(*refs):
        p_refs = refs[:n_p]
        x_refs = refs[n_p:n_p + n_x]
        o_refs = refs[n_p + n_x:n_p + n_x + n_o]
        st_refs = refs[n_p + n_x + n_o:n_p + n_x + n_o + n_c]
        c_refs = refs[n_p + n_x + n_o + n_c:]
        i = pl.program_id(0)
        blk = _blk(order, i, n)
        p = [r[...] for r in p_refs]
        x = [r[...] for r in x_refs]
        if n_c:
            @pl.when(i == 0)
            def _():
                for c in c_refs:
                    c[...] = jnp.zeros_like(c)
            c_in = [c[...] for c in c_refs]
            for sr, c in zip(st_refs, c_in):
                sr[0] = c
            c_out, ys = f(blk, p, c_in, x)
            for c, v in zip(c_refs, c_out):
                c[...] = v
        else:
            ys = f(blk, p, x)
        for o, y in zip(o_refs, ys):
            o[...] = y.astype(o.dtype)

    row = lambda i: (_blk(order, i, n), 0)
    in_specs = [pl.BlockSpec(p.shape, lambda i, nd=p.ndim: (0,) * nd) for p in params]
    in_specs += [pl.BlockSpec((rb, x.shape[1]), row) for x in xs]
    out_specs = [pl.BlockSpec((rb, c), row) for c, _ in out_sds]
    out_specs += [pl.BlockSpec((1,) + s, lambda i: (_blk(order, i, n), 0, 0)) for s in carry_sds]
    out_shape = [jax.ShapeDtypeStruct((t, c), d) for c, d in out_sds]
    out_shape += [jax.ShapeDtypeStruct((n,) + s, f32) for s in carry_sds]
    res = pl.pallas_call(
        body, name=name, grid=(n,), in_specs=in_specs, out_specs=out_specs, out_shape=out_shape,
        scratch_shapes=[pltpu.VMEM(s, f32) for s in carry_sds], compiler_params=_cparams(1),
    )(*params, *xs)
    return list(res[:n_o]), list(res[n_o:])


def _blocked_bwd(name, f, order, rb, params, xs, states, dys, carry_sds):
    t = xs[0].shape[0]
    n = t // rb
    rorder = _REV[order]
    n_p, n_x, n_o, n_c = len(params), len(xs), len(dys), len(carry_sds)

    def body(*refs):
        k = 0
        p_refs = refs[k:k + n_p]; k += n_p
        x_refs = refs[k:k + n_x]; k += n_x
        st_refs = refs[k:k + n_c]; k += n_c
        dy_refs = refs[k:k + n_o]; k += n_o
        dp_refs = refs[k:k + n_p]; k += n_p
        dx_refs = refs[k:k + n_x]; k += n_x
        dc_refs = refs[k:]
        i = pl.program_id(0)
        blk = _blk(rorder, i, n)
        p = [r[...] for r in p_refs]
        x = [r[...] for r in x_refs]
        dy = [r[...] for r in dy_refs]
        if n_c:
            @pl.when(i == 0)
            def _():
                for c in dc_refs:
                    c[...] = jnp.zeros_like(c)
            c_in = [r[0] for r in st_refs]
            dc = [c[...] for c in dc_refs]
            _, vjp = jax.vjp(lambda p_, c_, x_: f(blk, p_, c_, x_), p, c_in, x)
            dp, dcin, dx = vjp((dc, dy))
            for c, v in zip(dc_refs, dcin):
                c[...] = v
        else:
            _, vjp = jax.vjp(lambda p_, x_: f(blk, p_, x_), p, x)
            dp, dx = vjp(dy)

        @pl.when(i == 0)
        def _():
            for r, v in zip(dp_refs, dp):
                r[...] = v

        @pl.when(i > 0)
        def _():
            for r, v in zip(dp_refs, dp):
                r[...] += v
        for r, v in zip(dx_refs, dx):
            r[...] = v.astype(r.dtype)

    row = lambda i: (_blk(rorder, i, n), 0)
    in_specs = [pl.BlockSpec(p.shape, lambda i, nd=p.ndim: (0,) * nd) for p in params]
    in_specs += [pl.BlockSpec((rb, x.shape[1]), row) for x in xs]
    in_specs += [pl.BlockSpec((1,) + s, lambda i: (_blk(rorder, i, n), 0, 0)) for s in carry_sds]
    in_specs += [pl.BlockSpec((rb, d.shape[1]), row) for d in dys]
    out_specs = [pl.BlockSpec(p.shape, lambda i, nd=p.ndim: (0,) * nd) for p in params]
    out_specs += [pl.BlockSpec((rb, x.shape[1]), row) for x in xs]
    out_shape = [jax.ShapeDtypeStruct(p.shape, f32) for p in params]
    out_shape += [jax.ShapeDtypeStruct(x.shape, x.dtype) for x in xs]
    res = pl.pallas_call(
        body, name=name + "_bwd", grid=(n,), in_specs=in_specs, out_specs=out_specs, out_shape=out_shape,
        scratch_shapes=[pltpu.VMEM(s, f32) for s in carry_sds], compiler_params=_cparams(1),
    )(*params, *xs, *states, *dys)
    return list(res[:n_p]), list(res[n_p:])


def blocked_op(name, f, params, xs, out_sds, order="asc", carry_sds=(), rb=RB):
    carry_sds = tuple(carry_sds)

    @jax.custom_vjp
    def op(params, xs):
        return tuple(_blocked_fwd(name, f, order, rb, params, xs, out_sds, carry_sds)[0])

    def fwd(params, xs):
        ys, states = _blocked_fwd(name, f, order, rb, params, xs, out_sds, carry_sds)
        return tuple(ys), (params, xs, states)

    def bwd(res, dys):
        params, xs, states = res
        dp, dx = _blocked_bwd(name, f, order, rb, params, xs, states, list(dys), carry_sds)
        return list(dp), list(dx)

    op.defvjp(fwd, bwd)
    return op(list(params), list(xs))


def _cscan_call(name, order, asc, a, xr, xi, sr=None, si=None):
    t = xr.shape[0]
    n = t // RB
    tile = xr.shape[1:]
    xspec = pl.BlockSpec((RB,) + tile, lambda i: (_blk(order, i, n), 0, 0))
    aspec = pl.BlockSpec(a.shape, lambda i: (0, 0, 0))
    plane = jax.ShapeDtypeStruct(xr.shape, f32)

    def rowidx(tt):
        return tt if asc else RB - 1 - tt

    if sr is None:
        def body(a_ref, xr_ref, xi_ref, sr_ref, si_ref, c_ref):
            i = pl.program_id(0)

            @pl.when(i == 0)
            def _():
                c_ref[...] = jnp.zeros_like(c_ref)
            ar = a_ref[0]
            ai = a_ref[1]

            def step(tt, carry):
                cr, ci = carry
                r = rowidx(tt)
                nr = ar * cr - ai * ci + xr_ref[r]
                ni = ar * ci + ai * cr + xi_ref[r]
                sr_ref[r] = nr
                si_ref[r] = ni
                return nr, ni
            cr, ci = lax.fori_loop(0, RB, step, (c_ref[0], c_ref[1]), unroll=8)
            c_ref[0] = cr
            c_ref[1] = ci

        return pl.pallas_call(
            body, name=name, grid=(n,), in_specs=[aspec, xspec, xspec], out_specs=[xspec, xspec],
            out_shape=[plane, plane], scratch_shapes=[pltpu.VMEM(a.shape, f32)], compiler_params=_cparams(1),
        )(a, xr, xi)

    def body(a_ref, xr_ref, xi_ref, sr_ref, si_ref, gr_ref, gi_ref, da_ref, c_ref):
        i = pl.program_id(0)

        @pl.when(i == 0)
        def _():
            c_ref[...] = jnp.zeros_like(c_ref)
            da_ref[...] = jnp.zeros_like(da_ref)
        ar = a_ref[0]
        ai = a_ref[1]

        def step(tt, carry):
            gr, gi, dar, dai = carry
            r = rowidx(tt)
            vr = sr_ref[r]
            vi = si_ref[r]
            dar = dar + gr * vr + gi * vi
            dai = dai + gi * vr - gr * vi
            nr = xr_ref[r] + ar * gr + ai * gi
            ni = xi_ref[r] + ar * gi - ai * gr
            gr_ref[r] = nr
            gi_ref[r] = ni
            return nr, ni, dar, dai
        z = jnp.zeros(tile, f32)
        gr, gi, dar, dai = lax.fori_loop(0, RB, step, (c_ref[0], c_ref[1], z, z), unroll=8)
        c_ref[0] = gr
        c_ref[1] = gi
        da_ref[0] += dar
        da_ref[1] += dai

    return pl.pallas_call(
        body, name=name, grid=(n,), in_specs=[aspec] + [xspec] * 4, out_specs=[xspec, xspec, aspec],
        out_shape=[plane, plane, jax.ShapeDtypeStruct(a.shape, f32)],
        scratch_shapes=[pltpu.VMEM(a.shape, f32)], compiler_params=_cparams(1),
    )(a, xr, xi, sr, si)


def cscan(name, d, a, xr, xi):
    order = "d1" if d else "asc"

    @jax.custom_vjp
    def op(a, xr, xi):
        return tuple(_cscan_call(name, order, d == 0, a, xr, xi))

    def fwd(a, xr, xi):
        sr, si = op(a, xr, xi)
        return (sr, si), (a, sr, si)

    def bwd(res, ds):
        a, sr, si = res
        gr, gi, da = _cscan_call(name + "_bwd", _REV[order], d != 0, a, ds[0], ds[1], sr, si)
        return da, gr, gi

    op.defvjp(fwd, bwd)
    return op(a, xr, xi)


def _bd_call(name, a, b, mode, k=None):
    t = a.shape[0]
    if mode == "tn":
        ck, cn = a.shape[1] // k, b.shape[1] // k

        def body(a_ref, b_ref, o_ref):
            o_ref[0] = _dot(a_ref[...], b_ref[...], _TN, True)

        return pl.pallas_call(
            body, name=name, grid=(k,),
            in_specs=[pl.BlockSpec((t, ck), lambda j: (0, j)), pl.BlockSpec((t, cn), lambda j: (0, j))],
            out_specs=pl.BlockSpec((1, ck, cn), lambda j: (j, 0, 0)),
            out_shape=jax.ShapeDtypeStruct((k, ck, cn), f32), compiler_params=_cparams(1),
        )(a, b)
    k, ck, cn = b.shape
    tm = _pick(t, 256, _SUB)
    win, wout, dn = (ck, cn, _NN) if mode == "nn" else (cn, ck, _NT)

    def body(a_ref, b_ref, o_ref):
        o_ref[...] = _dot(a_ref[...], b_ref[0], dn, True)

    return pl.pallas_call(
        body, name=name, grid=(t // tm, k),
        in_specs=[pl.BlockSpec((tm, win), lambda i, j: (i, j)), pl.BlockSpec((1, ck, cn), lambda i, j: (j, 0, 0))],
        out_specs=pl.BlockSpec((tm, wout), lambda i, j: (i, j)),
        out_shape=jax.ShapeDtypeStruct((t, k * wout), f32), compiler_params=_cparams(2),
    )(a, b)


def bd_mm(name, a, b):
    @jax.custom_vjp
    def op(a, b):
        return _bd_call(name, a, b, "nn")

    def fwd(a, b):
        return op(a, b), (a, b)

    def bwd(res, g):
        a, b = res
        return _bd_call(name + "_da", g, b, "nt"), _bd_call(name + "_db", a, g, "tn", b.shape[0])

    op.defvjp(fwd, bwd)
    return op(a, b)


def _rscan_call(name, order, asc, a, x, hp=None):
    t = x.shape[0]
    n = t // RB
    cshape = x.shape[1:]
    xspec = pl.BlockSpec((RB,) + cshape, lambda i: (_blk(order, i, n), 0, 0))

    def rowidx(tt):
        return tt if asc else RB - 1 - tt

    if hp is None:
        def body(a_ref, x_ref, h_ref, hp_ref, c_ref):
            i = pl.program_id(0)

            @pl.when(i == 0)
            def _():
                c_ref[...] = jnp.zeros_like(c_ref)

            def step(tt, h):
                r = rowidx(tt)
                hp_ref[r] = h
                h = a_ref[r] * h + x_ref[r]
                h_ref[r] = h
                return h
            c_ref[...] = lax.fori_loop(0, RB, step, c_ref[...], unroll=8)

        return pl.pallas_call(
            body, name=name, grid=(n,), in_specs=[xspec, xspec], out_specs=[xspec, xspec],
            out_shape=[jax.ShapeDtypeStruct(x.shape, f32)] * 2, scratch_shapes=[pltpu.VMEM(cshape, f32)],
            compiler_params=_cparams(1),
        )(a, x)

    def body(a_ref, x_ref, hp_ref, da_ref, db_ref, c_ref):
        i = pl.program_id(0)

        @pl.when(i == 0)
        def _():
            c_ref[...] = jnp.zeros_like(c_ref)

        def step(tt, c):
            r = rowidx(tt)
            g = x_ref[r] + c
            db_ref[r] = g
            da_ref[r] = g * hp_ref[r]
            return a_ref[r] * g
        c_ref[...] = lax.fori_loop(0, RB, step, c_ref[...], unroll=8)

    return pl.pallas_call(
        body, name=name, grid=(n,), in_specs=[xspec, xspec, xspec], out_specs=[xspec, xspec],
        out_shape=[jax.ShapeDtypeStruct(x.shape, f32)] * 2, scratch_shapes=[pltpu.VMEM(cshape, f32)],
        compiler_params=_cparams(1),
    )(a, x, hp)


def rscan(name, d, a, x):
    order = "d1" if d else "asc"

    @jax.custom_vjp
    def op(a, x):
        return _rscan_call(name, order, d == 0, a, x)[0]

    def fwd(a, x):
        h, hp = _rscan_call(name, order, d == 0, a, x)
        return h, (a, hp)

    def bwd(res, dh):
        a, hp = res
        da, db = _rscan_call(name + "_bwd", _REV[order], d != 0, a, dh, hp)
        return da, db

    op.defvjp(fwd, bwd)
    return op(a, x)


def _mod_row(blk, mod, bm):
    return jnp.where(blk == 0, mod[1:2], mod[0:1]) + bm


def _f_silu(blk, p, x):
    return [_silu(x[0]).astype(bf16)]


def _f_normmod(blk, p, x):
    nw, mod, bm = p
    d = nw.shape[1]
    r = _mod_row(blk, mod, bm)
    return [(_rms(x[0], nw) * (1.0 + r[:, d:2 * d]) + r[:, :d]).astype(bf16)]


def _f_resid(blk, p, x):
    mod, bm = p
    d = x[0].shape[1]
    r = _mod_row(blk, mod, bm)
    return [x[0] + r[:, 2 * d:] * x[1]]


def _f_mix(blk, p, x):
    bg, = p
    gp = x[0]
    d = x[1].shape[1]
    acc = None
    for k in range(4):
        t = jax.nn.sigmoid(gp[:, k * d:(k + 1) * d] + bg[:, k * d:(k + 1) * d]) * x[1 + k]
        acc = t if acc is None else acc + t
    return [acc.astype(bf16)]


def _tri(rev):
    row = lax.broadcasted_iota(jnp.int32, (CHUNK, CHUNK), 0)
    col = lax.broadcasted_iota(jnp.int32, (CHUNK, CHUNK), 1)
    return (col >= row) if rev else (col <= row)


def _chunk_ids(rev):
    ids = list(range(RB // CHUNK))
    return ids[::-1] if rev else ids


def _f_hg(rev):
    def f(blk, p, c, x):
        lb, = p
        st, = c
        qi, fr = x
        q = _silu(qi[:, :BR_W])
        v = qi[:, BR_W:]
        fg = lb + (1.0 - lb) * jax.nn.sigmoid(fr)
        logf = jnp.log(fg)
        k = 1.0 - fg
        m = _tri(rev)
        mf = m.astype(f32)
        outs = [None] * (RB // CHUNK)
        for ci in _chunk_ids(rev):
            sl = slice(CHUNK * ci, CHUNK * ci + CHUNK)
            lf = logf[sl]
            b = _dot(mf, lf, hi=True)
            bend = jnp.sum(lf, axis=0, keepdims=True)
            mid = 0.5 * bend
            qe = q[sl] * jnp.exp(b - mid)
            ke = k[sl] * jnp.exp(mid - b)
            kd = k[sl] * jnp.exp(bend - b)
            qb = q[sl] * jnp.exp(b)
            dec = jnp.exp(bend)
            vc = v[sl]
            oh, ns = [], []
            for hh in range(HG_HEADS):
                cs = slice(HG_DK * hh, HG_DK * hh + HG_DK)
                sth = st[cs]
                att = jnp.where(m, _dot(qe[:, cs], ke[:, cs], _NT), 0.0)
                oh.append(_dot(att, vc[:, cs]) + _dot(qb[:, cs], sth, _NT))
                ns.append(sth * dec[:, cs] + _dot(vc[:, cs], kd[:, cs], _TN))
            st = jnp.concatenate(ns, axis=0)
            outs[ci] = jnp.concatenate(oh, axis=1)
        return [st], [jnp.concatenate(outs, axis=0)]
    return f


def _f_hg_final(blk, p, x):
    nw, = p
    o = x[0] + x[1]
    parts = []
    for hh in range(HG_HEADS):
        cs = slice(HG_DK * hh, HG_DK * hh + HG_DK)
        parts.append(_rms(o[:, cs], nw[:, cs]))
    return [(jnp.concatenate(parts, axis=1) * _silu(x[2])).astype(bf16)]


def _conv(x, cw, cb, blk):
    rows = x.shape[0]
    r = lax.broadcasted_iota(jnp.int32, (rows, 1), 0)
    rm = jnp.where(blk == 0, r, r % CHUNK)
    seg = jnp.where(blk == 0, rows, CHUNK)

    def vmask(o):
        return ((rm + o >= 0) & (rm + o < seg)).astype(f32)

    def shifted(o):
        @jax.custom_vjp
        def sh(x, mo, mn):
            return pltpu.roll(x, (-o) % rows, 0) * mo

        def fwd(x, mo, mn):
            return sh(x, mo, mn), (mo, mn)

        def bwd(res, g):
            mo, mn = res
            return pltpu.roll(g, o % rows, 0) * mn, jnp.zeros_like(mo), jnp.zeros_like(mn)
        sh.defvjp(fwd, bwd)
        return sh(x, vmask(o), vmask(-o))

    lo = (CONV_W - 1) // 2
    out = cb
    for k in range(CONV_W):
        o = k - lo
        out = out + cw[k:k + 1] * (x if o == 0 else shifted(o))
    return out


def _f_lru_a(blk, p, x):
    cw, cb, wg, gb, lam = p
    xc = _conv(x[0], cw, cb, blk)
    n_chunks = BR_W // _LANE
    xk = [xc[:, _LANE * k:_LANE * (k + 1)] for k in range(n_chunks)]

    def gate(j):
        pre = jnp.concatenate([_dot(xk[k], wg[j * n_chunks + k], hi=True) for k in range(n_chunks)], axis=1)
        return jax.nn.sigmoid(pre + gb[:, BR_W * j:BR_W * (j + 1)])

    outs = []
    for d in range(2):
        r = gate(2 * d)
        ig = gate(2 * d + 1)
        log_a = -LRU_C * r * _softplus(-lam[d:d + 1])
        outs.append(jnp.exp(log_a))
        outs.append(jnp.sqrt(_one_minus_exp(2.0 * log_a)) * (ig * xc))
    return outs


def _f_lru_c(blk, p, x):
    return [((x[0] + x[1]) * _silu(x[2])).astype(bf16)]


def _f_s5_c1(blk, p, x):
    dsk, = p
    return [jax.nn.gelu(x[0] + x[1] + x[2] + x[3] + dsk * x[4])]


def _f_s5_c2(blk, p, x):
    bglu, = p
    return [(x[0] * jax.nn.sigmoid(x[1] + bglu) * _silu(x[2])).astype(bf16)]


def _f_m2_a(blk, p, x):
    cw, cb, dtb = p
    return [_silu(_conv(x[0], cw, cb, blk)), _softplus(x[1] + dtb)]


def _f_ssd(d):
    rev = d == 1
    hpg = M2_HEADS // M2_GROUPS

    def f(blk, p, c, x):
        alog, = p
        st, = c
        xbc, dtp = x
        a = -jnp.exp(alog[:, M2_HEADS * d:M2_HEADS * (d + 1)])
        dt = dtp[:, M2_HEADS * d:M2_HEADS * (d + 1)]
        xs = xbc[:, :BR_W]
        bm = xbc[:, BR_W:BR_W + M2_GROUPS * M2_STATE]
        cm = xbc[:, BR_W + M2_GROUPS * M2_STATE:]
        m = _tri(rev)
        mf = m.astype(f32)
        outs = [None] * (RB // CHUNK)
        for ci in _chunk_ids(rev):
            sl = slice(CHUNK * ci, CHUNK * ci + CHUNK)
            dtc = dt[sl]
            dta = dtc * a
            cum = _dot(mf, dta, hi=True)
            cend = jnp.sum(dta, axis=0, keepdims=True)
            cum_t = cum.T
            dt_t = dtc.T
            ys, ns = [], []
            for g in range(M2_GROUPS):
                bmg = bm[sl, M2_STATE * g:M2_STATE * (g + 1)]
                cmg = cm[sl, M2_STATE * g:M2_STATE * (g + 1)]
                scores = _dot(cmg, bmg, _NT)
                for r in range(hpg):
                    h = g * hpg + r
                    hs = slice(M2_HEADDIM * h, M2_HEADDIM * (h + 1))
                    ci_col = cum[:, h:h + 1]
                    decay = jnp.exp(jnp.where(m, ci_col - cum_t[h:h + 1, :], -1e30))
                    w = scores * decay * dt_t[h:h + 1, :]
                    xh = xs[sl, hs]
                    sth = st[hs]
                    ys.append(_dot(w, xh) + _dot(cmg, sth, _NT) * jnp.exp(ci_col))
                    wx = (jnp.exp(cend[:, h:h + 1] - ci_col) * dtc[:, h:h + 1]) * xh
                    ns.append(jnp.exp(cend[:, h:h + 1]) * sth + _dot(wx, bmg, _TN))
            st = jnp.concatenate(ns, axis=0)
            outs[ci] = jnp.concatenate(ys, axis=1)
        return [st], [jnp.concatenate(outs, axis=0)]
    return f


def _f_m2_c(blk, p, x):
    dsk, nw = p
    y = x[0] + x[1] + dsk * x[2][:, :BR_W]
    return [_rms(y * _silu(x[3]), nw).astype(bf16)]


def _f_loss(blk, p, x):
    fnw, = p
    err = _rms(x[0], fnw) - x[1]
    return [0.5 * jnp.mean(err * err, axis=-1, keepdims=True)]


def _blockdiag(w):
    g, a, b = w.shape
    return jnp.einsum("gab,gh->gahb", w, jnp.eye(g, dtype=w.dtype)).reshape(g * a, g * b)


def _s5_params(l, w):
    a_scan, cds = [], []
    per = _LANE // S5_GROUP

    def chunks(m):
        return jnp.stack([_blockdiag(m[k * per:(k + 1) * per]) for k in range(S5_GROUPS // per)])

    b_re = jnp.transpose(w["s5_b_re"][l], (0, 2, 1))
    b_im = jnp.transpose(w["s5_b_im"][l], (0, 2, 1))
    bd = [chunks(b_re), chunks(b_im)]
    c_re = jnp.transpose(w["s5_c_re"][l], (0, 2, 1))
    c_im = jnp.transpose(w["s5_c_im"][l], (0, 2, 1))
    for d in range(2):
        lam_re = w["s5_a_re"][l, d]
        lam_im = w["s5_a_im"][l, d]
        step = jnp.exp(w["s5_log_step"][l, d])[:, None]
        mag = jnp.exp(lam_re * step)
        ab_re = mag * jnp.cos(lam_im * step)
        ab_im = mag * jnp.sin(lam_im * step)
        den = lam_re * lam_re + lam_im * lam_im
        nr = ab_re - 1.0
        co_re = (nr * lam_re + ab_im * lam_im) / den
        co_im = (ab_im * lam_re - nr * lam_im) / den
        n_state = S5_GROUPS * S5_STATE
        a_scan.append(jnp.stack([ab_re.reshape(_SUB, n_state // _SUB), ab_im.reshape(_SUB, n_state // _SUB)]))
        cp_re = c_re * co_re[:, :, None] - c_im * co_im[:, :, None]
        cp_im = c_re * co_im[:, :, None] + c_im * co_re[:, :, None]
        cds.append([chunks(cp_re), -chunks(cp_im)])
    return a_scan, bd, cds


def _lru_gate(l, w):
    gw = w["lru_gate_w"][l]
    per = _LANE // (BR_W // LRU_BLOCKS)
    chunks = [_blockdiag(gw[d, g, k * per:(k + 1) * per])
              for d in range(2) for g in range(2) for k in range(LRU_BLOCKS // per)]
    return jnp.stack(chunks), w["lru_gate_b"][l].reshape(1, -1)


def _pad_cols(a, n):
    return jnp.pad(a, ((0, 0), (0, n - a.shape[1])))


IN_SIZES = (BR_W,) * 9 + (M2_XBC, 2 * M2_HEADS, BR_W)
IN_OFFS = tuple(sum(IN_SIZES[:i]) for i in range(len(IN_SIZES) + 1))
IN_GROUPS = (("hg_qi", 0, 2, 1024), ("hg_ff", 2, 1, 512), ("hg_fb", 3, 1, 512), ("hg_z", 4, 1, 512),
             ("s5_u", 5, 1, 512), ("s5_z", 6, 1, 512), ("lru_x", 7, 1, 512), ("lru_z", 8, 1, 512),
             ("m2_xbc", 9, 1, 768), ("m2_dt", 10, 1, 128), ("m2_z", 11, 1, 512))


def _new_slots(big):
    slots = {n: jnp.zeros(w.shape, f32) for n, w in big.items() if n not in ("w_in", "w_gate")}
    n_layers, d_model = big["w_in"].shape[:2]
    slots["w_in"] = [{name: jnp.zeros((d_model, width), f32) for name, _, _, width in IN_GROUPS} for _ in range(n_layers)]
    slots["w_gate"] = jnp.zeros((n_layers, d_model, 4 * d_model), f32)
    return slots


def _slot_grads(g):
    out = dict(g)
    out["w_in"] = jnp.stack([
        jnp.concatenate([gl[name][:, :IN_OFFS[s0 + ns] - IN_OFFS[s0]] for name, s0, ns, _ in IN_GROUPS], axis=1)
        for gl in g["w_in"]])
    n_layers, d_model = g["w_gate"].shape[:2]
    out["w_gate"] = jnp.transpose(g["w_gate"].reshape(n_layers, d_model, 4, d_model), (0, 2, 1, 3))
    return out


def _forward(p, big, slots, x, ctx, c, target):
    n_layers = p["norm_w"].shape[0]
    d_model = x.shape[-1]
    xa = jnp.concatenate([ctx, x], axis=0)
    t = xa.shape[0]
    cc = jnp.concatenate([c, p["c_ctx"][None], jnp.zeros((_SUB - 2, d_model), f32)], axis=0)
    lb_all = jnp.cumsum(jax.nn.softmax(p["hg_lb_logits"], axis=0), axis=0)
    scc, = blocked_op("silu_c", _f_silu, [], [cc], [(d_model, bf16)], rb=_SUB)

    for l in range(n_layers):
        tag = "l%d_" % l
        mod = mm(tag + "mod", scc, big["w_mod"][l], slots["w_mod"][l])
        bm = p["b_mod"][l][None]
        h, = blocked_op(tag + "normmod", _f_normmod, [p["norm_w"][l][None], mod, bm], [xa], [(d_model, bf16)])
        gnames = [g[0] for g in IN_GROUPS]
        wvs = [_pad_cols(big["w_in"][l][:, IN_OFFS[s0]:IN_OFFS[s0 + ns]], width) for _, s0, ns, width in IN_GROUPS]
        u = dict(zip(gnames, multi_mm(tag + "in_", gnames, h, wvs, [slots["w_in"][l][g] for g in gnames])))

        o_dirs = []
        for d, fname in ((0, "hg_ff"), (1, "hg_fb")):
            o, = blocked_op(tag + "hg%d" % d, _f_hg(d == 1), [lb_all[l, d][None]], [u["hg_qi"], u[fname]],
                            [(BR_W, f32)], order="d1" if d else "asc", carry_sds=[(BR_W, HG_DK)])
            o_dirs.append(o)
        y_hg, = blocked_op(tag + "hg_fin", _f_hg_final, [p["hg_norm"][l][None]], o_dirs + [u["hg_z"]], [(BR_W, bf16)])

        a_scan, bd, cds = _s5_params(l, p)
        n_state = S5_GROUPS * S5_STATE
        bu = [bd_mm(tag + "s5_bu%d" % part, u["s5_u"], bd[part]).reshape(t, _SUB, n_state // _SUB) for part in range(2)]
        ysum = []
        for d in range(2):
            s = cscan(tag + "s5_scan%d" % d, d, a_scan[d], bu[0], bu[1])
            for part in range(2):
                ysum.append(bd_mm(tag + "s5_c%d%d" % (d, part), s[part].reshape(t, n_state), cds[d][part]))
        g5, = blocked_op(tag + "s5_c1", _f_s5_c1, [p["s5_d"][l][None]], ysum + [u["s5_u"]], [(BR_W, f32)])
        gl = mm(tag + "s5_glu", g5, big["s5_w_glu"][l], slots["s5_w_glu"][l])
        y_s5, = blocked_op(tag + "s5_c2", _f_s5_c2, [p["s5_b_glu"][l][None]], [g5, gl, u["s5_z"]], [(BR_W, bf16)])

        wg, gb = _lru_gate(l, p)
        ab = blocked_op(tag + "lru_a", _f_lru_a,
                        [p["lru_conv_w"][l], p["lru_conv_b"][l][None], wg, gb, p["lru_lam"][l]],
                        [u["lru_x"]], [(BR_W, f32)] * 4)
        hs = []
        for d in range(2):
            a3 = ab[2 * d].reshape(t, 4, BR_W // 4)
            b3 = ab[2 * d + 1].reshape(t, 4, BR_W // 4)
            hs.append(rscan(tag + "lru_scan%d" % d, d, a3, b3).reshape(t, BR_W))
        y_lru, = blocked_op(tag + "lru_c", _f_lru_c, [], hs + [u["lru_z"]], [(BR_W, bf16)])

        dtb = _pad_cols(p["m2_dt_bias"][l].reshape(1, -1), _LANE)
        xbc, dtp = blocked_op(tag + "m2_a", _f_m2_a, [p["m2_conv_w"][l], p["m2_conv_b"][l][None], dtb],
                              [u["m2_xbc"], u["m2_dt"]], [(M2_XBC, f32), (_LANE, f32)])
        alog = _pad_cols(p["m2_a_log"][l].reshape(1, -1), _LANE)
        y_dirs = []
        for d in range(2):
            y, = blocked_op(tag + "ssd%d" % d, _f_ssd(d), [alog], [xbc, dtp], [(BR_W, f32)],
                            order="d1" if d else "asc", carry_sds=[(BR_W, M2_STATE)])
            y_dirs.append(y)
        dsk = jnp.repeat(p["m2_d"][l], M2_HEADDIM)[None]
        y_m2, = blocked_op(tag + "m2_c", _f_m2_c, [dsk, p["m2_norm"][l][None]], y_dirs + [xbc, u["m2_z"]], [(BR_W, bf16)])

        wg_all = jnp.transpose(big["w_gate"][l], (1, 0, 2)).reshape(d_model, 4 * d_model)
        gp = mm(tag + "gate", h, wg_all, slots["w_gate"][l])
        bs = [mm(tag + "br%d" % k, yk, big["w_branch"][l, k], slots["w_branch"][l, k])
              for k, yk in enumerate((y_hg, y_s5, y_lru, y_m2))]
        mix, = blocked_op(tag + "mix", _f_mix, [p["b_gate"][l].reshape(1, -1)], [gp] + bs, [(d_model, bf16)])
        o = mm(tag + "out", mix, big["w_out"][l], slots["w_out"][l])
        xa, = blocked_op(tag + "resid", _f_resid, [mod, bm], [xa, o], [(d_model, f32)])

    rl, = blocked_op("loss", _f_loss, [p["final_norm"][None]], [xa[ctx.shape[0]:], target], [(1, f32)])
    return jnp.sum(rl)


_MESH = pl.DeviceIdType.MESH
_ANY = pl.BlockSpec(memory_space=pl.ANY)
W_PACK = 1024


def _place():
    x, y, c = lax.axis_index("x"), lax.axis_index("y"), lax.axis_index("c")
    chips = [(x, 1 - y), (1 - x, y), (1 - x, 1 - y)]
    return x, y, c, chips


def _rcopy(src, dst, ssem, rsem, k, to):
    return pltpu.make_async_remote_copy(src_ref=src, dst_ref=dst, send_sem=ssem.at[k], recv_sem=rsem.at[k],
                                        device_id=to, device_id_type=_MESH)


def gather_shards(xs):
    n = len(xs)

    def body(*refs):
        x_refs, o_refs = refs[:n], refs[n:2 * n]
        ssem, rsem, lsem = refs[2 * n:]
        x, y, c, chips = _place()
        j = 2 * x + y
        sib = (x, y, 1 - c)
        mine = [pltpu.make_async_copy(x_refs[a], o_refs[a].at[j], lsem.at[a]) for a in range(n)]
        for cp in mine:
            cp.start()
        first = [_rcopy(x_refs[a].at[c], o_refs[a].at[j, c], ssem, rsem, 6 * a + r, (*chips[r], c))
                 for r in range(3) for a in range(n)]
        for cp in first:
            cp.start()
        passed = []
        for r in range(3):
            jr = j ^ (r + 1)
            for a in range(n):
                _rcopy(x_refs[a].at[c], o_refs[a].at[jr, c], ssem, rsem, 6 * a + r, sib).wait_recv()
                cp = _rcopy(o_refs[a].at[jr, c], o_refs[a].at[jr, c], ssem, rsem, 6 * a + 3 + r, sib)
                cp.start()
                passed.append(cp)
        for r in range(3):
            jr = j ^ (r + 1)
            for a in range(n):
                _rcopy(x_refs[a].at[c], o_refs[a].at[jr, 1 - c], ssem, rsem, 6 * a + 3 + r, sib).wait_recv()
        for cp in first + passed:
            cp.wait_send()
        for cp in mine:
            cp.wait()

    return pl.pallas_call(
        body, name="gather_shards", out_shape=[jax.ShapeDtypeStruct((4,) + x.shape, x.dtype) for x in xs],
        in_specs=[_ANY] * n, out_specs=[_ANY] * n,
        scratch_shapes=[pltpu.SemaphoreType.DMA((6 * n,)), pltpu.SemaphoreType.DMA((6 * n,)), pltpu.SemaphoreType.DMA((n,))],
    )(*xs)


def sibling_halves(gs):
    n = len(gs)

    def body(*refs):
        g_refs, o_refs = refs[:n], refs[n:2 * n]
        ssem, rsem = refs[2 * n:]
        x, y, c, _ = _place()
        sib = (x, y, 1 - c)
        cps = [_rcopy(g_refs[a].at[k, 1 - c], o_refs[a].at[k], ssem, rsem, 4 * a + k, sib)
               for k in range(4) for a in range(n)]
        for cp in cps:
            cp.start()
        for cp in cps:
            cp.wait()

    return pl.pallas_call(
        body, name="sibling_halves", out_shape=[jax.ShapeDtypeStruct((4,) + g.shape[2:], g.dtype) for g in gs],
        in_specs=[_ANY] * n, out_specs=[_ANY] * n,
        scratch_shapes=[pltpu.SemaphoreType.DMA((4 * n,)), pltpu.SemaphoreType.DMA((4 * n,))],
    )(*gs)


def scatter_chips(ps):
    n = len(ps)

    def body(*refs):
        p_refs, o_refs = refs[:n], refs[n:2 * n]
        ssem, rsem = refs[2 * n:]
        x, y, c, chips = _place()
        j = 2 * x + y
        cps = [_rcopy(p_refs[a].at[j ^ (r + 1)], o_refs[a].at[r], ssem, rsem, 3 * a + r, (*chips[r], c))
               for r in range(3) for a in range(n)]
        for cp in cps:
            cp.start()
        for cp in cps:
            cp.wait()

    return pl.pallas_call(
        body, name="scatter_chips", out_shape=[jax.ShapeDtypeStruct((3,) + p.shape[1:], p.dtype) for p in ps],
        in_specs=[_ANY] * n, out_specs=[_ANY] * n,
        scratch_shapes=[pltpu.SemaphoreType.DMA((3 * n,)), pltpu.SemaphoreType.DMA((3 * n,))],
    )(*ps)


def join_halves(qs):
    n = len(qs)

    def body(*refs):
        o_refs = refs[n:2 * n]
        ssem, rsem = refs[2 * n:]
        x, y, c, _ = _place()
        sib = (x, y, 1 - c)
        cps = [_rcopy(o_refs[a].at[c], o_refs[a].at[c], ssem, rsem, a, sib) for a in range(n)]
        for cp in cps:
            cp.start()
        for a in range(n):
            _rcopy(o_refs[a].at[c], o_refs[a].at[1 - c], ssem, rsem, a, sib).wait_recv()
        for cp in cps:
            cp.wait_send()

    return pl.pallas_call(
        body, name="join_halves", out_shape=[jax.ShapeDtypeStruct(q.shape, q.dtype) for q in qs],
        in_specs=[_ANY] * n, out_specs=[_ANY] * n, input_output_aliases={a: a for a in range(n)},
        scratch_shapes=[pltpu.SemaphoreType.DMA((n,)), pltpu.SemaphoreType.DMA((n,))],
    )(*qs)


def _rows_block(r):
    return _pick(r, 256, _SUB)


def add_sibling(tag, g, r1, place):
    _, _, rows, w = g.shape
    rb = _rows_block(rows)

    def body(pl_ref, g_ref, r_ref, o_ref):
        o_ref[...] = (g_ref[0] + r_ref[...]).astype(bf16)

    return pl.pallas_call(
        body, name="add_sibling_" + tag, out_shape=jax.ShapeDtypeStruct((4, rows, w), bf16),
        grid_spec=pltpu.PrefetchScalarGridSpec(
            num_scalar_prefetch=1, grid=(4, rows // rb),
            in_specs=[pl.BlockSpec((1, 1, rb, w), lambda k, i, s: (k, s[1], i, 0)),
                      pl.BlockSpec((1, rb, w), lambda k, i, s: (k, i, 0))],
            out_specs=pl.BlockSpec((1, rb, w), lambda k, i, s: (k, i, 0))),
        compiler_params=_cparams(2),
    )(place, g, r1)


def add_chips(tag, p, r2, place):
    _, rows, w = p.shape
    rb = _rows_block(rows)

    def body(pl_ref, p_ref, r_ref, o_ref):
        j = pl_ref[0]
        own = p_ref[0].astype(f32)
        others = [r_ref[0].astype(f32), r_ref[1].astype(f32), r_ref[2].astype(f32)]
        acc = None
        for k in range(4):
            rel = k ^ j
            t = jnp.where(rel == 0, own, jnp.where(rel == 1, others[0], jnp.where(rel == 2, others[1], others[2])))
            acc = t if acc is None else acc + t
        o_ref[0] = acc

    return pl.pallas_call(
        body, name="add_chips_" + tag, out_shape=jax.ShapeDtypeStruct((2, rows, w), f32),
        grid_spec=pltpu.PrefetchScalarGridSpec(
            num_scalar_prefetch=1, grid=(rows // rb,),
            in_specs=[pl.BlockSpec((1, rb, w), lambda i, s: (s[0], i, 0)),
                      pl.BlockSpec((3, rb, w), lambda i, s: (0, i, 0))],
            out_specs=pl.BlockSpec((1, rb, w), lambda i, s: (s[1], i, 0))),
        compiler_params=_cparams(1),
    )(place, p, r2)


def adamw(tag, g, w, m, v):
    rows, wd = g.shape
    rb = _rows_block(rows)

    def body(g_ref, w_ref, m_ref, v_ref, d_ref, nm_ref, nv_ref):
        gv = g_ref[...]
        nm = ADAM_B1 * m_ref[...] + (1.0 - ADAM_B1) * gv
        nv = ADAM_B2 * v_ref[...] + (1.0 - ADAM_B2) * (gv * gv)
        m_hat = nm / (1.0 - ADAM_B1 ** ADAM_STEP)
        v_hat = nv / (1.0 - ADAM_B2 ** ADAM_STEP)
        d_ref[...] = -ADAM_LR * (m_hat / (jnp.sqrt(v_hat) + ADAM_EPS) + ADAM_WD * w_ref[...])
        nm_ref[...] = nm
        nv_ref[...] = nv

    spec = pl.BlockSpec((rb, wd), lambda i: (i, 0))
    return pl.pallas_call(
        body, name="adamw_" + tag, grid=(rows // rb,), in_specs=[spec] * 4, out_specs=[spec] * 3,
        out_shape=[jax.ShapeDtypeStruct(g.shape, f32)] * 3, compiler_params=_cparams(1),
    )(g, w, m, v)


WEIGHTS = ("c_ctx", "norm_w", "w_mod", "b_mod", "w_in", "hg_lb_logits", "hg_norm", "s5_a_re", "s5_a_im", "s5_log_step",
           "s5_b_re", "s5_b_im", "s5_c_re", "s5_c_im", "s5_d", "s5_w_glu", "s5_b_glu", "lru_conv_w", "lru_conv_b",
           "lru_gate_w", "lru_gate_b", "lru_lam", "m2_conv_w", "m2_conv_b", "m2_dt_bias", "m2_a_log", "m2_d", "m2_norm",
           "w_branch", "w_gate", "b_gate", "w_out", "final_norm")
SHARD_AXIS = {"w_mod": 2, "w_in": 2, "hg_lb_logits": 2, "s5_w_glu": 1, "lru_conv_w": 2, "lru_lam": 2, "m2_conv_w": 2,
              "w_branch": 3, "w_gate": 2, "b_gate": 2, "w_out": 1}
BIG = ("w_mod", "w_in", "s5_w_glu", "w_branch", "w_gate", "w_out")
N_CHIPS = 4


def _to_rows(flat, row_unit):
    n = flat.shape[-1]
    per = 2 * row_unit * W_PACK
    total = -(-n // per) * per
    flat = jnp.pad(flat, [(0, 0)] * (flat.ndim - 1) + [(0, total - n)])
    return flat.reshape(flat.shape[:-1] + (2, total // (2 * W_PACK), W_PACK))


SMALL_SHARDED = tuple(n for n in WEIGHTS if n in SHARD_AXIS and n not in BIG)
SMALL_REPLICATED = tuple(n for n in WEIGHTS if n not in SHARD_AXIS)


def _chip_slices(a, axis):
    width = a.shape[axis] // N_CHIPS
    return jnp.stack([lax.slice_in_dim(a, k * width, (k + 1) * width, axis=axis) for k in range(N_CHIPS)])


def _gather_weights(local):
    small = jnp.concatenate([lax.bitcast_convert_type(local[n], bf16).reshape(-1) for n in SMALL_SHARDED])
    got = gather_shards([local[n].astype(bf16) for n in BIG] + [_to_rows(small, 16)])
    full = {}
    for n, g in zip(BIG, got):
        full[n] = jnp.concatenate([g[j] for j in range(N_CHIPS)], axis=SHARD_AXIS[n])
    flat, off = got[-1].reshape(N_CHIPS, -1), 0
    for n in SMALL_SHARDED:
        shp = local[n].shape
        size = 2 * math.prod(shp)
        part = lax.bitcast_convert_type(flat[:, off:off + size].reshape((N_CHIPS,) + shp + (2,)), f32)
        off += size
        full[n] = jnp.concatenate([part[j] for j in range(N_CHIPS)], axis=SHARD_AXIS[n])
    return full


def _pack_small(vals, extra):
    return jnp.concatenate([vals[n].reshape(-1) for n in SMALL_SHARDED + SMALL_REPLICATED] + [extra.reshape(1)])


def _pack_small_grads(grads, loss):
    rep = [grads[n].reshape(-1) for n in SMALL_REPLICATED] + [loss.reshape(1)]
    sh = [_chip_slices(grads[n], SHARD_AXIS[n]).reshape(N_CHIPS, -1) for n in SMALL_SHARDED]
    return jnp.concatenate(sh + [jnp.broadcast_to(r, (N_CHIPS,) + r.shape) for r in rep], axis=1)


def _unpack_small(flat, like):
    out, off = {}, 0
    for n in SMALL_SHARDED + SMALL_REPLICATED:
        size = math.prod(like[n].shape)
        out[n] = flat[off:off + size].reshape(like[n].shape)
        off += size
    return out, flat[off]


def _reduce_grads(tags, gs):
    place = jnp.stack([2 * lax.axis_index("x") + lax.axis_index("y"), lax.axis_index("c")]).astype(jnp.int32)
    pairs = [add_sibling(t, g, r, place) for t, g, r in zip(tags, gs, sibling_halves(gs))]
    quads = [add_chips(t, p, r, place) for t, p, r in zip(tags, pairs, scatter_chips(pairs))]
    return join_halves(quads)


def kernel(x, c, ctx, c_ctx, norm_w, w_mod, b_mod, w_in, hg_lb_logits, hg_norm, s5_a_re, s5_a_im, s5_log_step, s5_b_re, s5_b_im, s5_c_re, s5_c_im, s5_d, s5_w_glu, s5_b_glu, lru_conv_w, lru_conv_b, lru_gate_w, lru_gate_b, lru_lam, m2_conv_w, m2_conv_b, m2_dt_bias, m2_a_log, m2_d, m2_norm, w_branch, w_gate, b_gate, w_out, final_norm, loss_target, m_c_ctx, m_norm_w, m_w_mod, m_b_mod, m_w_in, m_hg_lb_logits, m_hg_norm, m_s5_a_re, m_s5_a_im, m_s5_log_step, m_s5_b_re, m_s5_b_im, m_s5_c_re, m_s5_c_im, m_s5_d, m_s5_w_glu, m_s5_b_glu, m_lru_conv_w, m_lru_conv_b, m_lru_gate_w, m_lru_gate_b, m_lru_lam, m_m2_conv_w, m_m2_conv_b, m_m2_dt_bias, m_m2_a_log, m_m2_d, m_m2_norm, m_w_branch, m_w_gate, m_b_gate, m_w_out, m_final_norm, v_c_ctx, v_norm_w, v_w_mod, v_b_mod, v_w_in, v_hg_lb_logits, v_hg_norm, v_s5_a_re, v_s5_a_im, v_s5_log_step, v_s5_b_re, v_s5_b_im, v_s5_c_re, v_s5_c_im, v_s5_d, v_s5_w_glu, v_s5_b_glu, v_lru_conv_w, v_lru_conv_b, v_lru_gate_w, v_lru_gate_b, v_lru_lam, v_m2_conv_w, v_m2_conv_b, v_m2_dt_bias, v_m2_a_log, v_m2_d, v_m2_norm, v_w_branch, v_w_gate, v_b_gate, v_w_out, v_final_norm):
    given = dict(locals())
    w_loc = {n: given[n] for n in WEIGHTS}
    m_loc = {n: given["m_" + n] for n in WEIGHTS}
    v_loc = {n: given["v_" + n] for n in WEIGHTS}

    full = _gather_weights(w_loc)
    params = {n: (full[n] if n in SHARD_AXIS else w_loc[n]) for n in WEIGHTS if n not in BIG}
    big = {n: full[n] for n in BIG}
    def loss_fn(p, s, xx):
        return _forward(p, big, s, xx, ctx[0], c, loss_target[0])

    loss, (g_p, g_s, g_x) = jax.value_and_grad(loss_fn, argnums=(0, 1, 2))(params, _new_slots(big), x[0])
    grads = {**g_p, **_slot_grads(g_s)}

    def rows4(a):
        return a.reshape(a.shape[:2] + (-1, a.shape[-1]))

    g_big = [rows4(_chip_slices(grads[n], SHARD_AXIS[n])) for n in BIG]
    g_small = _to_rows(_pack_small_grads(grads, loss), 64)
    summed = _reduce_grads(list(BIG) + ["small"], g_big + [g_small])

    g_out, d_out, m_out, v_out = {}, {}, {}, {}
    for n, g in zip(BIG, summed):
        shp = w_loc[n].shape
        flat2 = lambda a: a.reshape(-1, shp[-1])
        g_out[n] = g.reshape(shp)
        d, nm, nv = adamw(n, flat2(g), flat2(w_loc[n]), flat2(m_loc[n]), flat2(v_loc[n]))
        d_out[n], m_out[n], v_out[n] = d.reshape(shp), nm.reshape(shp), nv.reshape(shp)
    zero = jnp.zeros((), f32)
    flat = lambda vals: _to_rows(_pack_small(vals, zero), 64).reshape(-1, W_PACK)
    gs = summed[-1].reshape(-1, W_PACK)
    d, nm, nv = adamw("small", gs, flat(w_loc), flat(m_loc), flat(v_loc))
    gsm, loss_out = _unpack_small(gs.reshape(-1), w_loc)
    g_out.update(gsm)
    d_out.update(_unpack_small(d.reshape(-1), w_loc)[0])
    m_out.update(_unpack_small(nm.reshape(-1), w_loc)[0])
    v_out.update(_unpack_small(nv.reshape(-1), w_loc)[0])
    outs = [loss_out, g_x[None]]
    for group in (g_out, d_out, m_out, v_out):
        outs += [group[n] for n in WEIGHTS]
    return tuple(outs)
```

```python
import functools
import math

import jax
import jax.numpy as jnp
from jax import lax
from jax.experimental import pallas as pl
from jax.experimental.pallas import tpu as pltpu

f32 = jnp.float32
bf16 = jnp.bfloat16
_MM_DTYPE = bf16
_HI = lax.Precision.HIGHEST
_VMEM_LIMIT = 56 * 1024 * 1024
_LANE = 128
_SUB = 8

EPS = 1e-6
CONV_W = 4
CHUNK = 64
RB = 256
BR_W = 512
HG_HEADS = 4
HG_DK = 128
S5_GROUPS = 32
S5_GROUP = 16
S5_STATE = 64
LRU_BLOCKS = 8
LRU_C = 8.0
M2_HEADS = 8
M2_HEADDIM = 64
M2_GROUPS = 2
M2_STATE = 64
M2_XBC = BR_W + 2 * M2_GROUPS * M2_STATE
ADAM_LR = 0.001
ADAM_B1 = 0.9
ADAM_B2 = 0.999
ADAM_EPS = 1e-08
ADAM_WD = 0.01
ADAM_STEP = 10

_NN = (((1,), (0,)), ((), ()))
_NT = (((1,), (1,)), ((), ()))
_TN = (((0,), (0,)), ((), ()))


def _silu(x):
    return x * jax.nn.sigmoid(x)


def _softplus(x):
    return jnp.maximum(x, 0.0) + jnp.log1p(jnp.exp(-jnp.abs(x)))


def _one_minus_exp(z):
    series = -z * (1.0 + z * 0.5 * (1.0 + z * (1.0 / 3.0) * (1.0 + z * 0.25 * (1.0 + z * 0.2))))
    return jnp.where(z > -0.05, series, 1.0 - jnp.exp(z))


def _rms(x, w):
    return x * lax.rsqrt(jnp.mean(x * x, axis=-1, keepdims=True) + EPS) * w


def _dot(a, b, dn=_NN, hi=False):
    return lax.dot_general(a, b, dn, precision=_HI if hi else None, preferred_element_type=f32)


def _cparams(n_grid):
    return pltpu.CompilerParams(dimension_semantics=("arbitrary",) * n_grid, vmem_limit_bytes=_VMEM_LIMIT)


_REV = {"asc": "desc", "d1": "d1r", "desc": "asc", "d1r": "d1"}


def _blk(order, i, n):
    if order == "asc":
        return i
    if order == "desc":
        return n - 1 - i
    if order == "d1":
        return jnp.where(i == 0, 0, n - i)
    return jnp.where(i == n - 1, 0, i + 1)


def _pick(n, cap, unit):
    if n <= cap:
        return n
    best = None
    d = unit
    while d <= cap:
        if n % d == 0:
            best = d
        d += unit
    return n if best is None else best


def _mm_call(name, a, b, mode, hi, out_dtype):
    if mode == "tn":
        k, m = a.shape
        n = b.shape[1]
        tm = _pick(m, 512, _LANE)
        tn = _pick(n, 512, _LANE)
        a_spec = pl.BlockSpec((k, tm), lambda i, j: (0, i))
        b_spec = pl.BlockSpec((k, tn), lambda i, j: (0, j))
    else:
        m, k = a.shape
        tm = _pick(m, max(256, min(1088, 4 * 1024 * 1024 // (k * a.dtype.itemsize))), _SUB)
        a_spec = pl.BlockSpec((tm, k), lambda i, j: (i, 0))
        if mode == "nn":
            n = b.shape[1]
            tn = _pick(n, max(_LANE, (4 * 1024 * 1024 // (k * 4)) // _LANE * _LANE), _LANE)
            b_spec = pl.BlockSpec((k, tn), lambda i, j: (0, j))
        else:
            n = b.shape[0]
            tn = _pick(n, max(_LANE, (4 * 1024 * 1024 // (k * 4)) // _LANE * _LANE), _LANE)
            b_spec = pl.BlockSpec((tn, k), lambda i, j: (j, 0))
    dn = {"nn": _NN, "nt": _NT, "tn": _TN}[mode]

    def body(a_ref, b_ref, o_ref):
        av = a_ref[...]
        bv = b_ref[...]
        if hi:
            av = av.astype(f32)
            bv = bv.astype(f32)
        else:
            av = av.astype(_MM_DTYPE)
            bv = bv.astype(_MM_DTYPE)
        o_ref[...] = _dot(av, bv, dn, hi).astype(o_ref.dtype)

    return pl.pallas_call(
        body, name=name, grid=(m // tm, n // tn), in_specs=[a_spec, b_spec],
        out_specs=pl.BlockSpec((tm, tn), lambda i, j: (i, j)),
        out_shape=jax.ShapeDtypeStruct((m, n), out_dtype), compiler_params=_cparams(2),
    )(a, b)


def mm(name, a, b, slot=None, hi=False):
    @jax.custom_vjp
    def op(a, b, slot):
        return _mm_call(name, a, b, "nn", hi, f32)

    def fwd(a, b, slot):
        return op(a, b, slot), (a, b)

    def bwd(res, g):
        a, b = res
        da = _mm_call(name + "_da", g, b, "nt", hi, a.dtype)
        db = _mm_call(name + "_db", a, g, "tn", hi, f32)
        if slot is None:
            return da, db.astype(b.dtype), None
        return da, jnp.zeros_like(b), db

    op.defvjp(fwd, bwd)
    return op(a, b, slot)


def _sum_nt_call(name, gs, ws, out_dtype):
    m = gs[0].shape[0]
    kdim = ws[0].shape[0]
    tm = _pick(m, 256, _SUB)
    n = len(gs)

    def body(*refs):
        acc = None
        for g_ref, w_ref in zip(refs[:n], refs[n:2 * n]):
            part = _dot(g_ref[...].astype(_MM_DTYPE), w_ref[...].astype(_MM_DTYPE), _NT)
            acc = part if acc is None else acc + part
        refs[2 * n][...] = acc.astype(out_dtype)

    in_specs = [pl.BlockSpec((tm, g.shape[1]), lambda i: (i, 0)) for g in gs]
    in_specs += [pl.BlockSpec(w.shape, lambda i: (0, 0)) for w in ws]
    return pl.pallas_call(
        body, name=name, grid=(m // tm,), in_specs=in_specs, out_specs=pl.BlockSpec((tm, kdim), lambda i: (i, 0)),
        out_shape=jax.ShapeDtypeStruct((m, kdim), out_dtype), compiler_params=_cparams(1),
    )(*gs, *ws)


def multi_mm(tag, names, a, ws, slots):
    @jax.custom_vjp
    def op(a, ws, slots):
        return tuple(_mm_call(tag + n, a, w, "nn", False, f32) for n, w in zip(names, ws))

    def fwd(a, ws, slots):
        return op(a, ws, slots), (a, ws)

    def bwd(res, gs):
        a, ws = res
        dws = [_mm_call(tag + n + "_db", a, g, "tn", False, f32) for n, g in zip(names, gs)]
        da = _sum_nt_call(tag + "da", list(gs), ws, a.dtype)
        return da, [jnp.zeros_like(w) for w in ws], dws

    op.defvjp(fwd, bwd)
    return op(a, list(ws), list(slots))


def _orders(order, n_x, n_o):
    if isinstance(order, str):
        return [order] * n_x, [order] * n_o
    return list(order[0]), list(order[1])


def _row(o, n):
    return lambda i: (_blk(o, i, n), 0)


def _blocked_fwd(name, f, order, rb, params, xs, out_sds, carry_sds):
    t = xs[0].shape[0]
    n = t // rb
    n_p, n_x, n_o, n_c = len(params), len(xs), len(out_sds), len(carry_sds)
    xo, oo = _orders(order, n_x, n_o)

    def body(*refs):
        p_refs = refs[:n_p]
        x_refs = refs[n_p:n_p + n_x]
        o_refs = refs[n_p + n_x:n_p + n_x + n_o]
        st_refs = refs[n_p + n_x + n_o:n_p + n_x + n_o + n_c]
        c_refs = refs[n_p + n_x + n_o + n_c:]
        i = pl.program_id(0)
        blk = _blk(xo[0], i, n)
        p = [r[...] for r in p_refs]
        x = [r[...] for r in x_refs]
        if n_c:
            @pl.when(i == 0)
            def _():
                for c in c_refs:
                    c[...] = jnp.zeros_like(c)
            c_in = [c[...] for c in c_refs]
            for sr, c in zip(st_refs, c_in):
                sr[0] = c
            c_out, ys = f(blk, p, c_in, x)
            for c, v in zip(c_refs, c_out):
                c[...] = v
        else:
            ys = f(blk, p, x)
        for o, y in zip(o_refs, ys):
            o[...] = y.astype(o.dtype)

    in_specs = [pl.BlockSpec(p.shape, lambda i, nd=p.ndim: (0,) * nd) for p in params]
    in_specs += [pl.BlockSpec((rb, x.shape[1]), _row(o, n)) for x, o in zip(xs, xo)]
    out_specs = [pl.BlockSpec((rb, c), _row(o, n)) for (c, _), o in zip(out_sds, oo)]
    out_specs += [pl.BlockSpec((1,) + s, lambda i: (i, 0, 0)) for s in carry_sds]
    out_shape = [jax.ShapeDtypeStruct((t, c), d) for c, d in out_sds]
    out_shape += [jax.ShapeDtypeStruct((n,) + s, f32) for s in carry_sds]
    res = pl.pallas_call(
        body, name=name, grid=(n,), in_specs=in_specs, out_specs=out_specs, out_shape=out_shape,
        scratch_shapes=[pltpu.VMEM(s, f32) for s in carry_sds], compiler_params=_cparams(1),
    )(*params, *xs)
    return list(res[:n_o]), list(res[n_o:])


def _blocked_bwd(name, f, order, rb, params, xs, states, dys, carry_sds):
    t = xs[0].shape[0]
    n = t // rb
    n_p, n_x, n_o, n_c = len(params), len(xs), len(dys), len(carry_sds)
    xo, oo = _orders(order, n_x, n_o)
    xo, oo = [_REV[o] for o in xo], [_REV[o] for o in oo]

    def body(*refs):
        k = 0
        p_refs = refs[k:k + n_p]; k += n_p
        x_refs = refs[k:k + n_x]; k += n_x
        st_refs = refs[k:k + n_c]; k += n_c
        dy_refs = refs[k:k + n_o]; k += n_o
        dp_refs = refs[k:k + n_p]; k += n_p
        dx_refs = refs[k:k + n_x]; k += n_x
        dc_refs = refs[k:]
        i = pl.program_id(0)
        blk = _blk(xo[0], i, n)
        p = [r[...] for r in p_refs]
        x = [r[...] for r in x_refs]
        dy = [r[...] for r in dy_refs]
        if n_c:
            @pl.when(i == 0)
            def _():
                for c in dc_refs:
                    c[...] = jnp.zeros_like(c)
            c_in = [r[0] for r in st_refs]
            dc = [c[...] for c in dc_refs]
            _, vjp = jax.vjp(lambda p_, c_, x_: f(blk, p_, c_, x_), p, c_in, x)
            dp, dcin, dx = vjp((dc, dy))
            for c, v in zip(dc_refs, dcin):
                c[...] = v
        else:
            _, vjp = jax.vjp(lambda p_, x_: f(blk, p_, x_), p, x)
            dp, dx = vjp(dy)

        @pl.when(i == 0)
        def _():
            for r, v in zip(dp_refs, dp):
                r[...] = v

        @pl.when(i > 0)
        def _():
            for r, v in zip(dp_refs, dp):
                r[...] += v
        for r, v in zip(dx_refs, dx):
            r[...] = v.astype(r.dtype)

    in_specs = [pl.BlockSpec(p.shape, lambda i, nd=p.ndim: (0,) * nd) for p in params]
    in_specs += [pl.BlockSpec((rb, x.shape[1]), _row(o, n)) for x, o in zip(xs, xo)]
    in_specs += [pl.BlockSpec((1,) + s, lambda i: (n - 1 - i, 0, 0)) for s in carry_sds]
    in_specs += [pl.BlockSpec((rb, d.shape[1]), _row(o, n)) for d, o in zip(dys, oo)]
    out_specs = [pl.BlockSpec(p.shape, lambda i, nd=p.ndim: (0,) * nd) for p in params]
    out_specs += [pl.BlockSpec((rb, x.shape[1]), _row(o, n)) for x, o in zip(xs, xo)]
    out_shape = [jax.ShapeDtypeStruct(p.shape, f32) for p in params]
    out_shape += [jax.ShapeDtypeStruct(x.shape, x.dtype) for x in xs]
    res = pl.pallas_call(
        body, name=name + "_bwd", grid=(n,), in_specs=in_specs, out_specs=out_specs, out_shape=out_shape,
        scratch_shapes=[pltpu.VMEM(s, f32) for s in carry_sds], compiler_params=_cparams(1),
    )(*params, *xs, *states, *dys)
    return list(res[:n_p]), list(res[n_p:])


def blocked_op(name, f, params, xs, out_sds, order="asc", carry_sds=(), rb=RB):
    carry_sds = tuple(carry_sds)

    @jax.custom_vjp
    def op(params, xs):
        return tuple(_blocked_fwd(name, f, order, rb, params, xs, out_sds, carry_sds)[0])

    def fwd(params, xs):
        ys, states = _blocked_fwd(name, f, order, rb, params, xs, out_sds, carry_sds)
        return tuple(ys), (params, xs, states)

    def bwd(res, dys):
        params, xs, states = res
        dp, dx = _blocked_bwd(name, f, order, rb, params, xs, states, list(dys), carry_sds)
        return list(dp), list(dx)

    op.defvjp(fwd, bwd)
    return op(list(params), list(xs))


def _cscan_call(name, order, asc, a, xr, xi, sr=None, si=None):
    t = xr.shape[0]
    n = t // RB
    tile = xr.shape[1:]
    xspec = pl.BlockSpec((RB,) + tile, lambda i: (_blk(order, i, n), 0, 0))
    aspec = pl.BlockSpec(a.shape, lambda i: (0, 0, 0))
    plane = jax.ShapeDtypeStruct(xr.shape, f32)

    def rowidx(tt):
        return tt if asc else RB - 1 - tt

    if sr is None:
        def body(a_ref, xr_ref, xi_ref, sr_ref, si_ref, c_ref):
            i = pl.program_id(0)

            @pl.when(i == 0)
            def _():
                c_ref[...] = jnp.zeros_like(c_ref)
            ar = a_ref[0]
            ai = a_ref[1]

            def step(tt, carry):
                cr, ci = carry
                r = rowidx(tt)
                nr = ar * cr - ai * ci + xr_ref[r]
                ni = ar * ci + ai * cr + xi_ref[r]
                sr_ref[r] = nr
                si_ref[r] = ni
                return nr, ni
            cr, ci = lax.fori_loop(0, RB, step, (c_ref[0], c_ref[1]), unroll=8)
            c_ref[0] = cr
            c_ref[1] = ci

        return pl.pallas_call(
            body, name=name, grid=(n,), in_specs=[aspec, xspec, xspec], out_specs=[xspec, xspec],
            out_shape=[plane, plane], scratch_shapes=[pltpu.VMEM(a.shape, f32)], compiler_params=_cparams(1),
        )(a, xr, xi)

    def body(a_ref, xr_ref, xi_ref, sr_ref, si_ref, gr_ref, gi_ref, da_ref, c_ref):
        i = pl.program_id(0)

        @pl.when(i == 0)
        def _():
            c_ref[...] = jnp.zeros_like(c_ref)
            da_ref[...] = jnp.zeros_like(da_ref)
        ar = a_ref[0]
        ai = a_ref[1]

        def step(tt, carry):
            gr, gi, dar, dai = carry
            r = rowidx(tt)
            vr = sr_ref[r]
            vi = si_ref[r]
            dar = dar + gr * vr + gi * vi
            dai = dai + gi * vr - gr * vi
            nr = xr_ref[r] + ar * gr + ai * gi
            ni = xi_ref[r] + ar * gi - ai * gr
            gr_ref[r] = nr
            gi_ref[r] = ni
            return nr, ni, dar, dai
        z = jnp.zeros(tile, f32)
        gr, gi, dar, dai = lax.fori_loop(0, RB, step, (c_ref[0], c_ref[1], z, z), unroll=8)
        c_ref[0] = gr
        c_ref[1] = gi
        da_ref[0] += dar
        da_ref[1] += dai

    return pl.pallas_call(
        body, name=name, grid=(n,), in_specs=[aspec] + [xspec] * 4, out_specs=[xspec, xspec, aspec],
        out_shape=[plane, plane, jax.ShapeDtypeStruct(a.shape, f32)],
        scratch_shapes=[pltpu.VMEM(a.shape, f32)], compiler_params=_cparams(1),
    )(a, xr, xi, sr, si)


def cscan(name, d, a, xr, xi):
    order = "d1" if d else "asc"

    @jax.custom_vjp
    def op(a, xr, xi):
        return tuple(_cscan_call(name, order, d == 0, a, xr, xi))

    def fwd(a, xr, xi):
        sr, si = op(a, xr, xi)
        return (sr, si), (a, sr, si)

    def bwd(res, ds):
        a, sr, si = res
        gr, gi, da = _cscan_call(name + "_bwd", _REV[order], d != 0, a, ds[0], ds[1], sr, si)
        return da, gr, gi

    op.defvjp(fwd, bwd)
    return op(a, xr, xi)


def _bd_call(name, a, b, mode, k=None):
    t = a.shape[0]
    if mode == "tn":
        ck, cn = a.shape[1] // k, b.shape[1] // k

        def body(a_ref, b_ref, o_ref):
            o_ref[0] = _dot(a_ref[...], b_ref[...], _TN, True)

        return pl.pallas_call(
            body, name=name, grid=(k,),
            in_specs=[pl.BlockSpec((t, ck), lambda j: (0, j)), pl.BlockSpec((t, cn), lambda j: (0, j))],
            out_specs=pl.BlockSpec((1, ck, cn), lambda j: (j, 0, 0)),
            out_shape=jax.ShapeDtypeStruct((k, ck, cn), f32), compiler_params=_cparams(1),
        )(a, b)
    k, ck, cn = b.shape
    tm = _pick(t, 1088, _SUB)
    win, wout, dn = (ck, cn, _NN) if mode == "nn" else (cn, ck, _NT)

    def body(a_ref, b_ref, o_ref):
        o_ref[...] = _dot(a_ref[...], b_ref[0], dn, True)

    return pl.pallas_call(
        body, name=name, grid=(t // tm, k),
        in_specs=[pl.BlockSpec((tm, win), lambda i, j: (i, j)), pl.BlockSpec((1, ck, cn), lambda i, j: (j, 0, 0))],
        out_specs=pl.BlockSpec((tm, wout), lambda i, j: (i, j)),
        out_shape=jax.ShapeDtypeStruct((t, k * wout), f32), compiler_params=_cparams(2),
    )(a, b)


def bd_mm(name, a, b):
    @jax.custom_vjp
    def op(a, b):
        return _bd_call(name, a, b, "nn")

    def fwd(a, b):
        return op(a, b), (a, b)

    def bwd(res, g):
        a, b = res
        return _bd_call(name + "_da", g, b, "nt"), _bd_call(name + "_db", a, g, "tn", b.shape[0])

    op.defvjp(fwd, bwd)
    return op(a, b)


def _rscan_call(name, order, asc, a, x, hp=None):
    t = x.shape[0]
    n = t // RB
    cshape = x.shape[1:]
    xspec = pl.BlockSpec((RB,) + cshape, lambda i: (_blk(order, i, n), 0, 0))

    def rowidx(tt):
        return tt if asc else RB - 1 - tt

    if hp is None:
        def body(a_ref, x_ref, h_ref, hp_ref, c_ref):
            i = pl.program_id(0)

            @pl.when(i == 0)
            def _():
                c_ref[...] = jnp.zeros_like(c_ref)

            def step(tt, h):
                r = rowidx(tt)
                hp_ref[r] = h
                h = a_ref[r] * h + x_ref[r]
                h_ref[r] = h
                return h
            c_ref[...] = lax.fori_loop(0, RB, step, c_ref[...], unroll=8)

        return pl.pallas_call(
            body, name=name, grid=(n,), in_specs=[xspec, xspec], out_specs=[xspec, xspec],
            out_shape=[jax.ShapeDtypeStruct(x.shape, f32)] * 2, scratch_shapes=[pltpu.VMEM(cshape, f32)],
            compiler_params=_cparams(1),
        )(a, x)

    def body(a_ref, x_ref, hp_ref, da_ref, db_ref, c_ref):
        i = pl.program_id(0)

        @pl.when(i == 0)
        def _():
            c_ref[...] = jnp.zeros_like(c_ref)

        def step(tt, c):
            r = rowidx(tt)
            g = x_ref[r] + c
            db_ref[r] = g
            da_ref[r] = g * hp_ref[r]
            return a_ref[r] * g
        c_ref[...] = lax.fori_loop(0, RB, step, c_ref[...], unroll=8)

    return pl.pallas_call(
        body, name=name, grid=(n,), in_specs=[xspec, xspec, xspec], out_specs=[xspec, xspec],
        out_shape=[jax.ShapeDtypeStruct(x.shape, f32)] * 2, scratch_shapes=[pltpu.VMEM(cshape, f32)],
        compiler_params=_cparams(1),
    )(a, x, hp)


def rscan(name, d, a, x):
    order = "d1" if d else "asc"

    @jax.custom_vjp
    def op(a, x):
        return _rscan_call(name, order, d == 0, a, x)[0]

    def fwd(a, x):
        h, hp = _rscan_call(name, order, d == 0, a, x)
        return h, (a, hp)

    def bwd(res, dh):
        a, hp = res
        da, db = _rscan_call(name + "_bwd", _REV[order], d != 0, a, dh, hp)
        return da, db

    op.defvjp(fwd, bwd)
    return op(a, x)


def _mod_row(blk, mod, bm):
    return jnp.where(blk == 0, mod[1:2], mod[0:1]) + bm


def _f_silu(blk, p, x):
    return [_silu(x[0]).astype(bf16)]


def _f_normmod(blk, p, x):
    nw, mod, bm = p
    d = nw.shape[1]
    r = _mod_row(blk, mod, bm)
    return [(_rms(x[0], nw) * (1.0 + r[:, d:2 * d]) + r[:, :d]).astype(bf16)]


def _f_resid(blk, p, x):
    mod, bm = p
    d = x[0].shape[1]
    r = _mod_row(blk, mod, bm)
    return [x[0] + r[:, 2 * d:] * x[1]]


def _f_mix(blk, p, x):
    bg, = p
    gp = x[0]
    d = x[1].shape[1]
    acc = None
    for k in range(4):
        t = jax.nn.sigmoid(gp[:, k * d:(k + 1) * d] + bg[:, k * d:(k + 1) * d]) * x[1 + k]
        acc = t if acc is None else acc + t
    return [acc.astype(bf16)]


def _tri(rev):
    row = lax.broadcasted_iota(jnp.int32, (CHUNK, CHUNK), 0)
    col = lax.broadcasted_iota(jnp.int32, (CHUNK, CHUNK), 1)
    return (col >= row) if rev else (col <= row)


def _chunk_ids(rev):
    ids = list(range(RB // CHUNK))
    return ids[::-1] if rev else ids


def _f_hg(rev):
    def f(blk, p, c, x):
        lb, = p
        st, = c
        qi, fr = x
        q = _silu(qi[:, :BR_W])
        v = qi[:, BR_W:]
        fg = lb + (1.0 - lb) * jax.nn.sigmoid(fr)
        logf = jnp.log(fg)
        k = 1.0 - fg
        m = _tri(rev)
        mf = m.astype(f32)
        outs = [None] * (RB // CHUNK)
        for ci in _chunk_ids(rev):
            sl = slice(CHUNK * ci, CHUNK * ci + CHUNK)
            lf = logf[sl]
            b = _dot(mf, lf, hi=True)
            bend = jnp.sum(lf, axis=0, keepdims=True)
            mid = 0.5 * bend
            qe = q[sl] * jnp.exp(b - mid)
            ke = k[sl] * jnp.exp(mid - b)
            kd = k[sl] * jnp.exp(bend - b)
            qb = q[sl] * jnp.exp(b)
            dec = jnp.exp(bend)
            vc = v[sl]
            oh, ns = [], []
            for hh in range(HG_HEADS):
                cs = slice(HG_DK * hh, HG_DK * hh + HG_DK)
                sth = st[cs]
                att = jnp.where(m, _dot(qe[:, cs], ke[:, cs], _NT), 0.0)
                oh.append(_dot(att, vc[:, cs]) + _dot(qb[:, cs], sth, _NT))
                ns.append(sth * dec[:, cs] + _dot(vc[:, cs], kd[:, cs], _TN))
            st = jnp.concatenate(ns, axis=0)
            outs[ci] = jnp.concatenate(oh, axis=1)
        return [st], [jnp.concatenate(outs, axis=0)]
    return f


def _both(f0, f1, n_p, n_x):
    def f(blk, p, c, x):
        c0, y0 = f0(blk, p[:n_p], c[:1], x[:n_x])
        c1, y1 = f1(blk, p[n_p:], c[1:], x[n_x:])
        return c0 + c1, y0 + y1
    return f


_BOTH_ORDERS = (["asc", "asc", "d1", "d1"], ["asc", "d1"])


def _f_hg_final(blk, p, x):
    nw, = p
    o = x[0] + x[1]
    parts = []
    for hh in range(HG_HEADS):
        cs = slice(HG_DK * hh, HG_DK * hh + HG_DK)
        parts.append(_rms(o[:, cs], nw[:, cs]))
    return [(jnp.concatenate(parts, axis=1) * _silu(x[2])).astype(bf16)]


def _conv(x, cw, cb, blk):
    rows = x.shape[0]
    r = lax.broadcasted_iota(jnp.int32, (rows, 1), 0)
    rm = jnp.where(blk == 0, r, r % CHUNK)
    seg = jnp.where(blk == 0, rows, CHUNK)

    def vmask(o):
        return ((rm + o >= 0) & (rm + o < seg)).astype(f32)

    def shifted(o):
        @jax.custom_vjp
        def sh(x, mo, mn):
            return pltpu.roll(x, (-o) % rows, 0) * mo

        def fwd(x, mo, mn):
            return sh(x, mo, mn), (mo, mn)

        def bwd(res, g):
            mo, mn = res
            return pltpu.roll(g, o % rows, 0) * mn, jnp.zeros_like(mo), jnp.zeros_like(mn)
        sh.defvjp(fwd, bwd)
        return sh(x, vmask(o), vmask(-o))

    lo = (CONV_W - 1) // 2
    out = cb
    for k in range(CONV_W):
        o = k - lo
        out = out + cw[k:k + 1] * (x if o == 0 else shifted(o))
    return out


def _f_lru_a(blk, p, x):
    cw, cb, wg, gb, lam = p
    xc = _conv(x[0], cw, cb, blk)
    n_chunks = BR_W // _LANE
    xk = [xc[:, _LANE * k:_LANE * (k + 1)] for k in range(n_chunks)]

    def gate(j):
        pre = jnp.concatenate([_dot(xk[k], wg[j * n_chunks + k], hi=True) for k in range(n_chunks)], axis=1)
        return jax.nn.sigmoid(pre + gb[:, BR_W * j:BR_W * (j + 1)])

    outs = []
    for d in range(2):
        r = gate(2 * d)
        ig = gate(2 * d + 1)
        log_a = -LRU_C * r * _softplus(-lam[d:d + 1])
        outs.append(jnp.exp(log_a))
        outs.append(jnp.sqrt(_one_minus_exp(2.0 * log_a)) * (ig * xc))
    return outs


def _f_lru_c(blk, p, x):
    return [((x[0] + x[1]) * _silu(x[2])).astype(bf16)]


def _f_s5_c1(blk, p, x):
    dsk, = p
    return [jax.nn.gelu(x[0] + x[1] + x[2] + x[3] + dsk * x[4])]


def _f_s5_c2(blk, p, x):
    bglu, = p
    return [(x[0] * jax.nn.sigmoid(x[1] + bglu) * _silu(x[2])).astype(bf16)]


def _f_m2_a(blk, p, x):
    cw, cb, dtb = p
    return [_silu(_conv(x[0], cw, cb, blk)), _softplus(x[1] + dtb)]


def _f_ssd(d):
    rev = d == 1
    hpg = M2_HEADS // M2_GROUPS

    def f(blk, p, c, x):
        alog, = p
        st, = c
        xbc, dtp = x
        a = -jnp.exp(alog[:, M2_HEADS * d:M2_HEADS * (d + 1)])
        dt = dtp[:, M2_HEADS * d:M2_HEADS * (d + 1)]
        xs = xbc[:, :BR_W]
        bm = xbc[:, BR_W:BR_W + M2_GROUPS * M2_STATE]
        cm = xbc[:, BR_W + M2_GROUPS * M2_STATE:]
        m = _tri(rev)
        mf = m.astype(f32)
        outs = [None] * (RB // CHUNK)
        for ci in _chunk_ids(rev):
            sl = slice(CHUNK * ci, CHUNK * ci + CHUNK)
            dtc = dt[sl]
            dta = dtc * a
            cum = _dot(mf, dta, hi=True)
            cend = jnp.sum(dta, axis=0, keepdims=True)
            cum_t = cum.T
            dt_t = dtc.T
            ys, ns = [], []
            for g in range(M2_GROUPS):
                bmg = bm[sl, M2_STATE * g:M2_STATE * (g + 1)]
                cmg = cm[sl, M2_STATE * g:M2_STATE * (g + 1)]
                scores = _dot(cmg, bmg, _NT)
                for r in range(hpg):
                    h = g * hpg + r
                    hs = slice(M2_HEADDIM * h, M2_HEADDIM * (h + 1))
                    ci_col = cum[:, h:h + 1]
                    decay = jnp.exp(jnp.where(m, ci_col - cum_t[h:h + 1, :], -1e30))
                    w = scores * decay * dt_t[h:h + 1, :]
                    xh = xs[sl, hs]
                    sth = st[hs]
                    ys.append(_dot(w, xh) + _dot(cmg, sth, _NT) * jnp.exp(ci_col))
                    wx = (jnp.exp(cend[:, h:h + 1] - ci_col) * dtc[:, h:h + 1]) * xh
                    ns.append(jnp.exp(cend[:, h:h + 1]) * sth + _dot(wx, bmg, _TN))
            st = jnp.concatenate(ns, axis=0)
            outs[ci] = jnp.concatenate(ys, axis=1)
        return [st], [jnp.concatenate(outs, axis=0)]
    return f


def _f_m2_c(blk, p, x):
    dsk, nw = p
    y = x[0] + x[1] + dsk * x[2][:, :BR_W]
    return [_rms(y * _silu(x[3]), nw).astype(bf16)]


def _f_loss(blk, p, x):
    fnw, = p
    err = _rms(x[0], fnw) - x[1]
    return [0.5 * jnp.mean(err * err, axis=-1, keepdims=True)]


def _blockdiag(w):
    g, a, b = w.shape
    return jnp.einsum("gab,gh->gahb", w, jnp.eye(g, dtype=w.dtype)).reshape(g * a, g * b)


def _s5_params(l, w):
    a_scan, cds = [], []
    per = _LANE // S5_GROUP

    def chunks(m):
        return jnp.stack([_blockdiag(m[k * per:(k + 1) * per]) for k in range(S5_GROUPS // per)])

    b_re = jnp.transpose(w["s5_b_re"][l], (0, 2, 1))
    b_im = jnp.transpose(w["s5_b_im"][l], (0, 2, 1))
    bd = [chunks(b_re), chunks(b_im)]
    c_re = jnp.transpose(w["s5_c_re"][l], (0, 2, 1))
    c_im = jnp.transpose(w["s5_c_im"][l], (0, 2, 1))
    for d in range(2):
        lam_re = w["s5_a_re"][l, d]
        lam_im = w["s5_a_im"][l, d]
        step = jnp.exp(w["s5_log_step"][l, d])[:, None]
        mag = jnp.exp(lam_re * step)
        ab_re = mag * jnp.cos(lam_im * step)
        ab_im = mag * jnp.sin(lam_im * step)
        den = lam_re * lam_re + lam_im * lam_im
        nr = ab_re - 1.0
        co_re = (nr * lam_re + ab_im * lam_im) / den
        co_im = (ab_im * lam_re - nr * lam_im) / den
        n_state = S5_GROUPS * S5_STATE
        a_scan.append(jnp.stack([ab_re.reshape(_SUB, n_state // _SUB), ab_im.reshape(_SUB, n_state // _SUB)]))
        cp_re = c_re * co_re[:, :, None] - c_im * co_im[:, :, None]
        cp_im = c_re * co_im[:, :, None] + c_im * co_re[:, :, None]
        cds.append([chunks(cp_re), -chunks(cp_im)])
    return a_scan, bd, cds


def _lru_gate(l, w):
    gw = w["lru_gate_w"][l]
    per = _LANE // (BR_W // LRU_BLOCKS)
    chunks = [_blockdiag(gw[d, g, k * per:(k + 1) * per])
              for d in range(2) for g in range(2) for k in range(LRU_BLOCKS // per)]
    return jnp.stack(chunks), w["lru_gate_b"][l].reshape(1, -1)


def _pad_cols(a, n):
    return jnp.pad(a, ((0, 0), (0, n - a.shape[1])))


IN_SIZES = (BR_W,) * 9 + (M2_XBC, 2 * M2_HEADS, BR_W)
IN_OFFS = tuple(sum(IN_SIZES[:i]) for i in range(len(IN_SIZES) + 1))
IN_GROUPS = (("hg_qi", 0, 2, 1024), ("hg_ff", 2, 1, 512), ("hg_fb", 3, 1, 512), ("hg_z", 4, 1, 512),
             ("s5_u", 5, 1, 512), ("s5_z", 6, 1, 512), ("lru_x", 7, 1, 512), ("lru_z", 8, 1, 512),
             ("m2_xbc", 9, 1, 768), ("m2_dt", 10, 1, 128), ("m2_z", 11, 1, 512))


def _new_slots(big):
    slots = {n: jnp.zeros(w.shape, f32) for n, w in big.items() if n not in ("w_in", "w_gate")}
    n_layers, d_model = big["w_in"].shape[:2]
    slots["w_in"] = [{name: jnp.zeros((d_model, width), f32) for name, _, _, width in IN_GROUPS} for _ in range(n_layers)]
    slots["w_gate"] = jnp.zeros((n_layers, d_model, 4 * d_model), f32)
    return slots


def _slot_grads(g):
    out = dict(g)
    out["w_in"] = jnp.stack([
        jnp.concatenate([gl[name][:, :IN_OFFS[s0 + ns] - IN_OFFS[s0]] for name, s0, ns, _ in IN_GROUPS], axis=1)
        for gl in g["w_in"]])
    n_layers, d_model = g["w_gate"].shape[:2]
    out["w_gate"] = jnp.transpose(g["w_gate"].reshape(n_layers, d_model, 4, d_model), (0, 2, 1, 3))
    return out


def _forward(p, big, slots, x, ctx, c, target):
    n_layers = p["norm_w"].shape[0]
    d_model = x.shape[-1]
    xa = jnp.concatenate([ctx, x], axis=0)
    t = xa.shape[0]
    cc = jnp.concatenate([c, p["c_ctx"][None], jnp.zeros((_SUB - 2, d_model), f32)], axis=0)
    lb_all = jnp.cumsum(jax.nn.softmax(p["hg_lb_logits"], axis=0), axis=0)
    scc, = blocked_op("silu_c", _f_silu, [], [cc], [(d_model, bf16)], rb=_SUB)

    for l in range(n_layers):
        tag = "l%d_" % l
        mod = mm(tag + "mod", scc, big["w_mod"][l], slots["w_mod"][l])
        bm = p["b_mod"][l][None]
        h, = blocked_op(tag + "normmod", _f_normmod, [p["norm_w"][l][None], mod, bm], [xa], [(d_model, bf16)])
        gnames = [g[0] for g in IN_GROUPS]
        wvs = [_pad_cols(big["w_in"][l][:, IN_OFFS[s0]:IN_OFFS[s0 + ns]], width) for _, s0, ns, width in IN_GROUPS]
        u = dict(zip(gnames, multi_mm(tag + "in_", gnames, h, wvs, [slots["w_in"][l][g] for g in gnames])))

        o_dirs = blocked_op(tag + "hg", _both(_f_hg(False), _f_hg(True), 1, 2), [lb_all[l, 0][None], lb_all[l, 1][None]],
                            [u["hg_qi"], u["hg_ff"], u["hg_qi"], u["hg_fb"]], [(BR_W, f32)] * 2,
                            order=_BOTH_ORDERS, carry_sds=[(BR_W, HG_DK)] * 2)
        y_hg, = blocked_op(tag + "hg_fin", _f_hg_final, [p["hg_norm"][l][None]], list(o_dirs) + [u["hg_z"]], [(BR_W, bf16)])

        a_scan, bd, cds = _s5_params(l, p)
        n_state = S5_GROUPS * S5_STATE
        bu = [bd_mm(tag + "s5_bu%d" % part, u["s5_u"], bd[part]).reshape(t, _SUB, n_state // _SUB) for part in range(2)]
        ysum = []
        for d in range(2):
            s = cscan(tag + "s5_scan%d" % d, d, a_scan[d], bu[0], bu[1])
            for part in range(2):
                ysum.append(bd_mm(tag + "s5_c%d%d" % (d, part), s[part].reshape(t, n_state), cds[d][part]))
        g5, = blocked_op(tag + "s5_c1", _f_s5_c1, [p["s5_d"][l][None]], ysum + [u["s5_u"]], [(BR_W, f32)])
        gl = mm(tag + "s5_glu", g5, big["s5_w_glu"][l], slots["s5_w_glu"][l])
        y_s5, = blocked_op(tag + "s5_c2", _f_s5_c2, [p["s5_b_glu"][l][None]], [g5, gl, u["s5_z"]], [(BR_W, bf16)])

        wg, gb = _lru_gate(l, p)
        ab = blocked_op(tag + "lru_a", _f_lru_a,
                        [p["lru_conv_w"][l], p["lru_conv_b"][l][None], wg, gb, p["lru_lam"][l]],
                        [u["lru_x"]], [(BR_W, f32)] * 4)
        hs = []
        for d in range(2):
            a3 = ab[2 * d].reshape(t, 4, BR_W // 4)
            b3 = ab[2 * d + 1].reshape(t, 4, BR_W // 4)
            hs.append(rscan(tag + "lru_scan%d" % d, d, a3, b3).reshape(t, BR_W))
        y_lru, = blocked_op(tag + "lru_c", _f_lru_c, [], hs + [u["lru_z"]], [(BR_W, bf16)])

        dtb = _pad_cols(p["m2_dt_bias"][l].reshape(1, -1), _LANE)
        xbc, dtp = blocked_op(tag + "m2_a", _f_m2_a, [p["m2_conv_w"][l], p["m2_conv_b"][l][None], dtb],
                              [u["m2_xbc"], u["m2_dt"]], [(M2_XBC, f32), (_LANE, f32)])
        alog = _pad_cols(p["m2_a_log"][l].reshape(1, -1), _LANE)
        y_dirs = blocked_op(tag + "ssd", _both(_f_ssd(0), _f_ssd(1), 1, 2), [alog, alog], [xbc, dtp, xbc, dtp],
                            [(BR_W, f32)] * 2, order=_BOTH_ORDERS, carry_sds=[(BR_W, M2_STATE)] * 2)
        dsk = jnp.repeat(p["m2_d"][l], M2_HEADDIM)[None]
        y_m2, = blocked_op(tag + "m2_c", _f_m2_c, [dsk, p["m2_norm"][l][None]], list(y_dirs) + [xbc, u["m2_z"]], [(BR_W, bf16)])

        wg_all = jnp.transpose(big["w_gate"][l], (1, 0, 2)).reshape(d_model, 4 * d_model)
        gp = mm(tag + "gate", h, wg_all, slots["w_gate"][l])
        bs = [mm(tag + "br%d" % k, yk, big["w_branch"][l, k], slots["w_branch"][l, k])
              for k, yk in enumerate((y_hg, y_s5, y_lru, y_m2))]
        mix, = blocked_op(tag + "mix", _f_mix, [p["b_gate"][l].reshape(1, -1)], [gp] + bs, [(d_model, bf16)])
        o = mm(tag + "out", mix, big["w_out"][l], slots["w_out"][l])
        xa, = blocked_op(tag + "resid", _f_resid, [mod, bm], [xa, o], [(d_model, f32)])

    rl, = blocked_op("loss", _f_loss, [p["final_norm"][None]], [xa[ctx.shape[0]:], target], [(1, f32)])
    return jnp.sum(rl)


_MESH = pl.DeviceIdType.MESH
_ANY = pl.BlockSpec(memory_space=pl.ANY)
W_PACK = 1024


def _place():
    x, y, c = lax.axis_index("x"), lax.axis_index("y"), lax.axis_index("c")
    chips = [(x, 1 - y), (1 - x, y), (1 - x, 1 - y)]
    return x, y, c, chips


def _rcopy(src, dst, ssem, rsem, k, to):
    return pltpu.make_async_remote_copy(src_ref=src, dst_ref=dst, send_sem=ssem.at[k], recv_sem=rsem.at[k],
                                        device_id=to, device_id_type=_MESH)


def gather_shards(xs):
    n = len(xs)

    def body(*refs):
        x_refs, o_refs = refs[:n], refs[n:2 * n]
        ssem, rsem, lsem = refs[2 * n:]
        x, y, c, chips = _place()
        j = 2 * x + y
        sib = (x, y, 1 - c)
        mine = [pltpu.make_async_copy(x_refs[a], o_refs[a].at[j], lsem.at[a]) for a in range(n)]
        for cp in mine:
            cp.start()
        first = [_rcopy(x_refs[a].at[c], o_refs[a].at[j, c], ssem, rsem, 6 * a + r, (*chips[r], c))
                 for r in range(3) for a in range(n)]
        for cp in first:
            cp.start()
        passed = []
        for r in range(3):
            jr = j ^ (r + 1)
            for a in range(n):
                _rcopy(x_refs[a].at[c], o_refs[a].at[jr, c], ssem, rsem, 6 * a + r, sib).wait_recv()
                cp = _rcopy(o_refs[a].at[jr, c], o_refs[a].at[jr, c], ssem, rsem, 6 * a + 3 + r, sib)
                cp.start()
                passed.append(cp)
        for r in range(3):
            jr = j ^ (r + 1)
            for a in range(n):
                _rcopy(x_refs[a].at[c], o_refs[a].at[jr, 1 - c], ssem, rsem, 6 * a + 3 + r, sib).wait_recv()
        for cp in first + passed:
            cp.wait_send()
        for cp in mine:
            cp.wait()

    return pl.pallas_call(
        body, name="gather_shards", out_shape=[jax.ShapeDtypeStruct((4,) + x.shape, x.dtype) for x in xs],
        in_specs=[_ANY] * n, out_specs=[_ANY] * n,
        scratch_shapes=[pltpu.SemaphoreType.DMA((6 * n,)), pltpu.SemaphoreType.DMA((6 * n,)), pltpu.SemaphoreType.DMA((n,))],
    )(*xs)


def sibling_halves(gs):
    n = len(gs)

    def body(*refs):
        g_refs, o_refs = refs[:n], refs[n:2 * n]
        ssem, rsem = refs[2 * n:]
        x, y, c, _ = _place()
        sib = (x, y, 1 - c)
        cps = [_rcopy(g_refs[a].at[k, 1 - c], o_refs[a].at[k], ssem, rsem, 4 * a + k, sib)
               for k in range(4) for a in range(n)]
        for cp in cps:
            cp.start()
        for cp in cps:
            cp.wait()

    return pl.pallas_call(
        body, name="sibling_halves", out_shape=[jax.ShapeDtypeStruct((4,) + g.shape[2:], g.dtype) for g in gs],
        in_specs=[_ANY] * n, out_specs=[_ANY] * n,
        scratch_shapes=[pltpu.SemaphoreType.DMA((4 * n,)), pltpu.SemaphoreType.DMA((4 * n,))],
    )(*gs)


def scatter_chips(ps):
    n = len(ps)

    def body(*refs):
        p_refs, o_refs = refs[:n], refs[n:2 * n]
        ssem, rsem = refs[2 * n:]
        x, y, c, chips = _place()
        j = 2 * x + y
        cps = [_rcopy(p_refs[a].at[j ^ (r + 1)], o_refs[a].at[r], ssem, rsem, 3 * a + r, (*chips[r], c))
               for r in range(3) for a in range(n)]
        for cp in cps:
            cp.start()
        for cp in cps:
            cp.wait()

    return pl.pallas_call(
        body, name="scatter_chips", out_shape=[jax.ShapeDtypeStruct((3,) + p.shape[1:], p.dtype) for p in ps],
        in_specs=[_ANY] * n, out_specs=[_ANY] * n,
        scratch_shapes=[pltpu.SemaphoreType.DMA((3 * n,)), pltpu.SemaphoreType.DMA((3 * n,))],
    )(*ps)


def join_halves(qs):
    n = len(qs)

    def body(*refs):
        o_refs = refs[n:2 * n]
        ssem, rsem = refs[2 * n:]
        x, y, c, _ = _place()
        sib = (x, y, 1 - c)
        cps = [_rcopy(o_refs[a].at[c], o_refs[a].at[c], ssem, rsem, a, sib) for a in range(n)]
        for cp in cps:
            cp.start()
        for a in range(n):
            _rcopy(o_refs[a].at[c], o_refs[a].at[1 - c], ssem, rsem, a, sib).wait_recv()
        for cp in cps:
            cp.wait_send()

    return pl.pallas_call(
        body, name="join_halves", out_shape=[jax.ShapeDtypeStruct(q.shape, q.dtype) for q in qs],
        in_specs=[_ANY] * n, out_specs=[_ANY] * n, input_output_aliases={a: a for a in range(n)},
        scratch_shapes=[pltpu.SemaphoreType.DMA((n,)), pltpu.SemaphoreType.DMA((n,))],
    )(*qs)


def _rows_block(r):
    return _pick(r, 256, _SUB)


def add_sibling(tag, g, r1, place, out_dtype):
    _, _, rows, w = g.shape
    rb = _rows_block(rows)

    def body(pl_ref, g_ref, r_ref, o_ref):
        o_ref[...] = (g_ref[0] + r_ref[...]).astype(out_dtype)

    return pl.pallas_call(
        body, name="add_sibling_" + tag, out_shape=jax.ShapeDtypeStruct((4, rows, w), out_dtype),
        grid_spec=pltpu.PrefetchScalarGridSpec(
            num_scalar_prefetch=1, grid=(4, rows // rb),
            in_specs=[pl.BlockSpec((1, 1, rb, w), lambda k, i, s: (k, s[1], i, 0)),
                      pl.BlockSpec((1, rb, w), lambda k, i, s: (k, i, 0))],
            out_specs=pl.BlockSpec((1, rb, w), lambda k, i, s: (k, i, 0))),
        compiler_params=_cparams(2),
    )(place, g, r1)


def add_chips(tag, p, r2, place):
    _, rows, w = p.shape
    rb = _rows_block(rows)

    def body(pl_ref, p_ref, r_ref, o_ref):
        j = pl_ref[0]
        own = p_ref[0].astype(f32)
        others = [r_ref[0].astype(f32), r_ref[1].astype(f32), r_ref[2].astype(f32)]
        acc = None
        for k in range(4):
            rel = k ^ j
            t = jnp.where(rel == 0, own, jnp.where(rel == 1, others[0], jnp.where(rel == 2, others[1], others[2])))
            acc = t if acc is None else acc + t
        o_ref[0] = acc

    return pl.pallas_call(
        body, name="add_chips_" + tag, out_shape=jax.ShapeDtypeStruct((2, rows, w), f32),
        grid_spec=pltpu.PrefetchScalarGridSpec(
            num_scalar_prefetch=1, grid=(rows // rb,),
            in_specs=[pl.BlockSpec((1, rb, w), lambda i, s: (s[0], i, 0)),
                      pl.BlockSpec((3, rb, w), lambda i, s: (0, i, 0))],
            out_specs=pl.BlockSpec((1, rb, w), lambda i, s: (s[1], i, 0))),
        compiler_params=_cparams(1),
    )(place, p, r2)


def adamw(tag, g, w, m, v):
    rows, wd = g.shape
    rb = _rows_block(rows)

    def body(g_ref, w_ref, m_ref, v_ref, d_ref, nm_ref, nv_ref):
        gv = g_ref[...]
        nm = ADAM_B1 * m_ref[...] + (1.0 - ADAM_B1) * gv
        nv = ADAM_B2 * v_ref[...] + (1.0 - ADAM_B2) * (gv * gv)
        m_hat = nm / (1.0 - ADAM_B1 ** ADAM_STEP)
        v_hat = nv / (1.0 - ADAM_B2 ** ADAM_STEP)
        d_ref[...] = -ADAM_LR * (m_hat / (jnp.sqrt(v_hat) + ADAM_EPS) + ADAM_WD * w_ref[...])
        nm_ref[...] = nm
        nv_ref[...] = nv

    spec = pl.BlockSpec((rb, wd), lambda i: (i, 0))
    return pl.pallas_call(
        body, name="adamw_" + tag, grid=(rows // rb,), in_specs=[spec] * 4, out_specs=[spec] * 3,
        out_shape=[jax.ShapeDtypeStruct(g.shape, f32)] * 3, compiler_params=_cparams(1),
    )(g, w, m, v)


WEIGHTS = ("c_ctx", "norm_w", "w_mod", "b_mod", "w_in", "hg_lb_logits", "hg_norm", "s5_a_re", "s5_a_im", "s5_log_step",
           "s5_b_re", "s5_b_im", "s5_c_re", "s5_c_im", "s5_d", "s5_w_glu", "s5_b_glu", "lru_conv_w", "lru_conv_b",
           "lru_gate_w", "lru_gate_b", "lru_lam", "m2_conv_w", "m2_conv_b", "m2_dt_bias", "m2_a_log", "m2_d", "m2_norm",
           "w_branch", "w_gate", "b_gate", "w_out", "final_norm")
SHARD_AXIS = {"w_mod": 2, "w_in": 2, "hg_lb_logits": 2, "s5_w_glu": 1, "lru_conv_w": 2, "lru_lam": 2, "m2_conv_w": 2,
              "w_branch": 3, "w_gate": 2, "b_gate": 2, "w_out": 1}
BIG = ("w_mod", "w_in", "s5_w_glu", "w_branch", "w_gate", "w_out")
N_CHIPS = 4


def _to_rows(flat, row_unit):
    n = flat.shape[-1]
    per = 2 * row_unit * W_PACK
    total = -(-n // per) * per
    flat = jnp.pad(flat, [(0, 0)] * (flat.ndim - 1) + [(0, total - n)])
    return flat.reshape(flat.shape[:-1] + (2, total // (2 * W_PACK), W_PACK))


SMALL_SHARDED = tuple(n for n in WEIGHTS if n in SHARD_AXIS and n not in BIG)
SMALL_REPLICATED = tuple(n for n in WEIGHTS if n not in SHARD_AXIS)


def _chip_slices(a, axis):
    width = a.shape[axis] // N_CHIPS
    return jnp.stack([lax.slice_in_dim(a, k * width, (k + 1) * width, axis=axis) for k in range(N_CHIPS)])


def _gather_weights(local):
    small = jnp.concatenate([lax.bitcast_convert_type(local[n], bf16).reshape(-1) for n in SMALL_SHARDED])
    got = gather_shards([local[n].astype(bf16) for n in BIG] + [_to_rows(small, 16)])
    full = {}
    for n, g in zip(BIG, got):
        full[n] = jnp.concatenate([g[j] for j in range(N_CHIPS)], axis=SHARD_AXIS[n])
    flat, off = got[-1].reshape(N_CHIPS, -1), 0
    for n in SMALL_SHARDED:
        shp = local[n].shape
        size = 2 * math.prod(shp)
        part = lax.bitcast_convert_type(flat[:, off:off + size].reshape((N_CHIPS,) + shp + (2,)), f32)
        off += size
        full[n] = jnp.concatenate([part[j] for j in range(N_CHIPS)], axis=SHARD_AXIS[n])
    return full


def _pack_small(vals, extra):
    return jnp.concatenate([vals[n].reshape(-1) for n in SMALL_SHARDED + SMALL_REPLICATED] + [extra.reshape(1)])


def _pack_small_grads(grads, loss):
    rep = [grads[n].reshape(-1) for n in SMALL_REPLICATED] + [loss.reshape(1)]
    sh = [_chip_slices(grads[n], SHARD_AXIS[n]).reshape(N_CHIPS, -1) for n in SMALL_SHARDED]
    return jnp.concatenate(sh + [jnp.broadcast_to(r, (N_CHIPS,) + r.shape) for r in rep], axis=1)


def _unpack_small(flat, like):
    out, off = {}, 0
    for n in SMALL_SHARDED + SMALL_REPLICATED:
        size = math.prod(like[n].shape)
        out[n] = flat[off:off + size].reshape(like[n].shape)
        off += size
    return out, flat[off]


def _reduce_grads(tags, gs):
    place = jnp.stack([2 * lax.axis_index("x") + lax.axis_index("y"), lax.axis_index("c")]).astype(jnp.int32)
    pairs = [add_sibling(t, g, r, place, bf16 if t in BIG else f32) for t, g, r in zip(tags, gs, sibling_halves(gs))]
    quads = [add_chips(t, p, r, place) for t, p, r in zip(tags, pairs, scatter_chips(pairs))]
    return join_halves(quads)


def kernel(x, c, ctx, c_ctx, norm_w, w_mod, b_mod, w_in, hg_lb_logits, hg_norm, s5_a_re, s5_a_im, s5_log_step, s5_b_re, s5_b_im, s5_c_re, s5_c_im, s5_d, s5_w_glu, s5_b_glu, lru_conv_w, lru_conv_b, lru_gate_w, lru_gate_b, lru_lam, m2_conv_w, m2_conv_b, m2_dt_bias, m2_a_log, m2_d, m2_norm, w_branch, w_gate, b_gate, w_out, final_norm, loss_target, m_c_ctx, m_norm_w, m_w_mod, m_b_mod, m_w_in, m_hg_lb_logits, m_hg_norm, m_s5_a_re, m_s5_a_im, m_s5_log_step, m_s5_b_re, m_s5_b_im, m_s5_c_re, m_s5_c_im, m_s5_d, m_s5_w_glu, m_s5_b_glu, m_lru_conv_w, m_lru_conv_b, m_lru_gate_w, m_lru_gate_b, m_lru_lam, m_m2_conv_w, m_m2_conv_b, m_m2_dt_bias, m_m2_a_log, m_m2_d, m_m2_norm, m_w_branch, m_w_gate, m_b_gate, m_w_out, m_final_norm, v_c_ctx, v_norm_w, v_w_mod, v_b_mod, v_w_in, v_hg_lb_logits, v_hg_norm, v_s5_a_re, v_s5_a_im, v_s5_log_step, v_s5_b_re, v_s5_b_im, v_s5_c_re, v_s5_c_im, v_s5_d, v_s5_w_glu, v_s5_b_glu, v_lru_conv_w, v_lru_conv_b, v_lru_gate_w, v_lru_gate_b, v_lru_lam, v_m2_conv_w, v_m2_conv_b, v_m2_dt_bias, v_m2_a_log, v_m2_d, v_m2_norm, v_w_branch, v_w_gate, v_b_gate, v_w_out, v_final_norm):
    given = dict(locals())
    w_loc = {n: given[n] for n in WEIGHTS}
    m_loc = {n: given["m_" + n] for n in WEIGHTS}
    v_loc = {n: given["v_" + n] for n in WEIGHTS}

    full = _gather_weights(w_loc)
    params = {n: (full[n] if n in SHARD_AXIS else w_loc[n]) for n in WEIGHTS if n not in BIG}
    big = {n: full[n] for n in BIG}
    def loss_fn(p, s, xx):
        return _forward(p, big, s, xx, ctx[0], c, loss_target[0])

    loss, (g_p, g_s, g_x) = jax.value_and_grad(loss_fn, argnums=(0, 1, 2))(params, _new_slots(big), x[0])
    grads = {**g_p, **_slot_grads(g_s)}

    def rows4(a):
        return a.reshape(a.shape[:2] + (-1, a.shape[-1]))

    g_big = [rows4(_chip_slices(grads[n], SHARD_AXIS[n])) for n in BIG]
    g_small = _to_rows(_pack_small_grads(grads, loss), 64)
    summed = _reduce_grads(list(BIG) + ["small"], g_big + [g_small])

    g_out, d_out, m_out, v_out = {}, {}, {}, {}
    for n, g in zip(BIG, summed):
        shp = w_loc[n].shape
        flat2 = lambda a: a.reshape(-1, shp[-1])
        g_out[n] = g.reshape(shp)
        d, nm, nv = adamw(n, flat2(g), flat2(w_loc[n]), flat2(m_loc[n]), flat2(v_loc[n]))
        d_out[n], m_out[n], v_out[n] = d.reshape(shp), nm.reshape(shp), nv.reshape(shp)
    zero = jnp.zeros((), f32)
    flat = lambda vals: _to_rows(_pack_small(vals, zero), 64).reshape(-1, W_PACK)
    gs = summed[-1].reshape(-1, W_PACK)
    d, nm, nv = adamw("small", gs, flat(w_loc), flat(m_loc), flat(v_loc))
    gsm, loss_out = _unpack_small(gs.reshape(-1), w_loc)
    g_out.update(gsm)
    d_out.update(_unpack_small(d.reshape(-1), w_loc)[0])
    m_out.update(_unpack_small(nm.reshape(-1), w_loc)[0])
    v_out.update(_unpack_small(nv.reshape(-1), w_loc)[0])
    outs = [loss_out, g_x[None]]
    for group in (g_out, d_out, m_out, v_out):
        outs += [group[n] for n in WEIGHTS]
    return tuple(outs)
```

```python
import functools
import math

import jax
import jax.numpy as jnp
from jax import lax
from jax.experimental import pallas as pl
from jax.experimental.pallas import tpu as pltpu

f32 = jnp.float32
bf16 = jnp.bfloat16
_MM_DTYPE = bf16
_HI = lax.Precision.HIGHEST
_VMEM_LIMIT = 56 * 1024 * 1024
_LANE = 128
_SUB = 8

EPS = 1e-6
CONV_W = 4
CHUNK = 64
RB = 256
BR_W = 512
HG_HEADS = 4
HG_DK = 128
S5_GROUPS = 32
S5_GROUP = 16
S5_STATE = 64
LRU_BLOCKS = 8
LRU_C = 8.0
M2_HEADS = 8
M2_HEADDIM = 64
M2_GROUPS = 2
M2_STATE = 64
M2_XBC = BR_W + 2 * M2_GROUPS * M2_STATE
ADAM_LR = 0.001
ADAM_B1 = 0.9
ADAM_B2 = 0.999
ADAM_EPS = 1e-08
ADAM_WD = 0.01
ADAM_STEP = 10

_NN = (((1,), (0,)), ((), ()))
_NT = (((1,), (1,)), ((), ()))
_TN = (((0,), (0,)), ((), ()))


def _silu(x):
    return x * jax.nn.sigmoid(x)


def _softplus(x):
    return jnp.maximum(x, 0.0) + jnp.log1p(jnp.exp(-jnp.abs(x)))


def _one_minus_exp(z):
    series = -z * (1.0 + z * 0.5 * (1.0 + z * (1.0 / 3.0) * (1.0 + z * 0.25 * (1.0 + z * 0.2))))
    return jnp.where(z > -0.05, series, 1.0 - jnp.exp(z))


def _rms(x, w):
    return x * lax.rsqrt(jnp.mean(x * x, axis=-1, keepdims=True) + EPS) * w


def _dot(a, b, dn=_NN, hi=False):
    return lax.dot_general(a, b, dn, precision=_HI if hi else None, preferred_element_type=f32)


def _cparams(n_grid):
    return pltpu.CompilerParams(dimension_semantics=("arbitrary",) * n_grid, vmem_limit_bytes=_VMEM_LIMIT)


_REV = {"asc": "desc", "d1": "d1r", "desc": "asc", "d1r": "d1"}


def _blk(order, i, n):
    if order == "asc":
        return i
    if order == "desc":
        return n - 1 - i
    if order == "d1":
        return jnp.where(i == 0, 0, n - i)
    return jnp.where(i == n - 1, 0, i + 1)


def _pick(n, cap, unit):
    if n <= cap:
        return n
    best = None
    d = unit
    while d <= cap:
        if n % d == 0:
            best = d
        d += unit
    return n if best is None else best


def _mm_call(name, a, b, mode, hi, out_dtype):
    if mode == "tn":
        k, m = a.shape
        n = b.shape[1]
        tm = _pick(m, 512, _LANE)
        tn = _pick(n, 512, _LANE)
        a_spec = pl.BlockSpec((k, tm), lambda i, j: (0, i))
        b_spec = pl.BlockSpec((k, tn), lambda i, j: (0, j))
    else:
        m, k = a.shape
        tm = _pick(m, max(256, min(1088, 4 * 1024 * 1024 // (k * a.dtype.itemsize))), _SUB)
        a_spec = pl.BlockSpec((tm, k), lambda i, j: (i, 0))
        if mode == "nn":
            n = b.shape[1]
            tn = _pick(n, max(_LANE, (4 * 1024 * 1024 // (k * 4)) // _LANE * _LANE), _LANE)
            b_spec = pl.BlockSpec((k, tn), lambda i, j: (0, j))
        else:
            n = b.shape[0]
            tn = _pick(n, max(_LANE, (4 * 1024 * 1024 // (k * 4)) // _LANE * _LANE), _LANE)
            b_spec = pl.BlockSpec((tn, k), lambda i, j: (j, 0))
    dn = {"nn": _NN, "nt": _NT, "tn": _TN}[mode]

    def body(a_ref, b_ref, o_ref):
        av = a_ref[...]
        bv = b_ref[...]
        if hi:
            av = av.astype(f32)
            bv = bv.astype(f32)
        else:
            av = av.astype(_MM_DTYPE)
            bv = bv.astype(_MM_DTYPE)
        o_ref[...] = _dot(av, bv, dn, hi).astype(o_ref.dtype)

    return pl.pallas_call(
        body, name=name, grid=(m // tm, n // tn), in_specs=[a_spec, b_spec],
        out_specs=pl.BlockSpec((tm, tn), lambda i, j: (i, j)),
        out_shape=jax.ShapeDtypeStruct((m, n), out_dtype), compiler_params=_cparams(2),
    )(a, b)


def mm(name, a, b, slot=None, hi=False):
    @jax.custom_vjp
    def op(a, b, slot):
        return _mm_call(name, a, b, "nn", hi, f32)

    def fwd(a, b, slot):
        return op(a, b, slot), (a, b)

    def bwd(res, g):
        a, b = res
        da = _mm_call(name + "_da", g, b, "nt", hi, a.dtype)
        db = _mm_call(name + "_db", a, g, "tn", hi, f32)
        if slot is None:
            return da, db.astype(b.dtype), None
        return da, jnp.zeros_like(b), db

    op.defvjp(fwd, bwd)
    return op(a, b, slot)


def _sum_nt_call(name, gs, ws, out_dtype):
    m = gs[0].shape[0]
    kdim = ws[0].shape[0]
    tm = _pick(m, 256, _SUB)
    n = len(gs)

    def body(*refs):
        acc = None
        for g_ref, w_ref in zip(refs[:n], refs[n:2 * n]):
            part = _dot(g_ref[...].astype(_MM_DTYPE), w_ref[...].astype(_MM_DTYPE), _NT)
            acc = part if acc is None else acc + part
        refs[2 * n][...] = acc.astype(out_dtype)

    in_specs = [pl.BlockSpec((tm, g.shape[1]), lambda i: (i, 0)) for g in gs]
    in_specs += [pl.BlockSpec(w.shape, lambda i: (0, 0)) for w in ws]
    return pl.pallas_call(
        body, name=name, grid=(m // tm,), in_specs=in_specs, out_specs=pl.BlockSpec((tm, kdim), lambda i: (i, 0)),
        out_shape=jax.ShapeDtypeStruct((m, kdim), out_dtype), compiler_params=_cparams(1),
    )(*gs, *ws)


def multi_mm(tag, names, a, ws, slots):
    @jax.custom_vjp
    def op(a, ws, slots):
        return tuple(_mm_call(tag + n, a, w, "nn", False, f32) for n, w in zip(names, ws))

    def fwd(a, ws, slots):
        return op(a, ws, slots), (a, ws)

    def bwd(res, gs):
        a, ws = res
        dws = [_mm_call(tag + n + "_db", a, g, "tn", False, f32) for n, g in zip(names, gs)]
        da = _sum_nt_call(tag + "da", list(gs), ws, a.dtype)
        return da, [jnp.zeros_like(w) for w in ws], dws

    op.defvjp(fwd, bwd)
    return op(a, list(ws), list(slots))


def _orders(order, n_x, n_o):
    if isinstance(order, str):
        return [order] * n_x, [order] * n_o
    return list(order[0]), list(order[1])


def _row(o, n):
    return lambda i: (_blk(o, i, n), 0)


def _blocked_fwd(name, f, order, rb, params, xs, out_sds, carry_sds):
    t = xs[0].shape[0]
    n = t // rb
    n_p, n_x, n_o, n_c = len(params), len(xs), len(out_sds), len(carry_sds)
    xo, oo = _orders(order, n_x, n_o)

    def body(*refs):
        p_refs = refs[:n_p]
        x_refs = refs[n_p:n_p + n_x]
        o_refs = refs[n_p + n_x:n_p + n_x + n_o]
        st_refs = refs[n_p + n_x + n_o:n_p + n_x + n_o + n_c]
        c_refs = refs[n_p + n_x + n_o + n_c:]
        i = pl.program_id(0)
        blk = _blk(xo[0], i, n)
        p = [r[...] for r in p_refs]
        x = [r[...] for r in x_refs]
        if n_c:
            @pl.when(i == 0)
            def _():
                for c in c_refs:
                    c[...] = jnp.zeros_like(c)
            c_in = [c[...] for c in c_refs]
            for sr, c in zip(st_refs, c_in):
                sr[0] = c
            c_out, ys = f(blk, p, c_in, x)
            for c, v in zip(c_refs, c_out):
                c[...] = v
        else:
            ys = f(blk, p, x)
        for o, y in zip(o_refs, ys):
            o[...] = y.astype(o.dtype)

    in_specs = [pl.BlockSpec(p.shape, lambda i, nd=p.ndim: (0,) * nd) for p in params]
    in_specs += [pl.BlockSpec((rb, x.shape[1]), _row(o, n)) for x, o in zip(xs, xo)]
    out_specs = [pl.BlockSpec((rb, c), _row(o, n)) for (c, _), o in zip(out_sds, oo)]
    out_specs += [pl.BlockSpec((1,) + s, lambda i: (i, 0, 0)) for s in carry_sds]
    out_shape = [jax.ShapeDtypeStruct((t, c), d) for c, d in out_sds]
    out_shape += [jax.ShapeDtypeStruct((n,) + s, f32) for s in carry_sds]
    res = pl.pallas_call(
        body, name=name, grid=(n,), in_specs=in_specs, out_specs=out_specs, out_shape=out_shape,
        scratch_shapes=[pltpu.VMEM(s, f32) for s in carry_sds], compiler_params=_cparams(1),
    )(*params, *xs)
    return list(res[:n_o]), list(res[n_o:])


def _blocked_bwd(name, f, order, rb, params, xs, states, dys, carry_sds):
    t = xs[0].shape[0]
    n = t // rb
    n_p, n_x, n_o, n_c = len(params), len(xs), len(dys), len(carry_sds)
    xo, oo = _orders(order, n_x, n_o)
    xo, oo = [_REV[o] for o in xo], [_REV[o] for o in oo]

    def body(*refs):
        k = 0
        p_refs = refs[k:k + n_p]; k += n_p
        x_refs = refs[k:k + n_x]; k += n_x
        st_refs = refs[k:k + n_c]; k += n_c
        dy_refs = refs[k:k + n_o]; k += n_o
        dp_refs = refs[k:k + n_p]; k += n_p
        dx_refs = refs[k:k + n_x]; k += n_x
        dc_refs = refs[k:]
        i = pl.program_id(0)
        blk = _blk(xo[0], i, n)
        p = [r[...] for r in p_refs]
        x = [r[...] for r in x_refs]
        dy = [r[...] for r in dy_refs]
        if n_c:
            @pl.when(i == 0)
            def _():
                for c in dc_refs:
                    c[...] = jnp.zeros_like(c)
            c_in = [r[0] for r in st_refs]
            dc = [c[...] for c in dc_refs]
            _, vjp = jax.vjp(lambda p_, c_, x_: f(blk, p_, c_, x_), p, c_in, x)
            dp, dcin, dx = vjp((dc, dy))
            for c, v in zip(dc_refs, dcin):
                c[...] = v
        else:
            _, vjp = jax.vjp(lambda p_, x_: f(blk, p_, x_), p, x)
            dp, dx = vjp(dy)

        @pl.when(i == 0)
        def _():
            for r, v in zip(dp_refs, dp):
                r[...] = v

        @pl.when(i > 0)
        def _():
            for r, v in zip(dp_refs, dp):
                r[...] += v
        for r, v in zip(dx_refs, dx):
            r[...] = v.astype(r.dtype)

    in_specs = [pl.BlockSpec(p.shape, lambda i, nd=p.ndim: (0,) * nd) for p in params]
    in_specs += [pl.BlockSpec((rb, x.shape[1]), _row(o, n)) for x, o in zip(xs, xo)]
    in_specs += [pl.BlockSpec((1,) + s, lambda i: (n - 1 - i, 0, 0)) for s in carry_sds]
    in_specs += [pl.BlockSpec((rb, d.shape[1]), _row(o, n)) for d, o in zip(dys, oo)]
    out_specs = [pl.BlockSpec(p.shape, lambda i, nd=p.ndim: (0,) * nd) for p in params]
    out_specs += [pl.BlockSpec((rb, x.shape[1]), _row(o, n)) for x, o in zip(xs, xo)]
    out_shape = [jax.ShapeDtypeStruct(p.shape, f32) for p in params]
    out_shape += [jax.ShapeDtypeStruct(x.shape, x.dtype) for x in xs]
    res = pl.pallas_call(
        body, name=name + "_bwd", grid=(n,), in_specs=in_specs, out_specs=out_specs, out_shape=out_shape,
        scratch_shapes=[pltpu.VMEM(s, f32) for s in carry_sds], compiler_params=_cparams(1),
    )(*params, *xs, *states, *dys)
    return list(res[:n_p]), list(res[n_p:])


def blocked_op(name, f, params, xs, out_sds, order="asc", carry_sds=(), rb=RB):
    carry_sds = tuple(carry_sds)

    @jax.custom_vjp
    def op(params, xs):
        return tuple(_blocked_fwd(name, f, order, rb, params, xs, out_sds, carry_sds)[0])

    def fwd(params, xs):
        ys, states = _blocked_fwd(name, f, order, rb, params, xs, out_sds, carry_sds)
        return tuple(ys), (params, xs, states)

    def bwd(res, dys):
        params, xs, states = res
        dp, dx = _blocked_bwd(name, f, order, rb, params, xs, states, list(dys), carry_sds)
        return list(dp), list(dx)

    op.defvjp(fwd, bwd)
    return op(list(params), list(xs))


def _cscan_call(name, order, asc, a, xr, xi, sr=None, si=None):
    t = xr.shape[0]
    n = t // RB
    tile = xr.shape[1:]
    xspec = pl.BlockSpec((RB,) + tile, lambda i: (_blk(order, i, n), 0, 0))
    aspec = pl.BlockSpec(a.shape, lambda i: (0, 0, 0))
    plane = jax.ShapeDtypeStruct(xr.shape, f32)

    def rowidx(tt):
        return tt if asc else RB - 1 - tt

    if sr is None:
        def body(a_ref, xr_ref, xi_ref, sr_ref, si_ref, c_ref):
            i = pl.program_id(0)

            @pl.when(i == 0)
            def _():
                c_ref[...] = jnp.zeros_like(c_ref)
            ar = a_ref[0]
            ai = a_ref[1]

            def step(tt, carry):
                cr, ci = carry
                r = rowidx(tt)
                nr = ar * cr - ai * ci + xr_ref[r]
                ni = ar * ci + ai * cr + xi_ref[r]
                sr_ref[r] = nr
                si_ref[r] = ni
                return nr, ni
            cr, ci = lax.fori_loop(0, RB, step, (c_ref[0], c_ref[1]), unroll=8)
            c_ref[0] = cr
            c_ref[1] = ci

        return pl.pallas_call(
            body, name=name, grid=(n,), in_specs=[aspec, xspec, xspec], out_specs=[xspec, xspec],
            out_shape=[plane, plane], scratch_shapes=[pltpu.VMEM(a.shape, f32)], compiler_params=_cparams(1),
        )(a, xr, xi)

    def body(a_ref, xr_ref, xi_ref, sr_ref, si_ref, gr_ref, gi_ref, da_ref, c_ref):
        i = pl.program_id(0)

        @pl.when(i == 0)
        def _():
            c_ref[...] = jnp.zeros_like(c_ref)
            da_ref[...] = jnp.zeros_like(da_ref)
        ar = a_ref[0]
        ai = a_ref[1]

        def step(tt, carry):
            gr, gi, dar, dai = carry
            r = rowidx(tt)
            vr = sr_ref[r]
            vi = si_ref[r]
            dar = dar + gr * vr + gi * vi
            dai = dai + gi * vr - gr * vi
            nr = xr_ref[r] + ar * gr + ai * gi
            ni = xi_ref[r] + ar * gi - ai * gr
            gr_ref[r] = nr
            gi_ref[r] = ni
            return nr, ni, dar, dai
        z = jnp.zeros(tile, f32)
        gr, gi, dar, dai = lax.fori_loop(0, RB, step, (c_ref[0], c_ref[1], z, z), unroll=8)
        c_ref[0] = gr
        c_ref[1] = gi
        da_ref[0] += dar
        da_ref[1] += dai

    return pl.pallas_call(
        body, name=name, grid=(n,), in_specs=[aspec] + [xspec] * 4, out_specs=[xspec, xspec, aspec],
        out_shape=[plane, plane, jax.ShapeDtypeStruct(a.shape, f32)],
        scratch_shapes=[pltpu.VMEM(a.shape, f32)], compiler_params=_cparams(1),
    )(a, xr, xi, sr, si)


def cscan(name, d, a, xr, xi):
    order = "d1" if d else "asc"

    @jax.custom_vjp
    def op(a, xr, xi):
        return tuple(_cscan_call(name, order, d == 0, a, xr, xi))

    def fwd(a, xr, xi):
        sr, si = op(a, xr, xi)
        return (sr, si), (a, sr, si)

    def bwd(res, ds):
        a, sr, si = res
        gr, gi, da = _cscan_call(name + "_bwd", _REV[order], d != 0, a, ds[0], ds[1], sr, si)
        return da, gr, gi

    op.defvjp(fwd, bwd)
    return op(a, xr, xi)


def _bd_call(name, a, b, mode, k=None):
    if mode == "tn":
        t = a.shape[0]
        ck, cn = a.shape[1] // k, b.shape[1] // k

        def body(a_ref, b_ref, o_ref):
            o_ref[0] = _dot(a_ref[...], b_ref[...], _TN, True)

        return pl.pallas_call(
            body, name=name, grid=(k,),
            in_specs=[pl.BlockSpec((t, ck), lambda j: (0, j)), pl.BlockSpec((t, cn), lambda j: (0, j))],
            out_specs=pl.BlockSpec((1, ck, cn), lambda j: (j, 0, 0)),
            out_shape=jax.ShapeDtypeStruct((k, ck, cn), f32), compiler_params=_cparams(1),
        )(a, b)
    k, ck, cn = b[0][0].shape
    n_i, n_o = len(b), len(b[0])
    n_x = len(a)
    t = a[0].shape[0]
    tm = _pick(t, 1088, _SUB)
    flat = [w for row in b for w in row]
    win, wout, n_out, dn = (ck, cn, n_o, _NN) if mode == "nn" else (cn, ck, n_i, _NT)

    def body(*refs):
        xv = [r[...] for r in refs[:n_x]]
        w_refs = refs[n_x:n_x + len(flat)]
        o_refs = refs[n_x + len(flat):]
        for q in range(n_out):
            acc = None
            for s in range(n_x):
                w = w_refs[s * n_o + q] if mode == "nn" else w_refs[q * n_o + s]
                part = _dot(xv[s], w[0], dn, True)
                acc = part if acc is None else acc + part
            o_refs[q][...] = acc

    return pl.pallas_call(
        body, name=name, grid=(t // tm, k),
        in_specs=[pl.BlockSpec((tm, win), lambda i, j: (i, j))] * n_x
        + [pl.BlockSpec((1, ck, cn), lambda i, j: (j, 0, 0))] * len(flat),
        out_specs=[pl.BlockSpec((tm, wout), lambda i, j: (i, j))] * n_out,
        out_shape=[jax.ShapeDtypeStruct((t, k * wout), f32)] * n_out, compiler_params=_cparams(2),
    )(*a, *flat)


def bd_mm(name, xs, ws):
    k = ws[0][0].shape[0]

    @jax.custom_vjp
    def op(xs, ws):
        return tuple(_bd_call(name, xs, ws, "nn"))

    def fwd(xs, ws):
        return op(xs, ws), (xs, ws)

    def bwd(res, gs):
        xs, ws = res
        dws = [[_bd_call(name + "_db%d%d" % (i, o), x, g, "tn", k) for o, g in enumerate(gs)] for i, x in enumerate(xs)]
        return list(_bd_call(name + "_da", list(gs), ws, "nt")), dws

    op.defvjp(fwd, bwd)
    return op(list(xs), [list(row) for row in ws])


def _rscan_call(name, order, asc, a, x, hp=None):
    t = x.shape[0]
    n = t // RB
    cshape = x.shape[1:]
    xspec = pl.BlockSpec((RB,) + cshape, lambda i: (_blk(order, i, n), 0, 0))

    def rowidx(tt):
        return tt if asc else RB - 1 - tt

    if hp is None:
        def body(a_ref, x_ref, h_ref, hp_ref, c_ref):
            i = pl.program_id(0)

            @pl.when(i == 0)
            def _():
                c_ref[...] = jnp.zeros_like(c_ref)

            def step(tt, h):
                r = rowidx(tt)
                hp_ref[r] = h
                h = a_ref[r] * h + x_ref[r]
                h_ref[r] = h
                return h
            c_ref[...] = lax.fori_loop(0, RB, step, c_ref[...], unroll=8)

        return pl.pallas_call(
            body, name=name, grid=(n,), in_specs=[xspec, xspec], out_specs=[xspec, xspec],
            out_shape=[jax.ShapeDtypeStruct(x.shape, f32)] * 2, scratch_shapes=[pltpu.VMEM(cshape, f32)],
            compiler_params=_cparams(1),
        )(a, x)

    def body(a_ref, x_ref, hp_ref, da_ref, db_ref, c_ref):
        i = pl.program_id(0)

        @pl.when(i == 0)
        def _():
            c_ref[...] = jnp.zeros_like(c_ref)

        def step(tt, c):
            r = rowidx(tt)
            g = x_ref[r] + c
            db_ref[r] = g
            da_ref[r] = g * hp_ref[r]
            return a_ref[r] * g
        c_ref[...] = lax.fori_loop(0, RB, step, c_ref[...], unroll=8)

    return pl.pallas_call(
        body, name=name, grid=(n,), in_specs=[xspec, xspec, xspec], out_specs=[xspec, xspec],
        out_shape=[jax.ShapeDtypeStruct(x.shape, f32)] * 2, scratch_shapes=[pltpu.VMEM(cshape, f32)],
        compiler_params=_cparams(1),
    )(a, x, hp)


def rscan(name, d, a, x):
    order = "d1" if d else "asc"

    @jax.custom_vjp
    def op(a, x):
        return _rscan_call(name, order, d == 0, a, x)[0]

    def fwd(a, x):
        h, hp = _rscan_call(name, order, d == 0, a, x)
        return h, (a, hp)

    def bwd(res, dh):
        a, hp = res
        da, db = _rscan_call(name + "_bwd", _REV[order], d != 0, a, dh, hp)
        return da, db

    op.defvjp(fwd, bwd)
    return op(a, x)


def _mod_row(blk, mod, bm):
    return jnp.where(blk == 0, mod[1:2], mod[0:1]) + bm


def _f_silu(blk, p, x):
    return [_silu(x[0]).astype(bf16)]


def _f_normmod(blk, p, x):
    nw, mod, bm = p
    d = nw.shape[1]
    r = _mod_row(blk, mod, bm)
    return [(_rms(x[0], nw) * (1.0 + r[:, d:2 * d]) + r[:, :d]).astype(bf16)]


def _f_resid(blk, p, x):
    mod, bm = p
    d = x[0].shape[1]
    r = _mod_row(blk, mod, bm)
    return [x[0] + r[:, 2 * d:] * x[1]]


def _f_mix(blk, p, x):
    bg, = p
    gp = x[0]
    d = x[1].shape[1]
    acc = None
    for k in range(4):
        t = jax.nn.sigmoid(gp[:, k * d:(k + 1) * d] + bg[:, k * d:(k + 1) * d]) * x[1 + k]
        acc = t if acc is None else acc + t
    return [acc.astype(bf16)]


def _tri(rev):
    row = lax.broadcasted_iota(jnp.int32, (CHUNK, CHUNK), 0)
    col = lax.broadcasted_iota(jnp.int32, (CHUNK, CHUNK), 1)
    return (col >= row) if rev else (col <= row)


def _chunk_ids(rev):
    ids = list(range(RB // CHUNK))
    return ids[::-1] if rev else ids


def _f_hg(rev):
    def f(blk, p, c, x):
        lb, = p
        st, = c
        qi, fr = x
        q = _silu(qi[:, :BR_W])
        v = qi[:, BR_W:]
        fg = lb + (1.0 - lb) * jax.nn.sigmoid(fr)
        logf = jnp.log(fg)
        k = 1.0 - fg
        m = _tri(rev)
        mf = m.astype(f32)
        outs = [None] * (RB // CHUNK)
        for ci in _chunk_ids(rev):
            sl = slice(CHUNK * ci, CHUNK * ci + CHUNK)
            lf = logf[sl]
            b = _dot(mf, lf, hi=True)
            bend = jnp.sum(lf, axis=0, keepdims=True)
            mid = 0.5 * bend
            qe = q[sl] * jnp.exp(b - mid)
            ke = k[sl] * jnp.exp(mid - b)
            kd = k[sl] * jnp.exp(bend - b)
            qb = q[sl] * jnp.exp(b)
            dec = jnp.exp(bend)
            vc = v[sl]
            oh, ns = [], []
            for hh in range(HG_HEADS):
                cs = slice(HG_DK * hh, HG_DK * hh + HG_DK)
                sth = st[cs]
                att = jnp.where(m, _dot(qe[:, cs], ke[:, cs], _NT), 0.0)
                oh.append(_dot(att, vc[:, cs]) + _dot(qb[:, cs], sth, _NT))
                ns.append(sth * dec[:, cs] + _dot(vc[:, cs], kd[:, cs], _TN))
            st = jnp.concatenate(ns, axis=0)
            outs[ci] = jnp.concatenate(oh, axis=1)
        return [st], [jnp.concatenate(outs, axis=0)]
    return f


def _both(f0, f1, n_p, n_x):
    def f(blk, p, c, x):
        c0, y0 = f0(blk, p[:n_p], c[:1], x[:n_x])
        c1, y1 = f1(blk, p[n_p:], c[1:], x[n_x:])
        return c0 + c1, y0 + y1
    return f


_BOTH_ORDERS = (["asc", "asc", "d1", "d1"], ["asc", "d1"])


def _f_hg_final(blk, p, x):
    nw, = p
    o = x[0] + x[1]
    parts = []
    for hh in range(HG_HEADS):
        cs = slice(HG_DK * hh, HG_DK * hh + HG_DK)
        parts.append(_rms(o[:, cs], nw[:, cs]))
    return [(jnp.concatenate(parts, axis=1) * _silu(x[2])).astype(bf16)]


def _conv(x, cw, cb, blk):
    rows = x.shape[0]
    r = lax.broadcasted_iota(jnp.int32, (rows, 1), 0)
    rm = jnp.where(blk == 0, r, r % CHUNK)
    seg = jnp.where(blk == 0, rows, CHUNK)

    def vmask(o):
        return ((rm + o >= 0) & (rm + o < seg)).astype(f32)

    def shifted(o):
        @jax.custom_vjp
        def sh(x, mo, mn):
            return pltpu.roll(x, (-o) % rows, 0) * mo

        def fwd(x, mo, mn):
            return sh(x, mo, mn), (mo, mn)

        def bwd(res, g):
            mo, mn = res
            return pltpu.roll(g, o % rows, 0) * mn, jnp.zeros_like(mo), jnp.zeros_like(mn)
        sh.defvjp(fwd, bwd)
        return sh(x, vmask(o), vmask(-o))

    lo = (CONV_W - 1) // 2
    out = cb
    for k in range(CONV_W):
        o = k - lo
        out = out + cw[k:k + 1] * (x if o == 0 else shifted(o))
    return out


def _f_lru_a(blk, p, x):
    cw, cb, wg, gb, lam = p
    xc = _conv(x[0], cw, cb, blk)
    n_chunks = BR_W // _LANE
    xk = [xc[:, _LANE * k:_LANE * (k + 1)] for k in range(n_chunks)]

    def gate(j):
        pre = jnp.concatenate([_dot(xk[k], wg[j * n_chunks + k], hi=True) for k in range(n_chunks)], axis=1)
        return jax.nn.sigmoid(pre + gb[:, BR_W * j:BR_W * (j + 1)])

    outs = []
    for d in range(2):
        r = gate(2 * d)
        ig = gate(2 * d + 1)
        log_a = -LRU_C * r * _softplus(-lam[d:d + 1])
        outs.append(jnp.exp(log_a))
        outs.append(jnp.sqrt(_one_minus_exp(2.0 * log_a)) * (ig * xc))
    return outs


def _f_lru_c(blk, p, x):
    return [((x[0] + x[1]) * _silu(x[2])).astype(bf16)]


def _f_s5_c1(blk, p, x):
    dsk, = p
    return [jax.nn.gelu(x[0] + dsk * x[1])]


def _f_s5_c2(blk, p, x):
    bglu, = p
    return [(x[0] * jax.nn.sigmoid(x[1] + bglu) * _silu(x[2])).astype(bf16)]


def _f_m2_a(blk, p, x):
    cw, cb, dtb = p
    return [_silu(_conv(x[0], cw, cb, blk)), _softplus(x[1] + dtb)]


def _f_ssd(d):
    rev = d == 1
    hpg = M2_HEADS // M2_GROUPS

    def f(blk, p, c, x):
        alog, = p
        st, = c
        xbc, dtp = x
        a = -jnp.exp(alog[:, M2_HEADS * d:M2_HEADS * (d + 1)])
        dt = dtp[:, M2_HEADS * d:M2_HEADS * (d + 1)]
        xs = xbc[:, :BR_W]
        bm = xbc[:, BR_W:BR_W + M2_GROUPS * M2_STATE]
        cm = xbc[:, BR_W + M2_GROUPS * M2_STATE:]
        gw = hpg * M2_HEADDIM
        mf = _tri(rev).astype(f32)
        row = lax.broadcasted_iota(jnp.int32, (CHUNK, gw), 0)
        col = lax.broadcasted_iota(jnp.int32, (CHUNK, gw), 1)
        m4 = (col % CHUNK >= row) if rev else (col % CHUNK <= row)
        spread = (lax.broadcasted_iota(jnp.int32, (hpg, gw), 0)
                  == lax.div(lax.broadcasted_iota(jnp.int32, (hpg, gw), 1), M2_HEADDIM)).astype(f32)
        own = [lax.div(lax.broadcasted_iota(jnp.int32, (1, gw), 1), M2_HEADDIM) == r for r in range(hpg)]
        outs = [None] * (RB // CHUNK)
        for ci in _chunk_ids(rev):
            sl = slice(CHUNK * ci, CHUNK * ci + CHUNK)
            dtc = dt[sl]
            dta = dtc * a
            cum = _dot(mf, dta, hi=True)
            cum_t = cum.T
            dt_t = dtc.T
            ys, ns = [], []
            for g in range(M2_GROUPS):
                hs = slice(hpg * g, hpg * (g + 1))
                bmg = bm[sl, M2_STATE * g:M2_STATE * (g + 1)]
                cmg = cm[sl, M2_STATE * g:M2_STATE * (g + 1)]
                xg = xs[sl, gw * g:gw * (g + 1)]
                stg = st[M2_STATE * g:M2_STATE * (g + 1)]
                cum_i = _dot(cum[:, hs], spread, hi=True)
                cum_j = jnp.concatenate([cum_t[hpg * g + r:hpg * g + r + 1] for r in range(hpg)], axis=1)
                dt_j = jnp.concatenate([dt_t[hpg * g + r:hpg * g + r + 1] for r in range(hpg)], axis=1)
                dt_i = _dot(dtc[:, hs], spread, hi=True)
                cend_g = jnp.sum(_dot(dta[:, hs], spread, hi=True), axis=0, keepdims=True)
                decay = jnp.exp(jnp.where(m4, cum_i - cum_j, -1e30))
                scores = _dot(cmg, jnp.concatenate([bmg] * hpg, axis=0), _NT)
                w = scores * decay * dt_j
                xdiag = jnp.concatenate([jnp.where(own[r], xg, 0.0) for r in range(hpg)], axis=0)
                ys.append(_dot(w, xdiag) + _dot(cmg, stg) * jnp.exp(cum_i))
                wx = jnp.exp(cend_g - cum_i) * dt_i * xg
                ns.append(jnp.exp(cend_g) * stg + _dot(bmg, wx, _TN))
            st = jnp.concatenate(ns, axis=0)
            outs[ci] = jnp.concatenate(ys, axis=1)
        return [st], [jnp.concatenate(outs, axis=0)]
    return f


def _f_m2_c(blk, p, x):
    dsk, nw = p
    y = x[0] + x[1] + dsk * x[2][:, :BR_W]
    return [_rms(y * _silu(x[3]), nw).astype(bf16)]


def _f_loss(blk, p, x):
    fnw, = p
    err = _rms(x[0], fnw) - x[1]
    return [0.5 * jnp.mean(err * err, axis=-1, keepdims=True)]


def _blockdiag(w):
    g, a, b = w.shape
    return jnp.einsum("gab,gh->gahb", w, jnp.eye(g, dtype=w.dtype)).reshape(g * a, g * b)


def _s5_params(l, w):
    a_scan, cds = [], []
    per = _LANE // S5_GROUP

    def chunks(m):
        return jnp.stack([_blockdiag(m[k * per:(k + 1) * per]) for k in range(S5_GROUPS // per)])

    b_re = jnp.transpose(w["s5_b_re"][l], (0, 2, 1))
    b_im = jnp.transpose(w["s5_b_im"][l], (0, 2, 1))
    bd = [chunks(b_re), chunks(b_im)]
    c_re = jnp.transpose(w["s5_c_re"][l], (0, 2, 1))
    c_im = jnp.transpose(w["s5_c_im"][l], (0, 2, 1))
    for d in range(2):
        lam_re = w["s5_a_re"][l, d]
        lam_im = w["s5_a_im"][l, d]
        step = jnp.exp(w["s5_log_step"][l, d])[:, None]
        mag = jnp.exp(lam_re * step)
        ab_re = mag * jnp.cos(lam_im * step)
        ab_im = mag * jnp.sin(lam_im * step)
        den = lam_re * lam_re + lam_im * lam_im
        nr = ab_re - 1.0
        co_re = (nr * lam_re + ab_im * lam_im) / den
        co_im = (ab_im * lam_re - nr * lam_im) / den
        n_state = S5_GROUPS * S5_STATE
        a_scan.append(jnp.stack([ab_re.reshape(_SUB, n_state // _SUB), ab_im.reshape(_SUB, n_state // _SUB)]))
        cp_re = c_re * co_re[:, :, None] - c_im * co_im[:, :, None]
        cp_im = c_re * co_im[:, :, None] + c_im * co_re[:, :, None]
        cds.append([chunks(cp_re), -chunks(cp_im)])
    return a_scan, bd, cds


def _lru_gate(l, w):
    gw = w["lru_gate_w"][l]
    per = _LANE // (BR_W // LRU_BLOCKS)
    chunks = [_blockdiag(gw[d, g, k * per:(k + 1) * per])
              for d in range(2) for g in range(2) for k in range(LRU_BLOCKS // per)]
    return jnp.stack(chunks), w["lru_gate_b"][l].reshape(1, -1)


def _pad_cols(a, n):
    return jnp.pad(a, ((0, 0), (0, n - a.shape[1])))


IN_SIZES = (BR_W,) * 9 + (M2_XBC, 2 * M2_HEADS, BR_W)
IN_OFFS = tuple(sum(IN_SIZES[:i]) for i in range(len(IN_SIZES) + 1))
IN_GROUPS = (("hg_qi", 0, 2, 1024), ("hg_ff", 2, 1, 512), ("hg_fb", 3, 1, 512), ("hg_z", 4, 1, 512),
             ("s5_u", 5, 1, 512), ("s5_z", 6, 1, 512), ("lru_x", 7, 1, 512), ("lru_z", 8, 1, 512),
             ("m2_xbc", 9, 1, 768), ("m2_dt", 10, 1, 128), ("m2_z", 11, 1, 512))


def _new_slots(big):
    slots = {n: jnp.zeros(w.shape, f32) for n, w in big.items() if n not in ("w_in", "w_gate")}
    n_layers, d_model = big["w_in"].shape[:2]
    slots["w_in"] = [{name: jnp.zeros((d_model, width), f32) for name, _, _, width in IN_GROUPS} for _ in range(n_layers)]
    slots["w_gate"] = jnp.zeros((n_layers, d_model, 4 * d_model), f32)
    return slots


def _slot_grads(g):
    out = dict(g)
    out["w_in"] = jnp.stack([
        jnp.concatenate([gl[name][:, :IN_OFFS[s0 + ns] - IN_OFFS[s0]] for name, s0, ns, _ in IN_GROUPS], axis=1)
        for gl in g["w_in"]])
    n_layers, d_model = g["w_gate"].shape[:2]
    out["w_gate"] = jnp.transpose(g["w_gate"].reshape(n_layers, d_model, 4, d_model), (0, 2, 1, 3))
    return out


def _forward(p, big, slots, x, ctx, c, target):
    n_layers = p["norm_w"].shape[0]
    d_model = x.shape[-1]
    xa = jnp.concatenate([ctx, x], axis=0)
    t = xa.shape[0]
    cc = jnp.concatenate([c, p["c_ctx"][None], jnp.zeros((_SUB - 2, d_model), f32)], axis=0)
    lb_all = jnp.cumsum(jax.nn.softmax(p["hg_lb_logits"], axis=0), axis=0)
    scc, = blocked_op("silu_c", _f_silu, [], [cc], [(d_model, bf16)], rb=_SUB)

    for l in range(n_layers):
        tag = "l%d_" % l
        mod = mm(tag + "mod", scc, big["w_mod"][l], slots["w_mod"][l])
        bm = p["b_mod"][l][None]
        h, = blocked_op(tag + "normmod", _f_normmod, [p["norm_w"][l][None], mod, bm], [xa], [(d_model, bf16)])
        gnames = [g[0] for g in IN_GROUPS]
        wvs = [_pad_cols(big["w_in"][l][:, IN_OFFS[s0]:IN_OFFS[s0 + ns]], width) for _, s0, ns, width in IN_GROUPS]
        u = dict(zip(gnames, multi_mm(tag + "in_", gnames, h, wvs, [slots["w_in"][l][g] for g in gnames])))

        o_dirs = blocked_op(tag + "hg", _both(_f_hg(False), _f_hg(True), 1, 2), [lb_all[l, 0][None], lb_all[l, 1][None]],
                            [u["hg_qi"], u["hg_ff"], u["hg_qi"], u["hg_fb"]], [(BR_W, f32)] * 2,
                            order=_BOTH_ORDERS, carry_sds=[(BR_W, HG_DK)] * 2)
        y_hg, = blocked_op(tag + "hg_fin", _f_hg_final, [p["hg_norm"][l][None]], list(o_dirs) + [u["hg_z"]], [(BR_W, bf16)])

        a_scan, bd, cds = _s5_params(l, p)
        n_state = S5_GROUPS * S5_STATE
        bu = [b.reshape(t, _SUB, n_state // _SUB) for b in bd_mm(tag + "s5_bu", [u["s5_u"]], [bd])]
        planes, maps = [], []
        for d in range(2):
            s = cscan(tag + "s5_scan%d" % d, d, a_scan[d], bu[0], bu[1])
            planes += [s[0].reshape(t, n_state), s[1].reshape(t, n_state)]
            maps += [[cds[d][0]], [cds[d][1]]]
        ysum, = bd_mm(tag + "s5_c", planes, maps)
        g5, = blocked_op(tag + "s5_c1", _f_s5_c1, [p["s5_d"][l][None]], [ysum, u["s5_u"]], [(BR_W, f32)])
        gl = mm(tag + "s5_glu", g5, big["s5_w_glu"][l], slots["s5_w_glu"][l])
        y_s5, = blocked_op(tag + "s5_c2", _f_s5_c2, [p["s5_b_glu"][l][None]], [g5, gl, u["s5_z"]], [(BR_W, bf16)])

        wg, gb = _lru_gate(l, p)
        ab = blocked_op(tag + "lru_a", _f_lru_a,
                        [p["lru_conv_w"][l], p["lru_conv_b"][l][None], wg, gb, p["lru_lam"][l]],
                        [u["lru_x"]], [(BR_W, f32)] * 4)
        hs = []
        for d in range(2):
            a3 = ab[2 * d].reshape(t, 4, BR_W // 4)
            b3 = ab[2 * d + 1].reshape(t, 4, BR_W // 4)
            hs.append(rscan(tag + "lru_scan%d" % d, d, a3, b3).reshape(t, BR_W))
        y_lru, = blocked_op(tag + "lru_c", _f_lru_c, [], hs + [u["lru_z"]], [(BR_W, bf16)])

        dtb = _pad_cols(p["m2_dt_bias"][l].reshape(1, -1), _LANE)
        xbc, dtp = blocked_op(tag + "m2_a", _f_m2_a, [p["m2_conv_w"][l], p["m2_conv_b"][l][None], dtb],
                              [u["m2_xbc"], u["m2_dt"]], [(M2_XBC, f32), (_LANE, f32)])
        alog = _pad_cols(p["m2_a_log"][l].reshape(1, -1), _LANE)
        y_dirs = blocked_op(tag + "ssd", _both(_f_ssd(0), _f_ssd(1), 1, 2), [alog, alog], [xbc, dtp, xbc, dtp],
                            [(BR_W, f32)] * 2, order=_BOTH_ORDERS,
                            carry_sds=[(M2_GROUPS * M2_STATE, BR_W // M2_GROUPS)] * 2)
        dsk = jnp.repeat(p["m2_d"][l], M2_HEADDIM)[None]
        y_m2, = blocked_op(tag + "m2_c", _f_m2_c, [dsk, p["m2_norm"][l][None]], list(y_dirs) + [xbc, u["m2_z"]], [(BR_W, bf16)])

        wg_all = jnp.transpose(big["w_gate"][l], (1, 0, 2)).reshape(d_model, 4 * d_model)
        gp = mm(tag + "gate", h, wg_all, slots["w_gate"][l])
        bs = [mm(tag + "br%d" % k, yk, big["w_branch"][l, k], slots["w_branch"][l, k])
              for k, yk in enumerate((y_hg, y_s5, y_lru, y_m2))]
        mix, = blocked_op(tag + "mix", _f_mix, [p["b_gate"][l].reshape(1, -1)], [gp] + bs, [(d_model, bf16)])
        o = mm(tag + "out", mix, big["w_out"][l], slots["w_out"][l])
        xa, = blocked_op(tag + "resid", _f_resid, [mod, bm], [xa, o], [(d_model, f32)])

    rl, = blocked_op("loss", _f_loss, [p["final_norm"][None]], [xa[ctx.shape[0]:], target], [(1, f32)])
    return jnp.sum(rl)


_MESH = pl.DeviceIdType.MESH
_ANY = pl.BlockSpec(memory_space=pl.ANY)
W_PACK = 1024


def _place():
    x, y, c = lax.axis_index("x"), lax.axis_index("y"), lax.axis_index("c")
    chips = [(x, 1 - y), (1 - x, y), (1 - x, 1 - y)]
    return x, y, c, chips


def _rcopy(src, dst, ssem, rsem, k, to):
    return pltpu.make_async_remote_copy(src_ref=src, dst_ref=dst, send_sem=ssem.at[k], recv_sem=rsem.at[k],
                                        device_id=to, device_id_type=_MESH)


def gather_shards(xs):
    n = len(xs)

    def body(*refs):
        x_refs, o_refs = refs[:n], refs[n:2 * n]
        ssem, rsem, lsem = refs[2 * n:]
        x, y, c, chips = _place()
        j = 2 * x + y
        sib = (x, y, 1 - c)
        mine = [pltpu.make_async_copy(x_refs[a], o_refs[a].at[j], lsem.at[a]) for a in range(n)]
        for cp in mine:
            cp.start()
        first = [_rcopy(x_refs[a].at[c], o_refs[a].at[j, c], ssem, rsem, 6 * a + r, (*chips[r], c))
                 for r in range(3) for a in range(n)]
        for cp in first:
            cp.start()
        passed = []
        for r in range(3):
            jr = j ^ (r + 1)
            for a in range(n):
                _rcopy(x_refs[a].at[c], o_refs[a].at[jr, c], ssem, rsem, 6 * a + r, sib).wait_recv()
                cp = _rcopy(o_refs[a].at[jr, c], o_refs[a].at[jr, c], ssem, rsem, 6 * a + 3 + r, sib)
                cp.start()
                passed.append(cp)
        for r in range(3):
            jr = j ^ (r + 1)
            for a in range(n):
                _rcopy(x_refs[a].at[c], o_refs[a].at[jr, 1 - c], ssem, rsem, 6 * a + 3 + r, sib).wait_recv()
        for cp in first + passed:
            cp.wait_send()
        for cp in mine:
            cp.wait()

    return pl.pallas_call(
        body, name="gather_shards", out_shape=[jax.ShapeDtypeStruct((4,) + x.shape, x.dtype) for x in xs],
        in_specs=[_ANY] * n, out_specs=[_ANY] * n,
        scratch_shapes=[pltpu.SemaphoreType.DMA((6 * n,)), pltpu.SemaphoreType.DMA((6 * n,)), pltpu.SemaphoreType.DMA((n,))],
    )(*xs)


def sibling_halves(gs):
    n = len(gs)

    def body(*refs):
        g_refs, o_refs = refs[:n], refs[n:2 * n]
        ssem, rsem = refs[2 * n:]
        x, y, c, _ = _place()
        sib = (x, y, 1 - c)
        cps = [_rcopy(g_refs[a].at[k, 1 - c], o_refs[a].at[k], ssem, rsem, 4 * a + k, sib)
               for k in range(4) for a in range(n)]
        for cp in cps:
            cp.start()
        for cp in cps:
            cp.wait()

    return pl.pallas_call(
        body, name="sibling_halves", out_shape=[jax.ShapeDtypeStruct((4,) + g.shape[2:], g.dtype) for g in gs],
        in_specs=[_ANY] * n, out_specs=[_ANY] * n,
        scratch_shapes=[pltpu.SemaphoreType.DMA((4 * n,)), pltpu.SemaphoreType.DMA((4 * n,))],
    )(*gs)


def scatter_chips(ps):
    n = len(ps)

    def body(*refs):
        p_refs, o_refs = refs[:n], refs[n:2 * n]
        ssem, rsem = refs[2 * n:]
        x, y, c, chips = _place()
        j = 2 * x + y
        cps = [_rcopy(p_refs[a].at[j ^ (r + 1)], o_refs[a].at[r], ssem, rsem, 3 * a + r, (*chips[r], c))
               for r in range(3) for a in range(n)]
        for cp in cps:
            cp.start()
        for cp in cps:
            cp.wait()

    return pl.pallas_call(
        body, name="scatter_chips", out_shape=[jax.ShapeDtypeStruct((3,) + p.shape[1:], p.dtype) for p in ps],
        in_specs=[_ANY] * n, out_specs=[_ANY] * n,
        scratch_shapes=[pltpu.SemaphoreType.DMA((3 * n,)), pltpu.SemaphoreType.DMA((3 * n,))],
    )(*ps)


def join_halves(qs):
    n = len(qs)

    def body(*refs):
        o_refs = refs[n:2 * n]
        ssem, rsem = refs[2 * n:]
        x, y, c, _ = _place()
        sib = (x, y, 1 - c)
        cps = [_rcopy(o_refs[a].at[c], o_refs[a].at[c], ssem, rsem, a, sib) for a in range(n)]
        for cp in cps:
            cp.start()
        for a in range(n):
            _rcopy(o_refs[a].at[c], o_refs[a].at[1 - c], ssem, rsem, a, sib).wait_recv()
        for cp in cps:
            cp.wait_send()

    return pl.pallas_call(
        body, name="join_halves", out_shape=[jax.ShapeDtypeStruct(q.shape, q.dtype) for q in qs],
        in_specs=[_ANY] * n, out_specs=[_ANY] * n, input_output_aliases={a: a for a in range(n)},
        scratch_shapes=[pltpu.SemaphoreType.DMA((n,)), pltpu.SemaphoreType.DMA((n,))],
    )(*qs)


def _rows_block(r):
    return _pick(r, 256, _SUB)


def add_sibling(tag, g, r1, place, out_dtype):
    _, _, rows, w = g.shape
    rb = _rows_block(rows)

    def body(pl_ref, g_ref, r_ref, o_ref):
        o_ref[...] = (g_ref[0] + r_ref[...]).astype(out_dtype)

    return pl.pallas_call(
        body, name="add_sibling_" + tag, out_shape=jax.ShapeDtypeStruct((4, rows, w), out_dtype),
        grid_spec=pltpu.PrefetchScalarGridSpec(
            num_scalar_prefetch=1, grid=(4, rows // rb),
            in_specs=[pl.BlockSpec((1, 1, rb, w), lambda k, i, s: (k, s[1], i, 0)),
                      pl.BlockSpec((1, rb, w), lambda k, i, s: (k, i, 0))],
            out_specs=pl.BlockSpec((1, rb, w), lambda k, i, s: (k, i, 0))),
        compiler_params=_cparams(2),
    )(place, g, r1)


def add_chips(tag, p, r2, place):
    _, rows, w = p.shape
    rb = _rows_block(rows)

    def body(pl_ref, p_ref, r_ref, o_ref):
        j = pl_ref[0]
        own = p_ref[0].astype(f32)
        others = [r_ref[0].astype(f32), r_ref[1].astype(f32), r_ref[2].astype(f32)]
        acc = None
        for k in range(4):
            rel = k ^ j
            t = jnp.where(rel == 0, own, jnp.where(rel == 1, others[0], jnp.where(rel == 2, others[1], others[2])))
            acc = t if acc is None else acc + t
        o_ref[0] = acc

    return pl.pallas_call(
        body, name="add_chips_" + tag, out_shape=jax.ShapeDtypeStruct((2, rows, w), f32),
        grid_spec=pltpu.PrefetchScalarGridSpec(
            num_scalar_prefetch=1, grid=(rows // rb,),
            in_specs=[pl.BlockSpec((1, rb, w), lambda i, s: (s[0], i, 0)),
                      pl.BlockSpec((3, rb, w), lambda i, s: (0, i, 0))],
            out_specs=pl.BlockSpec((1, rb, w), lambda i, s: (s[1], i, 0))),
        compiler_params=_cparams(1),
    )(place, p, r2)


def adamw(tag, g, w, m, v):
    rows, wd = g.shape
    rb = _rows_block(rows)

    def body(g_ref, w_ref, m_ref, v_ref, d_ref, nm_ref, nv_ref):
        gv = g_ref[...]
        nm = ADAM_B1 * m_ref[...] + (1.0 - ADAM_B1) * gv
        nv = ADAM_B2 * v_ref[...] + (1.0 - ADAM_B2) * (gv * gv)
        m_hat = nm / (1.0 - ADAM_B1 ** ADAM_STEP)
        v_hat = nv / (1.0 - ADAM_B2 ** ADAM_STEP)
        d_ref[...] = -ADAM_LR * (m_hat / (jnp.sqrt(v_hat) + ADAM_EPS) + ADAM_WD * w_ref[...])
        nm_ref[...] = nm
        nv_ref[...] = nv

    spec = pl.BlockSpec((rb, wd), lambda i: (i, 0))
    return pl.pallas_call(
        body, name="adamw_" + tag, grid=(rows // rb,), in_specs=[spec] * 4, out_specs=[spec] * 3,
        out_shape=[jax.ShapeDtypeStruct(g.shape, f32)] * 3, compiler_params=_cparams(1),
    )(g, w, m, v)


WEIGHTS = ("c_ctx", "norm_w", "w_mod", "b_mod", "w_in", "hg_lb_logits", "hg_norm", "s5_a_re", "s5_a_im", "s5_log_step",
           "s5_b_re", "s5_b_im", "s5_c_re", "s5_c_im", "s5_d", "s5_w_glu", "s5_b_glu", "lru_conv_w", "lru_conv_b",
           "lru_gate_w", "lru_gate_b", "lru_lam", "m2_conv_w", "m2_conv_b", "m2_dt_bias", "m2_a_log", "m2_d", "m2_norm",
           "w_branch", "w_gate", "b_gate", "w_out", "final_norm")
SHARD_AXIS = {"w_mod": 2, "w_in": 2, "hg_lb_logits": 2, "s5_w_glu": 1, "lru_conv_w": 2, "lru_lam": 2, "m2_conv_w": 2,
              "w_branch": 3, "w_gate": 2, "b_gate": 2, "w_out": 1}
BIG = ("w_mod", "w_in", "s5_w_glu", "w_branch", "w_gate", "w_out")
N_CHIPS = 4


def _to_rows(flat, row_unit):
    n = flat.shape[-1]
    per = 2 * row_unit * W_PACK
    total = -(-n // per) * per
    flat = jnp.pad(flat, [(0, 0)] * (flat.ndim - 1) + [(0, total - n)])
    return flat.reshape(flat.shape[:-1] + (2, total // (2 * W_PACK), W_PACK))


SMALL_SHARDED = tuple(n for n in WEIGHTS if n in SHARD_AXIS and n not in BIG)
SMALL_REPLICATED = tuple(n for n in WEIGHTS if n not in SHARD_AXIS)


def _chip_slices(a, axis):
    width = a.shape[axis] // N_CHIPS
    return jnp.stack([lax.slice_in_dim(a, k * width, (k + 1) * width, axis=axis) for k in range(N_CHIPS)])


def _gather_weights(local):
    small = jnp.concatenate([lax.bitcast_convert_type(local[n], bf16).reshape(-1) for n in SMALL_SHARDED])
    got = gather_shards([local[n].astype(bf16) for n in BIG] + [_to_rows(small, 16)])
    full = {}
    for n, g in zip(BIG, got):
        full[n] = jnp.concatenate([g[j] for j in range(N_CHIPS)], axis=SHARD_AXIS[n])
    flat, off = got[-1].reshape(N_CHIPS, -1), 0
    for n in SMALL_SHARDED:
        shp = local[n].shape
        size = 2 * math.prod(shp)
        part = lax.bitcast_convert_type(flat[:, off:off + size].reshape((N_CHIPS,) + shp + (2,)), f32)
        off += size
        full[n] = jnp.concatenate([part[j] for j in range(N_CHIPS)], axis=SHARD_AXIS[n])
    return full


def _pack_small(vals, extra):
    return jnp.concatenate([vals[n].reshape(-1) for n in SMALL_SHARDED + SMALL_REPLICATED] + [extra.reshape(1)])


def _pack_small_grads(grads, loss):
    rep = [grads[n].reshape(-1) for n in SMALL_REPLICATED] + [loss.reshape(1)]
    sh = [_chip_slices(grads[n], SHARD_AXIS[n]).reshape(N_CHIPS, -1) for n in SMALL_SHARDED]
    return jnp.concatenate(sh + [jnp.broadcast_to(r, (N_CHIPS,) + r.shape) for r in rep], axis=1)


def _unpack_small(flat, like):
    out, off = {}, 0
    for n in SMALL_SHARDED + SMALL_REPLICATED:
        size = math.prod(like[n].shape)
        out[n] = flat[off:off + size].reshape(like[n].shape)
        off += size
    return out, flat[off]


def _reduce_grads(tags, gs):
    place = jnp.stack([2 * lax.axis_index("x") + lax.axis_index("y"), lax.axis_index("c")]).astype(jnp.int32)
    pairs = [add_sibling(t, g, r, place, bf16 if t in BIG else f32) for t, g, r in zip(tags, gs, sibling_halves(gs))]
    quads = [add_chips(t, p, r, place) for t, p, r in zip(tags, pairs, scatter_chips(pairs))]
    return join_halves(quads)


def kernel(x, c, ctx, c_ctx, norm_w, w_mod, b_mod, w_in, hg_lb_logits, hg_norm, s5_a_re, s5_a_im, s5_log_step, s5_b_re, s5_b_im, s5_c_re, s5_c_im, s5_d, s5_w_glu, s5_b_glu, lru_conv_w, lru_conv_b, lru_gate_w, lru_gate_b, lru_lam, m2_conv_w, m2_conv_b, m2_dt_bias, m2_a_log, m2_d, m2_norm, w_branch, w_gate, b_gate, w_out, final_norm, loss_target, m_c_ctx, m_norm_w, m_w_mod, m_b_mod, m_w_in, m_hg_lb_logits, m_hg_norm, m_s5_a_re, m_s5_a_im, m_s5_log_step, m_s5_b_re, m_s5_b_im, m_s5_c_re, m_s5_c_im, m_s5_d, m_s5_w_glu, m_s5_b_glu, m_lru_conv_w, m_lru_conv_b, m_lru_gate_w, m_lru_gate_b, m_lru_lam, m_m2_conv_w, m_m2_conv_b, m_m2_dt_bias, m_m2_a_log, m_m2_d, m_m2_norm, m_w_branch, m_w_gate, m_b_gate, m_w_out, m_final_norm, v_c_ctx, v_norm_w, v_w_mod, v_b_mod, v_w_in, v_hg_lb_logits, v_hg_norm, v_s5_a_re, v_s5_a_im, v_s5_log_step, v_s5_b_re, v_s5_b_im, v_s5_c_re, v_s5_c_im, v_s5_d, v_s5_w_glu, v_s5_b_glu, v_lru_conv_w, v_lru_conv_b, v_lru_gate_w, v_lru_gate_b, v_lru_lam, v_m2_conv_w, v_m2_conv_b, v_m2_dt_bias, v_m2_a_log, v_m2_d, v_m2_norm, v_w_branch, v_w_gate, v_b_gate, v_w_out, v_final_norm):
    given = dict(locals())
    w_loc = {n: given[n] for n in WEIGHTS}
    m_loc = {n: given["m_" + n] for n in WEIGHTS}
    v_loc = {n: given["v_" + n] for n in WEIGHTS}

    full = _gather_weights(w_loc)
    params = {n: (full[n] if n in SHARD_AXIS else w_loc[n]) for n in WEIGHTS if n not in BIG}
    big = {n: full[n] for n in BIG}
    def loss_fn(p, s, xx):
        return _forward(p, big, s, xx, ctx[0], c, loss_target[0])

    loss, (g_p, g_s, g_x) = jax.value_and_grad(loss_fn, argnums=(0, 1, 2))(params, _new_slots(big), x[0])
    grads = {**g_p, **_slot_grads(g_s)}

    def rows4(a):
        return a.reshape(a.shape[:2] + (-1, a.shape[-1]))

    g_big = [rows4(_chip_slices(grads[n], SHARD_AXIS[n])) for n in BIG]
    g_small = _to_rows(_pack_small_grads(grads, loss), 64)
    summed = _reduce_grads(list(BIG) + ["small"], g_big + [g_small])

    g_out, d_out, m_out, v_out = {}, {}, {}, {}
    for n, g in zip(BIG, summed):
        shp = w_loc[n].shape
        flat2 = lambda a: a.reshape(-1, shp[-1])
        g_out[n] = g.reshape(shp)
        d, nm, nv = adamw(n, flat2(g), flat2(w_loc[n]), flat2(m_loc[n]), flat2(v_loc[n]))
        d_out[n], m_out[n], v_out[n] = d.reshape(shp), nm.reshape(shp), nv.reshape(shp)
    zero = jnp.zeros((), f32)
    flat = lambda vals: _to_rows(_pack_small(vals, zero), 64).reshape(-1, W_PACK)
    gs = summed[-1].reshape(-1, W_PACK)
    d, nm, nv = adamw("small", gs, flat(w_loc), flat(m_loc), flat(v_loc))
    gsm, loss_out = _unpack_small(gs.reshape(-1), w_loc)
    g_out.update(gsm)
    d_out.update(_unpack_small(d.reshape(-1), w_loc)[0])
    m_out.update(_unpack_small(nm.reshape(-1), w_loc)[0])
    v_out.update(_unpack_small(nv.reshape(-1), w_loc)[0])
    outs = [loss_out, g_x[None]]
    for group in (g_out, d_out, m_out, v_out):
        outs += [group[n] for n in WEIGHTS]
    return tuple(outs)
```

```python
import functools
import math

import jax
import jax.numpy as jnp
from jax import lax
from jax.experimental import pallas as pl
from jax.experimental.pallas import tpu as pltpu

f32 = jnp.float32
bf16 = jnp.bfloat16
_MM_DTYPE = bf16
_HI = lax.Precision.HIGHEST
_MAP_PREC = lax.Precision.HIGH
_VMEM_LIMIT = 56 * 1024 * 1024
_LANE = 128
_SUB = 8

EPS = 1e-6
CONV_W = 4
CHUNK = 64
RB = 256
BR_W = 512
HG_HEADS = 4
HG_DK = 128
S5_GROUPS = 32
S5_GROUP = 16
S5_STATE = 64
LRU_BLOCKS = 8
LRU_C = 8.0
M2_HEADS = 8
M2_HEADDIM = 64
M2_GROUPS = 2
M2_STATE = 64
M2_XBC = BR_W + 2 * M2_GROUPS * M2_STATE
ADAM_LR = 0.001
ADAM_B1 = 0.9
ADAM_B2 = 0.999
ADAM_EPS = 1e-08
ADAM_WD = 0.01
ADAM_STEP = 10

_NN = (((1,), (0,)), ((), ()))
_NT = (((1,), (1,)), ((), ()))
_TN = (((0,), (0,)), ((), ()))


def _silu(x):
    return x * jax.nn.sigmoid(x)


def _softplus(x):
    return jnp.maximum(x, 0.0) + jnp.log1p(jnp.exp(-jnp.abs(x)))


def _one_minus_exp(z):
    series = -z * (1.0 + z * 0.5 * (1.0 + z * (1.0 / 3.0) * (1.0 + z * 0.25 * (1.0 + z * 0.2))))
    return jnp.where(z > -0.05, series, 1.0 - jnp.exp(z))


def _rms(x, w):
    return x * lax.rsqrt(jnp.mean(x * x, axis=-1, keepdims=True) + EPS) * w


def _dot(a, b, dn=_NN, hi=False):
    prec = hi if isinstance(hi, lax.Precision) else (_HI if hi else None)
    return lax.dot_general(a, b, dn, precision=prec, preferred_element_type=f32)


def _cparams(n_grid):
    return pltpu.CompilerParams(dimension_semantics=("arbitrary",) * n_grid, vmem_limit_bytes=_VMEM_LIMIT)


_REV = {"asc": "desc", "d1": "d1r", "desc": "asc", "d1r": "d1"}


def _blk(order, i, n):
    if order == "asc":
        return i
    if order == "desc":
        return n - 1 - i
    if order == "d1":
        return jnp.where(i == 0, 0, n - i)
    return jnp.where(i == n - 1, 0, i + 1)


def _pick(n, cap, unit):
    if n <= cap:
        return n
    best = None
    d = unit
    while d <= cap:
        if n % d == 0:
            best = d
        d += unit
    return n if best is None else best


def _mm_call(name, a, b, mode, hi, out_dtype):
    if mode == "tn":
        k, m = a.shape
        n = b.shape[1]
        tm = _pick(m, 512, _LANE)
        tn = _pick(n, 512, _LANE)
        a_spec = pl.BlockSpec((k, tm), lambda i, j: (0, i))
        b_spec = pl.BlockSpec((k, tn), lambda i, j: (0, j))
    else:
        m, k = a.shape
        tm = _pick(m, max(256, min(1088, 4 * 1024 * 1024 // (k * a.dtype.itemsize))), _SUB)
        a_spec = pl.BlockSpec((tm, k), lambda i, j: (i, 0))
        if mode == "nn":
            n = b.shape[1]
            tn = _pick(n, max(_LANE, (4 * 1024 * 1024 // (k * 4)) // _LANE * _LANE), _LANE)
            b_spec = pl.BlockSpec((k, tn), lambda i, j: (0, j))
        else:
            n = b.shape[0]
            tn = _pick(n, max(_LANE, (4 * 1024 * 1024 // (k * 4)) // _LANE * _LANE), _LANE)
            b_spec = pl.BlockSpec((tn, k), lambda i, j: (j, 0))
    dn = {"nn": _NN, "nt": _NT, "tn": _TN}[mode]

    def body(a_ref, b_ref, o_ref):
        av = a_ref[...]
        bv = b_ref[...]
        if hi:
            av = av.astype(f32)
            bv = bv.astype(f32)
        else:
            av = av.astype(_MM_DTYPE)
            bv = bv.astype(_MM_DTYPE)
        o_ref[...] = _dot(av, bv, dn, hi).astype(o_ref.dtype)

    return pl.pallas_call(
        body, name=name, grid=(m // tm, n // tn), in_specs=[a_spec, b_spec],
        out_specs=pl.BlockSpec((tm, tn), lambda i, j: (i, j)),
        out_shape=jax.ShapeDtypeStruct((m, n), out_dtype), compiler_params=_cparams(2),
    )(a, b)


def mm(name, a, b, slot=None, hi=False):
    @jax.custom_vjp
    def op(a, b, slot):
        return _mm_call(name, a, b, "nn", hi, f32)

    def fwd(a, b, slot):
        return op(a, b, slot), (a, b)

    def bwd(res, g):
        a, b = res
        da = _mm_call(name + "_da", g, b, "nt", hi, a.dtype)
        db = _mm_call(name + "_db", a, g, "tn", hi, f32)
        if slot is None:
            return da, db.astype(b.dtype), None
        return da, jnp.zeros_like(b), db

    op.defvjp(fwd, bwd)
    return op(a, b, slot)


def _sum_nt_call(name, gs, ws, out_dtype):
    m = gs[0].shape[0]
    kdim = ws[0].shape[0]
    tm = _pick(m, 256, _SUB)
    n = len(gs)

    def body(*refs):
        acc = None
        for g_ref, w_ref in zip(refs[:n], refs[n:2 * n]):
            part = _dot(g_ref[...].astype(_MM_DTYPE), w_ref[...].astype(_MM_DTYPE), _NT)
            acc = part if acc is None else acc + part
        refs[2 * n][...] = acc.astype(out_dtype)

    in_specs = [pl.BlockSpec((tm, g.shape[1]), lambda i: (i, 0)) for g in gs]
    in_specs += [pl.BlockSpec(w.shape, lambda i: (0, 0)) for w in ws]
    return pl.pallas_call(
        body, name=name, grid=(m // tm,), in_specs=in_specs, out_specs=pl.BlockSpec((tm, kdim), lambda i: (i, 0)),
        out_shape=jax.ShapeDtypeStruct((m, kdim), out_dtype), compiler_params=_cparams(1),
    )(*gs, *ws)


def multi_mm(tag, names, a, ws, slots):
    @jax.custom_vjp
    def op(a, ws, slots):
        return tuple(_mm_call(tag + n, a, w, "nn", False, f32) for n, w in zip(names, ws))

    def fwd(a, ws, slots):
        return op(a, ws, slots), (a, ws)

    def bwd(res, gs):
        a, ws = res
        dws = [_mm_call(tag + n + "_db", a, g, "tn", False, f32) for n, g in zip(names, gs)]
        da = _sum_nt_call(tag + "da", list(gs), ws, a.dtype)
        return da, [jnp.zeros_like(w) for w in ws], dws

    op.defvjp(fwd, bwd)
    return op(a, list(ws), list(slots))


def _orders(order, n_x, n_o):
    if isinstance(order, str):
        return [order] * n_x, [order] * n_o
    return list(order[0]), list(order[1])


def _row(o, n):
    return lambda i: (_blk(o, i, n), 0)


def _blocked_fwd(name, f, order, rb, params, xs, out_sds, carry_sds):
    t = xs[0].shape[0]
    n = t // rb
    n_p, n_x, n_o, n_c = len(params), len(xs), len(out_sds), len(carry_sds)
    xo, oo = _orders(order, n_x, n_o)

    def body(*refs):
        p_refs = refs[:n_p]
        x_refs = refs[n_p:n_p + n_x]
        o_refs = refs[n_p + n_x:n_p + n_x + n_o]
        st_refs = refs[n_p + n_x + n_o:n_p + n_x + n_o + n_c]
        c_refs = refs[n_p + n_x + n_o + n_c:]
        i = pl.program_id(0)
        blk = _blk(xo[0], i, n)
        p = [r[...] for r in p_refs]
        x = [r[...] for r in x_refs]
        if n_c:
            @pl.when(i == 0)
            def _():
                for c in c_refs:
                    c[...] = jnp.zeros_like(c)
            c_in = [c[...] for c in c_refs]
            for sr, c in zip(st_refs, c_in):
                sr[0] = c
            c_out, ys = f(blk, p, c_in, x)
            for c, v in zip(c_refs, c_out):
                c[...] = v
        else:
            ys = f(blk, p, x)
        for o, y in zip(o_refs, ys):
            o[...] = y.astype(o.dtype)

    in_specs = [pl.BlockSpec(p.shape, lambda i, nd=p.ndim: (0,) * nd) for p in params]
    in_specs += [pl.BlockSpec((rb, x.shape[1]), _row(o, n)) for x, o in zip(xs, xo)]
    out_specs = [pl.BlockSpec((rb, c), _row(o, n)) for (c, _), o in zip(out_sds, oo)]
    out_specs += [pl.BlockSpec((1,) + s, lambda i: (i, 0, 0)) for s in carry_sds]
    out_shape = [jax.ShapeDtypeStruct((t, c), d) for c, d in out_sds]
    out_shape += [jax.ShapeDtypeStruct((n,) + s, f32) for s in carry_sds]
    res = pl.pallas_call(
        body, name=name, grid=(n,), in_specs=in_specs, out_specs=out_specs, out_shape=out_shape,
        scratch_shapes=[pltpu.VMEM(s, f32) for s in carry_sds], compiler_params=_cparams(1),
    )(*params, *xs)
    return list(res[:n_o]), list(res[n_o:])


def _blocked_bwd(name, f, order, rb, params, xs, states, dys, carry_sds):
    t = xs[0].shape[0]
    n = t // rb
    n_p, n_x, n_o, n_c = len(params), len(xs), len(dys), len(carry_sds)
    xo, oo = _orders(order, n_x, n_o)
    xo, oo = [_REV[o] for o in xo], [_REV[o] for o in oo]

    def body(*refs):
        k = 0
        p_refs = refs[k:k + n_p]; k += n_p
        x_refs = refs[k:k + n_x]; k += n_x
        st_refs = refs[k:k + n_c]; k += n_c
        dy_refs = refs[k:k + n_o]; k += n_o
        dp_refs = refs[k:k + n_p]; k += n_p
        dx_refs = refs[k:k + n_x]; k += n_x
        dc_refs = refs[k:]
        i = pl.program_id(0)
        blk = _blk(xo[0], i, n)
        p = [r[...] for r in p_refs]
        x = [r[...] for r in x_refs]
        dy = [r[...] for r in dy_refs]
        if n_c:
            @pl.when(i == 0)
            def _():
                for c in dc_refs:
                    c[...] = jnp.zeros_like(c)
            c_in = [r[0] for r in st_refs]
            dc = [c[...] for c in dc_refs]
            _, vjp = jax.vjp(lambda p_, c_, x_: f(blk, p_, c_, x_), p, c_in, x)
            dp, dcin, dx = vjp((dc, dy))
            for c, v in zip(dc_refs, dcin):
                c[...] = v
        else:
            _, vjp = jax.vjp(lambda p_, x_: f(blk, p_, x_), p, x)
            dp, dx = vjp(dy)

        @pl.when(i == 0)
        def _():
            for r, v in zip(dp_refs, dp):
                r[...] = v

        @pl.when(i > 0)
        def _():
            for r, v in zip(dp_refs, dp):
                r[...] += v
        for r, v in zip(dx_refs, dx):
            r[...] = v.astype(r.dtype)

    in_specs = [pl.BlockSpec(p.shape, lambda i, nd=p.ndim: (0,) * nd) for p in params]
    in_specs += [pl.BlockSpec((rb, x.shape[1]), _row(o, n)) for x, o in zip(xs, xo)]
    in_specs += [pl.BlockSpec((1,) + s, lambda i: (n - 1 - i, 0, 0)) for s in carry_sds]
    in_specs += [pl.BlockSpec((rb, d.shape[1]), _row(o, n)) for d, o in zip(dys, oo)]
    out_specs = [pl.BlockSpec(p.shape, lambda i, nd=p.ndim: (0,) * nd) for p in params]
    out_specs += [pl.BlockSpec((rb, x.shape[1]), _row(o, n)) for x, o in zip(xs, xo)]
    out_shape = [jax.ShapeDtypeStruct(p.shape, f32) for p in params]
    out_shape += [jax.ShapeDtypeStruct(x.shape, x.dtype) for x in xs]
    res = pl.pallas_call(
        body, name=name + "_bwd", grid=(n,), in_specs=in_specs, out_specs=out_specs, out_shape=out_shape,
        scratch_shapes=[pltpu.VMEM(s, f32) for s in carry_sds], compiler_params=_cparams(1),
    )(*params, *xs, *states, *dys)
    return list(res[:n_p]), list(res[n_p:])


def blocked_op(name, f, params, xs, out_sds, order="asc", carry_sds=(), rb=RB):
    carry_sds = tuple(carry_sds)

    @jax.custom_vjp
    def op(params, xs):
        return tuple(_blocked_fwd(name, f, order, rb, params, xs, out_sds, carry_sds)[0])

    def fwd(params, xs):
        ys, states = _blocked_fwd(name, f, order, rb, params, xs, out_sds, carry_sds)
        return tuple(ys), (params, xs, states)

    def bwd(res, dys):
        params, xs, states = res
        dp, dx = _blocked_bwd(name, f, order, rb, params, xs, states, list(dys), carry_sds)
        return list(dp), list(dx)

    op.defvjp(fwd, bwd)
    return op(list(params), list(xs))


def _cscan_call(name, order, asc, a, xr, xi, sr=None, si=None):
    t = xr.shape[0]
    n = t // RB
    tile = xr.shape[1:]
    xspec = pl.BlockSpec((RB,) + tile, lambda i: (_blk(order, i, n), 0, 0))
    aspec = pl.BlockSpec(a.shape, lambda i: (0, 0, 0))
    plane = jax.ShapeDtypeStruct(xr.shape, f32)

    def rowidx(tt):
        return tt if asc else RB - 1 - tt

    if sr is None:
        def body(a_ref, xr_ref, xi_ref, sr_ref, si_ref, c_ref):
            i = pl.program_id(0)

            @pl.when(i == 0)
            def _():
                c_ref[...] = jnp.zeros_like(c_ref)
            ar = a_ref[0]
            ai = a_ref[1]
            a2r = ar * ar - ai * ai
            a2i = 2.0 * ar * ai

            def step(tt, carry):
                cr, ci = carry
                r1 = rowidx(2 * tt)
                r2 = rowidx(2 * tt + 1)
                x1r, x1i, x2r, x2i = xr_ref[r1], xi_ref[r1], xr_ref[r2], xi_ref[r2]
                s1r = ar * cr - ai * ci + x1r
                s1i = ar * ci + ai * cr + x1i
                kr = ar * x1r - ai * x1i + x2r
                ki = ar * x1i + ai * x1r + x2i
                s2r = a2r * cr - a2i * ci + kr
                s2i = a2r * ci + a2i * cr + ki
                sr_ref[r1] = s1r
                si_ref[r1] = s1i
                sr_ref[r2] = s2r
                si_ref[r2] = s2i
                return s2r, s2i
            cr, ci = lax.fori_loop(0, RB // 2, step, (c_ref[0], c_ref[1]), unroll=4)
            c_ref[0] = cr
            c_ref[1] = ci

        return pl.pallas_call(
            body, name=name, grid=(n,), in_specs=[aspec, xspec, xspec], out_specs=[xspec, xspec],
            out_shape=[plane, plane], scratch_shapes=[pltpu.VMEM(a.shape, f32)], compiler_params=_cparams(1),
        )(a, xr, xi)

    def body(a_ref, xr_ref, xi_ref, sr_ref, si_ref, gr_ref, gi_ref, da_ref, c_ref):
        i = pl.program_id(0)

        @pl.when(i == 0)
        def _():
            c_ref[...] = jnp.zeros_like(c_ref)
            da_ref[...] = jnp.zeros_like(da_ref)
        ar = a_ref[0]
        ai = a_ref[1]
        a2r = ar * ar - ai * ai
        a2i = 2.0 * ar * ai

        def step(tt, carry):
            gr, gi, dar, dai, dbr, dbi = carry
            r1 = rowidx(2 * tt)
            r2 = rowidx(2 * tt + 1)
            x1r, x1i, x2r, x2i = xr_ref[r1], xi_ref[r1], xr_ref[r2], xi_ref[r2]
            v1r, v1i, v2r, v2i = sr_ref[r1], si_ref[r1], sr_ref[r2], si_ref[r2]
            g1r = x1r + ar * gr + ai * gi
            g1i = x1i + ar * gi - ai * gr
            kr = x2r + ar * x1r + ai * x1i
            ki = x2i + ar * x1i - ai * x1r
            g2r = kr + a2r * gr + a2i * gi
            g2i = ki + a2r * gi - a2i * gr
            dar = dar + gr * v1r + gi * v1i
            dai = dai + gi * v1r - gr * v1i
            dbr = dbr + g1r * v2r + g1i * v2i
            dbi = dbi + g1i * v2r - g1r * v2i
            gr_ref[r1] = g1r
            gi_ref[r1] = g1i
            gr_ref[r2] = g2r
            gi_ref[r2] = g2i
            return g2r, g2i, dar, dai, dbr, dbi
        z = jnp.zeros(tile, f32)
        gr, gi, dar, dai, dbr, dbi = lax.fori_loop(0, RB // 2, step, (c_ref[0], c_ref[1], z, z, z, z), unroll=4)
        c_ref[0] = gr
        c_ref[1] = gi
        da_ref[0] += dar + dbr
        da_ref[1] += dai + dbi

    return pl.pallas_call(
        body, name=name, grid=(n,), in_specs=[aspec] + [xspec] * 4, out_specs=[xspec, xspec, aspec],
        out_shape=[plane, plane, jax.ShapeDtypeStruct(a.shape, f32)],
        scratch_shapes=[pltpu.VMEM(a.shape, f32)], compiler_params=_cparams(1),
    )(a, xr, xi, sr, si)


def cscan(name, d, a, xr, xi):
    order = "d1" if d else "asc"

    @jax.custom_vjp
    def op(a, xr, xi):
        return tuple(_cscan_call(name, order, d == 0, a, xr, xi))

    def fwd(a, xr, xi):
        sr, si = op(a, xr, xi)
        return (sr, si), (a, sr, si)

    def bwd(res, ds):
        a, sr, si = res
        gr, gi, da = _cscan_call(name + "_bwd", _REV[order], d != 0, a, ds[0], ds[1], sr, si)
        return da, gr, gi

    op.defvjp(fwd, bwd)
    return op(a, xr, xi)


def _bd_call(name, a, b, mode, k=None):
    if mode == "tn":
        t = a.shape[0]
        ck, cn = a.shape[1] // k, b.shape[1] // k

        def body(a_ref, b_ref, o_ref):
            o_ref[0] = _dot(a_ref[...], b_ref[...], _TN, _MAP_PREC)

        return pl.pallas_call(
            body, name=name, grid=(k,),
            in_specs=[pl.BlockSpec((t, ck), lambda j: (0, j)), pl.BlockSpec((t, cn), lambda j: (0, j))],
            out_specs=pl.BlockSpec((1, ck, cn), lambda j: (j, 0, 0)),
            out_shape=jax.ShapeDtypeStruct((k, ck, cn), f32), compiler_params=_cparams(1),
        )(a, b)
    k, ck, cn = b[0][0].shape
    n_i, n_o = len(b), len(b[0])
    n_x = len(a)
    t = a[0].shape[0]
    tm = _pick(t, 1088, _SUB)
    flat = [w for row in b for w in row]
    win, wout, n_out, dn = (ck, cn, n_o, _NN) if mode == "nn" else (cn, ck, n_i, _NT)

    def body(*refs):
        xv = [r[...] for r in refs[:n_x]]
        w_refs = refs[n_x:n_x + len(flat)]
        o_refs = refs[n_x + len(flat):]
        for q in range(n_out):
            acc = None
            for s in range(n_x):
                w = w_refs[s * n_o + q] if mode == "nn" else w_refs[q * n_o + s]
                part = _dot(xv[s], w[0], dn, _MAP_PREC)
                acc = part if acc is None else acc + part
            o_refs[q][...] = acc

    return pl.pallas_call(
        body, name=name, grid=(t // tm, k),
        in_specs=[pl.BlockSpec((tm, win), lambda i, j: (i, j))] * n_x
        + [pl.BlockSpec((1, ck, cn), lambda i, j: (j, 0, 0))] * len(flat),
        out_specs=[pl.BlockSpec((tm, wout), lambda i, j: (i, j))] * n_out,
        out_shape=[jax.ShapeDtypeStruct((t, k * wout), f32)] * n_out, compiler_params=_cparams(2),
    )(*a, *flat)


def bd_mm(name, xs, ws):
    k = ws[0][0].shape[0]

    @jax.custom_vjp
    def op(xs, ws):
        return tuple(_bd_call(name, xs, ws, "nn"))

    def fwd(xs, ws):
        return op(xs, ws), (xs, ws)

    def bwd(res, gs):
        xs, ws = res
        dws = [[_bd_call(name + "_db%d%d" % (i, o), x, g, "tn", k) for o, g in enumerate(gs)] for i, x in enumerate(xs)]
        return list(_bd_call(name + "_da", list(gs), ws, "nt")), dws

    op.defvjp(fwd, bwd)
    return op(list(xs), [list(row) for row in ws])


def _rscan_call(name, order, asc, a, x, hp=None):
    t = x.shape[0]
    n = t // RB
    cshape = x.shape[1:]
    xspec = pl.BlockSpec((RB,) + cshape, lambda i: (_blk(order, i, n), 0, 0))

    def rowidx(tt):
        return tt if asc else RB - 1 - tt

    if hp is None:
        def body(a_ref, x_ref, h_ref, hp_ref, c_ref):
            i = pl.program_id(0)

            @pl.when(i == 0)
            def _():
                c_ref[...] = jnp.zeros_like(c_ref)

            def step(tt, h):
                r1 = rowidx(2 * tt)
                r2 = rowidx(2 * tt + 1)
                a1, a2, x1, x2 = a_ref[r1], a_ref[r2], x_ref[r1], x_ref[r2]
                h1 = a1 * h + x1
                h2 = (a2 * a1) * h + (a2 * x1 + x2)
                hp_ref[r1] = h
                h_ref[r1] = h1
                hp_ref[r2] = h1
                h_ref[r2] = h2
                return h2
            c_ref[...] = lax.fori_loop(0, RB // 2, step, c_ref[...], unroll=4)

        return pl.pallas_call(
            body, name=name, grid=(n,), in_specs=[xspec, xspec], out_specs=[xspec, xspec],
            out_shape=[jax.ShapeDtypeStruct(x.shape, f32)] * 2, scratch_shapes=[pltpu.VMEM(cshape, f32)],
            compiler_params=_cparams(1),
        )(a, x)

    def body(a_ref, x_ref, hp_ref, da_ref, db_ref, c_ref):
        i = pl.program_id(0)

        @pl.when(i == 0)
        def _():
            c_ref[...] = jnp.zeros_like(c_ref)

        def step(tt, c):
            r1 = rowidx(2 * tt)
            r2 = rowidx(2 * tt + 1)
            a1, a2, x1, x2 = a_ref[r1], a_ref[r2], x_ref[r1], x_ref[r2]
            g1 = x1 + c
            k = x2 + a1 * x1
            g2 = k + a1 * c
            db_ref[r1] = g1
            da_ref[r1] = g1 * hp_ref[r1]
            db_ref[r2] = g2
            da_ref[r2] = g2 * hp_ref[r2]
            return a2 * k + (a2 * a1) * c
        c_ref[...] = lax.fori_loop(0, RB // 2, step, c_ref[...], unroll=4)

    return pl.pallas_call(
        body, name=name, grid=(n,), in_specs=[xspec, xspec, xspec], out_specs=[xspec, xspec],
        out_shape=[jax.ShapeDtypeStruct(x.shape, f32)] * 2, scratch_shapes=[pltpu.VMEM(cshape, f32)],
        compiler_params=_cparams(1),
    )(a, x, hp)


def rscan(name, d, a, x):
    order = "d1" if d else "asc"

    @jax.custom_vjp
    def op(a, x):
        return _rscan_call(name, order, d == 0, a, x)[0]

    def fwd(a, x):
        h, hp = _rscan_call(name, order, d == 0, a, x)
        return h, (a, hp)

    def bwd(res, dh):
        a, hp = res
        da, db = _rscan_call(name + "_bwd", _REV[order], d != 0, a, dh, hp)
        return da, db

    op.defvjp(fwd, bwd)
    return op(a, x)


def _mod_row(blk, mod, bm):
    return jnp.where(blk == 0, mod[1:2], mod[0:1]) + bm


def _f_silu(blk, p, x):
    return [_silu(x[0]).astype(bf16)]


def _f_normmod(blk, p, x):
    nw, mod, bm = p
    d = nw.shape[1]
    r = _mod_row(blk, mod, bm)
    return [(_rms(x[0], nw) * (1.0 + r[:, d:2 * d]) + r[:, :d]).astype(bf16)]


def _f_resid(blk, p, x):
    mod, bm = p
    d = x[0].shape[1]
    r = _mod_row(blk, mod, bm)
    return [x[0] + r[:, 2 * d:] * x[1]]


def _f_mix(blk, p, x):
    bg, = p
    gp = x[0]
    d = x[1].shape[1]
    acc = None
    for k in range(4):
        t = jax.nn.sigmoid(gp[:, k * d:(k + 1) * d] + bg[:, k * d:(k + 1) * d]) * x[1 + k]
        acc = t if acc is None else acc + t
    return [acc.astype(bf16)]


def _tri(rev):
    row = lax.broadcasted_iota(jnp.int32, (CHUNK, CHUNK), 0)
    col = lax.broadcasted_iota(jnp.int32, (CHUNK, CHUNK), 1)
    return (col >= row) if rev else (col <= row)


def _chunk_ids(rev):
    ids = list(range(RB // CHUNK))
    return ids[::-1] if rev else ids


def _f_hg(rev):
    def f(blk, p, c, x):
        lb, = p
        st, = c
        qi, fr = x
        q = _silu(qi[:, :BR_W])
        v = qi[:, BR_W:]
        fg = lb + (1.0 - lb) * jax.nn.sigmoid(fr)
        logf = jnp.log(fg)
        k = 1.0 - fg
        m = _tri(rev)
        mf = m.astype(f32)
        outs = [None] * (RB // CHUNK)
        for ci in _chunk_ids(rev):
            sl = slice(CHUNK * ci, CHUNK * ci + CHUNK)
            lf = logf[sl]
            b = _dot(mf, lf, hi=True)
            bend = jnp.sum(lf, axis=0, keepdims=True)
            mid = 0.5 * bend
            qe = q[sl] * jnp.exp(b - mid)
            ke = k[sl] * jnp.exp(mid - b)
            kd = k[sl] * jnp.exp(bend - b)
            qb = q[sl] * jnp.exp(b)
            dec = jnp.exp(bend)
            vc = v[sl]
            oh, ns = [], []
            for hh in range(HG_HEADS):
                cs = slice(HG_DK * hh, HG_DK * hh + HG_DK)
                sth = st[cs]
                att = jnp.where(m, _dot(qe[:, cs], ke[:, cs], _NT), 0.0)
                oh.append(_dot(att, vc[:, cs]) + _dot(qb[:, cs], sth, _NT))
                ns.append(sth * dec[:, cs] + _dot(vc[:, cs], kd[:, cs], _TN))
            st = jnp.concatenate(ns, axis=0)
            outs[ci] = jnp.concatenate(oh, axis=1)
        return [st], [jnp.concatenate(outs, axis=0)]
    return f


def _both(f0, f1, n_p, n_x):
    def f(blk, p, c, x):
        c0, y0 = f0(blk, p[:n_p], c[:1], x[:n_x])
        c1, y1 = f1(blk, p[n_p:], c[1:], x[n_x:])
        return c0 + c1, y0 + y1
    return f


_BOTH_ORDERS = (["asc", "asc", "d1", "d1"], ["asc", "d1"])


def _f_hg_final(blk, p, x):
    nw, = p
    o = x[0] + x[1]
    parts = []
    for hh in range(HG_HEADS):
        cs = slice(HG_DK * hh, HG_DK * hh + HG_DK)
        parts.append(_rms(o[:, cs], nw[:, cs]))
    return [(jnp.concatenate(parts, axis=1) * _silu(x[2])).astype(bf16)]


def _conv(x, cw, cb, blk):
    rows = x.shape[0]
    r = lax.broadcasted_iota(jnp.int32, (rows, 1), 0)
    rm = jnp.where(blk == 0, r, r % CHUNK)
    seg = jnp.where(blk == 0, rows, CHUNK)

    def vmask(o):
        return ((rm + o >= 0) & (rm + o < seg)).astype(f32)

    def shifted(o):
        @jax.custom_vjp
        def sh(x, mo, mn):
            return pltpu.roll(x, (-o) % rows, 0) * mo

        def fwd(x, mo, mn):
            return sh(x, mo, mn), (mo, mn)

        def bwd(res, g):
            mo, mn = res
            return pltpu.roll(g, o % rows, 0) * mn, jnp.zeros_like(mo), jnp.zeros_like(mn)
        sh.defvjp(fwd, bwd)
        return sh(x, vmask(o), vmask(-o))

    lo = (CONV_W - 1) // 2
    out = cb
    for k in range(CONV_W):
        o = k - lo
        out = out + cw[k:k + 1] * (x if o == 0 else shifted(o))
    return out


def _f_lru_a(blk, p, x):
    cw, cb, wg, gb, lam = p
    xc = _conv(x[0], cw, cb, blk)
    n_chunks = BR_W // _LANE
    xk = [xc[:, _LANE * k:_LANE * (k + 1)] for k in range(n_chunks)]

    def gate(j):
        pre = jnp.concatenate([_dot(xk[k], wg[j * n_chunks + k], hi=_MAP_PREC) for k in range(n_chunks)], axis=1)
        return jax.nn.sigmoid(pre + gb[:, BR_W * j:BR_W * (j + 1)])

    outs = []
    for d in range(2):
        r = gate(2 * d)
        ig = gate(2 * d + 1)
        log_a = -LRU_C * r * _softplus(-lam[d:d + 1])
        outs.append(jnp.exp(log_a))
        outs.append(jnp.sqrt(_one_minus_exp(2.0 * log_a)) * (ig * xc))
    return outs


def _f_lru_c(blk, p, x):
    return [((x[0] + x[1]) * _silu(x[2])).astype(bf16)]


def _f_s5_c1(blk, p, x):
    dsk, = p
    return [jax.nn.gelu(x[0] + dsk * x[1])]


def _f_s5_c2(blk, p, x):
    bglu, = p
    return [(x[0] * jax.nn.sigmoid(x[1] + bglu) * _silu(x[2])).astype(bf16)]


def _f_m2_a(blk, p, x):
    cw, cb, dtb = p
    return [_silu(_conv(x[0], cw, cb, blk)), _softplus(x[1] + dtb)]


def _f_ssd(d):
    rev = d == 1
    hpg = M2_HEADS // M2_GROUPS

    def f(blk, p, c, x):
        alog, = p
        st, = c
        xbc, dtp = x
        a = -jnp.exp(alog[:, M2_HEADS * d:M2_HEADS * (d + 1)])
        dt = dtp[:, M2_HEADS * d:M2_HEADS * (d + 1)]
        xs = xbc[:, :BR_W]
        bm = xbc[:, BR_W:BR_W + M2_GROUPS * M2_STATE]
        cm = xbc[:, BR_W + M2_GROUPS * M2_STATE:]
        gw = hpg * M2_HEADDIM
        mf = _tri(rev).astype(f32)
        row = lax.broadcasted_iota(jnp.int32, (CHUNK, gw), 0)
        col = lax.broadcasted_iota(jnp.int32, (CHUNK, gw), 1)
        m4 = (col % CHUNK >= row) if rev else (col % CHUNK <= row)
        spread = (lax.broadcasted_iota(jnp.int32, (hpg, gw), 0)
                  == lax.div(lax.broadcasted_iota(jnp.int32, (hpg, gw), 1), M2_HEADDIM)).astype(f32)
        own = [lax.div(lax.broadcasted_iota(jnp.int32, (1, gw), 1), M2_HEADDIM) == r for r in range(hpg)]
        outs = [None] * (RB // CHUNK)
        for ci in _chunk_ids(rev):
            sl = slice(CHUNK * ci, CHUNK * ci + CHUNK)
            dtc = dt[sl]
            dta = dtc * a
            cum = _dot(mf, dta, hi=True)
            cum_t = cum.T
            dt_t = dtc.T
            ys, ns = [], []
            for g in range(M2_GROUPS):
                hs = slice(hpg * g, hpg * (g + 1))
                bmg = bm[sl, M2_STATE * g:M2_STATE * (g + 1)]
                cmg = cm[sl, M2_STATE * g:M2_STATE * (g + 1)]
                xg = xs[sl, gw * g:gw * (g + 1)]
                stg = st[M2_STATE * g:M2_STATE * (g + 1)]
                cum_i = _dot(cum[:, hs], spread, hi=True)
                cum_j = jnp.concatenate([cum_t[hpg * g + r:hpg * g + r + 1] for r in range(hpg)], axis=1)
                dt_j = jnp.concatenate([dt_t[hpg * g + r:hpg * g + r + 1] for r in range(hpg)], axis=1)
                dt_i = _dot(dtc[:, hs], spread, hi=True)
                cend_g = jnp.sum(_dot(dta[:, hs], spread, hi=True), axis=0, keepdims=True)
                decay = jnp.exp(jnp.where(m4, cum_i - cum_j, -1e30))
                scores = _dot(cmg, jnp.concatenate([bmg] * hpg, axis=0), _NT)
                w = scores * decay * dt_j
                xdiag = jnp.concatenate([jnp.where(own[r], xg, 0.0) for r in range(hpg)], axis=0)
                ys.append(_dot(w, xdiag) + _dot(cmg, stg) * jnp.exp(cum_i))
                wx = jnp.exp(cend_g - cum_i) * dt_i * xg
                ns.append(jnp.exp(cend_g) * stg + _dot(bmg, wx, _TN))
            st = jnp.concatenate(ns, axis=0)
            outs[ci] = jnp.concatenate(ys, axis=1)
        return [st], [jnp.concatenate(outs, axis=0)]
    return f


def _f_m2_c(blk, p, x):
    dsk, nw = p
    y = x[0] + x[1] + dsk * x[2][:, :BR_W]
    return [_rms(y * _silu(x[3]), nw).astype(bf16)]


def _f_loss(blk, p, x):
    fnw, = p
    err = _rms(x[0], fnw) - x[1]
    return [0.5 * jnp.mean(err * err, axis=-1, keepdims=True)]


def _blockdiag(w):
    g, a, b = w.shape
    return jnp.einsum("gab,gh->gahb", w, jnp.eye(g, dtype=w.dtype)).reshape(g * a, g * b)


def _s5_params(l, w):
    a_scan, cds = [], []
    per = _LANE // S5_GROUP

    def chunks(m):
        return jnp.stack([_blockdiag(m[k * per:(k + 1) * per]) for k in range(S5_GROUPS // per)])

    b_re = jnp.transpose(w["s5_b_re"][l], (0, 2, 1))
    b_im = jnp.transpose(w["s5_b_im"][l], (0, 2, 1))
    bd = [chunks(b_re), chunks(b_im)]
    c_re = jnp.transpose(w["s5_c_re"][l], (0, 2, 1))
    c_im = jnp.transpose(w["s5_c_im"][l], (0, 2, 1))
    for d in range(2):
        lam_re = w["s5_a_re"][l, d]
        lam_im = w["s5_a_im"][l, d]
        step = jnp.exp(w["s5_log_step"][l, d])[:, None]
        mag = jnp.exp(lam_re * step)
        ab_re = mag * jnp.cos(lam_im * step)
        ab_im = mag * jnp.sin(lam_im * step)
        den = lam_re * lam_re + lam_im * lam_im
        nr = ab_re - 1.0
        co_re = (nr * lam_re + ab_im * lam_im) / den
        co_im = (ab_im * lam_re - nr * lam_im) / den
        n_state = S5_GROUPS * S5_STATE
        a_scan.append(jnp.stack([ab_re.reshape(_SUB, n_state // _SUB), ab_im.reshape(_SUB, n_state // _SUB)]))
        cp_re = c_re * co_re[:, :, None] - c_im * co_im[:, :, None]
        cp_im = c_re * co_im[:, :, None] + c_im * co_re[:, :, None]
        cds.append([chunks(cp_re), -chunks(cp_im)])
    return a_scan, bd, cds


def _lru_gate(l, w):
    gw = w["lru_gate_w"][l]
    per = _LANE // (BR_W // LRU_BLOCKS)
    chunks = [_blockdiag(gw[d, g, k * per:(k + 1) * per])
              for d in range(2) for g in range(2) for k in range(LRU_BLOCKS // per)]
    return jnp.stack(chunks), w["lru_gate_b"][l].reshape(1, -1)


def _pad_cols(a, n):
    return jnp.pad(a, ((0, 0), (0, n - a.shape[1])))


IN_SIZES = (BR_W,) * 9 + (M2_XBC, 2 * M2_HEADS, BR_W)
IN_OFFS = tuple(sum(IN_SIZES[:i]) for i in range(len(IN_SIZES) + 1))
IN_GROUPS = (("hg_qi", 0, 2, 1024), ("hg_ff", 2, 1, 512), ("hg_fb", 3, 1, 512), ("hg_z", 4, 1, 512),
             ("s5_u", 5, 1, 512), ("s5_z", 6, 1, 512), ("lru_x", 7, 1, 512), ("lru_z", 8, 1, 512),
             ("m2_xbc", 9, 1, 768), ("m2_dt", 10, 1, 128), ("m2_z", 11, 1, 512))


def _new_slots(big):
    slots = {n: jnp.zeros(w.shape, f32) for n, w in big.items() if n not in ("w_in", "w_gate")}
    n_layers, d_model = big["w_in"].shape[:2]
    slots["w_in"] = [{name: jnp.zeros((d_model, width), f32) for name, _, _, width in IN_GROUPS} for _ in range(n_layers)]
    slots["w_gate"] = jnp.zeros((n_layers, d_model, 4 * d_model), f32)
    return slots


def _slot_grads(g):
    out = dict(g)
    out["w_in"] = jnp.stack([
        jnp.concatenate([gl[name][:, :IN_OFFS[s0 + ns] - IN_OFFS[s0]] for name, s0, ns, _ in IN_GROUPS], axis=1)
        for gl in g["w_in"]])
    n_layers, d_model = g["w_gate"].shape[:2]
    out["w_gate"] = jnp.transpose(g["w_gate"].reshape(n_layers, d_model, 4, d_model), (0, 2, 1, 3))
    return out


def _forward(p, big, slots, x, ctx, c, target):
    n_layers = p["norm_w"].shape[0]
    d_model = x.shape[-1]
    xa = jnp.concatenate([ctx, x], axis=0)
    t = xa.shape[0]
    cc = jnp.concatenate([c, p["c_ctx"][None], jnp.zeros((_SUB - 2, d_model), f32)], axis=0)
    lb_all = jnp.cumsum(jax.nn.softmax(p["hg_lb_logits"], axis=0), axis=0)
    scc, = blocked_op("silu_c", _f_silu, [], [cc], [(d_model, bf16)], rb=_SUB)

    for l in range(n_layers):
        tag = "l%d_" % l
        mod = mm(tag + "mod", scc, big["w_mod"][l], slots["w_mod"][l])
        bm = p["b_mod"][l][None]
        h, = blocked_op(tag + "normmod", _f_normmod, [p["norm_w"][l][None], mod, bm], [xa], [(d_model, bf16)])
        gnames = [g[0] for g in IN_GROUPS]
        wvs = [_pad_cols(big["w_in"][l][:, IN_OFFS[s0]:IN_OFFS[s0 + ns]], width) for _, s0, ns, width in IN_GROUPS]
        u = dict(zip(gnames, multi_mm(tag + "in_", gnames, h, wvs, [slots["w_in"][l][g] for g in gnames])))

        o_dirs = blocked_op(tag + "hg", _both(_f_hg(False), _f_hg(True), 1, 2), [lb_all[l, 0][None], lb_all[l, 1][None]],
                            [u["hg_qi"], u["hg_ff"], u["hg_qi"], u["hg_fb"]], [(BR_W, f32)] * 2,
                            order=_BOTH_ORDERS, carry_sds=[(BR_W, HG_DK)] * 2)
        y_hg, = blocked_op(tag + "hg_fin", _f_hg_final, [p["hg_norm"][l][None]], list(o_dirs) + [u["hg_z"]], [(BR_W, bf16)])

        a_scan, bd, cds = _s5_params(l, p)
        n_state = S5_GROUPS * S5_STATE
        bu = [b.reshape(t, _SUB, n_state // _SUB) for b in bd_mm(tag + "s5_bu", [u["s5_u"]], [bd])]
        planes, maps = [], []
        for d in range(2):
            s = cscan(tag + "s5_scan%d" % d, d, a_scan[d], bu[0], bu[1])
            planes += [s[0].reshape(t, n_state), s[1].reshape(t, n_state)]
            maps += [[cds[d][0]], [cds[d][1]]]
        ysum, = bd_mm(tag + "s5_c", planes, maps)
        g5, = blocked_op(tag + "s5_c1", _f_s5_c1, [p["s5_d"][l][None]], [ysum, u["s5_u"]], [(BR_W, f32)])
        gl = mm(tag + "s5_glu", g5, big["s5_w_glu"][l], slots["s5_w_glu"][l])
        y_s5, = blocked_op(tag + "s5_c2", _f_s5_c2, [p["s5_b_glu"][l][None]], [g5, gl, u["s5_z"]], [(BR_W, bf16)])

        wg, gb = _lru_gate(l, p)
        ab = blocked_op(tag + "lru_a", _f_lru_a,
                        [p["lru_conv_w"][l], p["lru_conv_b"][l][None], wg, gb, p["lru_lam"][l]],
                        [u["lru_x"]], [(BR_W, f32)] * 4)
        hs = []
        for d in range(2):
            a3 = ab[2 * d].reshape(t, 4, BR_W // 4)
            b3 = ab[2 * d + 1].reshape(t, 4, BR_W // 4)
            hs.append(rscan(tag + "lru_scan%d" % d, d, a3, b3).reshape(t, BR_W))
        y_lru, = blocked_op(tag + "lru_c", _f_lru_c, [], hs + [u["lru_z"]], [(BR_W, bf16)])

        dtb = _pad_cols(p["m2_dt_bias"][l].reshape(1, -1), _LANE)
        xbc, dtp = blocked_op(tag + "m2_a", _f_m2_a, [p["m2_conv_w"][l], p["m2_conv_b"][l][None], dtb],
                              [u["m2_xbc"], u["m2_dt"]], [(M2_XBC, f32), (_LANE, f32)])
        alog = _pad_cols(p["m2_a_log"][l].reshape(1, -1), _LANE)
        y_dirs = blocked_op(tag + "ssd", _both(_f_ssd(0), _f_ssd(1), 1, 2), [alog, alog], [xbc, dtp, xbc, dtp],
                            [(BR_W, f32)] * 2, order=_BOTH_ORDERS,
                            carry_sds=[(M2_GROUPS * M2_STATE, BR_W // M2_GROUPS)] * 2)
        dsk = jnp.repeat(p["m2_d"][l], M2_HEADDIM)[None]
        y_m2, = blocked_op(tag + "m2_c", _f_m2_c, [dsk, p["m2_norm"][l][None]], list(y_dirs) + [xbc, u["m2_z"]], [(BR_W, bf16)])

        wg_all = jnp.transpose(big["w_gate"][l], (1, 0, 2)).reshape(d_model, 4 * d_model)
        gp = mm(tag + "gate", h, wg_all, slots["w_gate"][l])
        bs = [mm(tag + "br%d" % k, yk, big["w_branch"][l, k], slots["w_branch"][l, k])
              for k, yk in enumerate((y_hg, y_s5, y_lru, y_m2))]
        mix, = blocked_op(tag + "mix", _f_mix, [p["b_gate"][l].reshape(1, -1)], [gp] + bs, [(d_model, bf16)])
        o = mm(tag + "out", mix, big["w_out"][l], slots["w_out"][l])
        xa, = blocked_op(tag + "resid", _f_resid, [mod, bm], [xa, o], [(d_model, f32)])

    rl, = blocked_op("loss", _f_loss, [p["final_norm"][None]], [xa[ctx.shape[0]:], target], [(1, f32)])
    return jnp.sum(rl)


_MESH = pl.DeviceIdType.MESH
_ANY = pl.BlockSpec(memory_space=pl.ANY)
W_PACK = 1024


def _place():
    x, y, c = lax.axis_index("x"), lax.axis_index("y"), lax.axis_index("c")
    chips = [(x, 1 - y), (1 - x, y), (1 - x, 1 - y)]
    return x, y, c, chips


def _rcopy(src, dst, ssem, rsem, k, to):
    return pltpu.make_async_remote_copy(src_ref=src, dst_ref=dst, send_sem=ssem.at[k], recv_sem=rsem.at[k],
                                        device_id=to, device_id_type=_MESH)


def gather_shards(xs):
    n = len(xs)

    def body(*refs):
        x_refs, o_refs = refs[:n], refs[n:2 * n]
        ssem, rsem, lsem = refs[2 * n:]
        x, y, c, chips = _place()
        j = 2 * x + y
        sib = (x, y, 1 - c)
        mine = [pltpu.make_async_copy(x_refs[a], o_refs[a].at[j], lsem.at[a]) for a in range(n)]
        for cp in mine:
            cp.start()
        first = [_rcopy(x_refs[a].at[c], o_refs[a].at[j, c], ssem, rsem, 6 * a + r, (*chips[r], c))
                 for r in range(3) for a in range(n)]
        for cp in first:
            cp.start()
        passed = []
        for r in range(3):
            jr = j ^ (r + 1)
            for a in range(n):
                _rcopy(x_refs[a].at[c], o_refs[a].at[jr, c], ssem, rsem, 6 * a + r, sib).wait_recv()
                cp = _rcopy(o_refs[a].at[jr, c], o_refs[a].at[jr, c], ssem, rsem, 6 * a + 3 + r, sib)
                cp.start()
                passed.append(cp)
        for r in range(3):
            jr = j ^ (r + 1)
            for a in range(n):
                _rcopy(x_refs[a].at[c], o_refs[a].at[jr, 1 - c], ssem, rsem, 6 * a + 3 + r, sib).wait_recv()
        for cp in first + passed:
            cp.wait_send()
        for cp in mine:
            cp.wait()

    return pl.pallas_call(
        body, name="gather_shards", out_shape=[jax.ShapeDtypeStruct((4,) + x.shape, x.dtype) for x in xs],
        in_specs=[_ANY] * n, out_specs=[_ANY] * n,
        scratch_shapes=[pltpu.SemaphoreType.DMA((6 * n,)), pltpu.SemaphoreType.DMA((6 * n,)), pltpu.SemaphoreType.DMA((n,))],
    )(*xs)


def sibling_halves(gs):
    n = len(gs)

    def body(*refs):
        g_refs, o_refs = refs[:n], refs[n:2 * n]
        ssem, rsem = refs[2 * n:]
        x, y, c, _ = _place()
        sib = (x, y, 1 - c)
        cps = [_rcopy(g_refs[a].at[k, 1 - c], o_refs[a].at[k], ssem, rsem, 4 * a + k, sib)
               for k in range(4) for a in range(n)]
        for cp in cps:
            cp.start()
        for cp in cps:
            cp.wait()

    return pl.pallas_call(
        body, name="sibling_halves", out_shape=[jax.ShapeDtypeStruct((4,) + g.shape[2:], g.dtype) for g in gs],
        in_specs=[_ANY] * n, out_specs=[_ANY] * n,
        scratch_shapes=[pltpu.SemaphoreType.DMA((4 * n,)), pltpu.SemaphoreType.DMA((4 * n,))],
    )(*gs)


def scatter_chips(ps):
    n = len(ps)

    def body(*refs):
        p_refs, o_refs = refs[:n], refs[n:2 * n]
        ssem, rsem = refs[2 * n:]
        x, y, c, chips = _place()
        j = 2 * x + y
        cps = [_rcopy(p_refs[a].at[j ^ (r + 1)], o_refs[a].at[r], ssem, rsem, 3 * a + r, (*chips[r], c))
               for r in range(3) for a in range(n)]
        for cp in cps:
            cp.start()
        for cp in cps:
            cp.wait()

    return pl.pallas_call(
        body, name="scatter_chips", out_shape=[jax.ShapeDtypeStruct((3,) + p.shape[1:], p.dtype) for p in ps],
        in_specs=[_ANY] * n, out_specs=[_ANY] * n,
        scratch_shapes=[pltpu.SemaphoreType.DMA((3 * n,)), pltpu.SemaphoreType.DMA((3 * n,))],
    )(*ps)


def join_halves(qs):
    n = len(qs)

    def body(*refs):
        o_refs = refs[n:2 * n]
        ssem, rsem = refs[2 * n:]
        x, y, c, _ = _place()
        sib = (x, y, 1 - c)
        cps = [_rcopy(o_refs[a].at[c], o_refs[a].at[c], ssem, rsem, a, sib) for a in range(n)]
        for cp in cps:
            cp.start()
        for a in range(n):
            _rcopy(o_refs[a].at[c], o_refs[a].at[1 - c], ssem, rsem, a, sib).wait_recv()
        for cp in cps:
            cp.wait_send()

    return pl.pallas_call(
        body, name="join_halves", out_shape=[jax.ShapeDtypeStruct(q.shape, q.dtype) for q in qs],
        in_specs=[_ANY] * n, out_specs=[_ANY] * n, input_output_aliases={a: a for a in range(n)},
        scratch_shapes=[pltpu.SemaphoreType.DMA((n,)), pltpu.SemaphoreType.DMA((n,))],
    )(*qs)


def _rows_block(r):
    return _pick(r, 256, _SUB)


def add_sibling(tag, g, r1, place, out_dtype):
    _, _, rows, w = g.shape
    rb = _rows_block(rows)

    def body(pl_ref, g_ref, r_ref, o_ref):
        o_ref[...] = (g_ref[0] + r_ref[...]).astype(out_dtype)

    return pl.pallas_call(
        body, name="add_sibling_" + tag, out_shape=jax.ShapeDtypeStruct((4, rows, w), out_dtype),
        grid_spec=pltpu.PrefetchScalarGridSpec(
            num_scalar_prefetch=1, grid=(4, rows // rb),
            in_specs=[pl.BlockSpec((1, 1, rb, w), lambda k, i, s: (k, s[1], i, 0)),
                      pl.BlockSpec((1, rb, w), lambda k, i, s: (k, i, 0))],
            out_specs=pl.BlockSpec((1, rb, w), lambda k, i, s: (k, i, 0))),
        compiler_params=_cparams(2),
    )(place, g, r1)


def add_chips(tag, p, r2, place):
    _, rows, w = p.shape
    rb = _rows_block(rows)

    def body(pl_ref, p_ref, r_ref, o_ref):
        j = pl_ref[0]
        own = p_ref[0].astype(f32)
        others = [r_ref[0].astype(f32), r_ref[1].astype(f32), r_ref[2].astype(f32)]
        acc = None
        for k in range(4):
            rel = k ^ j
            t = jnp.where(rel == 0, own, jnp.where(rel == 1, others[0], jnp.where(rel == 2, others[1], others[2])))
            acc = t if acc is None else acc + t
        o_ref[0] = acc

    return pl.pallas_call(
        body, name="add_chips_" + tag, out_shape=jax.ShapeDtypeStruct((2, rows, w), f32),
        grid_spec=pltpu.PrefetchScalarGridSpec(
            num_scalar_prefetch=1, grid=(rows // rb,),
            in_specs=[pl.BlockSpec((1, rb, w), lambda i, s: (s[0], i, 0)),
                      pl.BlockSpec((3, rb, w), lambda i, s: (0, i, 0))],
            out_specs=pl.BlockSpec((1, rb, w), lambda i, s: (s[1], i, 0))),
        compiler_params=_cparams(1),
    )(place, p, r2)


def adamw(tag, g, w, m, v):
    rows, wd = g.shape
    rb = _rows_block(rows)

    def body(g_ref, w_ref, m_ref, v_ref, d_ref, nm_ref, nv_ref):
        gv = g_ref[...]
        nm = ADAM_B1 * m_ref[...] + (1.0 - ADAM_B1) * gv
        nv = ADAM_B2 * v_ref[...] + (1.0 - ADAM_B2) * (gv * gv)
        m_hat = nm / (1.0 - ADAM_B1 ** ADAM_STEP)
        v_hat = nv / (1.0 - ADAM_B2 ** ADAM_STEP)
        d_ref[...] = -ADAM_LR * (m_hat / (jnp.sqrt(v_hat) + ADAM_EPS) + ADAM_WD * w_ref[...])
        nm_ref[...] = nm
        nv_ref[...] = nv

    spec = pl.BlockSpec((rb, wd), lambda i: (i, 0))
    return pl.pallas_call(
        body, name="adamw_" + tag, grid=(rows // rb,), in_specs=[spec] * 4, out_specs=[spec] * 3,
        out_shape=[jax.ShapeDtypeStruct(g.shape, f32)] * 3, compiler_params=_cparams(1),
    )(g, w, m, v)


WEIGHTS = ("c_ctx", "norm_w", "w_mod", "b_mod", "w_in", "hg_lb_logits", "hg_norm", "s5_a_re", "s5_a_im", "s5_log_step",
           "s5_b_re", "s5_b_im", "s5_c_re", "s5_c_im", "s5_d", "s5_w_glu", "s5_b_glu", "lru_conv_w", "lru_conv_b",
           "lru_gate_w", "lru_gate_b", "lru_lam", "m2_conv_w", "m2_conv_b", "m2_dt_bias", "m2_a_log", "m2_d", "m2_norm",
           "w_branch", "w_gate", "b_gate", "w_out", "final_norm")
SHARD_AXIS = {"w_mod": 2, "w_in": 2, "hg_lb_logits": 2, "s5_w_glu": 1, "lru_conv_w": 2, "lru_lam": 2, "m2_conv_w": 2,
              "w_branch": 3, "w_gate": 2, "b_gate": 2, "w_out": 1}
BIG = ("w_mod", "w_in", "s5_w_glu", "w_branch", "w_gate", "w_out")
N_CHIPS = 4


def _to_rows(flat, row_unit):
    n = flat.shape[-1]
    per = 2 * row_unit * W_PACK
    total = -(-n // per) * per
    flat = jnp.pad(flat, [(0, 0)] * (flat.ndim - 1) + [(0, total - n)])
    return flat.reshape(flat.shape[:-1] + (2, total // (2 * W_PACK), W_PACK))


SMALL_SHARDED = tuple(n for n in WEIGHTS if n in SHARD_AXIS and n not in BIG)
SMALL_REPLICATED = tuple(n for n in WEIGHTS if n not in SHARD_AXIS)


def _chip_slices(a, axis):
    width = a.shape[axis] // N_CHIPS
    return jnp.stack([lax.slice_in_dim(a, k * width, (k + 1) * width, axis=axis) for k in range(N_CHIPS)])


def _gather_weights(local):
    small = jnp.concatenate([lax.bitcast_convert_type(local[n], bf16).reshape(-1) for n in SMALL_SHARDED])
    got = gather_shards([local[n].astype(bf16) for n in BIG] + [_to_rows(small, 16)])
    full = {}
    for n, g in zip(BIG, got):
        full[n] = jnp.concatenate([g[j] for j in range(N_CHIPS)], axis=SHARD_AXIS[n])
    flat, off = got[-1].reshape(N_CHIPS, -1), 0
    for n in SMALL_SHARDED:
        shp = local[n].shape
        size = 2 * math.prod(shp)
        part = lax.bitcast_convert_type(flat[:, off:off + size].reshape((N_CHIPS,) + shp + (2,)), f32)
        off += size
        full[n] = jnp.concatenate([part[j] for j in range(N_CHIPS)], axis=SHARD_AXIS[n])
    return full


def _whole_rows(v):
    n = v.shape[-1]
    return jnp.pad(v, [(0, 0)] * (v.ndim - 1) + [(0, -n % W_PACK)])


def _pack_small(vals, extra):
    return jnp.concatenate([_whole_rows(vals[n].reshape(-1)) for n in SMALL_SHARDED + SMALL_REPLICATED]
                           + [_whole_rows(extra.reshape(1))])


def _pack_small_grads(grads, loss):
    rep = [grads[n].reshape(-1) for n in SMALL_REPLICATED] + [loss.reshape(1)]
    sh = [_chip_slices(grads[n], SHARD_AXIS[n]).reshape(N_CHIPS, -1) for n in SMALL_SHARDED]
    return jnp.concatenate([_whole_rows(a) for a in sh]
                           + [_whole_rows(jnp.broadcast_to(r, (N_CHIPS,) + r.shape)) for r in rep], axis=1)


def _unpack_small(rows, like):
    out, r0 = {}, 0
    for n in SMALL_SHARDED + SMALL_REPLICATED:
        size = math.prod(like[n].shape)
        nr = -(-size // W_PACK)
        piece = lax.optimization_barrier(rows[r0:r0 + nr])
        out[n] = piece.reshape(-1)[:size].reshape(like[n].shape)
        r0 += nr
    return out, lax.optimization_barrier(rows[r0:r0 + 1])[0, 0]


def _reduce_grads(tags, gs):
    place = jnp.stack([2 * lax.axis_index("x") + lax.axis_index("y"), lax.axis_index("c")]).astype(jnp.int32)
    pairs = [add_sibling(t, g, r, place, bf16 if t in BIG else f32) for t, g, r in zip(tags, gs, sibling_halves(gs))]
    quads = [add_chips(t, p, r, place) for t, p, r in zip(tags, pairs, scatter_chips(pairs))]
    return join_halves(quads)


def kernel(x, c, ctx, c_ctx, norm_w, w_mod, b_mod, w_in, hg_lb_logits, hg_norm, s5_a_re, s5_a_im, s5_log_step, s5_b_re, s5_b_im, s5_c_re, s5_c_im, s5_d, s5_w_glu, s5_b_glu, lru_conv_w, lru_conv_b, lru_gate_w, lru_gate_b, lru_lam, m2_conv_w, m2_conv_b, m2_dt_bias, m2_a_log, m2_d, m2_norm, w_branch, w_gate, b_gate, w_out, final_norm, loss_target, m_c_ctx, m_norm_w, m_w_mod, m_b_mod, m_w_in, m_hg_lb_logits, m_hg_norm, m_s5_a_re, m_s5_a_im, m_s5_log_step, m_s5_b_re, m_s5_b_im, m_s5_c_re, m_s5_c_im, m_s5_d, m_s5_w_glu, m_s5_b_glu, m_lru_conv_w, m_lru_conv_b, m_lru_gate_w, m_lru_gate_b, m_lru_lam, m_m2_conv_w, m_m2_conv_b, m_m2_dt_bias, m_m2_a_log, m_m2_d, m_m2_norm, m_w_branch, m_w_gate, m_b_gate, m_w_out, m_final_norm, v_c_ctx, v_norm_w, v_w_mod, v_b_mod, v_w_in, v_hg_lb_logits, v_hg_norm, v_s5_a_re, v_s5_a_im, v_s5_log_step, v_s5_b_re, v_s5_b_im, v_s5_c_re, v_s5_c_im, v_s5_d, v_s5_w_glu, v_s5_b_glu, v_lru_conv_w, v_lru_conv_b, v_lru_gate_w, v_lru_gate_b, v_lru_lam, v_m2_conv_w, v_m2_conv_b, v_m2_dt_bias, v_m2_a_log, v_m2_d, v_m2_norm, v_w_branch, v_w_gate, v_b_gate, v_w_out, v_final_norm):
    given = dict(locals())
    w_loc = {n: given[n] for n in WEIGHTS}
    m_loc = {n: given["m_" + n] for n in WEIGHTS}
    v_loc = {n: given["v_" + n] for n in WEIGHTS}

    full = _gather_weights(w_loc)
    params = {n: (full[n] if n in SHARD_AXIS else w_loc[n]) for n in WEIGHTS if n not in BIG}
    big = {n: full[n] for n in BIG}
    def loss_fn(p, s, xx):
        return _forward(p, big, s, xx, ctx[0], c, loss_target[0])

    loss, (g_p, g_s, g_x) = jax.value_and_grad(loss_fn, argnums=(0, 1, 2))(params, _new_slots(big), x[0])
    grads = {**g_p, **_slot_grads(g_s)}

    def rows4(a):
        return a.reshape(a.shape[:2] + (-1, a.shape[-1]))

    g_big = [rows4(_chip_slices(grads[n], SHARD_AXIS[n])) for n in BIG]
    g_small = _to_rows(_pack_small_grads(grads, loss), 64)
    summed = _reduce_grads(list(BIG) + ["small"], g_big + [g_small])

    g_out, d_out, m_out, v_out = {}, {}, {}, {}
    for n, g in zip(BIG, summed):
        shp = w_loc[n].shape
        flat2 = lambda a: a.reshape(-1, shp[-1])
        g_out[n] = g.reshape(shp)
        d, nm, nv = adamw(n, flat2(g), flat2(w_loc[n]), flat2(m_loc[n]), flat2(v_loc[n]))
        d_out[n], m_out[n], v_out[n] = d.reshape(shp), nm.reshape(shp), nv.reshape(shp)
    zero = jnp.zeros((), f32)
    flat = lambda vals: _to_rows(_pack_small(vals, zero), 64).reshape(-1, W_PACK)
    gs = summed[-1].reshape(-1, W_PACK)
    d, nm, nv = adamw("small", gs, flat(w_loc), flat(m_loc), flat(v_loc))
    gsm, loss_out = _unpack_small(gs, w_loc)
    g_out.update(gsm)
    d_out.update(_unpack_small(d, w_loc)[0])
    m_out.update(_unpack_small(nm, w_loc)[0])
    v_out.update(_unpack_small(nv, w_loc)[0])
    outs = [loss_out, g_x[None]]
    for group in (g_out, d_out, m_out, v_out):
        outs += [group[n] for n in WEIGHTS]
    return tuple(outs)
```

```python
import functools
import math

import jax
import jax.numpy as jnp
from jax import lax
from jax.experimental import pallas as pl
from jax.experimental.pallas import tpu as pltpu

f32 = jnp.float32
bf16 = jnp.bfloat16
_MM_DTYPE = bf16
_HI = lax.Precision.HIGHEST
_MAP_PREC = lax.Precision.HIGH
_VMEM_LIMIT = 56 * 1024 * 1024
_LANE = 128
_SUB = 8

EPS = 1e-6
CONV_W = 4
CHUNK = 64
RB = 256
BR_W = 512
HG_HEADS = 4
HG_DK = 128
S5_GROUPS = 32
S5_GROUP = 16
S5_STATE = 64
LRU_BLOCKS = 8
LRU_C = 8.0
M2_HEADS = 8
M2_HEADDIM = 64
M2_GROUPS = 2
M2_STATE = 64
M2_XBC = BR_W + 2 * M2_GROUPS * M2_STATE
ADAM_LR = 0.001
ADAM_B1 = 0.9
ADAM_B2 = 0.999
ADAM_EPS = 1e-08
ADAM_WD = 0.01
ADAM_STEP = 10

_NN = (((1,), (0,)), ((), ()))
_NT = (((1,), (1,)), ((), ()))
_TN = (((0,), (0,)), ((), ()))


def _silu(x):
    return x * jax.nn.sigmoid(x)


def _softplus(x):
    return jnp.maximum(x, 0.0) + jnp.log1p(jnp.exp(-jnp.abs(x)))


def _one_minus_exp(z):
    series = -z * (1.0 + z * 0.5 * (1.0 + z * (1.0 / 3.0) * (1.0 + z * 0.25 * (1.0 + z * 0.2))))
    return jnp.where(z > -0.05, series, 1.0 - jnp.exp(z))


def _rms(x, w):
    return x * lax.rsqrt(jnp.mean(x * x, axis=-1, keepdims=True) + EPS) * w


def _dot(a, b, dn=_NN, hi=False):
    prec = hi if isinstance(hi, lax.Precision) else (_HI if hi else None)
    return lax.dot_general(a, b, dn, precision=prec, preferred_element_type=f32)


def _cparams(n_grid):
    return pltpu.CompilerParams(dimension_semantics=("arbitrary",) * n_grid, vmem_limit_bytes=_VMEM_LIMIT)


_REV = {"asc": "desc", "d1": "d1r", "desc": "asc", "d1r": "d1"}


def _blk(order, i, n):
    if order == "asc":
        return i
    if order == "desc":
        return n - 1 - i
    if order == "d1":
        return jnp.where(i == 0, 0, n - i)
    return jnp.where(i == n - 1, 0, i + 1)


def _pick(n, cap, unit):
    if n <= cap:
        return n
    best = None
    d = unit
    while d <= cap:
        if n % d == 0:
            best = d
        d += unit
    return n if best is None else best


def _mm_call(name, a, b, mode, hi, out_dtype):
    if mode == "tn":
        k, m = a.shape
        n = b.shape[1]
        tm = _pick(m, 512, _LANE)
        tn = _pick(n, 512, _LANE)
        a_spec = pl.BlockSpec((k, tm), lambda i, j: (0, i))
        b_spec = pl.BlockSpec((k, tn), lambda i, j: (0, j))
    else:
        m, k = a.shape
        tm = _pick(m, max(256, min(1088, 4 * 1024 * 1024 // (k * a.dtype.itemsize))), _SUB)
        a_spec = pl.BlockSpec((tm, k), lambda i, j: (i, 0))
        if mode == "nn":
            n = b.shape[1]
            tn = _pick(n, max(_LANE, (4 * 1024 * 1024 // (k * 4)) // _LANE * _LANE), _LANE)
            b_spec = pl.BlockSpec((k, tn), lambda i, j: (0, j))
        else:
            n = b.shape[0]
            tn = _pick(n, max(_LANE, (4 * 1024 * 1024 // (k * 4)) // _LANE * _LANE), _LANE)
            b_spec = pl.BlockSpec((tn, k), lambda i, j: (j, 0))
    dn = {"nn": _NN, "nt": _NT, "tn": _TN}[mode]

    def body(a_ref, b_ref, o_ref):
        av = a_ref[...]
        bv = b_ref[...]
        if hi:
            av = av.astype(f32)
            bv = bv.astype(f32)
        else:
            av = av.astype(_MM_DTYPE)
            bv = bv.astype(_MM_DTYPE)
        o_ref[...] = _dot(av, bv, dn, hi).astype(o_ref.dtype)

    return pl.pallas_call(
        body, name=name, grid=(m // tm, n // tn), in_specs=[a_spec, b_spec],
        out_specs=pl.BlockSpec((tm, tn), lambda i, j: (i, j)),
        out_shape=jax.ShapeDtypeStruct((m, n), out_dtype), compiler_params=_cparams(2),
    )(a, b)


def mm(name, a, b, slot=None, hi=False, out_dtype=f32):
    @jax.custom_vjp
    def op(a, b, slot):
        return _mm_call(name, a, b, "nn", hi, out_dtype)

    def fwd(a, b, slot):
        return op(a, b, slot), (a, b)

    def bwd(res, g):
        a, b = res
        da = _mm_call(name + "_da", g, b, "nt", hi, a.dtype)
        db = _mm_call(name + "_db", a, g, "tn", hi, f32)
        if slot is None:
            return da, db.astype(b.dtype), None
        return da, jnp.zeros_like(b), db

    op.defvjp(fwd, bwd)
    return op(a, b, slot)


def _sum_nt_call(name, gs, ws, out_dtype):
    m = gs[0].shape[0]
    kdim = ws[0].shape[0]
    tm = _pick(m, 256, _SUB)
    n = len(gs)

    def body(*refs):
        acc = None
        for g_ref, w_ref in zip(refs[:n], refs[n:2 * n]):
            part = _dot(g_ref[...].astype(_MM_DTYPE), w_ref[...].astype(_MM_DTYPE), _NT)
            acc = part if acc is None else acc + part
        refs[2 * n][...] = acc.astype(out_dtype)

    in_specs = [pl.BlockSpec((tm, g.shape[1]), lambda i: (i, 0)) for g in gs]
    in_specs += [pl.BlockSpec(w.shape, lambda i: (0, 0)) for w in ws]
    return pl.pallas_call(
        body, name=name, grid=(m // tm,), in_specs=in_specs, out_specs=pl.BlockSpec((tm, kdim), lambda i: (i, 0)),
        out_shape=jax.ShapeDtypeStruct((m, kdim), out_dtype), compiler_params=_cparams(1),
    )(*gs, *ws)


def multi_mm(tag, names, a, ws, slots):
    @jax.custom_vjp
    def op(a, ws, slots):
        return tuple(_mm_call(tag + n, a, w, "nn", False, f32) for n, w in zip(names, ws))

    def fwd(a, ws, slots):
        return op(a, ws, slots), (a, ws)

    def bwd(res, gs):
        a, ws = res
        dws = [_mm_call(tag + n + "_db", a, g, "tn", False, f32) for n, g in zip(names, gs)]
        da = _sum_nt_call(tag + "da", list(gs), ws, a.dtype)
        return da, [jnp.zeros_like(w) for w in ws], dws

    op.defvjp(fwd, bwd)
    return op(a, list(ws), list(slots))


def _orders(order, n_x, n_o):
    if isinstance(order, str):
        return [order] * n_x, [order] * n_o
    return list(order[0]), list(order[1])


def _row(o, n):
    return lambda i: (_blk(o, i, n), 0)


def _blocked_fwd(name, f, order, rb, params, xs, out_sds, carry_sds):
    t = xs[0].shape[0]
    n = t // rb
    n_p, n_x, n_o, n_c = len(params), len(xs), len(out_sds), len(carry_sds)
    xo, oo = _orders(order, n_x, n_o)

    def body(*refs):
        p_refs = refs[:n_p]
        x_refs = refs[n_p:n_p + n_x]
        o_refs = refs[n_p + n_x:n_p + n_x + n_o]
        st_refs = refs[n_p + n_x + n_o:n_p + n_x + n_o + n_c]
        c_refs = refs[n_p + n_x + n_o + n_c:]
        i = pl.program_id(0)
        blk = _blk(xo[0], i, n)
        p = [r[...] for r in p_refs]
        x = [r[...] for r in x_refs]
        if n_c:
            @pl.when(i == 0)
            def _():
                for c in c_refs:
                    c[...] = jnp.zeros_like(c)
            c_in = [c[...] for c in c_refs]
            for sr, c in zip(st_refs, c_in):
                sr[0] = c
            c_out, ys = f(blk, p, c_in, x)
            for c, v in zip(c_refs, c_out):
                c[...] = v
        else:
            ys = f(blk, p, x)
        for o, y in zip(o_refs, ys):
            o[...] = y.astype(o.dtype)

    in_specs = [pl.BlockSpec(p.shape, lambda i, nd=p.ndim: (0,) * nd) for p in params]
    in_specs += [pl.BlockSpec((rb, x.shape[1]), _row(o, n)) for x, o in zip(xs, xo)]
    out_specs = [pl.BlockSpec((rb, c), _row(o, n)) for (c, _), o in zip(out_sds, oo)]
    out_specs += [pl.BlockSpec((1,) + s, lambda i: (i, 0, 0)) for s in carry_sds]
    out_shape = [jax.ShapeDtypeStruct((t, c), d) for c, d in out_sds]
    out_shape += [jax.ShapeDtypeStruct((n,) + s, f32) for s in carry_sds]
    res = pl.pallas_call(
        body, name=name, grid=(n,), in_specs=in_specs, out_specs=out_specs, out_shape=out_shape,
        scratch_shapes=[pltpu.VMEM(s, f32) for s in carry_sds], compiler_params=_cparams(1),
    )(*params, *xs)
    return list(res[:n_o]), list(res[n_o:])


def _blocked_bwd(name, f, order, rb, params, xs, states, dys, carry_sds):
    t = xs[0].shape[0]
    n = t // rb
    n_p, n_x, n_o, n_c = len(params), len(xs), len(dys), len(carry_sds)
    xo, oo = _orders(order, n_x, n_o)
    xo, oo = [_REV[o] for o in xo], [_REV[o] for o in oo]

    def body(*refs):
        k = 0
        p_refs = refs[k:k + n_p]; k += n_p
        x_refs = refs[k:k + n_x]; k += n_x
        st_refs = refs[k:k + n_c]; k += n_c
        dy_refs = refs[k:k + n_o]; k += n_o
        dp_refs = refs[k:k + n_p]; k += n_p
        dx_refs = refs[k:k + n_x]; k += n_x
        dc_refs = refs[k:]
        i = pl.program_id(0)
        blk = _blk(xo[0], i, n)
        p = [r[...] for r in p_refs]
        x = [r[...] for r in x_refs]
        dy = [r[...] for r in dy_refs]
        if n_c:
            @pl.when(i == 0)
            def _():
                for c in dc_refs:
                    c[...] = jnp.zeros_like(c)
            c_in = [r[0] for r in st_refs]
            dc = [c[...] for c in dc_refs]
            _, vjp = jax.vjp(lambda p_, c_, x_: f(blk, p_, c_, x_), p, c_in, x)
            dp, dcin, dx = vjp((dc, dy))
            for c, v in zip(dc_refs, dcin):
                c[...] = v
        else:
            _, vjp = jax.vjp(lambda p_, x_: f(blk, p_, x_), p, x)
            dp, dx = vjp(dy)

        @pl.when(i == 0)
        def _():
            for r, v in zip(dp_refs, dp):
                r[...] = v

        @pl.when(i > 0)
        def _():
            for r, v in zip(dp_refs, dp):
                r[...] += v
        for r, v in zip(dx_refs, dx):
            r[...] = v.astype(r.dtype)

    in_specs = [pl.BlockSpec(p.shape, lambda i, nd=p.ndim: (0,) * nd) for p in params]
    in_specs += [pl.BlockSpec((rb, x.shape[1]), _row(o, n)) for x, o in zip(xs, xo)]
    in_specs += [pl.BlockSpec((1,) + s, lambda i: (n - 1 - i, 0, 0)) for s in carry_sds]
    in_specs += [pl.BlockSpec((rb, d.shape[1]), _row(o, n)) for d, o in zip(dys, oo)]
    out_specs = [pl.BlockSpec(p.shape, lambda i, nd=p.ndim: (0,) * nd) for p in params]
    out_specs += [pl.BlockSpec((rb, x.shape[1]), _row(o, n)) for x, o in zip(xs, xo)]
    out_shape = [jax.ShapeDtypeStruct(p.shape, f32) for p in params]
    out_shape += [jax.ShapeDtypeStruct(x.shape, x.dtype) for x in xs]
    res = pl.pallas_call(
        body, name=name + "_bwd", grid=(n,), in_specs=in_specs, out_specs=out_specs, out_shape=out_shape,
        scratch_shapes=[pltpu.VMEM(s, f32) for s in carry_sds], compiler_params=_cparams(1),
    )(*params, *xs, *states, *dys)
    return list(res[:n_p]), list(res[n_p:])


def blocked_op(name, f, params, xs, out_sds, order="asc", carry_sds=(), rb=RB):
    carry_sds = tuple(carry_sds)

    @jax.custom_vjp
    def op(params, xs):
        return tuple(_blocked_fwd(name, f, order, rb, params, xs, out_sds, carry_sds)[0])

    def fwd(params, xs):
        ys, states = _blocked_fwd(name, f, order, rb, params, xs, out_sds, carry_sds)
        return tuple(ys), (params, xs, states)

    def bwd(res, dys):
        params, xs, states = res
        dp, dx = _blocked_bwd(name, f, order, rb, params, xs, states, list(dys), carry_sds)
        return list(dp), list(dx)

    op.defvjp(fwd, bwd)
    return op(list(params), list(xs))


def _cscan_call(name, order, asc, a, xr, xi, sr=None, si=None):
    t = xr.shape[0]
    n = t // RB
    tile = xr.shape[1:]
    xspec = pl.BlockSpec((RB,) + tile, lambda i: (_blk(order, i, n), 0, 0))
    aspec = pl.BlockSpec(a.shape, lambda i: (0, 0, 0))
    plane = jax.ShapeDtypeStruct(xr.shape, f32)

    def rowidx(tt):
        return tt if asc else RB - 1 - tt

    if sr is None:
        def body(a_ref, xr_ref, xi_ref, sr_ref, si_ref, c_ref):
            i = pl.program_id(0)

            @pl.when(i == 0)
            def _():
                c_ref[...] = jnp.zeros_like(c_ref)
            ar = a_ref[0]
            ai = a_ref[1]
            a2r = ar * ar - ai * ai
            a2i = 2.0 * ar * ai

            def step(tt, carry):
                cr, ci = carry
                r1 = rowidx(2 * tt)
                r2 = rowidx(2 * tt + 1)
                x1r, x1i, x2r, x2i = xr_ref[r1], xi_ref[r1], xr_ref[r2], xi_ref[r2]
                s1r = ar * cr - ai * ci + x1r
                s1i = ar * ci + ai * cr + x1i
                kr = ar * x1r - ai * x1i + x2r
                ki = ar * x1i + ai * x1r + x2i
                s2r = a2r * cr - a2i * ci + kr
                s2i = a2r * ci + a2i * cr + ki
                sr_ref[r1] = s1r
                si_ref[r1] = s1i
                sr_ref[r2] = s2r
                si_ref[r2] = s2i
                return s2r, s2i
            cr, ci = lax.fori_loop(0, RB // 2, step, (c_ref[0], c_ref[1]), unroll=4)
            c_ref[0] = cr
            c_ref[1] = ci

        return pl.pallas_call(
            body, name=name, grid=(n,), in_specs=[aspec, xspec, xspec], out_specs=[xspec, xspec],
            out_shape=[plane, plane], scratch_shapes=[pltpu.VMEM(a.shape, f32)], compiler_params=_cparams(1),
        )(a, xr, xi)

    def body(a_ref, xr_ref, xi_ref, sr_ref, si_ref, gr_ref, gi_ref, da_ref, c_ref):
        i = pl.program_id(0)

        @pl.when(i == 0)
        def _():
            c_ref[...] = jnp.zeros_like(c_ref)
            da_ref[...] = jnp.zeros_like(da_ref)
        ar = a_ref[0]
        ai = a_ref[1]
        a2r = ar * ar - ai * ai
        a2i = 2.0 * ar * ai

        def step(tt, carry):
            gr, gi, dar, dai, dbr, dbi = carry
            r1 = rowidx(2 * tt)
            r2 = rowidx(2 * tt + 1)
            x1r, x1i, x2r, x2i = xr_ref[r1], xi_ref[r1], xr_ref[r2], xi_ref[r2]
            v1r, v1i, v2r, v2i = sr_ref[r1], si_ref[r1], sr_ref[r2], si_ref[r2]
            g1r = x1r + ar * gr + ai * gi
            g1i = x1i + ar * gi - ai * gr
            kr = x2r + ar * x1r + ai * x1i
            ki = x2i + ar * x1i - ai * x1r
            g2r = kr + a2r * gr + a2i * gi
            g2i = ki + a2r * gi - a2i * gr
            dar = dar + gr * v1r + gi * v1i
            dai = dai + gi * v1r - gr * v1i
            dbr = dbr + g1r * v2r + g1i * v2i
            dbi = dbi + g1i * v2r - g1r * v2i
            gr_ref[r1] = g1r
            gi_ref[r1] = g1i
            gr_ref[r2] = g2r
            gi_ref[r2] = g2i
            return g2r, g2i, dar, dai, dbr, dbi
        z = jnp.zeros(tile, f32)
        gr, gi, dar, dai, dbr, dbi = lax.fori_loop(0, RB // 2, step, (c_ref[0], c_ref[1], z, z, z, z), unroll=4)
        c_ref[0] = gr
        c_ref[1] = gi
        da_ref[0] += dar + dbr
        da_ref[1] += dai + dbi

    return pl.pallas_call(
        body, name=name, grid=(n,), in_specs=[aspec] + [xspec] * 4, out_specs=[xspec, xspec, aspec],
        out_shape=[plane, plane, jax.ShapeDtypeStruct(a.shape, f32)],
        scratch_shapes=[pltpu.VMEM(a.shape, f32)], compiler_params=_cparams(1),
    )(a, xr, xi, sr, si)


def s5_states(tag, u, bd, a0, a1):
    t = u.shape[0]
    tile = a0.shape[1:]

    def run(u, bd, a0, a1):
        bu = [b.reshape((t,) + tile) for b in _bd_call(tag + "s5_bu", [u], [bd], "nn")]
        s0 = _cscan_call(tag + "s5_scan0", "asc", True, a0, bu[0], bu[1])
        s1 = _cscan_call(tag + "s5_scan1", "d1", False, a1, bu[0], bu[1])
        return (s0[0], s0[1], s1[0], s1[1])

    @jax.custom_vjp
    def op(u, bd, a0, a1):
        return run(u, bd, a0, a1)

    def fwd(u, bd, a0, a1):
        s = run(u, bd, a0, a1)
        return s, (u, bd, a0, a1, s)

    def bwd(res, ds):
        u, bd, a0, a1, s = res
        g0r, g0i, da0 = _cscan_call(tag + "s5_scan0_bwd", "desc", False, a0, ds[0], ds[1], s[0], s[1])
        g1r, g1i, da1 = _cscan_call(tag + "s5_scan1_bwd", "d1r", True, a1, ds[2], ds[3], s[2], s[3])
        gs = [g.reshape(t, -1) for g in (g0r, g0i, g1r, g1i)]
        du, = _bd_call(tag + "s5_bu_da", gs, [[bd[0], bd[1], bd[0], bd[1]]], "nt")
        k = bd[0].shape[0]
        dbd = [_bd_call(tag + "s5_bu_db%d" % j, u, gs[j], "tn", k) + _bd_call(tag + "s5_bu_db%d" % (j + 2), u, gs[j + 2], "tn", k)
               for j in range(2)]
        return du, dbd, da0, da1

    op.defvjp(fwd, bwd)
    return op(u, list(bd), a0, a1)


def _bd_call(name, a, b, mode, k=None):
    if mode == "tn":
        t = a.shape[0]
        ck, cn = a.shape[1] // k, b.shape[1] // k

        def body(a_ref, b_ref, o_ref):
            o_ref[0] = _dot(a_ref[...], b_ref[...], _TN, _MAP_PREC)

        return pl.pallas_call(
            body, name=name, grid=(k,),
            in_specs=[pl.BlockSpec((t, ck), lambda j: (0, j)), pl.BlockSpec((t, cn), lambda j: (0, j))],
            out_specs=pl.BlockSpec((1, ck, cn), lambda j: (j, 0, 0)),
            out_shape=jax.ShapeDtypeStruct((k, ck, cn), f32), compiler_params=_cparams(1),
        )(a, b)
    k, ck, cn = b[0][0].shape
    n_i, n_o = len(b), len(b[0])
    n_x = len(a)
    t = a[0].shape[0]
    tm = _pick(t, 1088, _SUB)
    flat = [w for row in b for w in row]
    win, wout, n_out, dn = (ck, cn, n_o, _NN) if mode == "nn" else (cn, ck, n_i, _NT)

    def body(*refs):
        xv = [r[...] for r in refs[:n_x]]
        w_refs = refs[n_x:n_x + len(flat)]
        o_refs = refs[n_x + len(flat):]
        for q in range(n_out):
            acc = None
            for s in range(n_x):
                w = w_refs[s * n_o + q] if mode == "nn" else w_refs[q * n_o + s]
                part = _dot(xv[s], w[0], dn, _MAP_PREC)
                acc = part if acc is None else acc + part
            o_refs[q][...] = acc

    return pl.pallas_call(
        body, name=name, grid=(t // tm, k),
        in_specs=[pl.BlockSpec((tm, win), lambda i, j: (i, j))] * n_x
        + [pl.BlockSpec((1, ck, cn), lambda i, j: (j, 0, 0))] * len(flat),
        out_specs=[pl.BlockSpec((tm, wout), lambda i, j: (i, j))] * n_out,
        out_shape=[jax.ShapeDtypeStruct((t, k * wout), f32)] * n_out, compiler_params=_cparams(2),
    )(*a, *flat)


def bd_mm(name, xs, ws):
    k = ws[0][0].shape[0]

    @jax.custom_vjp
    def op(xs, ws):
        return tuple(_bd_call(name, xs, ws, "nn"))

    def fwd(xs, ws):
        return op(xs, ws), (xs, ws)

    def bwd(res, gs):
        xs, ws = res
        dws = [[_bd_call(name + "_db%d%d" % (i, o), x, g, "tn", k) for o, g in enumerate(gs)] for i, x in enumerate(xs)]
        return list(_bd_call(name + "_da", list(gs), ws, "nt")), dws

    op.defvjp(fwd, bwd)
    return op(list(xs), [list(row) for row in ws])


def _rscan_call(name, order, asc, a, x, hp=None):
    t = x.shape[0]
    n = t // RB
    cshape = x.shape[1:]
    xspec = pl.BlockSpec((RB,) + cshape, lambda i: (_blk(order, i, n), 0, 0))

    def rowidx(tt):
        return tt if asc else RB - 1 - tt

    if hp is None:
        def body(a_ref, x_ref, h_ref, hp_ref, c_ref):
            i = pl.program_id(0)

            @pl.when(i == 0)
            def _():
                c_ref[...] = jnp.zeros_like(c_ref)

            def step(tt, h):
                r1 = rowidx(2 * tt)
                r2 = rowidx(2 * tt + 1)
                a1, a2, x1, x2 = a_ref[r1], a_ref[r2], x_ref[r1], x_ref[r2]
                h1 = a1 * h + x1
                h2 = (a2 * a1) * h + (a2 * x1 + x2)
                hp_ref[r1] = h
                h_ref[r1] = h1
                hp_ref[r2] = h1
                h_ref[r2] = h2
                return h2
            c_ref[...] = lax.fori_loop(0, RB // 2, step, c_ref[...], unroll=4)

        return pl.pallas_call(
            body, name=name, grid=(n,), in_specs=[xspec, xspec], out_specs=[xspec, xspec],
            out_shape=[jax.ShapeDtypeStruct(x.shape, f32)] * 2, scratch_shapes=[pltpu.VMEM(cshape, f32)],
            compiler_params=_cparams(1),
        )(a, x)

    def body(a_ref, x_ref, hp_ref, da_ref, db_ref, c_ref):
        i = pl.program_id(0)

        @pl.when(i == 0)
        def _():
            c_ref[...] = jnp.zeros_like(c_ref)

        def step(tt, c):
            r1 = rowidx(2 * tt)
            r2 = rowidx(2 * tt + 1)
            a1, a2, x1, x2 = a_ref[r1], a_ref[r2], x_ref[r1], x_ref[r2]
            g1 = x1 + c
            k = x2 + a1 * x1
            g2 = k + a1 * c
            db_ref[r1] = g1
            da_ref[r1] = g1 * hp_ref[r1]
            db_ref[r2] = g2
            da_ref[r2] = g2 * hp_ref[r2]
            return a2 * k + (a2 * a1) * c
        c_ref[...] = lax.fori_loop(0, RB // 2, step, c_ref[...], unroll=4)

    return pl.pallas_call(
        body, name=name, grid=(n,), in_specs=[xspec, xspec, xspec], out_specs=[xspec, xspec],
        out_shape=[jax.ShapeDtypeStruct(x.shape, f32)] * 2, scratch_shapes=[pltpu.VMEM(cshape, f32)],
        compiler_params=_cparams(1),
    )(a, x, hp)


def rscan(name, d, a, x):
    order = "d1" if d else "asc"

    @jax.custom_vjp
    def op(a, x):
        return _rscan_call(name, order, d == 0, a, x)[0]

    def fwd(a, x):
        h, hp = _rscan_call(name, order, d == 0, a, x)
        return h, (a, hp)

    def bwd(res, dh):
        a, hp = res
        da, db = _rscan_call(name + "_bwd", _REV[order], d != 0, a, dh, hp)
        return da, db

    op.defvjp(fwd, bwd)
    return op(a, x)


def _mod_row(blk, mod, bm):
    return jnp.where(blk == 0, mod[1:2], mod[0:1]) + bm


def _f_silu(blk, p, x):
    return [_silu(x[0]).astype(bf16)]


def _f_normmod(blk, p, x):
    nw, mod, bm = p
    d = nw.shape[1]
    r = _mod_row(blk, mod, bm)
    return [(_rms(x[0], nw) * (1.0 + r[:, d:2 * d]) + r[:, :d]).astype(bf16)]


def _f_resid(blk, p, x):
    mod, bm = p
    d = x[0].shape[1]
    r = _mod_row(blk, mod, bm)
    return [x[0] + r[:, 2 * d:] * x[1]]


def _f_mix(blk, p, x):
    bg, = p
    gp = x[0]
    d = x[1].shape[1]
    acc = None
    for k in range(4):
        t = jax.nn.sigmoid(gp[:, k * d:(k + 1) * d] + bg[:, k * d:(k + 1) * d]) * x[1 + k]
        acc = t if acc is None else acc + t
    return [acc.astype(bf16)]


def _tri(rev):
    row = lax.broadcasted_iota(jnp.int32, (CHUNK, CHUNK), 0)
    col = lax.broadcasted_iota(jnp.int32, (CHUNK, CHUNK), 1)
    return (col >= row) if rev else (col <= row)


def _chunk_ids(rev):
    ids = list(range(RB // CHUNK))
    return ids[::-1] if rev else ids


def _f_hg(rev):
    def f(blk, p, c, x):
        lb, = p
        st, = c
        qi, fr = x
        q = _silu(qi[:, :BR_W])
        v = qi[:, BR_W:]
        fg = lb + (1.0 - lb) * jax.nn.sigmoid(fr)
        logf = jnp.log(fg)
        k = 1.0 - fg
        m = _tri(rev)
        mf = m.astype(f32)
        outs = [None] * (RB // CHUNK)
        for ci in _chunk_ids(rev):
            sl = slice(CHUNK * ci, CHUNK * ci + CHUNK)
            lf = logf[sl]
            b = _dot(mf, lf, hi=True)
            bend = jnp.sum(lf, axis=0, keepdims=True)
            mid = 0.5 * bend
            qe = q[sl] * jnp.exp(b - mid)
            ke = k[sl] * jnp.exp(mid - b)
            kd = k[sl] * jnp.exp(bend - b)
            qb = q[sl] * jnp.exp(b)
            dec = jnp.exp(bend)
            vc = v[sl]
            oh, ns = [], []
            for hh in range(HG_HEADS):
                cs = slice(HG_DK * hh, HG_DK * hh + HG_DK)
                sth = st[cs]
                att = jnp.where(m, _dot(qe[:, cs], ke[:, cs], _NT), 0.0)
                oh.append(_dot(att, vc[:, cs]) + _dot(qb[:, cs], sth, _NT))
                ns.append(sth * dec[:, cs] + _dot(vc[:, cs], kd[:, cs], _TN))
            st = jnp.concatenate(ns, axis=0)
            outs[ci] = jnp.concatenate(oh, axis=1)
        return [st], [jnp.concatenate(outs, axis=0)]
    return f


def _both(f0, f1, n_p, n_x):
    def f(blk, p, c, x):
        c0, y0 = f0(blk, p[:n_p], c[:1], x[:n_x])
        c1, y1 = f1(blk, p[n_p:], c[1:], x[n_x:])
        return c0 + c1, y0 + y1
    return f


_BOTH_ORDERS = (["asc", "asc", "d1", "d1"], ["asc", "d1"])


def _f_hg_final(blk, p, x):
    nw, = p
    o = x[0] + x[1]
    parts = []
    for hh in range(HG_HEADS):
        cs = slice(HG_DK * hh, HG_DK * hh + HG_DK)
        parts.append(_rms(o[:, cs], nw[:, cs]))
    return [(jnp.concatenate(parts, axis=1) * _silu(x[2])).astype(bf16)]


def _conv(x, cw, cb, blk):
    rows = x.shape[0]
    r = lax.broadcasted_iota(jnp.int32, (rows, 1), 0)
    rm = jnp.where(blk == 0, r, r % CHUNK)
    seg = jnp.where(blk == 0, rows, CHUNK)

    def vmask(o):
        return ((rm + o >= 0) & (rm + o < seg)).astype(f32)

    def shifted(o):
        @jax.custom_vjp
        def sh(x, mo, mn):
            return pltpu.roll(x, (-o) % rows, 0) * mo

        def fwd(x, mo, mn):
            return sh(x, mo, mn), (mo, mn)

        def bwd(res, g):
            mo, mn = res
            return pltpu.roll(g, o % rows, 0) * mn, jnp.zeros_like(mo), jnp.zeros_like(mn)
        sh.defvjp(fwd, bwd)
        return sh(x, vmask(o), vmask(-o))

    lo = (CONV_W - 1) // 2
    out = cb
    for k in range(CONV_W):
        o = k - lo
        out = out + cw[k:k + 1] * (x if o == 0 else shifted(o))
    return out


def _f_lru_a(blk, p, x):
    cw, cb, wg, gb, lam = p
    xc = _conv(x[0], cw, cb, blk)
    n_chunks = BR_W // _LANE
    xk = [xc[:, _LANE * k:_LANE * (k + 1)] for k in range(n_chunks)]

    def gate(j):
        pre = jnp.concatenate([_dot(xk[k], wg[j * n_chunks + k], hi=_MAP_PREC) for k in range(n_chunks)], axis=1)
        return jax.nn.sigmoid(pre + gb[:, BR_W * j:BR_W * (j + 1)])

    outs = []
    for d in range(2):
        r = gate(2 * d)
        ig = gate(2 * d + 1)
        log_a = -LRU_C * r * _softplus(-lam[d:d + 1])
        outs.append(jnp.exp(log_a))
        outs.append(jnp.sqrt(_one_minus_exp(2.0 * log_a)) * (ig * xc))
    return outs


def _f_lru_c(blk, p, x):
    return [((x[0] + x[1]) * _silu(x[2])).astype(bf16)]


def _f_s5_c1(blk, p, x):
    dsk, = p
    return [jax.nn.gelu(x[0] + dsk * x[1])]


def _f_s5_c2(blk, p, x):
    bglu, = p
    return [(x[0] * jax.nn.sigmoid(x[1] + bglu) * _silu(x[2])).astype(bf16)]


def _f_m2_a(blk, p, x):
    cw, cb, dtb = p
    return [_silu(_conv(x[0], cw, cb, blk)), _softplus(x[1] + dtb)]


def _f_ssd(d):
    rev = d == 1
    hpg = M2_HEADS // M2_GROUPS

    def f(blk, p, c, x):
        alog, = p
        st, = c
        xbc, dtp = x
        a = -jnp.exp(alog[:, M2_HEADS * d:M2_HEADS * (d + 1)])
        dt = dtp[:, M2_HEADS * d:M2_HEADS * (d + 1)]
        xs = xbc[:, :BR_W]
        bm = xbc[:, BR_W:BR_W + M2_GROUPS * M2_STATE]
        cm = xbc[:, BR_W + M2_GROUPS * M2_STATE:]
        gw = hpg * M2_HEADDIM
        mf = _tri(rev).astype(f32)
        row = lax.broadcasted_iota(jnp.int32, (CHUNK, gw), 0)
        col = lax.broadcasted_iota(jnp.int32, (CHUNK, gw), 1)
        m4 = (col % CHUNK >= row) if rev else (col % CHUNK <= row)
        spread = (lax.broadcasted_iota(jnp.int32, (hpg, gw), 0)
                  == lax.div(lax.broadcasted_iota(jnp.int32, (hpg, gw), 1), M2_HEADDIM)).astype(f32)
        own = [lax.div(lax.broadcasted_iota(jnp.int32, (1, gw), 1), M2_HEADDIM) == r for r in range(hpg)]
        outs = [None] * (RB // CHUNK)
        for ci in _chunk_ids(rev):
            sl = slice(CHUNK * ci, CHUNK * ci + CHUNK)
            dtc = dt[sl]
            dta = dtc * a
            cum = _dot(mf, dta, hi=True)
            cum_t = cum.T
            dt_t = dtc.T
            ys, ns = [], []
            for g in range(M2_GROUPS):
                hs = slice(hpg * g, hpg * (g + 1))
                bmg = bm[sl, M2_STATE * g:M2_STATE * (g + 1)]
                cmg = cm[sl, M2_STATE * g:M2_STATE * (g + 1)]
                xg = xs[sl, gw * g:gw * (g + 1)]
                stg = st[M2_STATE * g:M2_STATE * (g + 1)]
                cum_i = _dot(cum[:, hs], spread, hi=True)
                cum_j = jnp.concatenate([cum_t[hpg * g + r:hpg * g + r + 1] for r in range(hpg)], axis=1)
                dt_j = jnp.concatenate([dt_t[hpg * g + r:hpg * g + r + 1] for r in range(hpg)], axis=1)
                dt_i = _dot(dtc[:, hs], spread, hi=True)
                cend_g = jnp.sum(_dot(dta[:, hs], spread, hi=True), axis=0, keepdims=True)
                decay = jnp.exp(jnp.where(m4, cum_i - cum_j, -1e30))
                scores = _dot(cmg, jnp.concatenate([bmg] * hpg, axis=0), _NT)
                w = scores * decay * dt_j
                xdiag = jnp.concatenate([jnp.where(own[r], xg, 0.0) for r in range(hpg)], axis=0)
                ys.append(_dot(w, xdiag) + _dot(cmg, stg) * jnp.exp(cum_i))
                wx = jnp.exp(cend_g - cum_i) * dt_i * xg
                ns.append(jnp.exp(cend_g) * stg + _dot(bmg, wx, _TN))
            st = jnp.concatenate(ns, axis=0)
            outs[ci] = jnp.concatenate(ys, axis=1)
        return [st], [jnp.concatenate(outs, axis=0)]
    return f


def _f_m2_c(blk, p, x):
    dsk, nw = p
    y = x[0] + x[1] + dsk * x[2][:, :BR_W]
    return [_rms(y * _silu(x[3]), nw).astype(bf16)]


def _f_loss(blk, p, x):
    fnw, = p
    err = _rms(x[0], fnw) - x[1]
    return [0.5 * jnp.mean(err * err, axis=-1, keepdims=True)]


def _blockdiag(w):
    g, a, b = w.shape
    return jnp.einsum("gab,gh->gahb", w, jnp.eye(g, dtype=w.dtype)).reshape(g * a, g * b)


def _s5_params(l, w):
    a_scan, cds = [], []
    per = _LANE // S5_GROUP

    def chunks(m):
        return jnp.stack([_blockdiag(m[k * per:(k + 1) * per]) for k in range(S5_GROUPS // per)])

    b_re = jnp.transpose(w["s5_b_re"][l], (0, 2, 1))
    b_im = jnp.transpose(w["s5_b_im"][l], (0, 2, 1))
    bd = [chunks(b_re), chunks(b_im)]
    c_re = jnp.transpose(w["s5_c_re"][l], (0, 2, 1))
    c_im = jnp.transpose(w["s5_c_im"][l], (0, 2, 1))
    for d in range(2):
        lam_re = w["s5_a_re"][l, d]
        lam_im = w["s5_a_im"][l, d]
        step = jnp.exp(w["s5_log_step"][l, d])[:, None]
        mag = jnp.exp(lam_re * step)
        ab_re = mag * jnp.cos(lam_im * step)
        ab_im = mag * jnp.sin(lam_im * step)
        den = lam_re * lam_re + lam_im * lam_im
        nr = ab_re - 1.0
        co_re = (nr * lam_re + ab_im * lam_im) / den
        co_im = (ab_im * lam_re - nr * lam_im) / den
        n_state = S5_GROUPS * S5_STATE
        a_scan.append(jnp.stack([ab_re.reshape(_SUB, n_state // _SUB), ab_im.reshape(_SUB, n_state // _SUB)]))
        cp_re = c_re * co_re[:, :, None] - c_im * co_im[:, :, None]
        cp_im = c_re * co_im[:, :, None] + c_im * co_re[:, :, None]
        cds.append([chunks(cp_re), -chunks(cp_im)])
    return a_scan, bd, cds


def _lru_gate(l, w):
    gw = w["lru_gate_w"][l]
    per = _LANE // (BR_W // LRU_BLOCKS)
    chunks = [_blockdiag(gw[d, g, k * per:(k + 1) * per])
              for d in range(2) for g in range(2) for k in range(LRU_BLOCKS // per)]
    return jnp.stack(chunks), w["lru_gate_b"][l].reshape(1, -1)


def _pad_cols(a, n):
    return jnp.pad(a, ((0, 0), (0, n - a.shape[1])))


IN_SIZES = (BR_W,) * 9 + (M2_XBC, 2 * M2_HEADS, BR_W)
IN_OFFS = tuple(sum(IN_SIZES[:i]) for i in range(len(IN_SIZES) + 1))
IN_GROUPS = (("hg_qi", 0, 2, 1024), ("hg_ff", 2, 1, 512), ("hg_fb", 3, 1, 512), ("hg_z", 4, 1, 512),
             ("s5_u", 5, 1, 512), ("s5_z", 6, 1, 512), ("lru_x", 7, 1, 512), ("lru_z", 8, 1, 512),
             ("m2_xbc", 9, 1, 768), ("m2_dt", 10, 1, 128), ("m2_z", 11, 1, 512))


def _new_slots(big):
    slots = {n: jnp.zeros(w.shape, f32) for n, w in big.items() if n not in ("w_in", "w_gate")}
    n_layers, d_model = big["w_in"].shape[:2]
    slots["w_in"] = [{name: jnp.zeros((d_model, width), f32) for name, _, _, width in IN_GROUPS} for _ in range(n_layers)]
    slots["w_gate"] = [jnp.zeros((d_model, 4 * d_model), f32) for _ in range(n_layers)]
    return slots


def _slot_grads(g):
    out = dict(g)
    out["w_in"] = jnp.stack([
        jnp.concatenate([gl[name][:, :IN_OFFS[s0 + ns] - IN_OFFS[s0]] for name, s0, ns, _ in IN_GROUPS], axis=1)
        for gl in g["w_in"]])
    d_model = g["w_gate"][0].shape[0]
    out["w_gate"] = jnp.stack([jnp.transpose(gl.reshape(d_model, 4, d_model), (1, 0, 2)) for gl in g["w_gate"]])
    return out


def _forward(p, big, slots, x, ctx, c, target):
    n_layers = p["norm_w"].shape[0]
    d_model = x.shape[-1]
    xa = jnp.concatenate([ctx, x], axis=0)
    t = xa.shape[0]
    cc = jnp.concatenate([c, p["c_ctx"][None], jnp.zeros((_SUB - 2, d_model), f32)], axis=0)
    lb_all = jnp.cumsum(jax.nn.softmax(p["hg_lb_logits"], axis=0), axis=0)
    scc, = blocked_op("silu_c", _f_silu, [], [cc], [(d_model, bf16)], rb=_SUB)

    for l in range(n_layers):
        tag = "l%d_" % l
        mod = mm(tag + "mod", scc, big["w_mod"][l], slots["w_mod"][l])
        bm = p["b_mod"][l][None]
        h, = blocked_op(tag + "normmod", _f_normmod, [p["norm_w"][l][None], mod, bm], [xa], [(d_model, bf16)])
        gnames = [g[0] for g in IN_GROUPS]
        wvs = [_pad_cols(big["w_in"][l][:, IN_OFFS[s0]:IN_OFFS[s0 + ns]], width) for _, s0, ns, width in IN_GROUPS]
        u = dict(zip(gnames, multi_mm(tag + "in_", gnames, h, wvs, [slots["w_in"][l][g] for g in gnames])))

        o_dirs = blocked_op(tag + "hg", _both(_f_hg(False), _f_hg(True), 1, 2), [lb_all[l, 0][None], lb_all[l, 1][None]],
                            [u["hg_qi"], u["hg_ff"], u["hg_qi"], u["hg_fb"]], [(BR_W, f32)] * 2,
                            order=_BOTH_ORDERS, carry_sds=[(BR_W, HG_DK)] * 2)
        y_hg, = blocked_op(tag + "hg_fin", _f_hg_final, [p["hg_norm"][l][None]], list(o_dirs) + [u["hg_z"]], [(BR_W, bf16)])

        a_scan, bd, cds = _s5_params(l, p)
        n_state = S5_GROUPS * S5_STATE
        planes = [s.reshape(t, n_state) for s in s5_states(tag, u["s5_u"], bd, a_scan[0], a_scan[1])]
        ysum, = bd_mm(tag + "s5_c", planes, [[cds[d][part]] for d in range(2) for part in range(2)])
        g5, = blocked_op(tag + "s5_c1", _f_s5_c1, [p["s5_d"][l][None]], [ysum, u["s5_u"]], [(BR_W, f32)])
        gl = mm(tag + "s5_glu", g5, big["s5_w_glu"][l], slots["s5_w_glu"][l])
        y_s5, = blocked_op(tag + "s5_c2", _f_s5_c2, [p["s5_b_glu"][l][None]], [g5, gl, u["s5_z"]], [(BR_W, bf16)])

        wg, gb = _lru_gate(l, p)
        ab = blocked_op(tag + "lru_a", _f_lru_a,
                        [p["lru_conv_w"][l], p["lru_conv_b"][l][None], wg, gb, p["lru_lam"][l]],
                        [u["lru_x"]], [(BR_W, f32)] * 4)
        hs = []
        for d in range(2):
            a3 = ab[2 * d].reshape(t, 4, BR_W // 4)
            b3 = ab[2 * d + 1].reshape(t, 4, BR_W // 4)
            hs.append(rscan(tag + "lru_scan%d" % d, d, a3, b3).reshape(t, BR_W))
        y_lru, = blocked_op(tag + "lru_c", _f_lru_c, [], hs + [u["lru_z"]], [(BR_W, bf16)])

        dtb = _pad_cols(p["m2_dt_bias"][l].reshape(1, -1), _LANE)
        xbc, dtp = blocked_op(tag + "m2_a", _f_m2_a, [p["m2_conv_w"][l], p["m2_conv_b"][l][None], dtb],
                              [u["m2_xbc"], u["m2_dt"]], [(M2_XBC, f32), (_LANE, f32)])
        alog = _pad_cols(p["m2_a_log"][l].reshape(1, -1), _LANE)
        y_dirs = blocked_op(tag + "ssd", _both(_f_ssd(0), _f_ssd(1), 1, 2), [alog, alog], [xbc, dtp, xbc, dtp],
                            [(BR_W, f32)] * 2, order=_BOTH_ORDERS,
                            carry_sds=[(M2_GROUPS * M2_STATE, BR_W // M2_GROUPS)] * 2)
        dsk = jnp.repeat(p["m2_d"][l], M2_HEADDIM)[None]
        y_m2, = blocked_op(tag + "m2_c", _f_m2_c, [dsk, p["m2_norm"][l][None]], list(y_dirs) + [xbc, u["m2_z"]], [(BR_W, bf16)])

        wg_all = jnp.transpose(big["w_gate"][l], (1, 0, 2)).reshape(d_model, 4 * d_model)
        gp = mm(tag + "gate", h, wg_all, slots["w_gate"][l], out_dtype=bf16)
        bs = [mm(tag + "br%d" % k, yk, big["w_branch"][l, k], slots["w_branch"][l, k], out_dtype=bf16)
              for k, yk in enumerate((y_hg, y_s5, y_lru, y_m2))]
        mix, = blocked_op(tag + "mix", _f_mix, [p["b_gate"][l].reshape(1, -1)], [gp] + bs, [(d_model, bf16)])
        o = mm(tag + "out", mix, big["w_out"][l], slots["w_out"][l])
        xa, = blocked_op(tag + "resid", _f_resid, [mod, bm], [xa, o], [(d_model, f32)])

    rl, = blocked_op("loss", _f_loss, [p["final_norm"][None]], [xa[ctx.shape[0]:], target], [(1, f32)])
    return jnp.sum(rl)


_MESH = pl.DeviceIdType.MESH
_ANY = pl.BlockSpec(memory_space=pl.ANY)
W_PACK = 1024


def _place():
    x, y, c = lax.axis_index("x"), lax.axis_index("y"), lax.axis_index("c")
    chips = [(x, 1 - y), (1 - x, y), (1 - x, 1 - y)]
    return x, y, c, chips


def _rcopy(src, dst, ssem, rsem, k, to):
    return pltpu.make_async_remote_copy(src_ref=src, dst_ref=dst, send_sem=ssem.at[k], recv_sem=rsem.at[k],
                                        device_id=to, device_id_type=_MESH)


def gather_shards(xs):
    n = len(xs)

    def body(*refs):
        x_refs, o_refs = refs[:n], refs[n:2 * n]
        ssem, rsem, lsem = refs[2 * n:]
        x, y, c, chips = _place()
        j = 2 * x + y
        sib = (x, y, 1 - c)
        mine = [pltpu.make_async_copy(x_refs[a], o_refs[a].at[j], lsem.at[a]) for a in range(n)]
        for cp in mine:
            cp.start()
        first = [_rcopy(x_refs[a].at[c], o_refs[a].at[j, c], ssem, rsem, 6 * a + r, (*chips[r], c))
                 for r in range(3) for a in range(n)]
        for cp in first:
            cp.start()
        passed = []
        for r in range(3):
            jr = j ^ (r + 1)
            for a in range(n):
                _rcopy(x_refs[a].at[c], o_refs[a].at[jr, c], ssem, rsem, 6 * a + r, sib).wait_recv()
                cp = _rcopy(o_refs[a].at[jr, c], o_refs[a].at[jr, c], ssem, rsem, 6 * a + 3 + r, sib)
                cp.start()
                passed.append(cp)
        for r in range(3):
            jr = j ^ (r + 1)
            for a in range(n):
                _rcopy(x_refs[a].at[c], o_refs[a].at[jr, 1 - c], ssem, rsem, 6 * a + 3 + r, sib).wait_recv()
        for cp in first + passed:
            cp.wait_send()
        for cp in mine:
            cp.wait()

    return pl.pallas_call(
        body, name="gather_shards", out_shape=[jax.ShapeDtypeStruct((4,) + x.shape, x.dtype) for x in xs],
        in_specs=[_ANY] * n, out_specs=[_ANY] * n,
        scratch_shapes=[pltpu.SemaphoreType.DMA((6 * n,)), pltpu.SemaphoreType.DMA((6 * n,)), pltpu.SemaphoreType.DMA((n,))],
    )(*xs)


def sibling_halves(gs):
    n = len(gs)

    def body(*refs):
        g_refs, o_refs = refs[:n], refs[n:2 * n]
        ssem, rsem = refs[2 * n:]
        x, y, c, _ = _place()
        sib = (x, y, 1 - c)
        cps = [_rcopy(g_refs[a].at[k, 1 - c], o_refs[a].at[k], ssem, rsem, 4 * a + k, sib)
               for k in range(4) for a in range(n)]
        for cp in cps:
            cp.start()
        for cp in cps:
            cp.wait()

    return pl.pallas_call(
        body, name="sibling_halves", out_shape=[jax.ShapeDtypeStruct((4,) + g.shape[2:], g.dtype) for g in gs],
        in_specs=[_ANY] * n, out_specs=[_ANY] * n,
        scratch_shapes=[pltpu.SemaphoreType.DMA((4 * n,)), pltpu.SemaphoreType.DMA((4 * n,))],
    )(*gs)


def scatter_chips(ps):
    n = len(ps)

    def body(*refs):
        p_refs, o_refs = refs[:n], refs[n:2 * n]
        ssem, rsem = refs[2 * n:]
        x, y, c, chips = _place()
        j = 2 * x + y
        cps = [_rcopy(p_refs[a].at[j ^ (r + 1)], o_refs[a].at[r], ssem, rsem, 3 * a + r, (*chips[r], c))
               for r in range(3) for a in range(n)]
        for cp in cps:
            cp.start()
        for cp in cps:
            cp.wait()

    return pl.pallas_call(
        body, name="scatter_chips", out_shape=[jax.ShapeDtypeStruct((3,) + p.shape[1:], p.dtype) for p in ps],
        in_specs=[_ANY] * n, out_specs=[_ANY] * n,
        scratch_shapes=[pltpu.SemaphoreType.DMA((3 * n,)), pltpu.SemaphoreType.DMA((3 * n,))],
    )(*ps)


def join_halves(qs):
    n = len(qs)

    def body(*refs):
        o_refs = refs[n:2 * n]
        ssem, rsem = refs[2 * n:]
        x, y, c, _ = _place()
        sib = (x, y, 1 - c)
        cps = [_rcopy(o_refs[a].at[c], o_refs[a].at[c], ssem, rsem, a, sib) for a in range(n)]
        for cp in cps:
            cp.start()
        for a in range(n):
            _rcopy(o_refs[a].at[c], o_refs[a].at[1 - c], ssem, rsem, a, sib).wait_recv()
        for cp in cps:
            cp.wait_send()

    return pl.pallas_call(
        body, name="join_halves", out_shape=[jax.ShapeDtypeStruct(q.shape, q.dtype) for q in qs],
        in_specs=[_ANY] * n, out_specs=[_ANY] * n, input_output_aliases={a: a for a in range(n)},
        scratch_shapes=[pltpu.SemaphoreType.DMA((n,)), pltpu.SemaphoreType.DMA((n,))],
    )(*qs)


def _rows_block(r):
    return _pick(r, 256, _SUB)


def add_sibling(tag, g, r1, place, out_dtype):
    _, _, rows, w = g.shape
    rb = _rows_block(rows)

    def body(pl_ref, g_ref, r_ref, o_ref):
        o_ref[...] = (g_ref[0] + r_ref[...]).astype(out_dtype)

    return pl.pallas_call(
        body, name="add_sibling_" + tag, out_shape=jax.ShapeDtypeStruct((4, rows, w), out_dtype),
        grid_spec=pltpu.PrefetchScalarGridSpec(
            num_scalar_prefetch=1, grid=(4, rows // rb),
            in_specs=[pl.BlockSpec((1, 1, rb, w), lambda k, i, s: (k, s[1], i, 0)),
                      pl.BlockSpec((1, rb, w), lambda k, i, s: (k, i, 0))],
            out_specs=pl.BlockSpec((1, rb, w), lambda k, i, s: (k, i, 0))),
        compiler_params=_cparams(2),
    )(place, g, r1)


def add_chips(tag, p, r2, place):
    _, rows, w = p.shape
    rb = _rows_block(rows)

    def body(pl_ref, p_ref, r_ref, o_ref):
        j = pl_ref[0]
        own = p_ref[0].astype(f32)
        others = [r_ref[0].astype(f32), r_ref[1].astype(f32), r_ref[2].astype(f32)]
        acc = None
        for k in range(4):
            rel = k ^ j
            t = jnp.where(rel == 0, own, jnp.where(rel == 1, others[0], jnp.where(rel == 2, others[1], others[2])))
            acc = t if acc is None else acc + t
        o_ref[0] = acc

    return pl.pallas_call(
        body, name="add_chips_" + tag, out_shape=jax.ShapeDtypeStruct((2, rows, w), f32),
        grid_spec=pltpu.PrefetchScalarGridSpec(
            num_scalar_prefetch=1, grid=(rows // rb,),
            in_specs=[pl.BlockSpec((1, rb, w), lambda i, s: (s[0], i, 0)),
                      pl.BlockSpec((3, rb, w), lambda i, s: (0, i, 0))],
            out_specs=pl.BlockSpec((1, rb, w), lambda i, s: (s[1], i, 0))),
        compiler_params=_cparams(1),
    )(place, p, r2)


def adamw(tag, g, w, m, v):
    rows, wd = g.shape
    rb = _rows_block(rows)

    def body(g_ref, w_ref, m_ref, v_ref, d_ref, nm_ref, nv_ref):
        gv = g_ref[...]
        nm = ADAM_B1 * m_ref[...] + (1.0 - ADAM_B1) * gv
        nv = ADAM_B2 * v_ref[...] + (1.0 - ADAM_B2) * (gv * gv)
        m_hat = nm / (1.0 - ADAM_B1 ** ADAM_STEP)
        v_hat = nv / (1.0 - ADAM_B2 ** ADAM_STEP)
        d_ref[...] = -ADAM_LR * (m_hat / (jnp.sqrt(v_hat) + ADAM_EPS) + ADAM_WD * w_ref[...])
        nm_ref[...] = nm
        nv_ref[...] = nv

    spec = pl.BlockSpec((rb, wd), lambda i: (i, 0))
    return pl.pallas_call(
        body, name="adamw_" + tag, grid=(rows // rb,), in_specs=[spec] * 4, out_specs=[spec] * 3,
        out_shape=[jax.ShapeDtypeStruct(g.shape, f32)] * 3, compiler_params=_cparams(1),
    )(g, w, m, v)


WEIGHTS = ("c_ctx", "norm_w", "w_mod", "b_mod", "w_in", "hg_lb_logits", "hg_norm", "s5_a_re", "s5_a_im", "s5_log_step",
           "s5_b_re", "s5_b_im", "s5_c_re", "s5_c_im", "s5_d", "s5_w_glu", "s5_b_glu", "lru_conv_w", "lru_conv_b",
           "lru_gate_w", "lru_gate_b", "lru_lam", "m2_conv_w", "m2_conv_b", "m2_dt_bias", "m2_a_log", "m2_d", "m2_norm",
           "w_branch", "w_gate", "b_gate", "w_out", "final_norm")
SHARD_AXIS = {"w_mod": 2, "w_in": 2, "hg_lb_logits": 2, "s5_w_glu": 1, "lru_conv_w": 2, "lru_lam": 2, "m2_conv_w": 2,
              "w_branch": 3, "w_gate": 2, "b_gate": 2, "w_out": 1}
BIG = ("w_mod", "w_in", "s5_w_glu", "w_branch", "w_gate", "w_out")
N_CHIPS = 4


def _to_rows(flat, row_unit):
    n = flat.shape[-1]
    per = 2 * row_unit * W_PACK
    total = -(-n // per) * per
    flat = jnp.pad(flat, [(0, 0)] * (flat.ndim - 1) + [(0, total - n)])
    return flat.reshape(flat.shape[:-1] + (2, total // (2 * W_PACK), W_PACK))


SMALL_SHARDED = tuple(n for n in WEIGHTS if n in SHARD_AXIS and n not in BIG)
SMALL_REPLICATED = tuple(n for n in WEIGHTS if n not in SHARD_AXIS)


def _chip_slices(a, axis):
    width = a.shape[axis] // N_CHIPS
    return jnp.stack([lax.slice_in_dim(a, k * width, (k + 1) * width, axis=axis) for k in range(N_CHIPS)])


def _gather_weights(local):
    small = jnp.concatenate([lax.bitcast_convert_type(local[n], bf16).reshape(-1) for n in SMALL_SHARDED])
    got = gather_shards([local[n].astype(bf16) for n in BIG] + [_to_rows(small, 16)])
    full = {}
    for n, g in zip(BIG, got):
        full[n] = jnp.concatenate([g[j] for j in range(N_CHIPS)], axis=SHARD_AXIS[n])
    flat, off = got[-1].reshape(N_CHIPS, -1), 0
    for n in SMALL_SHARDED:
        shp = local[n].shape
        size = 2 * math.prod(shp)
        part = lax.bitcast_convert_type(flat[:, off:off + size].reshape((N_CHIPS,) + shp + (2,)), f32)
        off += size
        full[n] = jnp.concatenate([part[j] for j in range(N_CHIPS)], axis=SHARD_AXIS[n])
    return full


def _whole_rows(v):
    n = v.shape[-1]
    return jnp.pad(v, [(0, 0)] * (v.ndim - 1) + [(0, -n % W_PACK)])


def _pack_small(vals, extra):
    return jnp.concatenate([_whole_rows(vals[n].reshape(-1)) for n in SMALL_SHARDED + SMALL_REPLICATED]
                           + [_whole_rows(extra.reshape(1))])


def _pack_small_grads(grads, loss):
    rep = [grads[n].reshape(-1) for n in SMALL_REPLICATED] + [loss.reshape(1)]
    sh = [_chip_slices(grads[n], SHARD_AXIS[n]).reshape(N_CHIPS, -1) for n in SMALL_SHARDED]
    return jnp.concatenate([_whole_rows(a) for a in sh]
                           + [_whole_rows(jnp.broadcast_to(r, (N_CHIPS,) + r.shape)) for r in rep], axis=1)


def _unpack_small(rows, like):
    out, r0 = {}, 0
    for n in SMALL_SHARDED + SMALL_REPLICATED:
        size = math.prod(like[n].shape)
        nr = -(-size // W_PACK)
        piece = lax.optimization_barrier(rows[r0:r0 + nr])
        out[n] = piece.reshape(-1)[:size].reshape(like[n].shape)
        r0 += nr
    return out, lax.optimization_barrier(rows[r0:r0 + 1])[0, 0]


def _reduce_grads(tags, gs):
    place = jnp.stack([2 * lax.axis_index("x") + lax.axis_index("y"), lax.axis_index("c")]).astype(jnp.int32)
    pairs = [add_sibling(t, g, r, place, bf16 if t in BIG else f32) for t, g, r in zip(tags, gs, sibling_halves(gs))]
    quads = [add_chips(t, p, r, place) for t, p, r in zip(tags, pairs, scatter_chips(pairs))]
    return join_halves(quads)


def kernel(x, c, ctx, c_ctx, norm_w, w_mod, b_mod, w_in, hg_lb_logits, hg_norm, s5_a_re, s5_a_im, s5_log_step, s5_b_re, s5_b_im, s5_c_re, s5_c_im, s5_d, s5_w_glu, s5_b_glu, lru_conv_w, lru_conv_b, lru_gate_w, lru_gate_b, lru_lam, m2_conv_w, m2_conv_b, m2_dt_bias, m2_a_log, m2_d, m2_norm, w_branch, w_gate, b_gate, w_out, final_norm, loss_target, m_c_ctx, m_norm_w, m_w_mod, m_b_mod, m_w_in, m_hg_lb_logits, m_hg_norm, m_s5_a_re, m_s5_a_im, m_s5_log_step, m_s5_b_re, m_s5_b_im, m_s5_c_re, m_s5_c_im, m_s5_d, m_s5_w_glu, m_s5_b_glu, m_lru_conv_w, m_lru_conv_b, m_lru_gate_w, m_lru_gate_b, m_lru_lam, m_m2_conv_w, m_m2_conv_b, m_m2_dt_bias, m_m2_a_log, m_m2_d, m_m2_norm, m_w_branch, m_w_gate, m_b_gate, m_w_out, m_final_norm, v_c_ctx, v_norm_w, v_w_mod, v_b_mod, v_w_in, v_hg_lb_logits, v_hg_norm, v_s5_a_re, v_s5_a_im, v_s5_log_step, v_s5_b_re, v_s5_b_im, v_s5_c_re, v_s5_c_im, v_s5_d, v_s5_w_glu, v_s5_b_glu, v_lru_conv_w, v_lru_conv_b, v_lru_gate_w, v_lru_gate_b, v_lru_lam, v_m2_conv_w, v_m2_conv_b, v_m2_dt_bias, v_m2_a_log, v_m2_d, v_m2_norm, v_w_branch, v_w_gate, v_b_gate, v_w_out, v_final_norm):
    given = dict(locals())
    w_loc = {n: given[n] for n in WEIGHTS}
    m_loc = {n: given["m_" + n] for n in WEIGHTS}
    v_loc = {n: given["v_" + n] for n in WEIGHTS}

    full = _gather_weights(w_loc)
    params = {n: (full[n] if n in SHARD_AXIS else w_loc[n]) for n in WEIGHTS if n not in BIG}
    big = {n: full[n] for n in BIG}
    def loss_fn(p, s, xx):
        return _forward(p, big, s, xx, ctx[0], c, loss_target[0])

    loss, (g_p, g_s, g_x) = jax.value_and_grad(loss_fn, argnums=(0, 1, 2))(params, _new_slots(big), x[0])
    grads = {**g_p, **_slot_grads(g_s)}

    def rows4(a):
        return a.reshape(a.shape[:2] + (-1, a.shape[-1]))

    g_big = [rows4(_chip_slices(grads[n], SHARD_AXIS[n])) for n in BIG]
    g_small = _to_rows(_pack_small_grads(grads, loss), 64)
    summed = _reduce_grads(list(BIG) + ["small"], g_big + [g_small])

    g_out, d_out, m_out, v_out = {}, {}, {}, {}
    for n, g in zip(BIG, summed):
        shp = w_loc[n].shape
        flat2 = lambda a: a.reshape(-1, shp[-1])
        g_out[n] = g.reshape(shp)
        d, nm, nv = adamw(n, flat2(g), flat2(w_loc[n]), flat2(m_loc[n]), flat2(v_loc[n]))
        d_out[n], m_out[n], v_out[n] = d.reshape(shp), nm.reshape(shp), nv.reshape(shp)
    zero = jnp.zeros((), f32)
    flat = lambda vals: _to_rows(_pack_small(vals, zero), 64).reshape(-1, W_PACK)
    gs = summed[-1].reshape(-1, W_PACK)
    d, nm, nv = adamw("small", gs, flat(w_loc), flat(m_loc), flat(v_loc))
    gsm, loss_out = _unpack_small(gs, w_loc)
    g_out.update(gsm)
    d_out.update(_unpack_small(d, w_loc)[0])
    m_out.update(_unpack_small(nm, w_loc)[0])
    v_out.update(_unpack_small(nv, w_loc)[0])
    outs = [loss_out, g_x[None]]
    for group in (g_out, d_out, m_out, v_out):
        outs += [group[n] for n in WEIGHTS]
    return tuple(outs)
```

```python
import functools
import math

import jax
import jax.numpy as jnp
from jax import lax
from jax.experimental import pallas as pl
from jax.experimental.pallas import tpu as pltpu

f32 = jnp.float32
bf16 = jnp.bfloat16
_MM_DTYPE = bf16
_HI = lax.Precision.HIGHEST
_MAP_PREC = lax.Precision.HIGH
_VMEM_LIMIT = 56 * 1024 * 1024
_LANE = 128
_SUB = 8

EPS = 1e-6
CONV_W = 4
CHUNK = 64
RB = 256
BR_W = 512
HG_HEADS = 4
HG_DK = 128
S5_GROUPS = 32
S5_GROUP = 16
S5_STATE = 64
LRU_BLOCKS = 8
LRU_C = 8.0
M2_HEADS = 8
M2_HEADDIM = 64
M2_GROUPS = 2
M2_STATE = 64
M2_XBC = BR_W + 2 * M2_GROUPS * M2_STATE
ADAM_LR = 0.001
ADAM_B1 = 0.9
ADAM_B2 = 0.999
ADAM_EPS = 1e-08
ADAM_WD = 0.01
ADAM_STEP = 10

_NN = (((1,), (0,)), ((), ()))
_NT = (((1,), (1,)), ((), ()))
_TN = (((0,), (0,)), ((), ()))


def _silu(x):
    return x * jax.nn.sigmoid(x)


def _softplus(x):
    return jnp.maximum(x, 0.0) + jnp.log1p(jnp.exp(-jnp.abs(x)))


def _one_minus_exp(z):
    series = -z * (1.0 + z * 0.5 * (1.0 + z * (1.0 / 3.0) * (1.0 + z * 0.25 * (1.0 + z * 0.2))))
    return jnp.where(z > -0.05, series, 1.0 - jnp.exp(z))


def _rms(x, w):
    return x * lax.rsqrt(jnp.mean(x * x, axis=-1, keepdims=True) + EPS) * w


def _dot(a, b, dn=_NN, hi=False):
    prec = hi if isinstance(hi, lax.Precision) else (_HI if hi else None)
    return lax.dot_general(a, b, dn, precision=prec, preferred_element_type=f32)


def _cparams(n_grid):
    return pltpu.CompilerParams(dimension_semantics=("arbitrary",) * n_grid, vmem_limit_bytes=_VMEM_LIMIT)


_REV = {"asc": "desc", "d1": "d1r", "desc": "asc", "d1r": "d1"}


def _blk(order, i, n):
    if order == "asc":
        return i
    if order == "desc":
        return n - 1 - i
    if order == "d1":
        return jnp.where(i == 0, 0, n - i)
    return jnp.where(i == n - 1, 0, i + 1)


def _pick(n, cap, unit):
    if n <= cap:
        return n
    best = None
    d = unit
    while d <= cap:
        if n % d == 0:
            best = d
        d += unit
    return n if best is None else best


def _mm_call(name, a, b, mode, hi, out_dtype):
    if mode == "tn":
        k, m = a.shape
        n = b.shape[1]
        tm = _pick(m, 512, _LANE)
        tn = _pick(n, 512, _LANE)
        a_spec = pl.BlockSpec((k, tm), lambda i, j: (0, i))
        b_spec = pl.BlockSpec((k, tn), lambda i, j: (0, j))
    else:
        m, k = a.shape
        tm = _pick(m, max(256, min(1088, 4 * 1024 * 1024 // (k * a.dtype.itemsize))), _SUB)
        a_spec = pl.BlockSpec((tm, k), lambda i, j: (i, 0))
        if mode == "nn":
            n = b.shape[1]
            tn = _pick(n, max(_LANE, (4 * 1024 * 1024 // (k * 4)) // _LANE * _LANE), _LANE)
            b_spec = pl.BlockSpec((k, tn), lambda i, j: (0, j))
        else:
            n = b.shape[0]
            tn = _pick(n, max(_LANE, (4 * 1024 * 1024 // (k * 4)) // _LANE * _LANE), _LANE)
            b_spec = pl.BlockSpec((tn, k), lambda i, j: (j, 0))
    dn = {"nn": _NN, "nt": _NT, "tn": _TN}[mode]

    def body(a_ref, b_ref, o_ref):
        av = a_ref[...]
        bv = b_ref[...]
        if hi:
            av = av.astype(f32)
            bv = bv.astype(f32)
        else:
            av = av.astype(_MM_DTYPE)
            bv = bv.astype(_MM_DTYPE)
        o_ref[...] = _dot(av, bv, dn, hi).astype(o_ref.dtype)

    return pl.pallas_call(
        body, name=name, grid=(m // tm, n // tn), in_specs=[a_spec, b_spec],
        out_specs=pl.BlockSpec((tm, tn), lambda i, j: (i, j)),
        out_shape=jax.ShapeDtypeStruct((m, n), out_dtype), compiler_params=_cparams(2),
    )(a, b)


def mm(name, a, b, slot=None, hi=False, out_dtype=f32):
    @jax.custom_vjp
    def op(a, b, slot):
        return _mm_call(name, a, b, "nn", hi, out_dtype)

    def fwd(a, b, slot):
        return op(a, b, slot), (a, b)

    def bwd(res, g):
        a, b = res
        da = _mm_call(name + "_da", g, b, "nt", hi, a.dtype)
        db = _mm_call(name + "_db", a, g, "tn", hi, f32)
        if slot is None:
            return da, db.astype(b.dtype), None
        return da, jnp.zeros_like(b), db

    op.defvjp(fwd, bwd)
    return op(a, b, slot)


def _sum_nt_call(name, gs, ws, out_dtype):
    m = gs[0].shape[0]
    kdim = ws[0].shape[0]
    tm = _pick(m, 256, _SUB)
    n = len(gs)

    def body(*refs):
        acc = None
        for g_ref, w_ref in zip(refs[:n], refs[n:2 * n]):
            part = _dot(g_ref[...].astype(_MM_DTYPE), w_ref[...].astype(_MM_DTYPE), _NT)
            acc = part if acc is None else acc + part
        refs[2 * n][...] = acc.astype(out_dtype)

    in_specs = [pl.BlockSpec((tm, g.shape[1]), lambda i: (i, 0)) for g in gs]
    in_specs += [pl.BlockSpec(w.shape, lambda i: (0, 0)) for w in ws]
    return pl.pallas_call(
        body, name=name, grid=(m // tm,), in_specs=in_specs, out_specs=pl.BlockSpec((tm, kdim), lambda i: (i, 0)),
        out_shape=jax.ShapeDtypeStruct((m, kdim), out_dtype), compiler_params=_cparams(1),
    )(*gs, *ws)


def multi_mm(tag, names, a, ws, slots):
    @jax.custom_vjp
    def op(a, ws, slots):
        return tuple(_mm_call(tag + n, a, w, "nn", False, f32) for n, w in zip(names, ws))

    def fwd(a, ws, slots):
        return op(a, ws, slots), (a, ws)

    def bwd(res, gs):
        a, ws = res
        dws = [_mm_call(tag + n + "_db", a, g, "tn", False, f32) for n, g in zip(names, gs)]
        da = _sum_nt_call(tag + "da", list(gs), ws, a.dtype)
        return da, [jnp.zeros_like(w) for w in ws], dws

    op.defvjp(fwd, bwd)
    return op(a, list(ws), list(slots))


def _orders(order, n_x, n_o):
    if isinstance(order, str):
        return [order] * n_x, [order] * n_o
    return list(order[0]), list(order[1])


def _row(o, n):
    return lambda i: (_blk(o, i, n), 0)


def _blocked_fwd(name, f, order, rb, params, xs, out_sds, carry_sds):
    t = xs[0].shape[0]
    n = t // rb
    n_p, n_x, n_o, n_c = len(params), len(xs), len(out_sds), len(carry_sds)
    xo, oo = _orders(order, n_x, n_o)

    def body(*refs):
        p_refs = refs[:n_p]
        x_refs = refs[n_p:n_p + n_x]
        o_refs = refs[n_p + n_x:n_p + n_x + n_o]
        st_refs = refs[n_p + n_x + n_o:n_p + n_x + n_o + n_c]
        c_refs = refs[n_p + n_x + n_o + n_c:]
        i = pl.program_id(0)
        blk = _blk(xo[0], i, n)
        p = [r[...] for r in p_refs]
        x = [r[...] for r in x_refs]
        if n_c:
            @pl.when(i == 0)
            def _():
                for c in c_refs:
                    c[...] = jnp.zeros_like(c)
            c_in = [c[...] for c in c_refs]
            for sr, c in zip(st_refs, c_in):
                sr[0] = c
            c_out, ys = f(blk, p, c_in, x)
            for c, v in zip(c_refs, c_out):
                c[...] = v
        else:
            ys = f(blk, p, x)
        for o, y in zip(o_refs, ys):
            o[...] = y.astype(o.dtype)

    in_specs = [pl.BlockSpec(p.shape, lambda i, nd=p.ndim: (0,) * nd) for p in params]
    in_specs += [pl.BlockSpec((rb, x.shape[1]), _row(o, n)) for x, o in zip(xs, xo)]
    out_specs = [pl.BlockSpec((rb, c), _row(o, n)) for (c, _), o in zip(out_sds, oo)]
    out_specs += [pl.BlockSpec((1,) + s, lambda i: (i, 0, 0)) for s in carry_sds]
    out_shape = [jax.ShapeDtypeStruct((t, c), d) for c, d in out_sds]
    out_shape += [jax.ShapeDtypeStruct((n,) + s, f32) for s in carry_sds]
    res = pl.pallas_call(
        body, name=name, grid=(n,), in_specs=in_specs, out_specs=out_specs, out_shape=out_shape,
        scratch_shapes=[pltpu.VMEM(s, f32) for s in carry_sds], compiler_params=_cparams(1),
    )(*params, *xs)
    return list(res[:n_o]), list(res[n_o:])


def _blocked_bwd(name, f, order, rb, params, xs, states, dys, carry_sds):
    t = xs[0].shape[0]
    n = t // rb
    n_p, n_x, n_o, n_c = len(params), len(xs), len(dys), len(carry_sds)
    xo, oo = _orders(order, n_x, n_o)
    xo, oo = [_REV[o] for o in xo], [_REV[o] for o in oo]

    def body(*refs):
        k = 0
        p_refs = refs[k:k + n_p]; k += n_p
        x_refs = refs[k:k + n_x]; k += n_x
        st_refs = refs[k:k + n_c]; k += n_c
        dy_refs = refs[k:k + n_o]; k += n_o
        dp_refs = refs[k:k + n_p]; k += n_p
        dx_refs = refs[k:k + n_x]; k += n_x
        dc_refs = refs[k:]
        i = pl.program_id(0)
        blk = _blk(xo[0], i, n)
        p = [r[...] for r in p_refs]
        x = [r[...] for r in x_refs]
        dy = [r[...] for r in dy_refs]
        if n_c:
            @pl.when(i == 0)
            def _():
                for c in dc_refs:
                    c[...] = jnp.zeros_like(c)
            c_in = [r[0] for r in st_refs]
            dc = [c[...] for c in dc_refs]
            _, vjp = jax.vjp(lambda p_, c_, x_: f(blk, p_, c_, x_), p, c_in, x)
            dp, dcin, dx = vjp((dc, dy))
            for c, v in zip(dc_refs, dcin):
                c[...] = v
        else:
            _, vjp = jax.vjp(lambda p_, x_: f(blk, p_, x_), p, x)
            dp, dx = vjp(dy)

        @pl.when(i == 0)
        def _():
            for r, v in zip(dp_refs, dp):
                r[...] = v

        @pl.when(i > 0)
        def _():
            for r, v in zip(dp_refs, dp):
                r[...] += v
        for r, v in zip(dx_refs, dx):
            r[...] = v.astype(r.dtype)

    in_specs = [pl.BlockSpec(p.shape, lambda i, nd=p.ndim: (0,) * nd) for p in params]
    in_specs += [pl.BlockSpec((rb, x.shape[1]), _row(o, n)) for x, o in zip(xs, xo)]
    in_specs += [pl.BlockSpec((1,) + s, lambda i: (n - 1 - i, 0, 0)) for s in carry_sds]
    in_specs += [pl.BlockSpec((rb, d.shape[1]), _row(o, n)) for d, o in zip(dys, oo)]
    out_specs = [pl.BlockSpec(p.shape, lambda i, nd=p.ndim: (0,) * nd) for p in params]
    out_specs += [pl.BlockSpec((rb, x.shape[1]), _row(o, n)) for x, o in zip(xs, xo)]
    out_shape = [jax.ShapeDtypeStruct(p.shape, f32) for p in params]
    out_shape += [jax.ShapeDtypeStruct(x.shape, x.dtype) for x in xs]
    res = pl.pallas_call(
        body, name=name + "_bwd", grid=(n,), in_specs=in_specs, out_specs=out_specs, out_shape=out_shape,
        scratch_shapes=[pltpu.VMEM(s, f32) for s in carry_sds], compiler_params=_cparams(1),
    )(*params, *xs, *states, *dys)
    return list(res[:n_p]), list(res[n_p:])


def blocked_op(name, f, params, xs, out_sds, order="asc", carry_sds=(), rb=RB):
    carry_sds = tuple(carry_sds)

    @jax.custom_vjp
    def op(params, xs):
        return tuple(_blocked_fwd(name, f, order, rb, params, xs, out_sds, carry_sds)[0])

    def fwd(params, xs):
        ys, states = _blocked_fwd(name, f, order, rb, params, xs, out_sds, carry_sds)
        return tuple(ys), (params, xs, states)

    def bwd(res, dys):
        params, xs, states = res
        dp, dx = _blocked_bwd(name, f, order, rb, params, xs, states, list(dys), carry_sds)
        return list(dp), list(dx)

    op.defvjp(fwd, bwd)
    return op(list(params), list(xs))


def _cscan_call(name, order, asc, a, xr, xi, sr=None, si=None):
    t = xr.shape[0]
    n = t // RB
    tile = xr.shape[1:]
    xspec = pl.BlockSpec((RB,) + tile, lambda i: (_blk(order, i, n), 0, 0))
    aspec = pl.BlockSpec(a.shape, lambda i: (0, 0, 0))
    plane = jax.ShapeDtypeStruct(xr.shape, f32)

    def rowidx(tt):
        return tt if asc else RB - 1 - tt

    if sr is None:
        def body(a_ref, xr_ref, xi_ref, sr_ref, si_ref, c_ref):
            i = pl.program_id(0)

            @pl.when(i == 0)
            def _():
                c_ref[...] = jnp.zeros_like(c_ref)
            ar = a_ref[0]
            ai = a_ref[1]
            a2r = ar * ar - ai * ai
            a2i = 2.0 * ar * ai

            def step(tt, carry):
                cr, ci = carry
                r1 = rowidx(2 * tt)
                r2 = rowidx(2 * tt + 1)
                x1r, x1i, x2r, x2i = xr_ref[r1], xi_ref[r1], xr_ref[r2], xi_ref[r2]
                s1r = ar * cr - ai * ci + x1r
                s1i = ar * ci + ai * cr + x1i
                kr = ar * x1r - ai * x1i + x2r
                ki = ar * x1i + ai * x1r + x2i
                s2r = a2r * cr - a2i * ci + kr
                s2i = a2r * ci + a2i * cr + ki
                sr_ref[r1] = s1r
                si_ref[r1] = s1i
                sr_ref[r2] = s2r
                si_ref[r2] = s2i
                return s2r, s2i
            cr, ci = lax.fori_loop(0, RB // 2, step, (c_ref[0], c_ref[1]), unroll=4)
            c_ref[0] = cr
            c_ref[1] = ci

        return pl.pallas_call(
            body, name=name, grid=(n,), in_specs=[aspec, xspec, xspec], out_specs=[xspec, xspec],
            out_shape=[plane, plane], scratch_shapes=[pltpu.VMEM(a.shape, f32)], compiler_params=_cparams(1),
        )(a, xr, xi)

    def body(a_ref, xr_ref, xi_ref, sr_ref, si_ref, gr_ref, gi_ref, da_ref, c_ref):
        i = pl.program_id(0)

        @pl.when(i == 0)
        def _():
            c_ref[...] = jnp.zeros_like(c_ref)
            da_ref[...] = jnp.zeros_like(da_ref)
        ar = a_ref[0]
        ai = a_ref[1]
        a2r = ar * ar - ai * ai
        a2i = 2.0 * ar * ai

        def step(tt, carry):
            gr, gi, dar, dai, dbr, dbi = carry
            r1 = rowidx(2 * tt)
            r2 = rowidx(2 * tt + 1)
            x1r, x1i, x2r, x2i = xr_ref[r1], xi_ref[r1], xr_ref[r2], xi_ref[r2]
            v1r, v1i, v2r, v2i = sr_ref[r1], si_ref[r1], sr_ref[r2], si_ref[r2]
            g1r = x1r + ar * gr + ai * gi
            g1i = x1i + ar * gi - ai * gr
            kr = x2r + ar * x1r + ai * x1i
            ki = x2i + ar * x1i - ai * x1r
            g2r = kr + a2r * gr + a2i * gi
            g2i = ki + a2r * gi - a2i * gr
            dar = dar + gr * v1r + gi * v1i
            dai = dai + gi * v1r - gr * v1i
            dbr = dbr + g1r * v2r + g1i * v2i
            dbi = dbi + g1i * v2r - g1r * v2i
            gr_ref[r1] = g1r
            gi_ref[r1] = g1i
            gr_ref[r2] = g2r
            gi_ref[r2] = g2i
            return g2r, g2i, dar, dai, dbr, dbi
        z = jnp.zeros(tile, f32)
        gr, gi, dar, dai, dbr, dbi = lax.fori_loop(0, RB // 2, step, (c_ref[0], c_ref[1], z, z, z, z), unroll=4)
        c_ref[0] = gr
        c_ref[1] = gi
        da_ref[0] += dar + dbr
        da_ref[1] += dai + dbi

    return pl.pallas_call(
        body, name=name, grid=(n,), in_specs=[aspec] + [xspec] * 4, out_specs=[xspec, xspec, aspec],
        out_shape=[plane, plane, jax.ShapeDtypeStruct(a.shape, f32)],
        scratch_shapes=[pltpu.VMEM(a.shape, f32)], compiler_params=_cparams(1),
    )(a, xr, xi, sr, si)


def s5_states(tag, u, bd, a0, a1):
    t = u.shape[0]
    tile = a0.shape[1:]

    def run(u, bd, a0, a1):
        bu = [b.reshape((t,) + tile) for b in _bd_call(tag + "s5_bu", [u], [bd], "nn")]
        s0 = _cscan_call(tag + "s5_scan0", "asc", True, a0, bu[0], bu[1])
        s1 = _cscan_call(tag + "s5_scan1", "d1", False, a1, bu[0], bu[1])
        return (s0[0], s0[1], s1[0], s1[1])

    @jax.custom_vjp
    def op(u, bd, a0, a1):
        return run(u, bd, a0, a1)

    def fwd(u, bd, a0, a1):
        s = run(u, bd, a0, a1)
        return s, (u, bd, a0, a1, s)

    def bwd(res, ds):
        u, bd, a0, a1, s = res
        g0r, g0i, da0 = _cscan_call(tag + "s5_scan0_bwd", "desc", False, a0, ds[0], ds[1], s[0], s[1])
        g1r, g1i, da1 = _cscan_call(tag + "s5_scan1_bwd", "d1r", True, a1, ds[2], ds[3], s[2], s[3])
        gs = [g.reshape(t, -1) for g in (g0r, g0i, g1r, g1i)]
        du, = _bd_call(tag + "s5_bu_da", gs, [[bd[0], bd[1], bd[0], bd[1]]], "nt")
        k = bd[0].shape[0]
        dbd = [_bd_call(tag + "s5_bu_db%d" % j, u, gs[j], "tn", k) + _bd_call(tag + "s5_bu_db%d" % (j + 2), u, gs[j + 2], "tn", k)
               for j in range(2)]
        return du, dbd, da0, da1

    op.defvjp(fwd, bwd)
    return op(u, list(bd), a0, a1)


def _bd_call(name, a, b, mode, k=None):
    if mode == "tn":
        t = a.shape[0]
        ck, cn = a.shape[1] // k, b.shape[1] // k

        def body(a_ref, b_ref, o_ref):
            o_ref[0] = _dot(a_ref[...], b_ref[...], _TN, _MAP_PREC)

        return pl.pallas_call(
            body, name=name, grid=(k,),
            in_specs=[pl.BlockSpec((t, ck), lambda j: (0, j)), pl.BlockSpec((t, cn), lambda j: (0, j))],
            out_specs=pl.BlockSpec((1, ck, cn), lambda j: (j, 0, 0)),
            out_shape=jax.ShapeDtypeStruct((k, ck, cn), f32), compiler_params=_cparams(1),
        )(a, b)
    k, ck, cn = b[0][0].shape
    n_i, n_o = len(b), len(b[0])
    n_x = len(a)
    t = a[0].shape[0]
    tm = _pick(t, 1088, _SUB)
    flat = [w for row in b for w in row]
    win, wout, n_out, dn = (ck, cn, n_o, _NN) if mode == "nn" else (cn, ck, n_i, _NT)

    def body(*refs):
        xv = [r[...] for r in refs[:n_x]]
        w_refs = refs[n_x:n_x + len(flat)]
        o_refs = refs[n_x + len(flat):]
        for q in range(n_out):
            acc = None
            for s in range(n_x):
                w = w_refs[s * n_o + q] if mode == "nn" else w_refs[q * n_o + s]
                part = _dot(xv[s], w[0], dn, _MAP_PREC)
                acc = part if acc is None else acc + part
            o_refs[q][...] = acc

    return pl.pallas_call(
        body, name=name, grid=(t // tm, k),
        in_specs=[pl.BlockSpec((tm, win), lambda i, j: (i, j))] * n_x
        + [pl.BlockSpec((1, ck, cn), lambda i, j: (j, 0, 0))] * len(flat),
        out_specs=[pl.BlockSpec((tm, wout), lambda i, j: (i, j))] * n_out,
        out_shape=[jax.ShapeDtypeStruct((t, k * wout), f32)] * n_out, compiler_params=_cparams(2),
    )(*a, *flat)


def bd_mm(name, xs, ws):
    k = ws[0][0].shape[0]

    @jax.custom_vjp
    def op(xs, ws):
        return tuple(_bd_call(name, xs, ws, "nn"))

    def fwd(xs, ws):
        return op(xs, ws), (xs, ws)

    def bwd(res, gs):
        xs, ws = res
        dws = [[_bd_call(name + "_db%d%d" % (i, o), x, g, "tn", k) for o, g in enumerate(gs)] for i, x in enumerate(xs)]
        return list(_bd_call(name + "_da", list(gs), ws, "nt")), dws

    op.defvjp(fwd, bwd)
    return op(list(xs), [list(row) for row in ws])


def _rscan_call(name, order, asc, a, x, hp=None):
    t = x.shape[0]
    n = t // RB
    cshape = x.shape[1:]
    xspec = pl.BlockSpec((RB,) + cshape, lambda i: (_blk(order, i, n), 0, 0))

    def rowidx(tt):
        return tt if asc else RB - 1 - tt

    if hp is None:
        def body(a_ref, x_ref, h_ref, hp_ref, c_ref):
            i = pl.program_id(0)

            @pl.when(i == 0)
            def _():
                c_ref[...] = jnp.zeros_like(c_ref)

            def step(tt, h):
                r1 = rowidx(2 * tt)
                r2 = rowidx(2 * tt + 1)
                a1, a2, x1, x2 = a_ref[r1], a_ref[r2], x_ref[r1], x_ref[r2]
                h1 = a1 * h + x1
                h2 = (a2 * a1) * h + (a2 * x1 + x2)
                hp_ref[r1] = h
                h_ref[r1] = h1
                hp_ref[r2] = h1
                h_ref[r2] = h2
                return h2
            c_ref[...] = lax.fori_loop(0, RB // 2, step, c_ref[...], unroll=4)

        return pl.pallas_call(
            body, name=name, grid=(n,), in_specs=[xspec, xspec], out_specs=[xspec, xspec],
            out_shape=[jax.ShapeDtypeStruct(x.shape, f32)] * 2, scratch_shapes=[pltpu.VMEM(cshape, f32)],
            compiler_params=_cparams(1),
        )(a, x)

    def body(a_ref, x_ref, hp_ref, da_ref, db_ref, c_ref):
        i = pl.program_id(0)

        @pl.when(i == 0)
        def _():
            c_ref[...] = jnp.zeros_like(c_ref)

        def step(tt, c):
            r1 = rowidx(2 * tt)
            r2 = rowidx(2 * tt + 1)
            a1, a2, x1, x2 = a_ref[r1], a_ref[r2], x_ref[r1], x_ref[r2]
            g1 = x1 + c
            k = x2 + a1 * x1
            g2 = k + a1 * c
            db_ref[r1] = g1
            da_ref[r1] = g1 * hp_ref[r1]
            db_ref[r2] = g2
            da_ref[r2] = g2 * hp_ref[r2]
            return a2 * k + (a2 * a1) * c
        c_ref[...] = lax.fori_loop(0, RB // 2, step, c_ref[...], unroll=4)

    return pl.pallas_call(
        body, name=name, grid=(n,), in_specs=[xspec, xspec, xspec], out_specs=[xspec, xspec],
        out_shape=[jax.ShapeDtypeStruct(x.shape, f32)] * 2, scratch_shapes=[pltpu.VMEM(cshape, f32)],
        compiler_params=_cparams(1),
    )(a, x, hp)


def rscan(name, d, a, x):
    order = "d1" if d else "asc"

    @jax.custom_vjp
    def op(a, x):
        return _rscan_call(name, order, d == 0, a, x)[0]

    def fwd(a, x):
        h, hp = _rscan_call(name, order, d == 0, a, x)
        return h, (a, hp)

    def bwd(res, dh):
        a, hp = res
        da, db = _rscan_call(name + "_bwd", _REV[order], d != 0, a, dh, hp)
        return da, db

    op.defvjp(fwd, bwd)
    return op(a, x)


def _wide_rows(t):
    return _pick(t, 544, 16)


def _mod_part(blk, rows, mod, bm, lo, hi):
    is_ctx = blk * rows + lax.broadcasted_iota(jnp.int32, (rows, 1), 0) < RB
    r = mod[:, lo:hi] + bm[:, lo:hi]
    return jnp.where(is_ctx, r[1:2], r[0:1])


def _f_silu(blk, p, x):
    return [_silu(x[0]).astype(bf16)]


def _f_normmod(blk, p, x):
    nw, mod, bm = p
    rows, d = x[0].shape
    shift = _mod_part(blk, rows, mod, bm, 0, d)
    scale = _mod_part(blk, rows, mod, bm, d, 2 * d)
    return [(_rms(x[0], nw) * (1.0 + scale) + shift).astype(bf16)]


def _f_resid(blk, p, x):
    mod, bm = p
    rows, d = x[0].shape
    return [x[0] + _mod_part(blk, rows, mod, bm, 2 * d, 3 * d) * x[1]]


def _f_mix(blk, p, x):
    bg, = p
    gp = x[0]
    d = x[1].shape[1]
    acc = None
    for k in range(4):
        t = jax.nn.sigmoid(gp[:, k * d:(k + 1) * d] + bg[:, k * d:(k + 1) * d]) * x[1 + k]
        acc = t if acc is None else acc + t
    return [acc.astype(bf16)]


def _tri(rev):
    row = lax.broadcasted_iota(jnp.int32, (CHUNK, CHUNK), 0)
    col = lax.broadcasted_iota(jnp.int32, (CHUNK, CHUNK), 1)
    return (col >= row) if rev else (col <= row)


def _chunk_ids(rev):
    ids = list(range(RB // CHUNK))
    return ids[::-1] if rev else ids


def _f_hg(rev):
    def f(blk, p, c, x):
        lb, = p
        st, = c
        qi, fr = x
        q = _silu(qi[:, :BR_W])
        v = qi[:, BR_W:]
        fg = lb + (1.0 - lb) * jax.nn.sigmoid(fr)
        logf = jnp.log(fg)
        k = 1.0 - fg
        m = _tri(rev)
        mf = m.astype(f32)
        outs = [None] * (RB // CHUNK)
        for ci in _chunk_ids(rev):
            sl = slice(CHUNK * ci, CHUNK * ci + CHUNK)
            lf = logf[sl]
            b = _dot(mf, lf, hi=True)
            bend = jnp.sum(lf, axis=0, keepdims=True)
            mid = 0.5 * bend
            qe = q[sl] * jnp.exp(b - mid)
            ke = k[sl] * jnp.exp(mid - b)
            kd = k[sl] * jnp.exp(bend - b)
            qb = q[sl] * jnp.exp(b)
            dec = jnp.exp(bend)
            vc = v[sl]
            oh, ns = [], []
            for hh in range(HG_HEADS):
                cs = slice(HG_DK * hh, HG_DK * hh + HG_DK)
                sth = st[cs]
                att = jnp.where(m, _dot(qe[:, cs], ke[:, cs], _NT), 0.0)
                oh.append(_dot(att, vc[:, cs]) + _dot(qb[:, cs], sth, _NT))
                ns.append(sth * dec[:, cs] + _dot(vc[:, cs], kd[:, cs], _TN))
            st = jnp.concatenate(ns, axis=0)
            outs[ci] = jnp.concatenate(oh, axis=1)
        return [st], [jnp.concatenate(outs, axis=0)]
    return f


def _both(f0, f1, n_p, n_x):
    def f(blk, p, c, x):
        c0, y0 = f0(blk, p[:n_p], c[:1], x[:n_x])
        c1, y1 = f1(blk, p[n_p:], c[1:], x[n_x:])
        return c0 + c1, y0 + y1
    return f


_BOTH_ORDERS = (["asc", "asc", "d1", "d1"], ["asc", "d1"])


def _f_hg_final(blk, p, x):
    nw, = p
    o = x[0] + x[1]
    parts = []
    for hh in range(HG_HEADS):
        cs = slice(HG_DK * hh, HG_DK * hh + HG_DK)
        parts.append(_rms(o[:, cs], nw[:, cs]))
    return [(jnp.concatenate(parts, axis=1) * _silu(x[2])).astype(bf16)]


def _conv(x, cw, cb, blk):
    rows = x.shape[0]
    r = lax.broadcasted_iota(jnp.int32, (rows, 1), 0)
    rm = jnp.where(blk == 0, r, r % CHUNK)
    seg = jnp.where(blk == 0, rows, CHUNK)

    def vmask(o):
        return ((rm + o >= 0) & (rm + o < seg)).astype(f32)

    def shifted(o):
        @jax.custom_vjp
        def sh(x, mo, mn):
            return pltpu.roll(x, (-o) % rows, 0) * mo

        def fwd(x, mo, mn):
            return sh(x, mo, mn), (mo, mn)

        def bwd(res, g):
            mo, mn = res
            return pltpu.roll(g, o % rows, 0) * mn, jnp.zeros_like(mo), jnp.zeros_like(mn)
        sh.defvjp(fwd, bwd)
        return sh(x, vmask(o), vmask(-o))

    lo = (CONV_W - 1) // 2
    out = cb
    for k in range(CONV_W):
        o = k - lo
        out = out + cw[k:k + 1] * (x if o == 0 else shifted(o))
    return out


def _f_lru_a(blk, p, x):
    cw, cb, wg, gb, lam = p
    xc = _conv(x[0], cw, cb, blk)
    n_chunks = BR_W // _LANE
    xk = [xc[:, _LANE * k:_LANE * (k + 1)] for k in range(n_chunks)]

    def gate(j):
        pre = jnp.concatenate([_dot(xk[k], wg[j * n_chunks + k], hi=_MAP_PREC) for k in range(n_chunks)], axis=1)
        return jax.nn.sigmoid(pre + gb[:, BR_W * j:BR_W * (j + 1)])

    outs = []
    for d in range(2):
        r = gate(2 * d)
        ig = gate(2 * d + 1)
        log_a = -LRU_C * r * _softplus(-lam[d:d + 1])
        outs.append(jnp.exp(log_a))
        outs.append(jnp.sqrt(_one_minus_exp(2.0 * log_a)) * (ig * xc))
    return outs


def _f_lru_c(blk, p, x):
    return [((x[0] + x[1]) * _silu(x[2])).astype(bf16)]


def _f_s5_c1(blk, p, x):
    dsk, = p
    return [jax.nn.gelu(x[0] + dsk * x[1])]


def _f_s5_c2(blk, p, x):
    bglu, = p
    return [(x[0] * jax.nn.sigmoid(x[1] + bglu) * _silu(x[2])).astype(bf16)]


def _f_m2_a(blk, p, x):
    cw, cb, dtb = p
    return [_silu(_conv(x[0], cw, cb, blk)), _softplus(x[1] + dtb)]


def _f_ssd(d):
    rev = d == 1
    hpg = M2_HEADS // M2_GROUPS

    def f(blk, p, c, x):
        alog, = p
        st, = c
        xbc, dtp = x
        a = -jnp.exp(alog[:, M2_HEADS * d:M2_HEADS * (d + 1)])
        dt = dtp[:, M2_HEADS * d:M2_HEADS * (d + 1)]
        xs = xbc[:, :BR_W]
        bm = xbc[:, BR_W:BR_W + M2_GROUPS * M2_STATE]
        cm = xbc[:, BR_W + M2_GROUPS * M2_STATE:]
        gw = hpg * M2_HEADDIM
        mf = _tri(rev).astype(f32)
        row = lax.broadcasted_iota(jnp.int32, (CHUNK, gw), 0)
        col = lax.broadcasted_iota(jnp.int32, (CHUNK, gw), 1)
        m4 = (col % CHUNK >= row) if rev else (col % CHUNK <= row)
        spread = (lax.broadcasted_iota(jnp.int32, (hpg, gw), 0)
                  == lax.div(lax.broadcasted_iota(jnp.int32, (hpg, gw), 1), M2_HEADDIM)).astype(f32)
        own = [lax.div(lax.broadcasted_iota(jnp.int32, (1, gw), 1), M2_HEADDIM) == r for r in range(hpg)]
        outs = [None] * (RB // CHUNK)
        for ci in _chunk_ids(rev):
            sl = slice(CHUNK * ci, CHUNK * ci + CHUNK)
            dtc = dt[sl]
            dta = dtc * a
            cum = _dot(mf, dta, hi=True)
            cum_t = cum.T
            dt_t = dtc.T
            ys, ns = [], []
            for g in range(M2_GROUPS):
                hs = slice(hpg * g, hpg * (g + 1))
                bmg = bm[sl, M2_STATE * g:M2_STATE * (g + 1)]
                cmg = cm[sl, M2_STATE * g:M2_STATE * (g + 1)]
                xg = xs[sl, gw * g:gw * (g + 1)]
                stg = st[M2_STATE * g:M2_STATE * (g + 1)]
                cum_i = _dot(cum[:, hs], spread, hi=True)
                cum_j = jnp.concatenate([cum_t[hpg * g + r:hpg * g + r + 1] for r in range(hpg)], axis=1)
                dt_j = jnp.concatenate([dt_t[hpg * g + r:hpg * g + r + 1] for r in range(hpg)], axis=1)
                dt_i = _dot(dtc[:, hs], spread, hi=True)
                cend_g = jnp.sum(_dot(dta[:, hs], spread, hi=True), axis=0, keepdims=True)
                decay = jnp.exp(jnp.where(m4, cum_i - cum_j, -1e30))
                scores = _dot(cmg, jnp.concatenate([bmg] * hpg, axis=0), _NT)
                w = scores * decay * dt_j
                xdiag = jnp.concatenate([jnp.where(own[r], xg, 0.0) for r in range(hpg)], axis=0)
                ys.append(_dot(w, xdiag) + _dot(cmg, stg) * jnp.exp(cum_i))
                wx = jnp.exp(cend_g - cum_i) * dt_i * xg
                ns.append(jnp.exp(cend_g) * stg + _dot(bmg, wx, _TN))
            st = jnp.concatenate(ns, axis=0)
            outs[ci] = jnp.concatenate(ys, axis=1)
        return [st], [jnp.concatenate(outs, axis=0)]
    return f


def _f_m2_c(blk, p, x):
    dsk, nw = p
    y = x[0] + x[1] + dsk * x[2][:, :BR_W]
    return [_rms(y * _silu(x[3]), nw).astype(bf16)]


def _f_loss(blk, p, x):
    fnw, = p
    err = _rms(x[0], fnw) - x[1]
    return [0.5 * jnp.mean(err * err, axis=-1, keepdims=True)]


def _blockdiag(w):
    g, a, b = w.shape
    return jnp.einsum("gab,gh->gahb", w, jnp.eye(g, dtype=w.dtype)).reshape(g * a, g * b)


def _s5_params(l, w):
    a_scan, cds = [], []
    per = _LANE // S5_GROUP

    def chunks(m):
        return jnp.stack([_blockdiag(m[k * per:(k + 1) * per]) for k in range(S5_GROUPS // per)])

    b_re = jnp.transpose(w["s5_b_re"][l], (0, 2, 1))
    b_im = jnp.transpose(w["s5_b_im"][l], (0, 2, 1))
    bd = [chunks(b_re), chunks(b_im)]
    c_re = jnp.transpose(w["s5_c_re"][l], (0, 2, 1))
    c_im = jnp.transpose(w["s5_c_im"][l], (0, 2, 1))
    for d in range(2):
        lam_re = w["s5_a_re"][l, d]
        lam_im = w["s5_a_im"][l, d]
        step = jnp.exp(w["s5_log_step"][l, d])[:, None]
        mag = jnp.exp(lam_re * step)
        ab_re = mag * jnp.cos(lam_im * step)
        ab_im = mag * jnp.sin(lam_im * step)
        den = lam_re * lam_re + lam_im * lam_im
        nr = ab_re - 1.0
        co_re = (nr * lam_re + ab_im * lam_im) / den
        co_im = (ab_im * lam_re - nr * lam_im) / den
        n_state = S5_GROUPS * S5_STATE
        a_scan.append(jnp.stack([ab_re.reshape(_SUB, n_state // _SUB), ab_im.reshape(_SUB, n_state // _SUB)]))
        cp_re = c_re * co_re[:, :, None] - c_im * co_im[:, :, None]
        cp_im = c_re * co_im[:, :, None] + c_im * co_re[:, :, None]
        cds.append([chunks(cp_re), -chunks(cp_im)])
    return a_scan, bd, cds


def _lru_gate(l, w):
    gw = w["lru_gate_w"][l]
    per = _LANE // (BR_W // LRU_BLOCKS)
    chunks = [_blockdiag(gw[d, g, k * per:(k + 1) * per])
              for d in range(2) for g in range(2) for k in range(LRU_BLOCKS // per)]
    return jnp.stack(chunks), w["lru_gate_b"][l].reshape(1, -1)


def _pad_cols(a, n):
    return jnp.pad(a, ((0, 0), (0, n - a.shape[1])))


IN_SIZES = (BR_W,) * 9 + (M2_XBC, 2 * M2_HEADS, BR_W)
IN_OFFS = tuple(sum(IN_SIZES[:i]) for i in range(len(IN_SIZES) + 1))
IN_GROUPS = (("hg_qi", 0, 2, 1024), ("hg_ff", 2, 1, 512), ("hg_fb", 3, 1, 512), ("hg_z", 4, 1, 512),
             ("s5_u", 5, 1, 512), ("s5_z", 6, 1, 512), ("lru_x", 7, 1, 512), ("lru_z", 8, 1, 512),
             ("m2_xbc", 9, 1, 768), ("m2_dt", 10, 1, 128), ("m2_z", 11, 1, 512))


def _new_slots(big):
    slots = {n: jnp.zeros(w.shape, f32) for n, w in big.items() if n not in ("w_in", "w_gate")}
    n_layers, d_model = big["w_in"].shape[:2]
    slots["w_in"] = [{name: jnp.zeros((d_model, width), f32) for name, _, _, width in IN_GROUPS} for _ in range(n_layers)]
    slots["w_gate"] = [jnp.zeros((d_model, 4 * d_model), f32) for _ in range(n_layers)]
    return slots


def _slot_grads(g):
    out = dict(g)
    out["w_in"] = jnp.stack([
        jnp.concatenate([gl[name][:, :IN_OFFS[s0 + ns] - IN_OFFS[s0]] for name, s0, ns, _ in IN_GROUPS], axis=1)
        for gl in g["w_in"]])
    d_model = g["w_gate"][0].shape[0]
    out["w_gate"] = jnp.stack([jnp.transpose(gl.reshape(d_model, 4, d_model), (1, 0, 2)) for gl in g["w_gate"]])
    return out


def _forward(p, big, slots, x, ctx, c, target):
    n_layers = p["norm_w"].shape[0]
    d_model = x.shape[-1]
    xa = jnp.concatenate([ctx, x], axis=0)
    t = xa.shape[0]
    cc = jnp.concatenate([c, p["c_ctx"][None], jnp.zeros((_SUB - 2, d_model), f32)], axis=0)
    lb_all = jnp.cumsum(jax.nn.softmax(p["hg_lb_logits"], axis=0), axis=0)
    scc, = blocked_op("silu_c", _f_silu, [], [cc], [(d_model, bf16)], rb=_SUB)
    wide = _wide_rows(t)

    for l in range(n_layers):
        tag = "l%d_" % l
        mod = mm(tag + "mod", scc, big["w_mod"][l], slots["w_mod"][l])
        bm = p["b_mod"][l][None]
        h, = blocked_op(tag + "normmod", _f_normmod, [p["norm_w"][l][None], mod, bm], [xa], [(d_model, bf16)], rb=wide)
        gnames = [g[0] for g in IN_GROUPS]
        wvs = [_pad_cols(big["w_in"][l][:, IN_OFFS[s0]:IN_OFFS[s0 + ns]], width) for _, s0, ns, width in IN_GROUPS]
        u = dict(zip(gnames, multi_mm(tag + "in_", gnames, h, wvs, [slots["w_in"][l][g] for g in gnames])))

        o_dirs = blocked_op(tag + "hg", _both(_f_hg(False), _f_hg(True), 1, 2), [lb_all[l, 0][None], lb_all[l, 1][None]],
                            [u["hg_qi"], u["hg_ff"], u["hg_qi"], u["hg_fb"]], [(BR_W, f32)] * 2,
                            order=_BOTH_ORDERS, carry_sds=[(BR_W, HG_DK)] * 2)
        y_hg, = blocked_op(tag + "hg_fin", _f_hg_final, [p["hg_norm"][l][None]], list(o_dirs) + [u["hg_z"]], [(BR_W, bf16)],
                           rb=wide)

        a_scan, bd, cds = _s5_params(l, p)
        n_state = S5_GROUPS * S5_STATE
        planes = [s.reshape(t, n_state) for s in s5_states(tag, u["s5_u"], bd, a_scan[0], a_scan[1])]
        ysum, = bd_mm(tag + "s5_c", planes, [[cds[d][part]] for d in range(2) for part in range(2)])
        g5, = blocked_op(tag + "s5_c1", _f_s5_c1, [p["s5_d"][l][None]], [ysum, u["s5_u"]], [(BR_W, f32)], rb=wide)
        gl = mm(tag + "s5_glu", g5, big["s5_w_glu"][l], slots["s5_w_glu"][l])
        y_s5, = blocked_op(tag + "s5_c2", _f_s5_c2, [p["s5_b_glu"][l][None]], [g5, gl, u["s5_z"]], [(BR_W, bf16)],
                           rb=wide)

        wg, gb = _lru_gate(l, p)
        ab = blocked_op(tag + "lru_a", _f_lru_a,
                        [p["lru_conv_w"][l], p["lru_conv_b"][l][None], wg, gb, p["lru_lam"][l]],
                        [u["lru_x"]], [(BR_W, f32)] * 4)
        hs = []
        for d in range(2):
            a3 = ab[2 * d].reshape(t, 4, BR_W // 4)
            b3 = ab[2 * d + 1].reshape(t, 4, BR_W // 4)
            hs.append(rscan(tag + "lru_scan%d" % d, d, a3, b3).reshape(t, BR_W))
        y_lru, = blocked_op(tag + "lru_c", _f_lru_c, [], hs + [u["lru_z"]], [(BR_W, bf16)], rb=wide)

        dtb = _pad_cols(p["m2_dt_bias"][l].reshape(1, -1), _LANE)
        xbc, dtp = blocked_op(tag + "m2_a", _f_m2_a, [p["m2_conv_w"][l], p["m2_conv_b"][l][None], dtb],
                              [u["m2_xbc"], u["m2_dt"]], [(M2_XBC, f32), (_LANE, f32)])
        alog = _pad_cols(p["m2_a_log"][l].reshape(1, -1), _LANE)
        y_dirs = blocked_op(tag + "ssd", _both(_f_ssd(0), _f_ssd(1), 1, 2), [alog, alog], [xbc, dtp, xbc, dtp],
                            [(BR_W, f32)] * 2, order=_BOTH_ORDERS,
                            carry_sds=[(M2_GROUPS * M2_STATE, BR_W // M2_GROUPS)] * 2)
        dsk = jnp.repeat(p["m2_d"][l], M2_HEADDIM)[None]
        y_m2, = blocked_op(tag + "m2_c", _f_m2_c, [dsk, p["m2_norm"][l][None]], list(y_dirs) + [xbc, u["m2_z"]], [(BR_W, bf16)],
                           rb=wide)

        wg_all = jnp.transpose(big["w_gate"][l], (1, 0, 2)).reshape(d_model, 4 * d_model)
        gp = mm(tag + "gate", h, wg_all, slots["w_gate"][l], out_dtype=bf16)
        bs = [mm(tag + "br%d" % k, yk, big["w_branch"][l, k], slots["w_branch"][l, k], out_dtype=bf16)
              for k, yk in enumerate((y_hg, y_s5, y_lru, y_m2))]
        mix, = blocked_op(tag + "mix", _f_mix, [p["b_gate"][l].reshape(1, -1)], [gp] + bs, [(d_model, bf16)])
        o = mm(tag + "out", mix, big["w_out"][l], slots["w_out"][l])
        xa, = blocked_op(tag + "resid", _f_resid, [mod, bm], [xa, o], [(d_model, f32)], rb=wide)

    rl, = blocked_op("loss", _f_loss, [p["final_norm"][None]], [xa[ctx.shape[0]:], target], [(1, f32)],
                     rb=_wide_rows(target.shape[0]))
    return jnp.sum(rl)


_MESH = pl.DeviceIdType.MESH
_ANY = pl.BlockSpec(memory_space=pl.ANY)
W_PACK = 1024


def _place():
    x, y, c = lax.axis_index("x"), lax.axis_index("y"), lax.axis_index("c")
    chips = [(x, 1 - y), (1 - x, y), (1 - x, 1 - y)]
    return x, y, c, chips


def _rcopy(src, dst, ssem, rsem, k, to):
    return pltpu.make_async_remote_copy(src_ref=src, dst_ref=dst, send_sem=ssem.at[k], recv_sem=rsem.at[k],
                                        device_id=to, device_id_type=_MESH)


def gather_shards(xs):
    n = len(xs)

    def body(*refs):
        x_refs, o_refs = refs[:n], refs[n:2 * n]
        ssem, rsem, lsem = refs[2 * n:]
        x, y, c, chips = _place()
        j = 2 * x + y
        sib = (x, y, 1 - c)
        mine = [pltpu.make_async_copy(x_refs[a], o_refs[a].at[j], lsem.at[a]) for a in range(n)]
        for cp in mine:
            cp.start()
        first = [_rcopy(x_refs[a].at[c], o_refs[a].at[j, c], ssem, rsem, 6 * a + r, (*chips[r], c))
                 for r in range(3) for a in range(n)]
        for cp in first:
            cp.start()
        passed = []
        for r in range(3):
            jr = j ^ (r + 1)
            for a in range(n):
                _rcopy(x_refs[a].at[c], o_refs[a].at[jr, c], ssem, rsem, 6 * a + r, sib).wait_recv()
                cp = _rcopy(o_refs[a].at[jr, c], o_refs[a].at[jr, c], ssem, rsem, 6 * a + 3 + r, sib)
                cp.start()
                passed.append(cp)
        for r in range(3):
            jr = j ^ (r + 1)
            for a in range(n):
                _rcopy(x_refs[a].at[c], o_refs[a].at[jr, 1 - c], ssem, rsem, 6 * a + 3 + r, sib).wait_recv()
        for cp in first + passed:
            cp.wait_send()
        for cp in mine:
            cp.wait()

    return pl.pallas_call(
        body, name="gather_shards", out_shape=[jax.ShapeDtypeStruct((4,) + x.shape, x.dtype) for x in xs],
        in_specs=[_ANY] * n, out_specs=[_ANY] * n,
        scratch_shapes=[pltpu.SemaphoreType.DMA((6 * n,)), pltpu.SemaphoreType.DMA((6 * n,)), pltpu.SemaphoreType.DMA((n,))],
    )(*xs)


def sibling_halves(gs):
    n = len(gs)

    def body(*refs):
        g_refs, o_refs = refs[:n], refs[n:2 * n]
        ssem, rsem = refs[2 * n:]
        x, y, c, _ = _place()
        sib = (x, y, 1 - c)
        cps = [_rcopy(g_refs[a].at[k, 1 - c], o_refs[a].at[k], ssem, rsem, 4 * a + k, sib)
               for k in range(4) for a in range(n)]
        for cp in cps:
            cp.start()
        for cp in cps:
            cp.wait()

    return pl.pallas_call(
        body, name="sibling_halves", out_shape=[jax.ShapeDtypeStruct((4,) + g.shape[2:], g.dtype) for g in gs],
        in_specs=[_ANY] * n, out_specs=[_ANY] * n,
        scratch_shapes=[pltpu.SemaphoreType.DMA((4 * n,)), pltpu.SemaphoreType.DMA((4 * n,))],
    )(*gs)


def scatter_chips(ps):
    n = len(ps)

    def body(*refs):
        p_refs, o_refs = refs[:n], refs[n:2 * n]
        ssem, rsem = refs[2 * n:]
        x, y, c, chips = _place()
        j = 2 * x + y
        cps = [_rcopy(p_refs[a].at[j ^ (r + 1)], o_refs[a].at[r], ssem, rsem, 3 * a + r, (*chips[r], c))
               for r in range(3) for a in range(n)]
        for cp in cps:
            cp.start()
        for cp in cps:
            cp.wait()

    return pl.pallas_call(
        body, name="scatter_chips", out_shape=[jax.ShapeDtypeStruct((3,) + p.shape[1:], p.dtype) for p in ps],
        in_specs=[_ANY] * n, out_specs=[_ANY] * n,
        scratch_shapes=[pltpu.SemaphoreType.DMA((3 * n,)), pltpu.SemaphoreType.DMA((3 * n,))],
    )(*ps)


def join_halves(qs):
    n = len(qs)

    def body(*refs):
        o_refs = refs[n:2 * n]
        ssem, rsem = refs[2 * n:]
        x, y, c, _ = _place()
        sib = (x, y, 1 - c)
        cps = [_rcopy(o_refs[a].at[c], o_refs[a].at[c], ssem, rsem, a, sib) for a in range(n)]
        for cp in cps:
            cp.start()
        for a in range(n):
            _rcopy(o_refs[a].at[c], o_refs[a].at[1 - c], ssem, rsem, a, sib).wait_recv()
        for cp in cps:
            cp.wait_send()

    return pl.pallas_call(
        body, name="join_halves", out_shape=[jax.ShapeDtypeStruct(q.shape, q.dtype) for q in qs],
        in_specs=[_ANY] * n, out_specs=[_ANY] * n, input_output_aliases={a: a for a in range(n)},
        scratch_shapes=[pltpu.SemaphoreType.DMA((n,)), pltpu.SemaphoreType.DMA((n,))],
    )(*qs)


def _rows_block(r):
    return _pick(r, 256, _SUB)


def add_sibling(tag, g, r1, place, out_dtype):
    _, _, rows, w = g.shape
    rb = _rows_block(rows)

    def body(pl_ref, g_ref, r_ref, o_ref):
        o_ref[...] = (g_ref[0] + r_ref[...]).astype(out_dtype)

    return pl.pallas_call(
        body, name="add_sibling_" + tag, out_shape=jax.ShapeDtypeStruct((4, rows, w), out_dtype),
        grid_spec=pltpu.PrefetchScalarGridSpec(
            num_scalar_prefetch=1, grid=(4, rows // rb),
            in_specs=[pl.BlockSpec((1, 1, rb, w), lambda k, i, s: (k, s[1], i, 0)),
                      pl.BlockSpec((1, rb, w), lambda k, i, s: (k, i, 0))],
            out_specs=pl.BlockSpec((1, rb, w), lambda k, i, s: (k, i, 0))),
        compiler_params=_cparams(2),
    )(place, g, r1)


def add_chips(tag, p, r2, place):
    _, rows, w = p.shape
    rb = _rows_block(rows)

    def body(pl_ref, p_ref, r_ref, o_ref):
        j = pl_ref[0]
        own = p_ref[0].astype(f32)
        others = [r_ref[0].astype(f32), r_ref[1].astype(f32), r_ref[2].astype(f32)]
        acc = None
        for k in range(4):
            rel = k ^ j
            t = jnp.where(rel == 0, own, jnp.where(rel == 1, others[0], jnp.where(rel == 2, others[1], others[2])))
            acc = t if acc is None else acc + t
        o_ref[0] = acc

    return pl.pallas_call(
        body, name="add_chips_" + tag, out_shape=jax.ShapeDtypeStruct((2, rows, w), f32),
        grid_spec=pltpu.PrefetchScalarGridSpec(
            num_scalar_prefetch=1, grid=(rows // rb,),
            in_specs=[pl.BlockSpec((1, rb, w), lambda i, s: (s[0], i, 0)),
                      pl.BlockSpec((3, rb, w), lambda i, s: (0, i, 0))],
            out_specs=pl.BlockSpec((1, rb, w), lambda i, s: (s[1], i, 0))),
        compiler_params=_cparams(1),
    )(place, p, r2)


def adamw(tag, g, w, m, v):
    rows, wd = g.shape
    rb = _rows_block(rows)

    def body(g_ref, w_ref, m_ref, v_ref, d_ref, nm_ref, nv_ref):
        gv = g_ref[...]
        nm = ADAM_B1 * m_ref[...] + (1.0 - ADAM_B1) * gv
        nv = ADAM_B2 * v_ref[...] + (1.0 - ADAM_B2) * (gv * gv)
        m_hat = nm / (1.0 - ADAM_B1 ** ADAM_STEP)
        v_hat = nv / (1.0 - ADAM_B2 ** ADAM_STEP)
        d_ref[...] = -ADAM_LR * (m_hat / (jnp.sqrt(v_hat) + ADAM_EPS) + ADAM_WD * w_ref[...])
        nm_ref[...] = nm
        nv_ref[...] = nv

    spec = pl.BlockSpec((rb, wd), lambda i: (i, 0))
    return pl.pallas_call(
        body, name="adamw_" + tag, grid=(rows // rb,), in_specs=[spec] * 4, out_specs=[spec] * 3,
        out_shape=[jax.ShapeDtypeStruct(g.shape, f32)] * 3, compiler_params=_cparams(1),
    )(g, w, m, v)


WEIGHTS = ("c_ctx", "norm_w", "w_mod", "b_mod", "w_in", "hg_lb_logits", "hg_norm", "s5_a_re", "s5_a_im", "s5_log_step",
           "s5_b_re", "s5_b_im", "s5_c_re", "s5_c_im", "s5_d", "s5_w_glu", "s5_b_glu", "lru_conv_w", "lru_conv_b",
           "lru_gate_w", "lru_gate_b", "lru_lam", "m2_conv_w", "m2_conv_b", "m2_dt_bias", "m2_a_log", "m2_d", "m2_norm",
           "w_branch", "w_gate", "b_gate", "w_out", "final_norm")
SHARD_AXIS = {"w_mod": 2, "w_in": 2, "hg_lb_logits": 2, "s5_w_glu": 1, "lru_conv_w": 2, "lru_lam": 2, "m2_conv_w": 2,
              "w_branch": 3, "w_gate": 2, "b_gate": 2, "w_out": 1}
BIG = ("w_mod", "w_in", "s5_w_glu", "w_branch", "w_gate", "w_out")
N_CHIPS = 4


def _to_rows(flat, row_unit):
    n = flat.shape[-1]
    per = 2 * row_unit * W_PACK
    total = -(-n // per) * per
    flat = jnp.pad(flat, [(0, 0)] * (flat.ndim - 1) + [(0, total - n)])
    return flat.reshape(flat.shape[:-1] + (2, total // (2 * W_PACK), W_PACK))


SMALL_SHARDED = tuple(n for n in WEIGHTS if n in SHARD_AXIS and n not in BIG)
SMALL_REPLICATED = tuple(n for n in WEIGHTS if n not in SHARD_AXIS)


def _chip_slices(a, axis):
    width = a.shape[axis] // N_CHIPS
    return jnp.stack([lax.slice_in_dim(a, k * width, (k + 1) * width, axis=axis) for k in range(N_CHIPS)])


def _gather_weights(local):
    small = jnp.concatenate([lax.bitcast_convert_type(local[n], bf16).reshape(-1) for n in SMALL_SHARDED])
    got = gather_shards([local[n].astype(bf16) for n in BIG] + [_to_rows(small, 16)])
    full = {}
    for n, g in zip(BIG, got):
        full[n] = jnp.concatenate([g[j] for j in range(N_CHIPS)], axis=SHARD_AXIS[n])
    flat, off = got[-1].reshape(N_CHIPS, -1), 0
    for n in SMALL_SHARDED:
        shp = local[n].shape
        size = 2 * math.prod(shp)
        part = lax.bitcast_convert_type(flat[:, off:off + size].reshape((N_CHIPS,) + shp + (2,)), f32)
        off += size
        full[n] = jnp.concatenate([part[j] for j in range(N_CHIPS)], axis=SHARD_AXIS[n])
    return full


def _whole_rows(v):
    n = v.shape[-1]
    return jnp.pad(v, [(0, 0)] * (v.ndim - 1) + [(0, -n % W_PACK)])


def _pack_small(vals, extra):
    return jnp.concatenate([_whole_rows(vals[n].reshape(-1)) for n in SMALL_SHARDED + SMALL_REPLICATED]
                           + [_whole_rows(extra.reshape(1))])


def _pack_small_grads(grads, loss):
    rep = [grads[n].reshape(-1) for n in SMALL_REPLICATED] + [loss.reshape(1)]
    sh = [_chip_slices(grads[n], SHARD_AXIS[n]).reshape(N_CHIPS, -1) for n in SMALL_SHARDED]
    return jnp.concatenate([_whole_rows(a) for a in sh]
                           + [_whole_rows(jnp.broadcast_to(r, (N_CHIPS,) + r.shape)) for r in rep], axis=1)


def _unpack_small(rows, like):
    out, r0 = {}, 0
    for n in SMALL_SHARDED + SMALL_REPLICATED:
        size = math.prod(like[n].shape)
        nr = -(-size // W_PACK)
        piece = lax.optimization_barrier(rows[r0:r0 + nr])
        out[n] = piece.reshape(-1)[:size].reshape(like[n].shape)
        r0 += nr
    return out, lax.optimization_barrier(rows[r0:r0 + 1])[0, 0]


def _reduce_grads(tags, gs):
    place = jnp.stack([2 * lax.axis_index("x") + lax.axis_index("y"), lax.axis_index("c")]).astype(jnp.int32)
    pairs = [add_sibling(t, g, r, place, bf16 if t in BIG else f32) for t, g, r in zip(tags, gs, sibling_halves(gs))]
    quads = [add_chips(t, p, r, place) for t, p, r in zip(tags, pairs, scatter_chips(pairs))]
    return join_halves(quads)


def kernel(x, c, ctx, c_ctx, norm_w, w_mod, b_mod, w_in, hg_lb_logits, hg_norm, s5_a_re, s5_a_im, s5_log_step, s5_b_re, s5_b_im, s5_c_re, s5_c_im, s5_d, s5_w_glu, s5_b_glu, lru_conv_w, lru_conv_b, lru_gate_w, lru_gate_b, lru_lam, m2_conv_w, m2_conv_b, m2_dt_bias, m2_a_log, m2_d, m2_norm, w_branch, w_gate, b_gate, w_out, final_norm, loss_target, m_c_ctx, m_norm_w, m_w_mod, m_b_mod, m_w_in, m_hg_lb_logits, m_hg_norm, m_s5_a_re, m_s5_a_im, m_s5_log_step, m_s5_b_re, m_s5_b_im, m_s5_c_re, m_s5_c_im, m_s5_d, m_s5_w_glu, m_s5_b_glu, m_lru_conv_w, m_lru_conv_b, m_lru_gate_w, m_lru_gate_b, m_lru_lam, m_m2_conv_w, m_m2_conv_b, m_m2_dt_bias, m_m2_a_log, m_m2_d, m_m2_norm, m_w_branch, m_w_gate, m_b_gate, m_w_out, m_final_norm, v_c_ctx, v_norm_w, v_w_mod, v_b_mod, v_w_in, v_hg_lb_logits, v_hg_norm, v_s5_a_re, v_s5_a_im, v_s5_log_step, v_s5_b_re, v_s5_b_im, v_s5_c_re, v_s5_c_im, v_s5_d, v_s5_w_glu, v_s5_b_glu, v_lru_conv_w, v_lru_conv_b, v_lru_gate_w, v_lru_gate_b, v_lru_lam, v_m2_conv_w, v_m2_conv_b, v_m2_dt_bias, v_m2_a_log, v_m2_d, v_m2_norm, v_w_branch, v_w_gate, v_b_gate, v_w_out, v_final_norm):
    given = dict(locals())
    w_loc = {n: given[n] for n in WEIGHTS}
    m_loc = {n: given["m_" + n] for n in WEIGHTS}
    v_loc = {n: given["v_" + n] for n in WEIGHTS}

    full = _gather_weights(w_loc)
    params = {n: (full[n] if n in SHARD_AXIS else w_loc[n]) for n in WEIGHTS if n not in BIG}
    big = {n: full[n] for n in BIG}
    def loss_fn(p, s, xx):
        return _forward(p, big, s, xx, ctx[0], c, loss_target[0])

    loss, (g_p, g_s, g_x) = jax.value_and_grad(loss_fn, argnums=(0, 1, 2))(params, _new_slots(big), x[0])
    grads = {**g_p, **_slot_grads(g_s)}

    def rows4(a):
        return a.reshape(a.shape[:2] + (-1, a.shape[-1]))

    g_big = [rows4(_chip_slices(grads[n], SHARD_AXIS[n])) for n in BIG]
    g_small = _to_rows(_pack_small_grads(grads, loss), 64)
    summed = _reduce_grads(list(BIG) + ["small"], g_big + [g_small])

    g_out, d_out, m_out, v_out = {}, {}, {}, {}
    for n, g in zip(BIG, summed):
        shp = w_loc[n].shape
        flat2 = lambda a: a.reshape(-1, shp[-1])
        g_out[n] = g.reshape(shp)
        d, nm, nv = adamw(n, flat2(g), flat2(w_loc[n]), flat2(m_loc[n]), flat2(v_loc[n]))
        d_out[n], m_out[n], v_out[n] = d.reshape(shp), nm.reshape(shp), nv.reshape(shp)
    zero = jnp.zeros((), f32)
    flat = lambda vals: _to_rows(_pack_small(vals, zero), 64).reshape(-1, W_PACK)
    gs = summed[-1].reshape(-1, W_PACK)
    d, nm, nv = adamw("small", gs, flat(w_loc), flat(m_loc), flat(v_loc))
    gsm, loss_out = _unpack_small(gs, w_loc)
    g_out.update(gsm)
    d_out.update(_unpack_small(d, w_loc)[0])
    m_out.update(_unpack_small(nm, w_loc)[0])
    v_out.update(_unpack_small(nv, w_loc)[0])
    outs = [loss_out, g_x[None]]
    for group in (g_out, d_out, m_out, v_out):
        outs += [group[n] for n in WEIGHTS]
    return tuple(outs)
```

```python
import functools
import math

import jax
import jax.numpy as jnp
from jax import lax
from jax.experimental import pallas as pl
from jax.experimental.pallas import tpu as pltpu

f32 = jnp.float32
bf16 = jnp.bfloat16
_MM_DTYPE = bf16
_HI = lax.Precision.HIGHEST
_MAP_PREC = lax.Precision.HIGH
_VMEM_LIMIT = 56 * 1024 * 1024
_LANE = 128
_SUB = 8

EPS = 1e-6
CONV_W = 4
CHUNK = 64
RB = 256
BR_W = 512
HG_HEADS = 4
HG_DK = 128
S5_GROUPS = 32
S5_GROUP = 16
S5_STATE = 64
LRU_BLOCKS = 8
LRU_C = 8.0
M2_HEADS = 8
M2_HEADDIM = 64
M2_GROUPS = 2
M2_STATE = 64
M2_XBC = BR_W + 2 * M2_GROUPS * M2_STATE
ADAM_LR = 0.001
ADAM_B1 = 0.9
ADAM_B2 = 0.999
ADAM_EPS = 1e-08
ADAM_WD = 0.01
ADAM_STEP = 10

_NN = (((1,), (0,)), ((), ()))
_NT = (((1,), (1,)), ((), ()))
_TN = (((0,), (0,)), ((), ()))


def _silu(x):
    return x * jax.nn.sigmoid(x)


def _softplus(x):
    return jnp.maximum(x, 0.0) + jnp.log1p(jnp.exp(-jnp.abs(x)))


def _one_minus_exp(z):
    series = -z * (1.0 + z * 0.5 * (1.0 + z * (1.0 / 3.0) * (1.0 + z * 0.25 * (1.0 + z * 0.2))))
    return jnp.where(z > -0.05, series, 1.0 - jnp.exp(z))


def _rms(x, w):
    return x * lax.rsqrt(jnp.mean(x * x, axis=-1, keepdims=True) + EPS) * w


def _dot(a, b, dn=_NN, hi=False):
    prec = hi if isinstance(hi, lax.Precision) else (_HI if hi else None)
    return lax.dot_general(a, b, dn, precision=prec, preferred_element_type=f32)


def _cparams(n_grid):
    return pltpu.CompilerParams(dimension_semantics=("arbitrary",) * n_grid, vmem_limit_bytes=_VMEM_LIMIT)


_REV = {"asc": "desc", "d1": "d1r", "desc": "asc", "d1r": "d1"}


def _blk(order, i, n):
    if order == "asc":
        return i
    if order == "desc":
        return n - 1 - i
    if order == "d1":
        return jnp.where(i == 0, 0, n - i)
    return jnp.where(i == n - 1, 0, i + 1)


def _pick(n, cap, unit):
    if n <= cap:
        return n
    best = None
    d = unit
    while d <= cap:
        if n % d == 0:
            best = d
        d += unit
    return n if best is None else best


def _mm_call(name, a, b, mode, hi, out_dtype):
    if mode == "tn":
        k, m = a.shape
        n = b.shape[1]
        tm = _pick(m, 512, _LANE)
        tn = _pick(n, 512, _LANE)
        a_spec = pl.BlockSpec((k, tm), lambda i, j: (0, i))
        b_spec = pl.BlockSpec((k, tn), lambda i, j: (0, j))
    else:
        m, k = a.shape
        tm = _pick(m, max(256, min(1088, 4 * 1024 * 1024 // (k * a.dtype.itemsize))), _SUB)
        a_spec = pl.BlockSpec((tm, k), lambda i, j: (i, 0))
        if mode == "nn":
            n = b.shape[1]
            tn = _pick(n, max(_LANE, (4 * 1024 * 1024 // (k * 4)) // _LANE * _LANE), _LANE)
            b_spec = pl.BlockSpec((k, tn), lambda i, j: (0, j))
        else:
            n = b.shape[0]
            tn = _pick(n, max(_LANE, (4 * 1024 * 1024 // (k * 4)) // _LANE * _LANE), _LANE)
            b_spec = pl.BlockSpec((tn, k), lambda i, j: (j, 0))
    dn = {"nn": _NN, "nt": _NT, "tn": _TN}[mode]

    def body(a_ref, b_ref, o_ref):
        av = a_ref[...]
        bv = b_ref[...]
        if hi:
            av = av.astype(f32)
            bv = bv.astype(f32)
        else:
            av = av.astype(_MM_DTYPE)
            bv = bv.astype(_MM_DTYPE)
        o_ref[...] = _dot(av, bv, dn, hi).astype(o_ref.dtype)

    return pl.pallas_call(
        body, name=name, grid=(m // tm, n // tn), in_specs=[a_spec, b_spec],
        out_specs=pl.BlockSpec((tm, tn), lambda i, j: (i, j)),
        out_shape=jax.ShapeDtypeStruct((m, n), out_dtype), compiler_params=_cparams(2),
    )(a, b)


def mm(name, a, b, slot=None, hi=False, out_dtype=f32):
    @jax.custom_vjp
    def op(a, b, slot):
        return _mm_call(name, a, b, "nn", hi, out_dtype)

    def fwd(a, b, slot):
        return op(a, b, slot), (a, b)

    def bwd(res, g):
        a, b = res
        da = _mm_call(name + "_da", g, b, "nt", hi, a.dtype)
        db = _mm_call(name + "_db", a, g, "tn", hi, f32)
        if slot is None:
            return da, db.astype(b.dtype), None
        return da, jnp.zeros_like(b), db

    op.defvjp(fwd, bwd)
    return op(a, b, slot)


def _sum_nt_call(name, gs, ws, out_dtype):
    m = gs[0].shape[0]
    kdim = ws[0].shape[0]
    tm = _pick(m, 256, _SUB)
    n = len(gs)

    def body(*refs):
        acc = None
        for g_ref, w_ref in zip(refs[:n], refs[n:2 * n]):
            part = _dot(g_ref[...].astype(_MM_DTYPE), w_ref[...].astype(_MM_DTYPE), _NT)
            acc = part if acc is None else acc + part
        refs[2 * n][...] = acc.astype(out_dtype)

    in_specs = [pl.BlockSpec((tm, g.shape[1]), lambda i: (i, 0)) for g in gs]
    in_specs += [pl.BlockSpec(w.shape, lambda i: (0, 0)) for w in ws]
    return pl.pallas_call(
        body, name=name, grid=(m // tm,), in_specs=in_specs, out_specs=pl.BlockSpec((tm, kdim), lambda i: (i, 0)),
        out_shape=jax.ShapeDtypeStruct((m, kdim), out_dtype), compiler_params=_cparams(1),
    )(*gs, *ws)


def multi_mm(tag, names, a, ws, slots):
    @jax.custom_vjp
    def op(a, ws, slots):
        return tuple(_mm_call(tag + n, a, w, "nn", False, f32) for n, w in zip(names, ws))

    def fwd(a, ws, slots):
        return op(a, ws, slots), (a, ws)

    def bwd(res, gs):
        a, ws = res
        dws = [_mm_call(tag + n + "_db", a, g, "tn", False, f32) for n, g in zip(names, gs)]
        da = _sum_nt_call(tag + "da", list(gs), ws, a.dtype)
        return da, [jnp.zeros_like(w) for w in ws], dws

    op.defvjp(fwd, bwd)
    return op(a, list(ws), list(slots))


def _orders(order, n_x, n_o):
    if isinstance(order, str):
        return [order] * n_x, [order] * n_o
    return list(order[0]), list(order[1])


def _row(o, n):
    return lambda i: (_blk(o, i, n), 0)


def _blocked_fwd(name, f, order, rb, params, xs, out_sds, carry_sds):
    t = xs[0].shape[0]
    n = t // rb
    n_p, n_x, n_o, n_c = len(params), len(xs), len(out_sds), len(carry_sds)
    xo, oo = _orders(order, n_x, n_o)

    def body(*refs):
        p_refs = refs[:n_p]
        x_refs = refs[n_p:n_p + n_x]
        o_refs = refs[n_p + n_x:n_p + n_x + n_o]
        st_refs = refs[n_p + n_x + n_o:n_p + n_x + n_o + n_c]
        c_refs = refs[n_p + n_x + n_o + n_c:]
        i = pl.program_id(0)
        blk = _blk(xo[0], i, n)
        p = [r[...] for r in p_refs]
        x = [r[...] for r in x_refs]
        if n_c:
            @pl.when(i == 0)
            def _():
                for c in c_refs:
                    c[...] = jnp.zeros_like(c)
            c_in = [c[...] for c in c_refs]
            for sr, c in zip(st_refs, c_in):
                sr[0] = c
            c_out, ys = f(blk, p, c_in, x)
            for c, v in zip(c_refs, c_out):
                c[...] = v
        else:
            ys = f(blk, p, x)
        for o, y in zip(o_refs, ys):
            o[...] = y.astype(o.dtype)

    in_specs = [pl.BlockSpec(p.shape, lambda i, nd=p.ndim: (0,) * nd) for p in params]
    in_specs += [pl.BlockSpec((rb, x.shape[1]), _row(o, n)) for x, o in zip(xs, xo)]
    out_specs = [pl.BlockSpec((rb, c), _row(o, n)) for (c, _), o in zip(out_sds, oo)]
    out_specs += [pl.BlockSpec((1,) + s, lambda i: (i, 0, 0)) for s in carry_sds]
    out_shape = [jax.ShapeDtypeStruct((t, c), d) for c, d in out_sds]
    out_shape += [jax.ShapeDtypeStruct((n,) + s, f32) for s in carry_sds]
    res = pl.pallas_call(
        body, name=name, grid=(n,), in_specs=in_specs, out_specs=out_specs, out_shape=out_shape,
        scratch_shapes=[pltpu.VMEM(s, f32) for s in carry_sds], compiler_params=_cparams(1),
    )(*params, *xs)
    return list(res[:n_o]), list(res[n_o:])


def _blocked_bwd(name, f, order, rb, params, xs, states, dys, carry_sds):
    t = xs[0].shape[0]
    n = t // rb
    n_p, n_x, n_o, n_c = len(params), len(xs), len(dys), len(carry_sds)
    xo, oo = _orders(order, n_x, n_o)
    xo, oo = [_REV[o] for o in xo], [_REV[o] for o in oo]

    def body(*refs):
        k = 0
        p_refs = refs[k:k + n_p]; k += n_p
        x_refs = refs[k:k + n_x]; k += n_x
        st_refs = refs[k:k + n_c]; k += n_c
        dy_refs = refs[k:k + n_o]; k += n_o
        dp_refs = refs[k:k + n_p]; k += n_p
        dx_refs = refs[k:k + n_x]; k += n_x
        dc_refs = refs[k:]
        i = pl.program_id(0)
        blk = _blk(xo[0], i, n)
        p = [r[...] for r in p_refs]
        x = [r[...] for r in x_refs]
        dy = [r[...] for r in dy_refs]
        if n_c:
            @pl.when(i == 0)
            def _():
                for c in dc_refs:
                    c[...] = jnp.zeros_like(c)
            c_in = [r[0] for r in st_refs]
            dc = [c[...] for c in dc_refs]
            _, vjp = jax.vjp(lambda p_, c_, x_: f(blk, p_, c_, x_), p, c_in, x)
            dp, dcin, dx = vjp((dc, dy))
            for c, v in zip(dc_refs, dcin):
                c[...] = v
        else:
            _, vjp = jax.vjp(lambda p_, x_: f(blk, p_, x_), p, x)
            dp, dx = vjp(dy)

        @pl.when(i == 0)
        def _():
            for r, v in zip(dp_refs, dp):
                r[...] = v

        @pl.when(i > 0)
        def _():
            for r, v in zip(dp_refs, dp):
                r[...] += v
        for r, v in zip(dx_refs, dx):
            r[...] = v.astype(r.dtype)

    in_specs = [pl.BlockSpec(p.shape, lambda i, nd=p.ndim: (0,) * nd) for p in params]
    in_specs += [pl.BlockSpec((rb, x.shape[1]), _row(o, n)) for x, o in zip(xs, xo)]
    in_specs += [pl.BlockSpec((1,) + s, lambda i: (n - 1 - i, 0, 0)) for s in carry_sds]
    in_specs += [pl.BlockSpec((rb, d.shape[1]), _row(o, n)) for d, o in zip(dys, oo)]
    out_specs = [pl.BlockSpec(p.shape, lambda i, nd=p.ndim: (0,) * nd) for p in params]
    out_specs += [pl.BlockSpec((rb, x.shape[1]), _row(o, n)) for x, o in zip(xs, xo)]
    out_shape = [jax.ShapeDtypeStruct(p.shape, f32) for p in params]
    out_shape += [jax.ShapeDtypeStruct(x.shape, x.dtype) for x in xs]
    res = pl.pallas_call(
        body, name=name + "_bwd", grid=(n,), in_specs=in_specs, out_specs=out_specs, out_shape=out_shape,
        scratch_shapes=[pltpu.VMEM(s, f32) for s in carry_sds], compiler_params=_cparams(1),
    )(*params, *xs, *states, *dys)
    return list(res[:n_p]), list(res[n_p:])


def blocked_op(name, f, params, xs, out_sds, order="asc", carry_sds=(), rb=RB):
    carry_sds = tuple(carry_sds)

    @jax.custom_vjp
    def op(params, xs):
        return tuple(_blocked_fwd(name, f, order, rb, params, xs, out_sds, carry_sds)[0])

    def fwd(params, xs):
        ys, states = _blocked_fwd(name, f, order, rb, params, xs, out_sds, carry_sds)
        return tuple(ys), (params, xs, states)

    def bwd(res, dys):
        params, xs, states = res
        dp, dx = _blocked_bwd(name, f, order, rb, params, xs, states, list(dys), carry_sds)
        return list(dp), list(dx)

    op.defvjp(fwd, bwd)
    return op(list(params), list(xs))


def _cscan_call(name, order, asc, a, xr, xi, sr=None, si=None):
    t = xr.shape[0]
    n = t // RB
    tile = xr.shape[1:]
    xspec = pl.BlockSpec((RB,) + tile, lambda i: (_blk(order, i, n), 0, 0))
    aspec = pl.BlockSpec(a.shape, lambda i: (0, 0, 0))
    plane = jax.ShapeDtypeStruct(xr.shape, f32)

    def rowidx(tt):
        return tt if asc else RB - 1 - tt

    if sr is None:
        def body(a_ref, xr_ref, xi_ref, sr_ref, si_ref, c_ref):
            i = pl.program_id(0)

            @pl.when(i == 0)
            def _():
                c_ref[...] = jnp.zeros_like(c_ref)
            ar = a_ref[0]
            ai = a_ref[1]
            a2r = ar * ar - ai * ai
            a2i = 2.0 * ar * ai

            def step(tt, carry):
                cr, ci = carry
                r1 = rowidx(2 * tt)
                r2 = rowidx(2 * tt + 1)
                x1r, x1i, x2r, x2i = xr_ref[r1], xi_ref[r1], xr_ref[r2], xi_ref[r2]
                s1r = ar * cr - ai * ci + x1r
                s1i = ar * ci + ai * cr + x1i
                kr = ar * x1r - ai * x1i + x2r
                ki = ar * x1i + ai * x1r + x2i
                s2r = a2r * cr - a2i * ci + kr
                s2i = a2r * ci + a2i * cr + ki
                sr_ref[r1] = s1r
                si_ref[r1] = s1i
                sr_ref[r2] = s2r
                si_ref[r2] = s2i
                return s2r, s2i
            cr, ci = lax.fori_loop(0, RB // 2, step, (c_ref[0], c_ref[1]), unroll=4)
            c_ref[0] = cr
            c_ref[1] = ci

        return pl.pallas_call(
            body, name=name, grid=(n,), in_specs=[aspec, xspec, xspec], out_specs=[xspec, xspec],
            out_shape=[plane, plane], scratch_shapes=[pltpu.VMEM(a.shape, f32)], compiler_params=_cparams(1),
        )(a, xr, xi)

    def body(a_ref, xr_ref, xi_ref, sr_ref, si_ref, gr_ref, gi_ref, da_ref, c_ref):
        i = pl.program_id(0)

        @pl.when(i == 0)
        def _():
            c_ref[...] = jnp.zeros_like(c_ref)
            da_ref[...] = jnp.zeros_like(da_ref)
        ar = a_ref[0]
        ai = a_ref[1]
        a2r = ar * ar - ai * ai
        a2i = 2.0 * ar * ai

        def step(tt, carry):
            gr, gi, dar, dai, dbr, dbi = carry
            r1 = rowidx(2 * tt)
            r2 = rowidx(2 * tt + 1)
            x1r, x1i, x2r, x2i = xr_ref[r1], xi_ref[r1], xr_ref[r2], xi_ref[r2]
            v1r, v1i, v2r, v2i = sr_ref[r1], si_ref[r1], sr_ref[r2], si_ref[r2]
            g1r = x1r + ar * gr + ai * gi
            g1i = x1i + ar * gi - ai * gr
            kr = x2r + ar * x1r + ai * x1i
            ki = x2i + ar * x1i - ai * x1r
            g2r = kr + a2r * gr + a2i * gi
            g2i = ki + a2r * gi - a2i * gr
            dar = dar + gr * v1r + gi * v1i
            dai = dai + gi * v1r - gr * v1i
            dbr = dbr + g1r * v2r + g1i * v2i
            dbi = dbi + g1i * v2r - g1r * v2i
            gr_ref[r1] = g1r
            gi_ref[r1] = g1i
            gr_ref[r2] = g2r
            gi_ref[r2] = g2i
            return g2r, g2i, dar, dai, dbr, dbi
        z = jnp.zeros(tile, f32)
        gr, gi, dar, dai, dbr, dbi = lax.fori_loop(0, RB // 2, step, (c_ref[0], c_ref[1], z, z, z, z), unroll=4)
        c_ref[0] = gr
        c_ref[1] = gi
        da_ref[0] += dar + dbr
        da_ref[1] += dai + dbi

    return pl.pallas_call(
        body, name=name, grid=(n,), in_specs=[aspec] + [xspec] * 4, out_specs=[xspec, xspec, aspec],
        out_shape=[plane, plane, jax.ShapeDtypeStruct(a.shape, f32)],
        scratch_shapes=[pltpu.VMEM(a.shape, f32)], compiler_params=_cparams(1),
    )(a, xr, xi, sr, si)


def s5_states(tag, u, bd, a0, a1):
    t = u.shape[0]
    tile = a0.shape[1:]

    def run(u, bd, a0, a1):
        bu = [b.reshape((t,) + tile) for b in _bd_call(tag + "s5_bu", [u], [bd], "nn")]
        s0 = _cscan_call(tag + "s5_scan0", "asc", True, a0, bu[0], bu[1])
        s1 = _cscan_call(tag + "s5_scan1", "d1", False, a1, bu[0], bu[1])
        return (s0[0], s0[1], s1[0], s1[1])

    @jax.custom_vjp
    def op(u, bd, a0, a1):
        return run(u, bd, a0, a1)

    def fwd(u, bd, a0, a1):
        s = run(u, bd, a0, a1)
        return s, (u, bd, a0, a1, s)

    def bwd(res, ds):
        u, bd, a0, a1, s = res
        g0r, g0i, da0 = _cscan_call(tag + "s5_scan0_bwd", "desc", False, a0, ds[0], ds[1], s[0], s[1])
        g1r, g1i, da1 = _cscan_call(tag + "s5_scan1_bwd", "d1r", True, a1, ds[2], ds[3], s[2], s[3])
        gs = [g.reshape(t, -1) for g in (g0r, g0i, g1r, g1i)]
        du, = _bd_call(tag + "s5_bu_da", gs, [[bd[0], bd[1], bd[0], bd[1]]], "nt")
        k = bd[0].shape[0]
        dbd = [_bd_call(tag + "s5_bu_db%d" % j, u, gs[j], "tn", k) + _bd_call(tag + "s5_bu_db%d" % (j + 2), u, gs[j + 2], "tn", k)
               for j in range(2)]
        return du, dbd, da0, da1

    op.defvjp(fwd, bwd)
    return op(u, list(bd), a0, a1)


def _bd_call(name, a, b, mode, k=None):
    if mode == "tn":
        t = a.shape[0]
        ck, cn = a.shape[1] // k, b.shape[1] // k

        def body(a_ref, b_ref, o_ref):
            o_ref[0] = _dot(a_ref[...], b_ref[...], _TN, _MAP_PREC)

        return pl.pallas_call(
            body, name=name, grid=(k,),
            in_specs=[pl.BlockSpec((t, ck), lambda j: (0, j)), pl.BlockSpec((t, cn), lambda j: (0, j))],
            out_specs=pl.BlockSpec((1, ck, cn), lambda j: (j, 0, 0)),
            out_shape=jax.ShapeDtypeStruct((k, ck, cn), f32), compiler_params=_cparams(1),
        )(a, b)
    k, ck, cn = b[0][0].shape
    n_i, n_o = len(b), len(b[0])
    n_x = len(a)
    t = a[0].shape[0]
    tm = _pick(t, 1088, _SUB)
    flat = [w for row in b for w in row]
    win, wout, n_out, dn = (ck, cn, n_o, _NN) if mode == "nn" else (cn, ck, n_i, _NT)

    def body(*refs):
        xv = [r[...] for r in refs[:n_x]]
        w_refs = refs[n_x:n_x + len(flat)]
        o_refs = refs[n_x + len(flat):]
        for q in range(n_out):
            acc = None
            for s in range(n_x):
                w = w_refs[s * n_o + q] if mode == "nn" else w_refs[q * n_o + s]
                part = _dot(xv[s], w[0], dn, _MAP_PREC)
                acc = part if acc is None else acc + part
            o_refs[q][...] = acc

    return pl.pallas_call(
        body, name=name, grid=(t // tm, k),
        in_specs=[pl.BlockSpec((tm, win), lambda i, j: (i, j))] * n_x
        + [pl.BlockSpec((1, ck, cn), lambda i, j: (j, 0, 0))] * len(flat),
        out_specs=[pl.BlockSpec((tm, wout), lambda i, j: (i, j))] * n_out,
        out_shape=[jax.ShapeDtypeStruct((t, k * wout), f32)] * n_out, compiler_params=_cparams(2),
    )(*a, *flat)


def bd_mm(name, xs, ws):
    k = ws[0][0].shape[0]

    @jax.custom_vjp
    def op(xs, ws):
        return tuple(_bd_call(name, xs, ws, "nn"))

    def fwd(xs, ws):
        return op(xs, ws), (xs, ws)

    def bwd(res, gs):
        xs, ws = res
        dws = [[_bd_call(name + "_db%d%d" % (i, o), x, g, "tn", k) for o, g in enumerate(gs)] for i, x in enumerate(xs)]
        return list(_bd_call(name + "_da", list(gs), ws, "nt")), dws

    op.defvjp(fwd, bwd)
    return op(list(xs), [list(row) for row in ws])


def _rscan_call(name, order, asc, a, x, hp=None):
    t = x.shape[0]
    n = t // RB
    cshape = (1, x.shape[1])
    xspec = pl.BlockSpec((RB, x.shape[1]), lambda i: (_blk(order, i, n), 0))

    def rd(ref, r):
        return ref[pl.ds(r, 1), :]

    def wr(ref, r, v):
        ref[pl.ds(r, 1), :] = v

    def rowidx(tt):
        return tt if asc else RB - 1 - tt

    if hp is None:
        def body(a_ref, x_ref, h_ref, hp_ref, c_ref):
            i = pl.program_id(0)

            @pl.when(i == 0)
            def _():
                c_ref[...] = jnp.zeros_like(c_ref)

            def step(tt, h):
                r1 = rowidx(2 * tt)
                r2 = rowidx(2 * tt + 1)
                a1, a2, x1, x2 = rd(a_ref, r1), rd(a_ref, r2), rd(x_ref, r1), rd(x_ref, r2)
                h1 = a1 * h + x1
                h2 = (a2 * a1) * h + (a2 * x1 + x2)
                wr(hp_ref, r1, h)
                wr(h_ref, r1, h1)
                wr(hp_ref, r2, h1)
                wr(h_ref, r2, h2)
                return h2
            c_ref[...] = lax.fori_loop(0, RB // 2, step, c_ref[...], unroll=4)

        return pl.pallas_call(
            body, name=name, grid=(n,), in_specs=[xspec, xspec], out_specs=[xspec, xspec],
            out_shape=[jax.ShapeDtypeStruct(x.shape, f32)] * 2, scratch_shapes=[pltpu.VMEM(cshape, f32)],
            compiler_params=_cparams(1),
        )(a, x)

    def body(a_ref, x_ref, hp_ref, da_ref, db_ref, c_ref):
        i = pl.program_id(0)

        @pl.when(i == 0)
        def _():
            c_ref[...] = jnp.zeros_like(c_ref)

        def step(tt, c):
            r1 = rowidx(2 * tt)
            r2 = rowidx(2 * tt + 1)
            a1, a2, x1, x2 = rd(a_ref, r1), rd(a_ref, r2), rd(x_ref, r1), rd(x_ref, r2)
            g1 = x1 + c
            k = x2 + a1 * x1
            g2 = k + a1 * c
            wr(db_ref, r1, g1)
            wr(da_ref, r1, g1 * rd(hp_ref, r1))
            wr(db_ref, r2, g2)
            wr(da_ref, r2, g2 * rd(hp_ref, r2))
            return a2 * k + (a2 * a1) * c
        c_ref[...] = lax.fori_loop(0, RB // 2, step, c_ref[...], unroll=4)

    return pl.pallas_call(
        body, name=name, grid=(n,), in_specs=[xspec, xspec, xspec], out_specs=[xspec, xspec],
        out_shape=[jax.ShapeDtypeStruct(x.shape, f32)] * 2, scratch_shapes=[pltpu.VMEM(cshape, f32)],
        compiler_params=_cparams(1),
    )(a, x, hp)


def rscan(name, d, a, x):
    order = "d1" if d else "asc"

    @jax.custom_vjp
    def op(a, x):
        return _rscan_call(name, order, d == 0, a, x)[0]

    def fwd(a, x):
        h, hp = _rscan_call(name, order, d == 0, a, x)
        return h, (a, hp)

    def bwd(res, dh):
        a, hp = res
        da, db = _rscan_call(name + "_bwd", _REV[order], d != 0, a, dh, hp)
        return da, db

    op.defvjp(fwd, bwd)
    return op(a, x)


def _wide_rows(t):
    return _pick(t, 544, 16)


def _mod_part(blk, rows, mod, bm, lo, hi):
    is_ctx = blk * rows + lax.broadcasted_iota(jnp.int32, (rows, 1), 0) < RB
    r = mod[:, lo:hi] + bm[:, lo:hi]
    return jnp.where(is_ctx, r[1:2], r[0:1])


def _f_silu(blk, p, x):
    return [_silu(x[0]).astype(bf16)]


def _f_normmod(blk, p, x):
    nw, mod, bm = p
    rows, d = x[0].shape
    shift = _mod_part(blk, rows, mod, bm, 0, d)
    scale = _mod_part(blk, rows, mod, bm, d, 2 * d)
    return [(_rms(x[0], nw) * (1.0 + scale) + shift).astype(bf16)]


def _f_resid(blk, p, x):
    mod, bm = p
    rows, d = x[0].shape
    return [x[0] + _mod_part(blk, rows, mod, bm, 2 * d, 3 * d) * x[1]]


def _f_mix(blk, p, x):
    bg, = p
    gp = x[0]
    d = x[1].shape[1]
    acc = None
    for k in range(4):
        t = jax.nn.sigmoid(gp[:, k * d:(k + 1) * d] + bg[:, k * d:(k + 1) * d]) * x[1 + k]
        acc = t if acc is None else acc + t
    return [acc.astype(bf16)]


def _tri(rev):
    row = lax.broadcasted_iota(jnp.int32, (CHUNK, CHUNK), 0)
    col = lax.broadcasted_iota(jnp.int32, (CHUNK, CHUNK), 1)
    return (col >= row) if rev else (col <= row)


def _chunk_ids(rev):
    ids = list(range(RB // CHUNK))
    return ids[::-1] if rev else ids


def _f_hg(rev):
    def f(blk, p, c, x):
        lb, = p
        st, = c
        qi, fr = x
        q = _silu(qi[:, :BR_W])
        v = qi[:, BR_W:]
        fg = lb + (1.0 - lb) * jax.nn.sigmoid(fr)
        logf = jnp.log(fg)
        k = 1.0 - fg
        m = _tri(rev)
        mf = m.astype(f32)
        outs = [None] * (RB // CHUNK)
        for ci in _chunk_ids(rev):
            sl = slice(CHUNK * ci, CHUNK * ci + CHUNK)
            lf = logf[sl]
            b = _dot(mf, lf, hi=True)
            bend = jnp.sum(lf, axis=0, keepdims=True)
            mid = 0.5 * bend
            qe = q[sl] * jnp.exp(b - mid)
            ke = k[sl] * jnp.exp(mid - b)
            kd = k[sl] * jnp.exp(bend - b)
            qb = q[sl] * jnp.exp(b)
            dec = jnp.exp(bend)
            vc = v[sl]
            oh, ns = [], []
            for hh in range(HG_HEADS):
                cs = slice(HG_DK * hh, HG_DK * hh + HG_DK)
                sth = st[cs]
                att = jnp.where(m, _dot(qe[:, cs], ke[:, cs], _NT), 0.0)
                oh.append(_dot(att, vc[:, cs]) + _dot(qb[:, cs], sth, _NT))
                ns.append(sth * dec[:, cs] + _dot(vc[:, cs], kd[:, cs], _TN))
            st = jnp.concatenate(ns, axis=0)
            outs[ci] = jnp.concatenate(oh, axis=1)
        return [st], [jnp.concatenate(outs, axis=0)]
    return f


def _both(f0, f1, n_p, n_x):
    def f(blk, p, c, x):
        c0, y0 = f0(blk, p[:n_p], c[:1], x[:n_x])
        c1, y1 = f1(blk, p[n_p:], c[1:], x[n_x:])
        return c0 + c1, y0 + y1
    return f


_BOTH_ORDERS = (["asc", "asc", "d1", "d1"], ["asc", "d1"])


def _f_hg_final(blk, p, x):
    nw, = p
    o = x[0] + x[1]
    parts = []
    for hh in range(HG_HEADS):
        cs = slice(HG_DK * hh, HG_DK * hh + HG_DK)
        parts.append(_rms(o[:, cs], nw[:, cs]))
    return [(jnp.concatenate(parts, axis=1) * _silu(x[2])).astype(bf16)]


def _conv(x, cw, cb, blk):
    rows = x.shape[0]
    r = lax.broadcasted_iota(jnp.int32, (rows, 1), 0)
    rm = jnp.where(blk == 0, r, r % CHUNK)
    seg = jnp.where(blk == 0, rows, CHUNK)

    def vmask(o):
        return ((rm + o >= 0) & (rm + o < seg)).astype(f32)

    def shifted(o):
        @jax.custom_vjp
        def sh(x, mo, mn):
            return pltpu.roll(x, (-o) % rows, 0) * mo

        def fwd(x, mo, mn):
            return sh(x, mo, mn), (mo, mn)

        def bwd(res, g):
            mo, mn = res
            return pltpu.roll(g, o % rows, 0) * mn, jnp.zeros_like(mo), jnp.zeros_like(mn)
        sh.defvjp(fwd, bwd)
        return sh(x, vmask(o), vmask(-o))

    lo = (CONV_W - 1) // 2
    out = cb
    for k in range(CONV_W):
        o = k - lo
        out = out + cw[k:k + 1] * (x if o == 0 else shifted(o))
    return out


def _f_lru_a(blk, p, x):
    cw, cb, wg, gb, lam = p
    xc = _conv(x[0], cw, cb, blk)
    n_chunks = BR_W // _LANE
    xk = [xc[:, _LANE * k:_LANE * (k + 1)] for k in range(n_chunks)]

    def gate(j):
        pre = jnp.concatenate([_dot(xk[k], wg[j * n_chunks + k], hi=_MAP_PREC) for k in range(n_chunks)], axis=1)
        return jax.nn.sigmoid(pre + gb[:, BR_W * j:BR_W * (j + 1)])

    outs = []
    for d in range(2):
        r = gate(2 * d)
        ig = gate(2 * d + 1)
        log_a = -LRU_C * r * _softplus(-lam[d:d + 1])
        outs.append(jnp.exp(log_a))
        outs.append(jnp.sqrt(_one_minus_exp(2.0 * log_a)) * (ig * xc))
    return outs


def _f_lru_c(blk, p, x):
    return [((x[0] + x[1]) * _silu(x[2])).astype(bf16)]


def _f_s5_c1(blk, p, x):
    dsk, = p
    return [jax.nn.gelu(x[0] + dsk * x[1])]


def _f_s5_c2(blk, p, x):
    bglu, = p
    return [(x[0] * jax.nn.sigmoid(x[1] + bglu) * _silu(x[2])).astype(bf16)]


def _f_m2_a(blk, p, x):
    cw, cb, dtb = p
    return [_silu(_conv(x[0], cw, cb, blk)), _softplus(x[1] + dtb)]


def _f_ssd(d):
    rev = d == 1
    hpg = M2_HEADS // M2_GROUPS

    def f(blk, p, c, x):
        alog, = p
        st, = c
        xbc, dtp = x
        a = -jnp.exp(alog[:, M2_HEADS * d:M2_HEADS * (d + 1)])
        dt = dtp[:, M2_HEADS * d:M2_HEADS * (d + 1)]
        xs = xbc[:, :BR_W]
        bm = xbc[:, BR_W:BR_W + M2_GROUPS * M2_STATE]
        cm = xbc[:, BR_W + M2_GROUPS * M2_STATE:]
        gw = hpg * M2_HEADDIM
        mf = _tri(rev).astype(f32)
        row = lax.broadcasted_iota(jnp.int32, (CHUNK, gw), 0)
        col = lax.broadcasted_iota(jnp.int32, (CHUNK, gw), 1)
        m4 = (col % CHUNK >= row) if rev else (col % CHUNK <= row)
        spread = (lax.broadcasted_iota(jnp.int32, (hpg, gw), 0)
                  == lax.div(lax.broadcasted_iota(jnp.int32, (hpg, gw), 1), M2_HEADDIM)).astype(f32)
        own = [lax.div(lax.broadcasted_iota(jnp.int32, (1, gw), 1), M2_HEADDIM) == r for r in range(hpg)]
        outs = [None] * (RB // CHUNK)
        for ci in _chunk_ids(rev):
            sl = slice(CHUNK * ci, CHUNK * ci + CHUNK)
            dtc = dt[sl]
            dta = dtc * a
            cum = _dot(mf, dta, hi=True)
            cum_t = cum.T
            dt_t = dtc.T
            ys, ns = [], []
            for g in range(M2_GROUPS):
                hs = slice(hpg * g, hpg * (g + 1))
                bmg = bm[sl, M2_STATE * g:M2_STATE * (g + 1)]
                cmg = cm[sl, M2_STATE * g:M2_STATE * (g + 1)]
                xg = xs[sl, gw * g:gw * (g + 1)]
                stg = st[M2_STATE * g:M2_STATE * (g + 1)]
                cum_i = _dot(cum[:, hs], spread, hi=True)
                cum_j = jnp.concatenate([cum_t[hpg * g + r:hpg * g + r + 1] for r in range(hpg)], axis=1)
                dt_j = jnp.concatenate([dt_t[hpg * g + r:hpg * g + r + 1] for r in range(hpg)], axis=1)
                dt_i = _dot(dtc[:, hs], spread, hi=True)
                cend_g = jnp.sum(_dot(dta[:, hs], spread, hi=True), axis=0, keepdims=True)
                decay = jnp.exp(jnp.where(m4, cum_i - cum_j, -1e30))
                scores = _dot(cmg, jnp.concatenate([bmg] * hpg, axis=0), _NT)
                w = scores * decay * dt_j
                xdiag = jnp.concatenate([jnp.where(own[r], xg, 0.0) for r in range(hpg)], axis=0)
                ys.append(_dot(w, xdiag) + _dot(cmg, stg) * jnp.exp(cum_i))
                wx = jnp.exp(cend_g - cum_i) * dt_i * xg
                ns.append(jnp.exp(cend_g) * stg + _dot(bmg, wx, _TN))
            st = jnp.concatenate(ns, axis=0)
            outs[ci] = jnp.concatenate(ys, axis=1)
        return [st], [jnp.concatenate(outs, axis=0)]
    return f


def _f_m2_c(blk, p, x):
    dsk, nw = p
    y = x[0] + x[1] + dsk * x[2][:, :BR_W]
    return [_rms(y * _silu(x[3]), nw).astype(bf16)]


def _f_loss(blk, p, x):
    fnw, = p
    err = _rms(x[0], fnw) - x[1]
    return [0.5 * jnp.mean(err * err, axis=-1, keepdims=True)]


def _blockdiag(w):
    g, a, b = w.shape
    return jnp.einsum("gab,gh->gahb", w, jnp.eye(g, dtype=w.dtype)).reshape(g * a, g * b)


def _s5_params(l, w):
    a_scan, cds = [], []
    per = _LANE // S5_GROUP

    def chunks(m):
        return jnp.stack([_blockdiag(m[k * per:(k + 1) * per]) for k in range(S5_GROUPS // per)])

    b_re = jnp.transpose(w["s5_b_re"][l], (0, 2, 1))
    b_im = jnp.transpose(w["s5_b_im"][l], (0, 2, 1))
    bd = [chunks(b_re), chunks(b_im)]
    c_re = jnp.transpose(w["s5_c_re"][l], (0, 2, 1))
    c_im = jnp.transpose(w["s5_c_im"][l], (0, 2, 1))
    for d in range(2):
        lam_re = w["s5_a_re"][l, d]
        lam_im = w["s5_a_im"][l, d]
        step = jnp.exp(w["s5_log_step"][l, d])[:, None]
        mag = jnp.exp(lam_re * step)
        ab_re = mag * jnp.cos(lam_im * step)
        ab_im = mag * jnp.sin(lam_im * step)
        den = lam_re * lam_re + lam_im * lam_im
        nr = ab_re - 1.0
        co_re = (nr * lam_re + ab_im * lam_im) / den
        co_im = (ab_im * lam_re - nr * lam_im) / den
        n_state = S5_GROUPS * S5_STATE
        a_scan.append(jnp.stack([ab_re.reshape(_SUB, n_state // _SUB), ab_im.reshape(_SUB, n_state // _SUB)]))
        cp_re = c_re * co_re[:, :, None] - c_im * co_im[:, :, None]
        cp_im = c_re * co_im[:, :, None] + c_im * co_re[:, :, None]
        cds.append([chunks(cp_re), -chunks(cp_im)])
    return a_scan, bd, cds


def _lru_gate(l, w):
    gw = w["lru_gate_w"][l]
    per = _LANE // (BR_W // LRU_BLOCKS)
    chunks = [_blockdiag(gw[d, g, k * per:(k + 1) * per])
              for d in range(2) for g in range(2) for k in range(LRU_BLOCKS // per)]
    return jnp.stack(chunks), w["lru_gate_b"][l].reshape(1, -1)


def _pad_cols(a, n):
    return jnp.pad(a, ((0, 0), (0, n - a.shape[1])))


IN_SIZES = (BR_W,) * 9 + (M2_XBC, 2 * M2_HEADS, BR_W)
IN_OFFS = tuple(sum(IN_SIZES[:i]) for i in range(len(IN_SIZES) + 1))
IN_GROUPS = (("hg_qi", 0, 2, 1024), ("hg_ff", 2, 1, 512), ("hg_fb", 3, 1, 512), ("hg_z", 4, 1, 512),
             ("s5_u", 5, 1, 512), ("s5_z", 6, 1, 512), ("lru_x", 7, 1, 512), ("lru_z", 8, 1, 512),
             ("m2_xbc", 9, 1, 768), ("m2_dt", 10, 1, 128), ("m2_z", 11, 1, 512))


def _new_slots(big):
    slots = {n: [jnp.zeros(w.shape, f32) for w in ws] for n, ws in big.items() if n not in ("w_in", "w_gate", "w_branch")}
    n_layers, d_model = len(big["w_in"]), big["w_in"][0].shape[0]
    slots["w_in"] = [{name: jnp.zeros((d_model, width), f32) for name, _, _, width in IN_GROUPS} for _ in range(n_layers)]
    slots["w_gate"] = [jnp.zeros((d_model, 4 * d_model), f32) for _ in range(n_layers)]
    slots["w_branch"] = [[jnp.zeros(w.shape[1:], f32) for _ in range(w.shape[0])] for w in big["w_branch"]]
    return slots


def _slot_grads(g):
    out = {n: jnp.stack(v) for n, v in g.items() if n not in ("w_in", "w_gate", "w_branch")}
    out["w_branch"] = jnp.stack([jnp.stack(gl) for gl in g["w_branch"]])
    out["w_in"] = jnp.stack([
        jnp.concatenate([gl[name][:, :IN_OFFS[s0 + ns] - IN_OFFS[s0]] for name, s0, ns, _ in IN_GROUPS], axis=1)
        for gl in g["w_in"]])
    d_model = g["w_gate"][0].shape[0]
    out["w_gate"] = jnp.stack([jnp.transpose(gl.reshape(d_model, 4, d_model), (1, 0, 2)) for gl in g["w_gate"]])
    return out


def _forward(p, big, slots, x, ctx, c, target):
    n_layers = p["norm_w"].shape[0]
    d_model = x.shape[-1]
    xa = jnp.concatenate([ctx, x], axis=0)
    t = xa.shape[0]
    cc = jnp.concatenate([c, p["c_ctx"][None], jnp.zeros((_SUB - 2, d_model), f32)], axis=0)
    lb_all = jnp.cumsum(jax.nn.softmax(p["hg_lb_logits"], axis=0), axis=0)
    scc, = blocked_op("silu_c", _f_silu, [], [cc], [(d_model, bf16)], rb=_SUB)
    wide = _wide_rows(t)

    for l in range(n_layers):
        tag = "l%d_" % l
        mod = mm(tag + "mod", scc, big["w_mod"][l], slots["w_mod"][l])
        bm = p["b_mod"][l][None]
        h, = blocked_op(tag + "normmod", _f_normmod, [p["norm_w"][l][None], mod, bm], [xa], [(d_model, bf16)], rb=wide)
        gnames = [g[0] for g in IN_GROUPS]
        wvs = [_pad_cols(big["w_in"][l][:, IN_OFFS[s0]:IN_OFFS[s0 + ns]], width) for _, s0, ns, width in IN_GROUPS]
        u = dict(zip(gnames, multi_mm(tag + "in_", gnames, h, wvs, [slots["w_in"][l][g] for g in gnames])))

        o_dirs = blocked_op(tag + "hg", _both(_f_hg(False), _f_hg(True), 1, 2), [lb_all[l, 0][None], lb_all[l, 1][None]],
                            [u["hg_qi"], u["hg_ff"], u["hg_qi"], u["hg_fb"]], [(BR_W, f32)] * 2,
                            order=_BOTH_ORDERS, carry_sds=[(BR_W, HG_DK)] * 2)
        y_hg, = blocked_op(tag + "hg_fin", _f_hg_final, [p["hg_norm"][l][None]], list(o_dirs) + [u["hg_z"]], [(BR_W, bf16)],
                           rb=wide)

        a_scan, bd, cds = _s5_params(l, p)
        n_state = S5_GROUPS * S5_STATE
        planes = [s.reshape(t, n_state) for s in s5_states(tag, u["s5_u"], bd, a_scan[0], a_scan[1])]
        ysum, = bd_mm(tag + "s5_c", planes, [[cds[d][part]] for d in range(2) for part in range(2)])
        g5, = blocked_op(tag + "s5_c1", _f_s5_c1, [p["s5_d"][l][None]], [ysum, u["s5_u"]], [(BR_W, f32)], rb=wide)
        gl = mm(tag + "s5_glu", g5, big["s5_w_glu"][l], slots["s5_w_glu"][l])
        y_s5, = blocked_op(tag + "s5_c2", _f_s5_c2, [p["s5_b_glu"][l][None]], [g5, gl, u["s5_z"]], [(BR_W, bf16)],
                           rb=wide)

        wg, gb = _lru_gate(l, p)
        ab = blocked_op(tag + "lru_a", _f_lru_a,
                        [p["lru_conv_w"][l], p["lru_conv_b"][l][None], wg, gb, p["lru_lam"][l]],
                        [u["lru_x"]], [(BR_W, f32)] * 4)
        hs = []
        for d in range(2):
            hs.append(rscan(tag + "lru_scan%d" % d, d, ab[2 * d], ab[2 * d + 1]))
        y_lru, = blocked_op(tag + "lru_c", _f_lru_c, [], hs + [u["lru_z"]], [(BR_W, bf16)], rb=wide)

        dtb = _pad_cols(p["m2_dt_bias"][l].reshape(1, -1), _LANE)
        xbc, dtp = blocked_op(tag + "m2_a", _f_m2_a, [p["m2_conv_w"][l], p["m2_conv_b"][l][None], dtb],
                              [u["m2_xbc"], u["m2_dt"]], [(M2_XBC, f32), (_LANE, f32)])
        alog = _pad_cols(p["m2_a_log"][l].reshape(1, -1), _LANE)
        y_dirs = blocked_op(tag + "ssd", _both(_f_ssd(0), _f_ssd(1), 1, 2), [alog, alog], [xbc, dtp, xbc, dtp],
                            [(BR_W, f32)] * 2, order=_BOTH_ORDERS,
                            carry_sds=[(M2_GROUPS * M2_STATE, BR_W // M2_GROUPS)] * 2)
        dsk = jnp.repeat(p["m2_d"][l], M2_HEADDIM)[None]
        y_m2, = blocked_op(tag + "m2_c", _f_m2_c, [dsk, p["m2_norm"][l][None]], list(y_dirs) + [xbc, u["m2_z"]], [(BR_W, bf16)],
                           rb=wide)

        wg_all = jnp.transpose(big["w_gate"][l], (1, 0, 2)).reshape(d_model, 4 * d_model)
        gp = mm(tag + "gate", h, wg_all, slots["w_gate"][l], out_dtype=bf16)
        bs = [mm(tag + "br%d" % k, yk, big["w_branch"][l][k], slots["w_branch"][l][k], out_dtype=bf16)
              for k, yk in enumerate((y_hg, y_s5, y_lru, y_m2))]
        mix, = blocked_op(tag + "mix", _f_mix, [p["b_gate"][l].reshape(1, -1)], [gp] + bs, [(d_model, bf16)])
        o = mm(tag + "out", mix, big["w_out"][l], slots["w_out"][l])
        xa, = blocked_op(tag + "resid", _f_resid, [mod, bm], [xa, o], [(d_model, f32)], rb=wide)

    rl, = blocked_op("loss", _f_loss, [p["final_norm"][None]], [xa[ctx.shape[0]:], target], [(1, f32)],
                     rb=_wide_rows(target.shape[0]))
    return jnp.sum(rl)


_MESH = pl.DeviceIdType.MESH
_ANY = pl.BlockSpec(memory_space=pl.ANY)
W_PACK = 1024


def _place():
    x, y, c = lax.axis_index("x"), lax.axis_index("y"), lax.axis_index("c")
    chips = [(x, 1 - y), (1 - x, y), (1 - x, 1 - y)]
    return x, y, c, chips


def _rcopy(src, dst, ssem, rsem, k, to):
    return pltpu.make_async_remote_copy(src_ref=src, dst_ref=dst, send_sem=ssem.at[k], recv_sem=rsem.at[k],
                                        device_id=to, device_id_type=_MESH)


def gather_shards(xs):
    n = len(xs)

    def body(*refs):
        x_refs, o_refs = refs[:n], refs[n:2 * n]
        ssem, rsem, lsem = refs[2 * n:]
        x, y, c, chips = _place()
        j = 2 * x + y
        sib = (x, y, 1 - c)
        mine = [pltpu.make_async_copy(x_refs[a], o_refs[a].at[j], lsem.at[a]) for a in range(n)]
        for cp in mine:
            cp.start()
        first = [_rcopy(x_refs[a].at[c], o_refs[a].at[j, c], ssem, rsem, 6 * a + r, (*chips[r], c))
                 for r in range(3) for a in range(n)]
        for cp in first:
            cp.start()
        passed = []
        for r in range(3):
            jr = j ^ (r + 1)
            for a in range(n):
                _rcopy(x_refs[a].at[c], o_refs[a].at[jr, c], ssem, rsem, 6 * a + r, sib).wait_recv()
                cp = _rcopy(o_refs[a].at[jr, c], o_refs[a].at[jr, c], ssem, rsem, 6 * a + 3 + r, sib)
                cp.start()
                passed.append(cp)
        for r in range(3):
            jr = j ^ (r + 1)
            for a in range(n):
                _rcopy(x_refs[a].at[c], o_refs[a].at[jr, 1 - c], ssem, rsem, 6 * a + 3 + r, sib).wait_recv()
        for cp in first + passed:
            cp.wait_send()
        for cp in mine:
            cp.wait()

    return pl.pallas_call(
        body, name="gather_shards", out_shape=[jax.ShapeDtypeStruct((4,) + x.shape, x.dtype) for x in xs],
        in_specs=[_ANY] * n, out_specs=[_ANY] * n,
        scratch_shapes=[pltpu.SemaphoreType.DMA((6 * n,)), pltpu.SemaphoreType.DMA((6 * n,)), pltpu.SemaphoreType.DMA((n,))],
    )(*xs)


def sibling_halves(gs):
    n = len(gs)

    def body(*refs):
        g_refs, o_refs = refs[:n], refs[n:2 * n]
        ssem, rsem = refs[2 * n:]
        x, y, c, _ = _place()
        sib = (x, y, 1 - c)
        cps = [_rcopy(g_refs[a].at[k, 1 - c], o_refs[a].at[k], ssem, rsem, 4 * a + k, sib)
               for k in range(4) for a in range(n)]
        for cp in cps:
            cp.start()
        for cp in cps:
            cp.wait()

    return pl.pallas_call(
        body, name="sibling_halves", out_shape=[jax.ShapeDtypeStruct((4,) + g.shape[2:], g.dtype) for g in gs],
        in_specs=[_ANY] * n, out_specs=[_ANY] * n,
        scratch_shapes=[pltpu.SemaphoreType.DMA((4 * n,)), pltpu.SemaphoreType.DMA((4 * n,))],
    )(*gs)


def scatter_chips(ps):
    n = len(ps)

    def body(*refs):
        p_refs, o_refs = refs[:n], refs[n:2 * n]
        ssem, rsem = refs[2 * n:]
        x, y, c, chips = _place()
        j = 2 * x + y
        cps = [_rcopy(p_refs[a].at[j ^ (r + 1)], o_refs[a].at[r], ssem, rsem, 3 * a + r, (*chips[r], c))
               for r in range(3) for a in range(n)]
        for cp in cps:
            cp.start()
        for cp in cps:
            cp.wait()

    return pl.pallas_call(
        body, name="scatter_chips", out_shape=[jax.ShapeDtypeStruct((3,) + p.shape[1:], p.dtype) for p in ps],
        in_specs=[_ANY] * n, out_specs=[_ANY] * n,
        scratch_shapes=[pltpu.SemaphoreType.DMA((3 * n,)), pltpu.SemaphoreType.DMA((3 * n,))],
    )(*ps)


def join_halves(qs):
    n = len(qs)

    def body(*refs):
        o_refs = refs[n:2 * n]
        ssem, rsem = refs[2 * n:]
        x, y, c, _ = _place()
        sib = (x, y, 1 - c)
        cps = [_rcopy(o_refs[a].at[c], o_refs[a].at[c], ssem, rsem, a, sib) for a in range(n)]
        for cp in cps:
            cp.start()
        for a in range(n):
            _rcopy(o_refs[a].at[c], o_refs[a].at[1 - c], ssem, rsem, a, sib).wait_recv()
        for cp in cps:
            cp.wait_send()

    return pl.pallas_call(
        body, name="join_halves", out_shape=[jax.ShapeDtypeStruct(q.shape, q.dtype) for q in qs],
        in_specs=[_ANY] * n, out_specs=[_ANY] * n, input_output_aliases={a: a for a in range(n)},
        scratch_shapes=[pltpu.SemaphoreType.DMA((n,)), pltpu.SemaphoreType.DMA((n,))],
    )(*qs)


def _rows_block(r):
    return _pick(r, 256, _SUB)


def add_sibling(tag, g, r1, place, out_dtype):
    _, _, rows, w = g.shape
    rb = _rows_block(rows)

    def body(pl_ref, g_ref, r_ref, o_ref):
        o_ref[...] = (g_ref[0] + r_ref[...]).astype(out_dtype)

    return pl.pallas_call(
        body, name="add_sibling_" + tag, out_shape=jax.ShapeDtypeStruct((4, rows, w), out_dtype),
        grid_spec=pltpu.PrefetchScalarGridSpec(
            num_scalar_prefetch=1, grid=(4, rows // rb),
            in_specs=[pl.BlockSpec((1, 1, rb, w), lambda k, i, s: (k, s[1], i, 0)),
                      pl.BlockSpec((1, rb, w), lambda k, i, s: (k, i, 0))],
            out_specs=pl.BlockSpec((1, rb, w), lambda k, i, s: (k, i, 0))),
        compiler_params=_cparams(2),
    )(place, g, r1)


def add_chips(tag, p, r2, place):
    _, rows, w = p.shape
    rb = _rows_block(rows)

    def body(pl_ref, p_ref, r_ref, o_ref):
        j = pl_ref[0]
        own = p_ref[0].astype(f32)
        others = [r_ref[0].astype(f32), r_ref[1].astype(f32), r_ref[2].astype(f32)]
        acc = None
        for k in range(4):
            rel = k ^ j
            t = jnp.where(rel == 0, own, jnp.where(rel == 1, others[0], jnp.where(rel == 2, others[1], others[2])))
            acc = t if acc is None else acc + t
        o_ref[0] = acc

    return pl.pallas_call(
        body, name="add_chips_" + tag, out_shape=jax.ShapeDtypeStruct((2, rows, w), f32),
        grid_spec=pltpu.PrefetchScalarGridSpec(
            num_scalar_prefetch=1, grid=(rows // rb,),
            in_specs=[pl.BlockSpec((1, rb, w), lambda i, s: (s[0], i, 0)),
                      pl.BlockSpec((3, rb, w), lambda i, s: (0, i, 0))],
            out_specs=pl.BlockSpec((1, rb, w), lambda i, s: (s[1], i, 0))),
        compiler_params=_cparams(1),
    )(place, p, r2)


def adamw(tag, g, w, m, v):
    rows, wd = g.shape
    rb = _rows_block(rows)

    def body(g_ref, w_ref, m_ref, v_ref, d_ref, nm_ref, nv_ref):
        gv = g_ref[...]
        nm = ADAM_B1 * m_ref[...] + (1.0 - ADAM_B1) * gv
        nv = ADAM_B2 * v_ref[...] + (1.0 - ADAM_B2) * (gv * gv)
        m_hat = nm / (1.0 - ADAM_B1 ** ADAM_STEP)
        v_hat = nv / (1.0 - ADAM_B2 ** ADAM_STEP)
        d_ref[...] = -ADAM_LR * (m_hat / (jnp.sqrt(v_hat) + ADAM_EPS) + ADAM_WD * w_ref[...])
        nm_ref[...] = nm
        nv_ref[...] = nv

    spec = pl.BlockSpec((rb, wd), lambda i: (i, 0))
    return pl.pallas_call(
        body, name="adamw_" + tag, grid=(rows // rb,), in_specs=[spec] * 4, out_specs=[spec] * 3,
        out_shape=[jax.ShapeDtypeStruct(g.shape, f32)] * 3, compiler_params=_cparams(1),
    )(g, w, m, v)


WEIGHTS = ("c_ctx", "norm_w", "w_mod", "b_mod", "w_in", "hg_lb_logits", "hg_norm", "s5_a_re", "s5_a_im", "s5_log_step",
           "s5_b_re", "s5_b_im", "s5_c_re", "s5_c_im", "s5_d", "s5_w_glu", "s5_b_glu", "lru_conv_w", "lru_conv_b",
           "lru_gate_w", "lru_gate_b", "lru_lam", "m2_conv_w", "m2_conv_b", "m2_dt_bias", "m2_a_log", "m2_d", "m2_norm",
           "w_branch", "w_gate", "b_gate", "w_out", "final_norm")
SHARD_AXIS = {"w_mod": 2, "w_in": 2, "hg_lb_logits": 2, "s5_w_glu": 1, "lru_conv_w": 2, "lru_lam": 2, "m2_conv_w": 2,
              "w_branch": 3, "w_gate": 2, "b_gate": 2, "w_out": 1}
BIG = ("w_mod", "w_in", "s5_w_glu", "w_branch", "w_gate", "w_out")
N_CHIPS = 4


def _to_rows(flat, row_unit):
    n = flat.shape[-1]
    per = 2 * row_unit * W_PACK
    total = -(-n // per) * per
    flat = jnp.pad(flat, [(0, 0)] * (flat.ndim - 1) + [(0, total - n)])
    return flat.reshape(flat.shape[:-1] + (2, total // (2 * W_PACK), W_PACK))


SMALL_SHARDED = tuple(n for n in WEIGHTS if n in SHARD_AXIS and n not in BIG)
SMALL_REPLICATED = tuple(n for n in WEIGHTS if n not in SHARD_AXIS)


def _chip_slices(a, axis):
    width = a.shape[axis] // N_CHIPS
    return jnp.stack([lax.slice_in_dim(a, k * width, (k + 1) * width, axis=axis) for k in range(N_CHIPS)])


def _gather_weights(local):
    small = jnp.concatenate([lax.bitcast_convert_type(local[n], bf16).reshape(-1) for n in SMALL_SHARDED])
    got = gather_shards([local[n].astype(bf16) for n in BIG] + [_to_rows(small, 16)])
    full = {}
    for n, g in zip(BIG, got):
        full[n] = [jnp.concatenate([g[j, l] for j in range(N_CHIPS)], axis=SHARD_AXIS[n] - 1) for l in range(g.shape[1])]
    flat, off = got[-1].reshape(N_CHIPS, -1), 0
    for n in SMALL_SHARDED:
        shp = local[n].shape
        size = 2 * math.prod(shp)
        part = lax.bitcast_convert_type(flat[:, off:off + size].reshape((N_CHIPS,) + shp + (2,)), f32)
        off += size
        full[n] = jnp.concatenate([part[j] for j in range(N_CHIPS)], axis=SHARD_AXIS[n])
    return full


def _whole_rows(v):
    n = v.shape[-1]
    return jnp.pad(v, [(0, 0)] * (v.ndim - 1) + [(0, -n % W_PACK)])


def _pack_small(vals, extra):
    return jnp.concatenate([_whole_rows(vals[n].reshape(-1)) for n in SMALL_SHARDED + SMALL_REPLICATED]
                           + [_whole_rows(extra.reshape(1))])


def _pack_small_grads(grads, loss):
    rep = [grads[n].reshape(-1) for n in SMALL_REPLICATED] + [loss.reshape(1)]
    sh = [_chip_slices(grads[n], SHARD_AXIS[n]).reshape(N_CHIPS, -1) for n in SMALL_SHARDED]
    return jnp.concatenate([_whole_rows(a) for a in sh]
                           + [_whole_rows(jnp.broadcast_to(r, (N_CHIPS,) + r.shape)) for r in rep], axis=1)


def _unpack_small(rows, like):
    out, r0 = {}, 0
    for n in SMALL_SHARDED + SMALL_REPLICATED:
        size = math.prod(like[n].shape)
        nr = -(-size // W_PACK)
        piece = lax.optimization_barrier(rows[r0:r0 + nr])
        out[n] = piece.reshape(-1)[:size].reshape(like[n].shape)
        r0 += nr
    return out, lax.optimization_barrier(rows[r0:r0 + 1])[0, 0]


def _reduce_grads(tags, gs):
    place = jnp.stack([2 * lax.axis_index("x") + lax.axis_index("y"), lax.axis_index("c")]).astype(jnp.int32)
    pairs = [add_sibling(t, g, r, place, bf16 if t in BIG else f32) for t, g, r in zip(tags, gs, sibling_halves(gs))]
    quads = [add_chips(t, p, r, place) for t, p, r in zip(tags, pairs, scatter_chips(pairs))]
    return join_halves(quads)


def kernel(x, c, ctx, c_ctx, norm_w, w_mod, b_mod, w_in, hg_lb_logits, hg_norm, s5_a_re, s5_a_im, s5_log_step, s5_b_re, s5_b_im, s5_c_re, s5_c_im, s5_d, s5_w_glu, s5_b_glu, lru_conv_w, lru_conv_b, lru_gate_w, lru_gate_b, lru_lam, m2_conv_w, m2_conv_b, m2_dt_bias, m2_a_log, m2_d, m2_norm, w_branch, w_gate, b_gate, w_out, final_norm, loss_target, m_c_ctx, m_norm_w, m_w_mod, m_b_mod, m_w_in, m_hg_lb_logits, m_hg_norm, m_s5_a_re, m_s5_a_im, m_s5_log_step, m_s5_b_re, m_s5_b_im, m_s5_c_re, m_s5_c_im, m_s5_d, m_s5_w_glu, m_s5_b_glu, m_lru_conv_w, m_lru_conv_b, m_lru_gate_w, m_lru_gate_b, m_lru_lam, m_m2_conv_w, m_m2_conv_b, m_m2_dt_bias, m_m2_a_log, m_m2_d, m_m2_norm, m_w_branch, m_w_gate, m_b_gate, m_w_out, m_final_norm, v_c_ctx, v_norm_w, v_w_mod, v_b_mod, v_w_in, v_hg_lb_logits, v_hg_norm, v_s5_a_re, v_s5_a_im, v_s5_log_step, v_s5_b_re, v_s5_b_im, v_s5_c_re, v_s5_c_im, v_s5_d, v_s5_w_glu, v_s5_b_glu, v_lru_conv_w, v_lru_conv_b, v_lru_gate_w, v_lru_gate_b, v_lru_lam, v_m2_conv_w, v_m2_conv_b, v_m2_dt_bias, v_m2_a_log, v_m2_d, v_m2_norm, v_w_branch, v_w_gate, v_b_gate, v_w_out, v_final_norm):
    given = dict(locals())
    w_loc = {n: given[n] for n in WEIGHTS}
    m_loc = {n: given["m_" + n] for n in WEIGHTS}
    v_loc = {n: given["v_" + n] for n in WEIGHTS}

    full = _gather_weights(w_loc)
    params = {n: (full[n] if n in SHARD_AXIS else w_loc[n]) for n in WEIGHTS if n not in BIG}
    big = {n: full[n] for n in BIG}
    def loss_fn(p, s, xx):
        return _forward(p, big, s, xx, ctx[0], c, loss_target[0])

    loss, (g_p, g_s, g_x) = jax.value_and_grad(loss_fn, argnums=(0, 1, 2))(params, _new_slots(big), x[0])
    grads = {**g_p, **_slot_grads(g_s)}

    def rows4(a):
        return a.reshape(a.shape[:2] + (-1, a.shape[-1]))

    g_big = [rows4(_chip_slices(grads[n], SHARD_AXIS[n])) for n in BIG]
    g_small = _to_rows(_pack_small_grads(grads, loss), 64)
    summed = _reduce_grads(list(BIG) + ["small"], g_big + [g_small])

    g_out, d_out, m_out, v_out = {}, {}, {}, {}
    for n, g in zip(BIG, summed):
        shp = w_loc[n].shape
        flat2 = lambda a: a.reshape(-1, shp[-1])
        g_out[n] = g.reshape(shp)
        d, nm, nv = adamw(n, flat2(g), flat2(w_loc[n]), flat2(m_loc[n]), flat2(v_loc[n]))
        d_out[n], m_out[n], v_out[n] = d.reshape(shp), nm.reshape(shp), nv.reshape(shp)
    zero = jnp.zeros((), f32)
    flat = lambda vals: _to_rows(_pack_small(vals, zero), 64).reshape(-1, W_PACK)
    gs = summed[-1].reshape(-1, W_PACK)
    d, nm, nv = adamw("small", gs, flat(w_loc), flat(m_loc), flat(v_loc))
    gsm, loss_out = _unpack_small(gs, w_loc)
    g_out.update(gsm)
    d_out.update(_unpack_small(d, w_loc)[0])
    m_out.update(_unpack_small(nm, w_loc)[0])
    v_out.update(_unpack_small(nv, w_loc)[0])
    outs = [loss_out, g_x[None]]
    for group in (g_out, d_out, m_out, v_out):
        outs += [group[n] for n in WEIGHTS]
    return tuple(outs)
```

```python
import functools
import math

import jax
import jax.numpy as jnp
from jax import lax
from jax.experimental import pallas as pl
from jax.experimental.pallas import tpu as pltpu

f32 = jnp.float32
bf16 = jnp.bfloat16
_MM_DTYPE = bf16
_HI = lax.Precision.HIGHEST
_MAP_PREC = lax.Precision.HIGH
_VMEM_LIMIT = 56 * 1024 * 1024
_LANE = 128
_SUB = 8

EPS = 1e-6
CONV_W = 4
CHUNK = 64
RB = 256
BR_W = 512
HG_HEADS = 4
HG_DK = 128
S5_GROUPS = 32
S5_GROUP = 16
S5_STATE = 64
LRU_BLOCKS = 8
LRU_C = 8.0
M2_HEADS = 8
M2_HEADDIM = 64
M2_GROUPS = 2
M2_STATE = 64
M2_XBC = BR_W + 2 * M2_GROUPS * M2_STATE
ADAM_LR = 0.001
ADAM_B1 = 0.9
ADAM_B2 = 0.999
ADAM_EPS = 1e-08
ADAM_WD = 0.01
ADAM_STEP = 10

_NN = (((1,), (0,)), ((), ()))
_NT = (((1,), (1,)), ((), ()))
_TN = (((0,), (0,)), ((), ()))


def _silu(x):
    return x * jax.nn.sigmoid(x)


def _softplus(x):
    return jnp.maximum(x, 0.0) + jnp.log1p(jnp.exp(-jnp.abs(x)))


def _one_minus_exp(z):
    series = -z * (1.0 + z * 0.5 * (1.0 + z * (1.0 / 3.0) * (1.0 + z * 0.25 * (1.0 + z * 0.2))))
    return jnp.where(z > -0.05, series, 1.0 - jnp.exp(z))


def _rms(x, w):
    return x * lax.rsqrt(jnp.mean(x * x, axis=-1, keepdims=True) + EPS) * w


def _dot(a, b, dn=_NN, hi=False):
    prec = hi if isinstance(hi, lax.Precision) else (_HI if hi else None)
    return lax.dot_general(a, b, dn, precision=prec, preferred_element_type=f32)


def _cparams(n_grid):
    return pltpu.CompilerParams(dimension_semantics=("arbitrary",) * n_grid, vmem_limit_bytes=_VMEM_LIMIT)


_REV = {"asc": "desc", "d1": "d1r", "desc": "asc", "d1r": "d1"}


def _blk(order, i, n):
    if order == "asc":
        return i
    if order == "desc":
        return n - 1 - i
    if order == "d1":
        return jnp.where(i == 0, 0, n - i)
    return jnp.where(i == n - 1, 0, i + 1)


def _pick(n, cap, unit):
    if n <= cap:
        return n
    best = None
    d = unit
    while d <= cap:
        if n % d == 0:
            best = d
        d += unit
    return n if best is None else best


def _mm_call(name, a, b, mode, hi, out_dtype):
    if mode == "tn":
        k, m = a.shape
        n = b.shape[1]
        tm = _pick(m, 512, _LANE)
        tn = _pick(n, 512, _LANE)
        a_spec = pl.BlockSpec((k, tm), lambda i, j: (0, i))
        b_spec = pl.BlockSpec((k, tn), lambda i, j: (0, j))
    else:
        m, k = a.shape
        tm = _pick(m, max(256, min(1088, 4 * 1024 * 1024 // (k * a.dtype.itemsize))), _SUB)
        a_spec = pl.BlockSpec((tm, k), lambda i, j: (i, 0))
        if mode == "nn":
            n = b.shape[1]
            tn = _pick(n, max(_LANE, (8 * 1024 * 1024 // (k * b.dtype.itemsize)) // _LANE * _LANE), _LANE)
            b_spec = pl.BlockSpec((k, tn), lambda i, j: (0, j))
        else:
            n = b.shape[0]
            tn = _pick(n, max(_LANE, (8 * 1024 * 1024 // (k * b.dtype.itemsize)) // _LANE * _LANE), _LANE)
            b_spec = pl.BlockSpec((tn, k), lambda i, j: (j, 0))
    dn = {"nn": _NN, "nt": _NT, "tn": _TN}[mode]

    def body(a_ref, b_ref, o_ref):
        av = a_ref[...]
        bv = b_ref[...]
        if hi:
            av = av.astype(f32)
            bv = bv.astype(f32)
        else:
            av = av.astype(_MM_DTYPE)
            bv = bv.astype(_MM_DTYPE)
        o_ref[...] = _dot(av, bv, dn, hi).astype(o_ref.dtype)

    return pl.pallas_call(
        body, name=name, grid=(m // tm, n // tn), in_specs=[a_spec, b_spec],
        out_specs=pl.BlockSpec((tm, tn), lambda i, j: (i, j)),
        out_shape=jax.ShapeDtypeStruct((m, n), out_dtype), compiler_params=_cparams(2),
    )(a, b)


def mm(name, a, b, slot=None, hi=False, out_dtype=f32):
    @jax.custom_vjp
    def op(a, b, slot):
        return _mm_call(name, a, b, "nn", hi, out_dtype)

    def fwd(a, b, slot):
        return op(a, b, slot), (a, b)

    def bwd(res, g):
        a, b = res
        da = _mm_call(name + "_da", g, b, "nt", hi, a.dtype)
        db = _mm_call(name + "_db", a, g, "tn", hi, f32)
        if slot is None:
            return da, db.astype(b.dtype), None
        return da, jnp.zeros_like(b), db

    op.defvjp(fwd, bwd)
    return op(a, b, slot)


def _sum_nt_call(name, gs, ws, out_dtype):
    m = gs[0].shape[0]
    kdim = ws[0].shape[0]
    tm = _pick(m, 256, _SUB)
    n = len(gs)

    def body(*refs):
        acc = None
        for g_ref, w_ref in zip(refs[:n], refs[n:2 * n]):
            part = _dot(g_ref[...].astype(_MM_DTYPE), w_ref[...].astype(_MM_DTYPE), _NT)
            acc = part if acc is None else acc + part
        refs[2 * n][...] = acc.astype(out_dtype)

    in_specs = [pl.BlockSpec((tm, g.shape[1]), lambda i: (i, 0)) for g in gs]
    in_specs += [pl.BlockSpec(w.shape, lambda i: (0, 0)) for w in ws]
    return pl.pallas_call(
        body, name=name, grid=(m // tm,), in_specs=in_specs, out_specs=pl.BlockSpec((tm, kdim), lambda i: (i, 0)),
        out_shape=jax.ShapeDtypeStruct((m, kdim), out_dtype), compiler_params=_cparams(1),
    )(*gs, *ws)


def multi_mm(tag, names, a, ws, slots):
    @jax.custom_vjp
    def op(a, ws, slots):
        return tuple(_mm_call(tag + n, a, w, "nn", False, f32) for n, w in zip(names, ws))

    def fwd(a, ws, slots):
        return op(a, ws, slots), (a, ws)

    def bwd(res, gs):
        a, ws = res
        dws = [_mm_call(tag + n + "_db", a, g, "tn", False, f32) for n, g in zip(names, gs)]
        da = _sum_nt_call(tag + "da", list(gs), ws, a.dtype)
        return da, [jnp.zeros_like(w) for w in ws], dws

    op.defvjp(fwd, bwd)
    return op(a, list(ws), list(slots))


def _orders(order, n_x, n_o):
    if isinstance(order, str):
        return [order] * n_x, [order] * n_o
    return list(order[0]), list(order[1])


def _row(o, n):
    return lambda i: (_blk(o, i, n), 0)


def _blocked_fwd(name, f, order, rb, params, xs, out_sds, carry_sds):
    t = xs[0].shape[0]
    n = t // rb
    n_p, n_x, n_o, n_c = len(params), len(xs), len(out_sds), len(carry_sds)
    xo, oo = _orders(order, n_x, n_o)

    def body(*refs):
        p_refs = refs[:n_p]
        x_refs = refs[n_p:n_p + n_x]
        o_refs = refs[n_p + n_x:n_p + n_x + n_o]
        st_refs = refs[n_p + n_x + n_o:n_p + n_x + n_o + n_c]
        c_refs = refs[n_p + n_x + n_o + n_c:]
        i = pl.program_id(0)
        blk = _blk(xo[0], i, n)
        p = [r[...] for r in p_refs]
        x = [r[...] for r in x_refs]
        if n_c:
            @pl.when(i == 0)
            def _():
                for c in c_refs:
                    c[...] = jnp.zeros_like(c)
            c_in = [c[...] for c in c_refs]
            for sr, c in zip(st_refs, c_in):
                sr[0] = c
            c_out, ys = f(blk, p, c_in, x)
            for c, v in zip(c_refs, c_out):
                c[...] = v
        else:
            ys = f(blk, p, x)
        for o, y in zip(o_refs, ys):
            o[...] = y.astype(o.dtype)

    in_specs = [pl.BlockSpec(p.shape, lambda i, nd=p.ndim: (0,) * nd) for p in params]
    in_specs += [pl.BlockSpec((rb, x.shape[1]), _row(o, n)) for x, o in zip(xs, xo)]
    out_specs = [pl.BlockSpec((rb, c), _row(o, n)) for (c, _), o in zip(out_sds, oo)]
    out_specs += [pl.BlockSpec((1,) + s, lambda i: (i, 0, 0)) for s in carry_sds]
    out_shape = [jax.ShapeDtypeStruct((t, c), d) for c, d in out_sds]
    out_shape += [jax.ShapeDtypeStruct((n,) + s, f32) for s in carry_sds]
    res = pl.pallas_call(
        body, name=name, grid=(n,), in_specs=in_specs, out_specs=out_specs, out_shape=out_shape,
        scratch_shapes=[pltpu.VMEM(s, f32) for s in carry_sds], compiler_params=_cparams(1),
    )(*params, *xs)
    return list(res[:n_o]), list(res[n_o:])


def _blocked_bwd(name, f, order, rb, params, xs, states, dys, carry_sds):
    t = xs[0].shape[0]
    n = t // rb
    n_p, n_x, n_o, n_c = len(params), len(xs), len(dys), len(carry_sds)
    xo, oo = _orders(order, n_x, n_o)
    xo, oo = [_REV[o] for o in xo], [_REV[o] for o in oo]

    def body(*refs):
        k = 0
        p_refs = refs[k:k + n_p]; k += n_p
        x_refs = refs[k:k + n_x]; k += n_x
        st_refs = refs[k:k + n_c]; k += n_c
        dy_refs = refs[k:k + n_o]; k += n_o
        dp_refs = refs[k:k + n_p]; k += n_p
        dx_refs = refs[k:k + n_x]; k += n_x
        dc_refs = refs[k:]
        i = pl.program_id(0)
        blk = _blk(xo[0], i, n)
        p = [r[...] for r in p_refs]
        x = [r[...] for r in x_refs]
        dy = [r[...] for r in dy_refs]
        if n_c:
            @pl.when(i == 0)
            def _():
                for c in dc_refs:
                    c[...] = jnp.zeros_like(c)
            c_in = [r[0] for r in st_refs]
            dc = [c[...] for c in dc_refs]
            _, vjp = jax.vjp(lambda p_, c_, x_: f(blk, p_, c_, x_), p, c_in, x)
            dp, dcin, dx = vjp((dc, dy))
            for c, v in zip(dc_refs, dcin):
                c[...] = v
        else:
            _, vjp = jax.vjp(lambda p_, x_: f(blk, p_, x_), p, x)
            dp, dx = vjp(dy)

        @pl.when(i == 0)
        def _():
            for r, v in zip(dp_refs, dp):
                r[...] = v

        @pl.when(i > 0)
        def _():
            for r, v in zip(dp_refs, dp):
                r[...] += v
        for r, v in zip(dx_refs, dx):
            r[...] = v.astype(r.dtype)

    in_specs = [pl.BlockSpec(p.shape, lambda i, nd=p.ndim: (0,) * nd) for p in params]
    in_specs += [pl.BlockSpec((rb, x.shape[1]), _row(o, n)) for x, o in zip(xs, xo)]
    in_specs += [pl.BlockSpec((1,) + s, lambda i: (n - 1 - i, 0, 0)) for s in carry_sds]
    in_specs += [pl.BlockSpec((rb, d.shape[1]), _row(o, n)) for d, o in zip(dys, oo)]
    out_specs = [pl.BlockSpec(p.shape, lambda i, nd=p.ndim: (0,) * nd) for p in params]
    out_specs += [pl.BlockSpec((rb, x.shape[1]), _row(o, n)) for x, o in zip(xs, xo)]
    out_shape = [jax.ShapeDtypeStruct(p.shape, f32) for p in params]
    out_shape += [jax.ShapeDtypeStruct(x.shape, x.dtype) for x in xs]
    res = pl.pallas_call(
        body, name=name + "_bwd", grid=(n,), in_specs=in_specs, out_specs=out_specs, out_shape=out_shape,
        scratch_shapes=[pltpu.VMEM(s, f32) for s in carry_sds], compiler_params=_cparams(1),
    )(*params, *xs, *states, *dys)
    return list(res[:n_p]), list(res[n_p:])


def blocked_op(name, f, params, xs, out_sds, order="asc", carry_sds=(), rb=RB):
    carry_sds = tuple(carry_sds)

    @jax.custom_vjp
    def op(params, xs):
        return tuple(_blocked_fwd(name, f, order, rb, params, xs, out_sds, carry_sds)[0])

    def fwd(params, xs):
        ys, states = _blocked_fwd(name, f, order, rb, params, xs, out_sds, carry_sds)
        return tuple(ys), (params, xs, states)

    def bwd(res, dys):
        params, xs, states = res
        dp, dx = _blocked_bwd(name, f, order, rb, params, xs, states, list(dys), carry_sds)
        return list(dp), list(dx)

    op.defvjp(fwd, bwd)
    return op(list(params), list(xs))


def _cscan_call(name, order, asc, a, xr, xi, sr=None, si=None):
    t = xr.shape[0]
    n = t // RB
    tile = xr.shape[1:]
    xspec = pl.BlockSpec((RB,) + tile, lambda i: (_blk(order, i, n), 0, 0))
    aspec = pl.BlockSpec(a.shape, lambda i: (0, 0, 0))
    plane = jax.ShapeDtypeStruct(xr.shape, f32)

    def rowidx(tt):
        return tt if asc else RB - 1 - tt

    if sr is None:
        def body(a_ref, xr_ref, xi_ref, sr_ref, si_ref, c_ref):
            i = pl.program_id(0)

            @pl.when(i == 0)
            def _():
                c_ref[...] = jnp.zeros_like(c_ref)
            ar = a_ref[0]
            ai = a_ref[1]
            a2r = ar * ar - ai * ai
            a2i = 2.0 * ar * ai

            def step(tt, carry):
                cr, ci = carry
                r1 = rowidx(2 * tt)
                r2 = rowidx(2 * tt + 1)
                x1r, x1i, x2r, x2i = xr_ref[r1], xi_ref[r1], xr_ref[r2], xi_ref[r2]
                s1r = ar * cr - ai * ci + x1r
                s1i = ar * ci + ai * cr + x1i
                kr = ar * x1r - ai * x1i + x2r
                ki = ar * x1i + ai * x1r + x2i
                s2r = a2r * cr - a2i * ci + kr
                s2i = a2r * ci + a2i * cr + ki
                sr_ref[r1] = s1r
                si_ref[r1] = s1i
                sr_ref[r2] = s2r
                si_ref[r2] = s2i
                return s2r, s2i
            cr, ci = lax.fori_loop(0, RB // 2, step, (c_ref[0], c_ref[1]), unroll=4)
            c_ref[0] = cr
            c_ref[1] = ci

        return pl.pallas_call(
            body, name=name, grid=(n,), in_specs=[aspec, xspec, xspec], out_specs=[xspec, xspec],
            out_shape=[plane, plane], scratch_shapes=[pltpu.VMEM(a.shape, f32)], compiler_params=_cparams(1),
        )(a, xr, xi)

    def body(a_ref, xr_ref, xi_ref, sr_ref, si_ref, gr_ref, gi_ref, da_ref, c_ref):
        i = pl.program_id(0)

        @pl.when(i == 0)
        def _():
            c_ref[...] = jnp.zeros_like(c_ref)
            da_ref[...] = jnp.zeros_like(da_ref)
        ar = a_ref[0]
        ai = a_ref[1]
        a2r = ar * ar - ai * ai
        a2i = 2.0 * ar * ai

        def step(tt, carry):
            gr, gi, dar, dai, dbr, dbi = carry
            r1 = rowidx(2 * tt)
            r2 = rowidx(2 * tt + 1)
            x1r, x1i, x2r, x2i = xr_ref[r1], xi_ref[r1], xr_ref[r2], xi_ref[r2]
            v1r, v1i, v2r, v2i = sr_ref[r1], si_ref[r1], sr_ref[r2], si_ref[r2]
            g1r = x1r + ar * gr + ai * gi
            g1i = x1i + ar * gi - ai * gr
            kr = x2r + ar * x1r + ai * x1i
            ki = x2i + ar * x1i - ai * x1r
            g2r = kr + a2r * gr + a2i * gi
            g2i = ki + a2r * gi - a2i * gr
            dar = dar + gr * v1r + gi * v1i
            dai = dai + gi * v1r - gr * v1i
            dbr = dbr + g1r * v2r + g1i * v2i
            dbi = dbi + g1i * v2r - g1r * v2i
            gr_ref[r1] = g1r
            gi_ref[r1] = g1i
            gr_ref[r2] = g2r
            gi_ref[r2] = g2i
            return g2r, g2i, dar, dai, dbr, dbi
        z = jnp.zeros(tile, f32)
        gr, gi, dar, dai, dbr, dbi = lax.fori_loop(0, RB // 2, step, (c_ref[0], c_ref[1], z, z, z, z), unroll=4)
        c_ref[0] = gr
        c_ref[1] = gi
        da_ref[0] += dar + dbr
        da_ref[1] += dai + dbi

    return pl.pallas_call(
        body, name=name, grid=(n,), in_specs=[aspec] + [xspec] * 4, out_specs=[xspec, xspec, aspec],
        out_shape=[plane, plane, jax.ShapeDtypeStruct(a.shape, f32)],
        scratch_shapes=[pltpu.VMEM(a.shape, f32)], compiler_params=_cparams(1),
    )(a, xr, xi, sr, si)


def s5_states(tag, u, bd, a0, a1):
    t = u.shape[0]
    tile = a0.shape[1:]

    def run(u, bd, a0, a1):
        bu = [b.reshape((t,) + tile) for b in _bd_call(tag + "s5_bu", [u], [bd], "nn")]
        s0 = _cscan_call(tag + "s5_scan0", "asc", True, a0, bu[0], bu[1])
        s1 = _cscan_call(tag + "s5_scan1", "d1", False, a1, bu[0], bu[1])
        return (s0[0], s0[1], s1[0], s1[1])

    @jax.custom_vjp
    def op(u, bd, a0, a1):
        return run(u, bd, a0, a1)

    def fwd(u, bd, a0, a1):
        s = run(u, bd, a0, a1)
        return s, (u, bd, a0, a1, s)

    def bwd(res, ds):
        u, bd, a0, a1, s = res
        g0r, g0i, da0 = _cscan_call(tag + "s5_scan0_bwd", "desc", False, a0, ds[0], ds[1], s[0], s[1])
        g1r, g1i, da1 = _cscan_call(tag + "s5_scan1_bwd", "d1r", True, a1, ds[2], ds[3], s[2], s[3])
        gs = [g.reshape(t, -1) for g in (g0r, g0i, g1r, g1i)]
        du, = _bd_call(tag + "s5_bu_da", gs, [[bd[0], bd[1], bd[0], bd[1]]], "nt")
        k = bd[0].shape[0]
        dbd = [_bd_call(tag + "s5_bu_db%d" % j, u, gs[j], "tn", k) + _bd_call(tag + "s5_bu_db%d" % (j + 2), u, gs[j + 2], "tn", k)
               for j in range(2)]
        return du, dbd, da0, da1

    op.defvjp(fwd, bwd)
    return op(u, list(bd), a0, a1)


def _bd_call(name, a, b, mode, k=None):
    if mode == "tn":
        t = a.shape[0]
        ck, cn = a.shape[1] // k, b.shape[1] // k

        def body(a_ref, b_ref, o_ref):
            o_ref[0] = _dot(a_ref[...], b_ref[...], _TN, _MAP_PREC)

        return pl.pallas_call(
            body, name=name, grid=(k,),
            in_specs=[pl.BlockSpec((t, ck), lambda j: (0, j)), pl.BlockSpec((t, cn), lambda j: (0, j))],
            out_specs=pl.BlockSpec((1, ck, cn), lambda j: (j, 0, 0)),
            out_shape=jax.ShapeDtypeStruct((k, ck, cn), f32), compiler_params=_cparams(1),
        )(a, b)
    k, ck, cn = b[0][0].shape
    n_i, n_o = len(b), len(b[0])
    n_x = len(a)
    t = a[0].shape[0]
    tm = _pick(t, 1088, _SUB)
    flat = [w for row in b for w in row]
    win, wout, n_out, dn = (ck, cn, n_o, _NN) if mode == "nn" else (cn, ck, n_i, _NT)

    def body(*refs):
        xv = [r[...] for r in refs[:n_x]]
        w_refs = refs[n_x:n_x + len(flat)]
        o_refs = refs[n_x + len(flat):]
        for q in range(n_out):
            acc = None
            for s in range(n_x):
                w = w_refs[s * n_o + q] if mode == "nn" else w_refs[q * n_o + s]
                part = _dot(xv[s], w[0], dn, _MAP_PREC)
                acc = part if acc is None else acc + part
            o_refs[q][...] = acc

    return pl.pallas_call(
        body, name=name, grid=(t // tm, k),
        in_specs=[pl.BlockSpec((tm, win), lambda i, j: (i, j))] * n_x
        + [pl.BlockSpec((1, ck, cn), lambda i, j: (j, 0, 0))] * len(flat),
        out_specs=[pl.BlockSpec((tm, wout), lambda i, j: (i, j))] * n_out,
        out_shape=[jax.ShapeDtypeStruct((t, k * wout), f32)] * n_out, compiler_params=_cparams(2),
    )(*a, *flat)


def bd_mm(name, xs, ws):
    k = ws[0][0].shape[0]

    @jax.custom_vjp
    def op(xs, ws):
        return tuple(_bd_call(name, xs, ws, "nn"))

    def fwd(xs, ws):
        return op(xs, ws), (xs, ws)

    def bwd(res, gs):
        xs, ws = res
        dws = [[_bd_call(name + "_db%d%d" % (i, o), x, g, "tn", k) for o, g in enumerate(gs)] for i, x in enumerate(xs)]
        return list(_bd_call(name + "_da", list(gs), ws, "nt")), dws

    op.defvjp(fwd, bwd)
    return op(list(xs), [list(row) for row in ws])


def _rscan_call(name, order, asc, a, x, hp=None):
    t = x.shape[0]
    n = t // RB
    cshape = (1, x.shape[1])
    xspec = pl.BlockSpec((RB, x.shape[1]), lambda i: (_blk(order, i, n), 0))

    def rd(ref, r):
        return ref[pl.ds(r, 1), :]

    def wr(ref, r, v):
        ref[pl.ds(r, 1), :] = v

    def rowidx(tt):
        return tt if asc else RB - 1 - tt

    if hp is None:
        def body(a_ref, x_ref, h_ref, hp_ref, c_ref):
            i = pl.program_id(0)

            @pl.when(i == 0)
            def _():
                c_ref[...] = jnp.zeros_like(c_ref)

            def step(tt, h):
                r1 = rowidx(2 * tt)
                r2 = rowidx(2 * tt + 1)
                a1, a2, x1, x2 = rd(a_ref, r1), rd(a_ref, r2), rd(x_ref, r1), rd(x_ref, r2)
                h1 = a1 * h + x1
                h2 = (a2 * a1) * h + (a2 * x1 + x2)
                wr(hp_ref, r1, h)
                wr(h_ref, r1, h1)
                wr(hp_ref, r2, h1)
                wr(h_ref, r2, h2)
                return h2
            c_ref[...] = lax.fori_loop(0, RB // 2, step, c_ref[...], unroll=4)

        return pl.pallas_call(
            body, name=name, grid=(n,), in_specs=[xspec, xspec], out_specs=[xspec, xspec],
            out_shape=[jax.ShapeDtypeStruct(x.shape, f32)] * 2, scratch_shapes=[pltpu.VMEM(cshape, f32)],
            compiler_params=_cparams(1),
        )(a, x)

    def body(a_ref, x_ref, hp_ref, da_ref, db_ref, c_ref):
        i = pl.program_id(0)

        @pl.when(i == 0)
        def _():
            c_ref[...] = jnp.zeros_like(c_ref)

        def step(tt, c):
            r1 = rowidx(2 * tt)
            r2 = rowidx(2 * tt + 1)
            a1, a2, x1, x2 = rd(a_ref, r1), rd(a_ref, r2), rd(x_ref, r1), rd(x_ref, r2)
            g1 = x1 + c
            k = x2 + a1 * x1
            g2 = k + a1 * c
            wr(db_ref, r1, g1)
            wr(da_ref, r1, g1 * rd(hp_ref, r1))
            wr(db_ref, r2, g2)
            wr(da_ref, r2, g2 * rd(hp_ref, r2))
            return a2 * k + (a2 * a1) * c
        c_ref[...] = lax.fori_loop(0, RB // 2, step, c_ref[...], unroll=4)

    return pl.pallas_call(
        body, name=name, grid=(n,), in_specs=[xspec, xspec, xspec], out_specs=[xspec, xspec],
        out_shape=[jax.ShapeDtypeStruct(x.shape, f32)] * 2, scratch_shapes=[pltpu.VMEM(cshape, f32)],
        compiler_params=_cparams(1),
    )(a, x, hp)


def rscan(name, d, a, x):
    order = "d1" if d else "asc"

    @jax.custom_vjp
    def op(a, x):
        return _rscan_call(name, order, d == 0, a, x)[0]

    def fwd(a, x):
        h, hp = _rscan_call(name, order, d == 0, a, x)
        return h, (a, hp)

    def bwd(res, dh):
        a, hp = res
        da, db = _rscan_call(name + "_bwd", _REV[order], d != 0, a, dh, hp)
        return da, db

    op.defvjp(fwd, bwd)
    return op(a, x)


def _wide_rows(t):
    return _pick(t, 544, 16)


def _mod_part(blk, rows, mod, bm, lo, hi):
    is_ctx = blk * rows + lax.broadcasted_iota(jnp.int32, (rows, 1), 0) < RB
    r = mod[:, lo:hi] + bm[:, lo:hi]
    return jnp.where(is_ctx, r[1:2], r[0:1])


def _f_silu(blk, p, x):
    return [_silu(x[0]).astype(bf16)]


def _f_normmod(blk, p, x):
    nw, mod, bm = p
    rows, d = x[0].shape
    shift = _mod_part(blk, rows, mod, bm, 0, d)
    scale = _mod_part(blk, rows, mod, bm, d, 2 * d)
    return [(_rms(x[0], nw) * (1.0 + scale) + shift).astype(bf16)]


def _f_resid(blk, p, x):
    mod, bm = p
    rows, d = x[0].shape
    return [x[0] + _mod_part(blk, rows, mod, bm, 2 * d, 3 * d) * x[1]]


def _f_mix(blk, p, x):
    bg, = p
    gp = x[0]
    d = x[1].shape[1]
    acc = None
    for k in range(4):
        t = jax.nn.sigmoid(gp[:, k * d:(k + 1) * d] + bg[:, k * d:(k + 1) * d]) * x[1 + k]
        acc = t if acc is None else acc + t
    return [acc.astype(bf16)]


def _tri(rev):
    row = lax.broadcasted_iota(jnp.int32, (CHUNK, CHUNK), 0)
    col = lax.broadcasted_iota(jnp.int32, (CHUNK, CHUNK), 1)
    return (col >= row) if rev else (col <= row)


def _chunk_ids(rev):
    ids = list(range(RB // CHUNK))
    return ids[::-1] if rev else ids


def _f_hg(rev):
    def f(blk, p, c, x):
        lb, = p
        st, = c
        qi, fr = x
        q = _silu(qi[:, :BR_W])
        v = qi[:, BR_W:]
        fg = lb + (1.0 - lb) * jax.nn.sigmoid(fr)
        logf = jnp.log(fg)
        k = 1.0 - fg
        m = _tri(rev)
        mf = m.astype(f32)
        outs = [None] * (RB // CHUNK)
        for ci in _chunk_ids(rev):
            sl = slice(CHUNK * ci, CHUNK * ci + CHUNK)
            lf = logf[sl]
            b = _dot(mf, lf, hi=True)
            bend = jnp.sum(lf, axis=0, keepdims=True)
            mid = 0.5 * bend
            qe = q[sl] * jnp.exp(b - mid)
            ke = k[sl] * jnp.exp(mid - b)
            kd = k[sl] * jnp.exp(bend - b)
            qb = q[sl] * jnp.exp(b)
            dec = jnp.exp(bend)
            vc = v[sl]
            oh, ns = [], []
            for hh in range(HG_HEADS):
                cs = slice(HG_DK * hh, HG_DK * hh + HG_DK)
                sth = st[cs]
                att = jnp.where(m, _dot(qe[:, cs], ke[:, cs], _NT), 0.0)
                oh.append(_dot(att, vc[:, cs]) + _dot(qb[:, cs], sth, _NT))
                ns.append(sth * dec[:, cs] + _dot(vc[:, cs], kd[:, cs], _TN))
            st = jnp.concatenate(ns, axis=0)
            outs[ci] = jnp.concatenate(oh, axis=1)
        return [st], [jnp.concatenate(outs, axis=0)]
    return f


def _both(f0, f1, n_p, n_x):
    def f(blk, p, c, x):
        c0, y0 = f0(blk, p[:n_p], c[:1], x[:n_x])
        c1, y1 = f1(blk, p[n_p:], c[1:], x[n_x:])
        return c0 + c1, y0 + y1
    return f


_BOTH_ORDERS = (["asc", "asc", "d1", "d1"], ["asc", "d1"])


def _f_hg_final(blk, p, x):
    nw, = p
    o = x[0] + x[1]
    parts = []
    for hh in range(HG_HEADS):
        cs = slice(HG_DK * hh, HG_DK * hh + HG_DK)
        parts.append(_rms(o[:, cs], nw[:, cs]))
    return [(jnp.concatenate(parts, axis=1) * _silu(x[2])).astype(bf16)]


def _conv(x, cw, cb, blk):
    rows = x.shape[0]
    r = lax.broadcasted_iota(jnp.int32, (rows, 1), 0)
    rm = jnp.where(blk == 0, r, r % CHUNK)
    seg = jnp.where(blk == 0, rows, CHUNK)

    def vmask(o):
        return ((rm + o >= 0) & (rm + o < seg)).astype(f32)

    def shifted(o):
        @jax.custom_vjp
        def sh(x, mo, mn):
            return pltpu.roll(x, (-o) % rows, 0) * mo

        def fwd(x, mo, mn):
            return sh(x, mo, mn), (mo, mn)

        def bwd(res, g):
            mo, mn = res
            return pltpu.roll(g, o % rows, 0) * mn, jnp.zeros_like(mo), jnp.zeros_like(mn)
        sh.defvjp(fwd, bwd)
        return sh(x, vmask(o), vmask(-o))

    lo = (CONV_W - 1) // 2
    out = cb
    for k in range(CONV_W):
        o = k - lo
        out = out + cw[k:k + 1] * (x if o == 0 else shifted(o))
    return out


def _f_lru_a(blk, p, x):
    cw, cb, wg, gb, lam = p
    xc = _conv(x[0], cw, cb, blk)
    n_chunks = BR_W // _LANE
    xk = [xc[:, _LANE * k:_LANE * (k + 1)] for k in range(n_chunks)]

    def gate(j):
        pre = jnp.concatenate([_dot(xk[k], wg[j * n_chunks + k], hi=_MAP_PREC) for k in range(n_chunks)], axis=1)
        return jax.nn.sigmoid(pre + gb[:, BR_W * j:BR_W * (j + 1)])

    outs = []
    for d in range(2):
        r = gate(2 * d)
        ig = gate(2 * d + 1)
        log_a = -LRU_C * r * _softplus(-lam[d:d + 1])
        outs.append(jnp.exp(log_a))
        outs.append(jnp.sqrt(_one_minus_exp(2.0 * log_a)) * (ig * xc))
    return outs


def _f_lru_c(blk, p, x):
    return [((x[0] + x[1]) * _silu(x[2])).astype(bf16)]


def _f_s5_c1(blk, p, x):
    dsk, = p
    return [jax.nn.gelu(x[0] + dsk * x[1])]


def _f_s5_c2(blk, p, x):
    bglu, = p
    return [(x[0] * jax.nn.sigmoid(x[1] + bglu) * _silu(x[2])).astype(bf16)]


def _f_m2_a(blk, p, x):
    cw, cb, dtb = p
    return [_silu(_conv(x[0], cw, cb, blk)), _softplus(x[1] + dtb)]


def _f_ssd(d):
    rev = d == 1
    hpg = M2_HEADS // M2_GROUPS

    def f(blk, p, c, x):
        alog, = p
        st, = c
        xbc, dtp = x
        a = -jnp.exp(alog[:, M2_HEADS * d:M2_HEADS * (d + 1)])
        dt = dtp[:, M2_HEADS * d:M2_HEADS * (d + 1)]
        xs = xbc[:, :BR_W]
        bm = xbc[:, BR_W:BR_W + M2_GROUPS * M2_STATE]
        cm = xbc[:, BR_W + M2_GROUPS * M2_STATE:]
        gw = hpg * M2_HEADDIM
        mf = _tri(rev).astype(f32)
        row = lax.broadcasted_iota(jnp.int32, (CHUNK, gw), 0)
        col = lax.broadcasted_iota(jnp.int32, (CHUNK, gw), 1)
        m4 = (col % CHUNK >= row) if rev else (col % CHUNK <= row)
        spread = (lax.broadcasted_iota(jnp.int32, (hpg, gw), 0)
                  == lax.div(lax.broadcasted_iota(jnp.int32, (hpg, gw), 1), M2_HEADDIM)).astype(f32)
        own = [lax.div(lax.broadcasted_iota(jnp.int32, (1, gw), 1), M2_HEADDIM) == r for r in range(hpg)]
        outs = [None] * (RB // CHUNK)
        for ci in _chunk_ids(rev):
            sl = slice(CHUNK * ci, CHUNK * ci + CHUNK)
            dtc = dt[sl]
            dta = dtc * a
            cum = _dot(mf, dta, hi=True)
            cum_t = cum.T
            dt_t = dtc.T
            ys, ns = [], []
            for g in range(M2_GROUPS):
                hs = slice(hpg * g, hpg * (g + 1))
                bmg = bm[sl, M2_STATE * g:M2_STATE * (g + 1)]
                cmg = cm[sl, M2_STATE * g:M2_STATE * (g + 1)]
                xg = xs[sl, gw * g:gw * (g + 1)]
                stg = st[M2_STATE * g:M2_STATE * (g + 1)]
                cum_i = _dot(cum[:, hs], spread, hi=True)
                cum_j = jnp.concatenate([cum_t[hpg * g + r:hpg * g + r + 1] for r in range(hpg)], axis=1)
                dt_j = jnp.concatenate([dt_t[hpg * g + r:hpg * g + r + 1] for r in range(hpg)], axis=1)
                dt_i = _dot(dtc[:, hs], spread, hi=True)
                cend_g = jnp.sum(_dot(dta[:, hs], spread, hi=True), axis=0, keepdims=True)
                decay = jnp.exp(jnp.where(m4, cum_i - cum_j, -1e30))
                scores = _dot(cmg, jnp.concatenate([bmg] * hpg, axis=0), _NT)
                w = scores * decay * dt_j
                xdiag = jnp.concatenate([jnp.where(own[r], xg, 0.0) for r in range(hpg)], axis=0)
                ys.append(_dot(w, xdiag) + _dot(cmg, stg) * jnp.exp(cum_i))
                wx = jnp.exp(cend_g - cum_i) * dt_i * xg
                ns.append(jnp.exp(cend_g) * stg + _dot(bmg, wx, _TN))
            st = jnp.concatenate(ns, axis=0)
            outs[ci] = jnp.concatenate(ys, axis=1)
        return [st], [jnp.concatenate(outs, axis=0)]
    return f


def _f_m2_c(blk, p, x):
    dsk, nw = p
    y = x[0] + x[1] + dsk * x[2][:, :BR_W]
    return [_rms(y * _silu(x[3]), nw).astype(bf16)]


def _f_loss(blk, p, x):
    fnw, = p
    err = _rms(x[0], fnw) - x[1]
    return [0.5 * jnp.mean(err * err, axis=-1, keepdims=True)]


def _blockdiag(w):
    g, a, b = w.shape
    return jnp.einsum("gab,gh->gahb", w, jnp.eye(g, dtype=w.dtype)).reshape(g * a, g * b)


def _s5_params(l, w):
    a_scan, cds = [], []
    per = _LANE // S5_GROUP

    def chunks(m):
        return jnp.stack([_blockdiag(m[k * per:(k + 1) * per]) for k in range(S5_GROUPS // per)])

    b_re = jnp.transpose(w["s5_b_re"][l], (0, 2, 1))
    b_im = jnp.transpose(w["s5_b_im"][l], (0, 2, 1))
    bd = [chunks(b_re), chunks(b_im)]
    c_re = jnp.transpose(w["s5_c_re"][l], (0, 2, 1))
    c_im = jnp.transpose(w["s5_c_im"][l], (0, 2, 1))
    for d in range(2):
        lam_re = w["s5_a_re"][l, d]
        lam_im = w["s5_a_im"][l, d]
        step = jnp.exp(w["s5_log_step"][l, d])[:, None]
        mag = jnp.exp(lam_re * step)
        ab_re = mag * jnp.cos(lam_im * step)
        ab_im = mag * jnp.sin(lam_im * step)
        den = lam_re * lam_re + lam_im * lam_im
        nr = ab_re - 1.0
        co_re = (nr * lam_re + ab_im * lam_im) / den
        co_im = (ab_im * lam_re - nr * lam_im) / den
        n_state = S5_GROUPS * S5_STATE
        a_scan.append(jnp.stack([ab_re.reshape(_SUB, n_state // _SUB), ab_im.reshape(_SUB, n_state // _SUB)]))
        cp_re = c_re * co_re[:, :, None] - c_im * co_im[:, :, None]
        cp_im = c_re * co_im[:, :, None] + c_im * co_re[:, :, None]
        cds.append([chunks(cp_re), -chunks(cp_im)])
    return a_scan, bd, cds


def _lru_gate(l, w):
    gw = w["lru_gate_w"][l]
    per = _LANE // (BR_W // LRU_BLOCKS)
    chunks = [_blockdiag(gw[d, g, k * per:(k + 1) * per])
              for d in range(2) for g in range(2) for k in range(LRU_BLOCKS // per)]
    return jnp.stack(chunks), w["lru_gate_b"][l].reshape(1, -1)


def _pad_cols(a, n):
    return jnp.pad(a, ((0, 0), (0, n - a.shape[1])))


IN_SIZES = (BR_W,) * 9 + (M2_XBC, 2 * M2_HEADS, BR_W)
IN_OFFS = tuple(sum(IN_SIZES[:i]) for i in range(len(IN_SIZES) + 1))
IN_GROUPS = (("hg_qi", 0, 2, 1024), ("hg_ff", 2, 1, 512), ("hg_fb", 3, 1, 512), ("hg_z", 4, 1, 512),
             ("s5_u", 5, 1, 512), ("s5_z", 6, 1, 512), ("lru_x", 7, 1, 512), ("lru_z", 8, 1, 512),
             ("m2_xbc", 9, 1, 768), ("m2_dt", 10, 1, 128), ("m2_z", 11, 1, 512))


def _new_slots(big):
    slots = {n: [jnp.zeros(w.shape, f32) for w in ws] for n, ws in big.items() if n not in ("w_in", "w_gate", "w_branch")}
    n_layers, d_model = len(big["w_in"]), big["w_in"][0].shape[0]
    slots["w_in"] = [{name: jnp.zeros((d_model, width), f32) for name, _, _, width in IN_GROUPS} for _ in range(n_layers)]
    slots["w_gate"] = [jnp.zeros((d_model, 4 * d_model), f32) for _ in range(n_layers)]
    slots["w_branch"] = [[jnp.zeros(w.shape[1:], f32) for _ in range(w.shape[0])] for w in big["w_branch"]]
    return slots


def _slot_grads(g):
    out = {n: jnp.stack(v) for n, v in g.items() if n not in ("w_in", "w_gate", "w_branch")}
    out["w_branch"] = jnp.stack([jnp.stack(gl) for gl in g["w_branch"]])
    out["w_in"] = jnp.stack([
        jnp.concatenate([gl[name][:, :IN_OFFS[s0 + ns] - IN_OFFS[s0]] for name, s0, ns, _ in IN_GROUPS], axis=1)
        for gl in g["w_in"]])
    d_model = g["w_gate"][0].shape[0]
    out["w_gate"] = jnp.stack([jnp.transpose(gl.reshape(d_model, 4, d_model), (1, 0, 2)) for gl in g["w_gate"]])
    return out


def _forward(p, big, slots, x, ctx, c, target):
    n_layers = p["norm_w"].shape[0]
    d_model = x.shape[-1]
    xa = jnp.concatenate([ctx, x], axis=0)
    t = xa.shape[0]
    cc = jnp.concatenate([c, p["c_ctx"][None], jnp.zeros((_SUB - 2, d_model), f32)], axis=0)
    lb_all = jnp.cumsum(jax.nn.softmax(p["hg_lb_logits"], axis=0), axis=0)
    scc, = blocked_op("silu_c", _f_silu, [], [cc], [(d_model, bf16)], rb=_SUB)
    wide = _wide_rows(t)

    for l in range(n_layers):
        tag = "l%d_" % l
        mod = mm(tag + "mod", scc, big["w_mod"][l], slots["w_mod"][l])
        bm = p["b_mod"][l][None]
        h, = blocked_op(tag + "normmod", _f_normmod, [p["norm_w"][l][None], mod, bm], [xa], [(d_model, bf16)], rb=wide)
        gnames = [g[0] for g in IN_GROUPS]
        wvs = [_pad_cols(big["w_in"][l][:, IN_OFFS[s0]:IN_OFFS[s0 + ns]], width) for _, s0, ns, width in IN_GROUPS]
        u = dict(zip(gnames, multi_mm(tag + "in_", gnames, h, wvs, [slots["w_in"][l][g] for g in gnames])))

        o_dirs = blocked_op(tag + "hg", _both(_f_hg(False), _f_hg(True), 1, 2), [lb_all[l, 0][None], lb_all[l, 1][None]],
                            [u["hg_qi"], u["hg_ff"], u["hg_qi"], u["hg_fb"]], [(BR_W, f32)] * 2,
                            order=_BOTH_ORDERS, carry_sds=[(BR_W, HG_DK)] * 2)
        y_hg, = blocked_op(tag + "hg_fin", _f_hg_final, [p["hg_norm"][l][None]], list(o_dirs) + [u["hg_z"]], [(BR_W, bf16)],
                           rb=wide)

        a_scan, bd, cds = _s5_params(l, p)
        n_state = S5_GROUPS * S5_STATE
        planes = [s.reshape(t, n_state) for s in s5_states(tag, u["s5_u"], bd, a_scan[0], a_scan[1])]
        ysum, = bd_mm(tag + "s5_c", planes, [[cds[d][part]] for d in range(2) for part in range(2)])
        g5, = blocked_op(tag + "s5_c1", _f_s5_c1, [p["s5_d"][l][None]], [ysum, u["s5_u"]], [(BR_W, f32)], rb=wide)
        gl = mm(tag + "s5_glu", g5, big["s5_w_glu"][l], slots["s5_w_glu"][l])
        y_s5, = blocked_op(tag + "s5_c2", _f_s5_c2, [p["s5_b_glu"][l][None]], [g5, gl, u["s5_z"]], [(BR_W, bf16)],
                           rb=wide)

        wg, gb = _lru_gate(l, p)
        ab = blocked_op(tag + "lru_a", _f_lru_a,
                        [p["lru_conv_w"][l], p["lru_conv_b"][l][None], wg, gb, p["lru_lam"][l]],
                        [u["lru_x"]], [(BR_W, f32)] * 4)
        hs = []
        for d in range(2):
            hs.append(rscan(tag + "lru_scan%d" % d, d, ab[2 * d], ab[2 * d + 1]))
        y_lru, = blocked_op(tag + "lru_c", _f_lru_c, [], hs + [u["lru_z"]], [(BR_W, bf16)], rb=wide)

        dtb = _pad_cols(p["m2_dt_bias"][l].reshape(1, -1), _LANE)
        xbc, dtp = blocked_op(tag + "m2_a", _f_m2_a, [p["m2_conv_w"][l], p["m2_conv_b"][l][None], dtb],
                              [u["m2_xbc"], u["m2_dt"]], [(M2_XBC, f32), (_LANE, f32)])
        alog = _pad_cols(p["m2_a_log"][l].reshape(1, -1), _LANE)
        y_dirs = blocked_op(tag + "ssd", _both(_f_ssd(0), _f_ssd(1), 1, 2), [alog, alog], [xbc, dtp, xbc, dtp],
                            [(BR_W, f32)] * 2, order=_BOTH_ORDERS,
                            carry_sds=[(M2_GROUPS * M2_STATE, BR_W // M2_GROUPS)] * 2)
        dsk = jnp.repeat(p["m2_d"][l], M2_HEADDIM)[None]
        y_m2, = blocked_op(tag + "m2_c", _f_m2_c, [dsk, p["m2_norm"][l][None]], list(y_dirs) + [xbc, u["m2_z"]], [(BR_W, bf16)],
                           rb=wide)

        wg_all = jnp.transpose(big["w_gate"][l], (1, 0, 2)).reshape(d_model, 4 * d_model)
        gp = mm(tag + "gate", h, wg_all, slots["w_gate"][l], out_dtype=bf16)
        bs = [mm(tag + "br%d" % k, yk, big["w_branch"][l][k], slots["w_branch"][l][k], out_dtype=bf16)
              for k, yk in enumerate((y_hg, y_s5, y_lru, y_m2))]
        mix, = blocked_op(tag + "mix", _f_mix, [p["b_gate"][l].reshape(1, -1)], [gp] + bs, [(d_model, bf16)])
        o = mm(tag + "out", mix, big["w_out"][l], slots["w_out"][l])
        xa, = blocked_op(tag + "resid", _f_resid, [mod, bm], [xa, o], [(d_model, f32)], rb=wide)

    rl, = blocked_op("loss", _f_loss, [p["final_norm"][None]], [xa[ctx.shape[0]:], target], [(1, f32)],
                     rb=_wide_rows(target.shape[0]))
    return jnp.sum(rl)


_MESH = pl.DeviceIdType.MESH
_ANY = pl.BlockSpec(memory_space=pl.ANY)
W_PACK = 1024


def _place():
    x, y, c = lax.axis_index("x"), lax.axis_index("y"), lax.axis_index("c")
    chips = [(x, 1 - y), (1 - x, y), (1 - x, 1 - y)]
    return x, y, c, chips


def _rcopy(src, dst, ssem, rsem, k, to):
    return pltpu.make_async_remote_copy(src_ref=src, dst_ref=dst, send_sem=ssem.at[k], recv_sem=rsem.at[k],
                                        device_id=to, device_id_type=_MESH)


def gather_shards(xs):
    n = len(xs)

    def body(*refs):
        x_refs, o_refs = refs[:n], refs[n:2 * n]
        ssem, rsem, lsem = refs[2 * n:]
        x, y, c, chips = _place()
        j = 2 * x + y
        sib = (x, y, 1 - c)
        mine = [pltpu.make_async_copy(x_refs[a], o_refs[a].at[j], lsem.at[a]) for a in range(n)]
        for cp in mine:
            cp.start()
        first = [_rcopy(x_refs[a].at[c], o_refs[a].at[j, c], ssem, rsem, 6 * a + r, (*chips[r], c))
                 for r in range(3) for a in range(n)]
        for cp in first:
            cp.start()
        passed = []
        for r in range(3):
            jr = j ^ (r + 1)
            for a in range(n):
                _rcopy(x_refs[a].at[c], o_refs[a].at[jr, c], ssem, rsem, 6 * a + r, sib).wait_recv()
                cp = _rcopy(o_refs[a].at[jr, c], o_refs[a].at[jr, c], ssem, rsem, 6 * a + 3 + r, sib)
                cp.start()
                passed.append(cp)
        for r in range(3):
            jr = j ^ (r + 1)
            for a in range(n):
                _rcopy(x_refs[a].at[c], o_refs[a].at[jr, 1 - c], ssem, rsem, 6 * a + 3 + r, sib).wait_recv()
        for cp in first + passed:
            cp.wait_send()
        for cp in mine:
            cp.wait()

    return pl.pallas_call(
        body, name="gather_shards", out_shape=[jax.ShapeDtypeStruct((4,) + x.shape, x.dtype) for x in xs],
        in_specs=[_ANY] * n, out_specs=[_ANY] * n,
        scratch_shapes=[pltpu.SemaphoreType.DMA((6 * n,)), pltpu.SemaphoreType.DMA((6 * n,)), pltpu.SemaphoreType.DMA((n,))],
    )(*xs)


def sibling_halves(gs):
    n = len(gs)

    def body(*refs):
        g_refs, o_refs = refs[:n], refs[n:2 * n]
        ssem, rsem = refs[2 * n:]
        x, y, c, _ = _place()
        sib = (x, y, 1 - c)
        cps = [_rcopy(g_refs[a].at[k, 1 - c], o_refs[a].at[k], ssem, rsem, 4 * a + k, sib)
               for k in range(4) for a in range(n)]
        for cp in cps:
            cp.start()
        for cp in cps:
            cp.wait()

    return pl.pallas_call(
        body, name="sibling_halves", out_shape=[jax.ShapeDtypeStruct((4,) + g.shape[2:], g.dtype) for g in gs],
        in_specs=[_ANY] * n, out_specs=[_ANY] * n,
        scratch_shapes=[pltpu.SemaphoreType.DMA((4 * n,)), pltpu.SemaphoreType.DMA((4 * n,))],
    )(*gs)


def scatter_chips(ps):
    n = len(ps)

    def body(*refs):
        p_refs, o_refs = refs[:n], refs[n:2 * n]
        ssem, rsem = refs[2 * n:]
        x, y, c, chips = _place()
        j = 2 * x + y
        cps = [_rcopy(p_refs[a].at[j ^ (r + 1)], o_refs[a].at[r], ssem, rsem, 3 * a + r, (*chips[r], c))
               for r in range(3) for a in range(n)]
        for cp in cps:
            cp.start()
        for cp in cps:
            cp.wait()

    return pl.pallas_call(
        body, name="scatter_chips", out_shape=[jax.ShapeDtypeStruct((3,) + p.shape[1:], p.dtype) for p in ps],
        in_specs=[_ANY] * n, out_specs=[_ANY] * n,
        scratch_shapes=[pltpu.SemaphoreType.DMA((3 * n,)), pltpu.SemaphoreType.DMA((3 * n,))],
    )(*ps)


def join_halves(qs):
    n = len(qs)

    def body(*refs):
        o_refs = refs[n:2 * n]
        ssem, rsem = refs[2 * n:]
        x, y, c, _ = _place()
        sib = (x, y, 1 - c)
        cps = [_rcopy(o_refs[a].at[c], o_refs[a].at[c], ssem, rsem, a, sib) for a in range(n)]
        for cp in cps:
            cp.start()
        for a in range(n):
            _rcopy(o_refs[a].at[c], o_refs[a].at[1 - c], ssem, rsem, a, sib).wait_recv()
        for cp in cps:
            cp.wait_send()

    return pl.pallas_call(
        body, name="join_halves", out_shape=[jax.ShapeDtypeStruct(q.shape, q.dtype) for q in qs],
        in_specs=[_ANY] * n, out_specs=[_ANY] * n, input_output_aliases={a: a for a in range(n)},
        scratch_shapes=[pltpu.SemaphoreType.DMA((n,)), pltpu.SemaphoreType.DMA((n,))],
    )(*qs)


def _rows_block(r):
    return _pick(r, 256, _SUB)


def add_sibling(tag, g, r1, place, out_dtype):
    _, _, rows, w = g.shape
    rb = _rows_block(rows)

    def body(pl_ref, g_ref, r_ref, o_ref):
        o_ref[...] = (g_ref[0] + r_ref[...]).astype(out_dtype)

    return pl.pallas_call(
        body, name="add_sibling_" + tag, out_shape=jax.ShapeDtypeStruct((4, rows, w), out_dtype),
        grid_spec=pltpu.PrefetchScalarGridSpec(
            num_scalar_prefetch=1, grid=(4, rows // rb),
            in_specs=[pl.BlockSpec((1, 1, rb, w), lambda k, i, s: (k, s[1], i, 0)),
                      pl.BlockSpec((1, rb, w), lambda k, i, s: (k, i, 0))],
            out_specs=pl.BlockSpec((1, rb, w), lambda k, i, s: (k, i, 0))),
        compiler_params=_cparams(2),
    )(place, g, r1)


def add_chips(tag, p, r2, place):
    _, rows, w = p.shape
    rb = _rows_block(rows)

    def body(pl_ref, p_ref, r_ref, o_ref):
        j = pl_ref[0]
        own = p_ref[0].astype(f32)
        others = [r_ref[0].astype(f32), r_ref[1].astype(f32), r_ref[2].astype(f32)]
        acc = None
        for k in range(4):
            rel = k ^ j
            t = jnp.where(rel == 0, own, jnp.where(rel == 1, others[0], jnp.where(rel == 2, others[1], others[2])))
            acc = t if acc is None else acc + t
        o_ref[0] = acc

    return pl.pallas_call(
        body, name="add_chips_" + tag, out_shape=jax.ShapeDtypeStruct((2, rows, w), f32),
        grid_spec=pltpu.PrefetchScalarGridSpec(
            num_scalar_prefetch=1, grid=(rows // rb,),
            in_specs=[pl.BlockSpec((1, rb, w), lambda i, s: (s[0], i, 0)),
                      pl.BlockSpec((3, rb, w), lambda i, s: (0, i, 0))],
            out_specs=pl.BlockSpec((1, rb, w), lambda i, s: (s[1], i, 0))),
        compiler_params=_cparams(1),
    )(place, p, r2)


def adamw(tag, g, w, m, v):
    rows, wd = g.shape
    rb = _rows_block(rows)

    def body(g_ref, w_ref, m_ref, v_ref, d_ref, nm_ref, nv_ref):
        gv = g_ref[...]
        nm = ADAM_B1 * m_ref[...] + (1.0 - ADAM_B1) * gv
        nv = ADAM_B2 * v_ref[...] + (1.0 - ADAM_B2) * (gv * gv)
        m_hat = nm / (1.0 - ADAM_B1 ** ADAM_STEP)
        v_hat = nv / (1.0 - ADAM_B2 ** ADAM_STEP)
        d_ref[...] = -ADAM_LR * (m_hat / (jnp.sqrt(v_hat) + ADAM_EPS) + ADAM_WD * w_ref[...])
        nm_ref[...] = nm
        nv_ref[...] = nv

    spec = pl.BlockSpec((rb, wd), lambda i: (i, 0))
    return pl.pallas_call(
        body, name="adamw_" + tag, grid=(rows // rb,), in_specs=[spec] * 4, out_specs=[spec] * 3,
        out_shape=[jax.ShapeDtypeStruct(g.shape, f32)] * 3, compiler_params=_cparams(1),
    )(g, w, m, v)


WEIGHTS = ("c_ctx", "norm_w", "w_mod", "b_mod", "w_in", "hg_lb_logits", "hg_norm", "s5_a_re", "s5_a_im", "s5_log_step",
           "s5_b_re", "s5_b_im", "s5_c_re", "s5_c_im", "s5_d", "s5_w_glu", "s5_b_glu", "lru_conv_w", "lru_conv_b",
           "lru_gate_w", "lru_gate_b", "lru_lam", "m2_conv_w", "m2_conv_b", "m2_dt_bias", "m2_a_log", "m2_d", "m2_norm",
           "w_branch", "w_gate", "b_gate", "w_out", "final_norm")
SHARD_AXIS = {"w_mod": 2, "w_in": 2, "hg_lb_logits": 2, "s5_w_glu": 1, "lru_conv_w": 2, "lru_lam": 2, "m2_conv_w": 2,
              "w_branch": 3, "w_gate": 2, "b_gate": 2, "w_out": 1}
BIG = ("w_mod", "w_in", "s5_w_glu", "w_branch", "w_gate", "w_out")
N_CHIPS = 4


def _to_rows(flat, row_unit):
    n = flat.shape[-1]
    per = 2 * row_unit * W_PACK
    total = -(-n // per) * per
    flat = jnp.pad(flat, [(0, 0)] * (flat.ndim - 1) + [(0, total - n)])
    return flat.reshape(flat.shape[:-1] + (2, total // (2 * W_PACK), W_PACK))


SMALL_SHARDED = tuple(n for n in WEIGHTS if n in SHARD_AXIS and n not in BIG)
SMALL_REPLICATED = tuple(n for n in WEIGHTS if n not in SHARD_AXIS)


def _chip_slices(a, axis):
    width = a.shape[axis] // N_CHIPS
    return jnp.stack([lax.slice_in_dim(a, k * width, (k + 1) * width, axis=axis) for k in range(N_CHIPS)])


def _gather_weights(local):
    small = jnp.concatenate([lax.bitcast_convert_type(local[n], bf16).reshape(-1) for n in SMALL_SHARDED])
    got = gather_shards([local[n].astype(bf16) for n in BIG] + [_to_rows(small, 16)])
    full = {}
    for n, g in zip(BIG, got):
        full[n] = [jnp.concatenate([g[j, l] for j in range(N_CHIPS)], axis=SHARD_AXIS[n] - 1) for l in range(g.shape[1])]
    flat, off = got[-1].reshape(N_CHIPS, -1), 0
    for n in SMALL_SHARDED:
        shp = local[n].shape
        size = 2 * math.prod(shp)
        part = lax.bitcast_convert_type(flat[:, off:off + size].reshape((N_CHIPS,) + shp + (2,)), f32)
        off += size
        full[n] = jnp.concatenate([part[j] for j in range(N_CHIPS)], axis=SHARD_AXIS[n])
    return full


def _whole_rows(v):
    n = v.shape[-1]
    return jnp.pad(v, [(0, 0)] * (v.ndim - 1) + [(0, -n % W_PACK)])


def _pack_small(vals, extra):
    return jnp.concatenate([_whole_rows(vals[n].reshape(-1)) for n in SMALL_SHARDED + SMALL_REPLICATED]
                           + [_whole_rows(extra.reshape(1))])


def _pack_small_grads(grads, loss):
    rep = [grads[n].reshape(-1) for n in SMALL_REPLICATED] + [loss.reshape(1)]
    sh = [_chip_slices(grads[n], SHARD_AXIS[n]).reshape(N_CHIPS, -1) for n in SMALL_SHARDED]
    return jnp.concatenate([_whole_rows(a) for a in sh]
                           + [_whole_rows(jnp.broadcast_to(r, (N_CHIPS,) + r.shape)) for r in rep], axis=1)


def _unpack_small(rows, like):
    out, r0 = {}, 0
    for n in SMALL_SHARDED + SMALL_REPLICATED:
        size = math.prod(like[n].shape)
        nr = -(-size // W_PACK)
        piece = lax.optimization_barrier(rows[r0:r0 + nr])
        out[n] = piece.reshape(-1)[:size].reshape(like[n].shape)
        r0 += nr
    return out, lax.optimization_barrier(rows[r0:r0 + 1])[0, 0]


def _reduce_grads(tags, gs):
    place = jnp.stack([2 * lax.axis_index("x") + lax.axis_index("y"), lax.axis_index("c")]).astype(jnp.int32)
    pairs = [add_sibling(t, g, r, place, bf16 if t in BIG else f32) for t, g, r in zip(tags, gs, sibling_halves(gs))]
    quads = [add_chips(t, p, r, place) for t, p, r in zip(tags, pairs, scatter_chips(pairs))]
    return join_halves(quads)


def kernel(x, c, ctx, c_ctx, norm_w, w_mod, b_mod, w_in, hg_lb_logits, hg_norm, s5_a_re, s5_a_im, s5_log_step, s5_b_re, s5_b_im, s5_c_re, s5_c_im, s5_d, s5_w_glu, s5_b_glu, lru_conv_w, lru_conv_b, lru_gate_w, lru_gate_b, lru_lam, m2_conv_w, m2_conv_b, m2_dt_bias, m2_a_log, m2_d, m2_norm, w_branch, w_gate, b_gate, w_out, final_norm, loss_target, m_c_ctx, m_norm_w, m_w_mod, m_b_mod, m_w_in, m_hg_lb_logits, m_hg_norm, m_s5_a_re, m_s5_a_im, m_s5_log_step, m_s5_b_re, m_s5_b_im, m_s5_c_re, m_s5_c_im, m_s5_d, m_s5_w_glu, m_s5_b_glu, m_lru_conv_w, m_lru_conv_b, m_lru_gate_w, m_lru_gate_b, m_lru_lam, m_m2_conv_w, m_m2_conv_b, m_m2_dt_bias, m_m2_a_log, m_m2_d, m_m2_norm, m_w_branch, m_w_gate, m_b_gate, m_w_out, m_final_norm, v_c_ctx, v_norm_w, v_w_mod, v_b_mod, v_w_in, v_hg_lb_logits, v_hg_norm, v_s5_a_re, v_s5_a_im, v_s5_log_step, v_s5_b_re, v_s5_b_im, v_s5_c_re, v_s5_c_im, v_s5_d, v_s5_w_glu, v_s5_b_glu, v_lru_conv_w, v_lru_conv_b, v_lru_gate_w, v_lru_gate_b, v_lru_lam, v_m2_conv_w, v_m2_conv_b, v_m2_dt_bias, v_m2_a_log, v_m2_d, v_m2_norm, v_w_branch, v_w_gate, v_b_gate, v_w_out, v_final_norm):
    given = dict(locals())
    w_loc = {n: given[n] for n in WEIGHTS}
    m_loc = {n: given["m_" + n] for n in WEIGHTS}
    v_loc = {n: given["v_" + n] for n in WEIGHTS}

    full = _gather_weights(w_loc)
    params = {n: (full[n] if n in SHARD_AXIS else w_loc[n]) for n in WEIGHTS if n not in BIG}
    big = {n: full[n] for n in BIG}
    def loss_fn(p, s, xx):
        return _forward(p, big, s, xx, ctx[0], c, loss_target[0])

    loss, (g_p, g_s, g_x) = jax.value_and_grad(loss_fn, argnums=(0, 1, 2))(params, _new_slots(big), x[0])
    grads = {**g_p, **_slot_grads(g_s)}

    def rows4(a):
        return a.reshape(a.shape[:2] + (-1, a.shape[-1]))

    g_big = [rows4(_chip_slices(grads[n], SHARD_AXIS[n])) for n in BIG]
    g_small = _to_rows(_pack_small_grads(grads, loss), 64)
    summed = _reduce_grads(list(BIG) + ["small"], g_big + [g_small])

    g_out, d_out, m_out, v_out = {}, {}, {}, {}
    for n, g in zip(BIG, summed):
        shp = w_loc[n].shape
        flat2 = lambda a: a.reshape(-1, shp[-1])
        g_out[n] = g.reshape(shp)
        d, nm, nv = adamw(n, flat2(g), flat2(w_loc[n]), flat2(m_loc[n]), flat2(v_loc[n]))
        d_out[n], m_out[n], v_out[n] = d.reshape(shp), nm.reshape(shp), nv.reshape(shp)
    zero = jnp.zeros((), f32)
    flat = lambda vals: _to_rows(_pack_small(vals, zero), 64).reshape(-1, W_PACK)
    gs = summed[-1].reshape(-1, W_PACK)
    d, nm, nv = adamw("small", gs, flat(w_loc), flat(m_loc), flat(v_loc))
    gsm, loss_out = _unpack_small(gs, w_loc)
    g_out.update(gsm)
    d_out.update(_unpack_small(d, w_loc)[0])
    m_out.update(_unpack_small(nm, w_loc)[0])
    v_out.update(_unpack_small(nv, w_loc)[0])
    outs = [loss_out, g_x[None]]
    for group in (g_out, d_out, m_out, v_out):
        outs += [group[n] for n in WEIGHTS]
    return tuple(outs)
```

```python
import functools
import math

import jax
import jax.numpy as jnp
from jax import lax
from jax.experimental import pallas as pl
from jax.experimental.pallas import tpu as pltpu

f32 = jnp.float32
bf16 = jnp.bfloat16
_MM_DTYPE = bf16
_HI = lax.Precision.HIGHEST
_MAP_PREC = lax.Precision.HIGH
_VMEM_LIMIT = 56 * 1024 * 1024
_LANE = 128
_SUB = 8

EPS = 1e-6
CONV_W = 4
CHUNK = 64
RB = 256
BR_W = 512
HG_HEADS = 4
HG_DK = 128
S5_GROUPS = 32
S5_GROUP = 16
S5_STATE = 64
LRU_BLOCKS = 8
LRU_C = 8.0
M2_HEADS = 8
M2_HEADDIM = 64
M2_GROUPS = 2
M2_STATE = 64
M2_XBC = BR_W + 2 * M2_GROUPS * M2_STATE
ADAM_LR = 0.001
ADAM_B1 = 0.9
ADAM_B2 = 0.999
ADAM_EPS = 1e-08
ADAM_WD = 0.01
ADAM_STEP = 10

_NN = (((1,), (0,)), ((), ()))
_NT = (((1,), (1,)), ((), ()))
_TN = (((0,), (0,)), ((), ()))


def _silu(x):
    return x * jax.nn.sigmoid(x)


def _softplus(x):
    return jnp.maximum(x, 0.0) + jnp.log1p(jnp.exp(-jnp.abs(x)))


def _one_minus_exp(z):
    series = -z * (1.0 + z * 0.5 * (1.0 + z * (1.0 / 3.0) * (1.0 + z * 0.25 * (1.0 + z * 0.2))))
    return jnp.where(z > -0.05, series, 1.0 - jnp.exp(z))


def _rms(x, w):
    return x * lax.rsqrt(jnp.mean(x * x, axis=-1, keepdims=True) + EPS) * w


def _dot(a, b, dn=_NN, hi=False):
    prec = hi if isinstance(hi, lax.Precision) else (_HI if hi else None)
    return lax.dot_general(a, b, dn, precision=prec, preferred_element_type=f32)


def _cparams(n_grid):
    return pltpu.CompilerParams(dimension_semantics=("arbitrary",) * n_grid, vmem_limit_bytes=_VMEM_LIMIT)


_REV = {"asc": "desc", "d1": "d1r", "desc": "asc", "d1r": "d1"}


def _blk(order, i, n):
    if order == "asc":
        return i
    if order == "desc":
        return n - 1 - i
    if order == "d1":
        return jnp.where(i == 0, 0, n - i)
    return jnp.where(i == n - 1, 0, i + 1)


def _pick(n, cap, unit):
    if n <= cap:
        return n
    best = None
    d = unit
    while d <= cap:
        if n % d == 0:
            best = d
        d += unit
    return n if best is None else best


def _mm_call(name, a, b, mode, hi, out_dtype):
    if mode == "tn":
        k, m = a.shape
        n = b.shape[1]
        tm = _pick(m, 512, _LANE)
        tn = _pick(n, 512, _LANE)
        a_spec = pl.BlockSpec((k, tm), lambda i, j: (0, i))
        b_spec = pl.BlockSpec((k, tn), lambda i, j: (0, j))
    else:
        m, k = a.shape
        tm = _pick(m, max(256, min(1088, 4 * 1024 * 1024 // (k * a.dtype.itemsize))), _SUB)
        a_spec = pl.BlockSpec((tm, k), lambda i, j: (i, 0))
        if mode == "nn":
            n = b.shape[1]
            tn = _pick(n, max(_LANE, (8 * 1024 * 1024 // (k * b.dtype.itemsize)) // _LANE * _LANE), _LANE)
            b_spec = pl.BlockSpec((k, tn), lambda i, j: (0, j))
        else:
            n = b.shape[0]
            tn = _pick(n, max(_LANE, (8 * 1024 * 1024 // (k * b.dtype.itemsize)) // _LANE * _LANE), _LANE)
            b_spec = pl.BlockSpec((tn, k), lambda i, j: (j, 0))
    dn = {"nn": _NN, "nt": _NT, "tn": _TN}[mode]

    def body(a_ref, b_ref, o_ref):
        av = a_ref[...]
        bv = b_ref[...]
        if hi:
            av = av.astype(f32)
            bv = bv.astype(f32)
        else:
            av = av.astype(_MM_DTYPE)
            bv = bv.astype(_MM_DTYPE)
        o_ref[...] = _dot(av, bv, dn, hi).astype(o_ref.dtype)

    return pl.pallas_call(
        body, name=name, grid=(m // tm, n // tn), in_specs=[a_spec, b_spec],
        out_specs=pl.BlockSpec((tm, tn), lambda i, j: (i, j)),
        out_shape=jax.ShapeDtypeStruct((m, n), out_dtype), compiler_params=_cparams(2),
    )(a, b)


def mm(name, a, b, slot=None, hi=False, out_dtype=f32):
    @jax.custom_vjp
    def op(a, b, slot):
        return _mm_call(name, a, b, "nn", hi, out_dtype)

    def fwd(a, b, slot):
        return op(a, b, slot), (a, b)

    def bwd(res, g):
        a, b = res
        da = _mm_call(name + "_da", g, b, "nt", hi, a.dtype)
        db = _mm_call(name + "_db", a, g, "tn", hi, f32)
        if slot is None:
            return da, db.astype(b.dtype), None
        return da, jnp.zeros_like(b), db

    op.defvjp(fwd, bwd)
    return op(a, b, slot)


def _sum_nt_call(name, gs, ws, out_dtype):
    m = gs[0].shape[0]
    kdim = ws[0].shape[0]
    tm = _pick(m, 256, _SUB)
    n = len(gs)

    def body(*refs):
        acc = None
        for g_ref, w_ref in zip(refs[:n], refs[n:2 * n]):
            part = _dot(g_ref[...].astype(_MM_DTYPE), w_ref[...].astype(_MM_DTYPE), _NT)
            acc = part if acc is None else acc + part
        refs[2 * n][...] = acc.astype(out_dtype)

    in_specs = [pl.BlockSpec((tm, g.shape[1]), lambda i: (i, 0)) for g in gs]
    in_specs += [pl.BlockSpec(w.shape, lambda i: (0, 0)) for w in ws]
    return pl.pallas_call(
        body, name=name, grid=(m // tm,), in_specs=in_specs, out_specs=pl.BlockSpec((tm, kdim), lambda i: (i, 0)),
        out_shape=jax.ShapeDtypeStruct((m, kdim), out_dtype), compiler_params=_cparams(1),
    )(*gs, *ws)


def multi_mm(tag, names, a, ws, slots):
    @jax.custom_vjp
    def op(a, ws, slots):
        return tuple(_mm_call(tag + n, a, w, "nn", False, f32) for n, w in zip(names, ws))

    def fwd(a, ws, slots):
        return op(a, ws, slots), (a, ws)

    def bwd(res, gs):
        a, ws = res
        dws = [_mm_call(tag + n + "_db", a, g, "tn", False, f32) for n, g in zip(names, gs)]
        da = _sum_nt_call(tag + "da", list(gs), ws, a.dtype)
        return da, [jnp.zeros_like(w) for w in ws], dws

    op.defvjp(fwd, bwd)
    return op(a, list(ws), list(slots))


def _orders(order, n_x, n_o):
    if isinstance(order, str):
        return [order] * n_x, [order] * n_o
    return list(order[0]), list(order[1])


def _row(o, n):
    return lambda i: (_blk(o, i, n), 0)


def _blocked_fwd(name, f, order, rb, params, xs, out_sds, carry_sds):
    t = xs[0].shape[0]
    n = t // rb
    n_p, n_x, n_o, n_c = len(params), len(xs), len(out_sds), len(carry_sds)
    xo, oo = _orders(order, n_x, n_o)

    def body(*refs):
        p_refs = refs[:n_p]
        x_refs = refs[n_p:n_p + n_x]
        o_refs = refs[n_p + n_x:n_p + n_x + n_o]
        st_refs = refs[n_p + n_x + n_o:n_p + n_x + n_o + n_c]
        c_refs = refs[n_p + n_x + n_o + n_c:]
        i = pl.program_id(0)
        blk = _blk(xo[0], i, n)
        p = [r[...] for r in p_refs]
        x = [r[...] for r in x_refs]
        if n_c:
            @pl.when(i == 0)
            def _():
                for c in c_refs:
                    c[...] = jnp.zeros_like(c)
            c_in = [c[...] for c in c_refs]
            for sr, c in zip(st_refs, c_in):
                sr[0] = c
            c_out, ys = f(blk, p, c_in, x)
            for c, v in zip(c_refs, c_out):
                c[...] = v
        else:
            ys = f(blk, p, x)
        for o, y in zip(o_refs, ys):
            o[...] = y.astype(o.dtype)

    in_specs = [pl.BlockSpec(p.shape, lambda i, nd=p.ndim: (0,) * nd) for p in params]
    in_specs += [pl.BlockSpec((rb, x.shape[1]), _row(o, n)) for x, o in zip(xs, xo)]
    out_specs = [pl.BlockSpec((rb, c), _row(o, n)) for (c, _), o in zip(out_sds, oo)]
    out_specs += [pl.BlockSpec((1,) + s, lambda i: (i, 0, 0)) for s in carry_sds]
    out_shape = [jax.ShapeDtypeStruct((t, c), d) for c, d in out_sds]
    out_shape += [jax.ShapeDtypeStruct((n,) + s, f32) for s in carry_sds]
    res = pl.pallas_call(
        body, name=name, grid=(n,), in_specs=in_specs, out_specs=out_specs, out_shape=out_shape,
        scratch_shapes=[pltpu.VMEM(s, f32) for s in carry_sds], compiler_params=_cparams(1),
    )(*params, *xs)
    return list(res[:n_o]), list(res[n_o:])


def _blocked_bwd(name, f, order, rb, params, xs, states, dys, carry_sds):
    t = xs[0].shape[0]
    n = t // rb
    n_p, n_x, n_o, n_c = len(params), len(xs), len(dys), len(carry_sds)
    xo, oo = _orders(order, n_x, n_o)
    xo, oo = [_REV[o] for o in xo], [_REV[o] for o in oo]

    def body(*refs):
        k = 0
        p_refs = refs[k:k + n_p]; k += n_p
        x_refs = refs[k:k + n_x]; k += n_x
        st_refs = refs[k:k + n_c]; k += n_c
        dy_refs = refs[k:k + n_o]; k += n_o
        dp_refs = refs[k:k + n_p]; k += n_p
        dx_refs = refs[k:k + n_x]; k += n_x
        dc_refs = refs[k:]
        i = pl.program_id(0)
        blk = _blk(xo[0], i, n)
        p = [r[...] for r in p_refs]
        x = [r[...] for r in x_refs]
        dy = [r[...] for r in dy_refs]
        if n_c:
            @pl.when(i == 0)
            def _():
                for c in dc_refs:
                    c[...] = jnp.zeros_like(c)
            c_in = [r[0] for r in st_refs]
            dc = [c[...] for c in dc_refs]
            _, vjp = jax.vjp(lambda p_, c_, x_: f(blk, p_, c_, x_), p, c_in, x)
            dp, dcin, dx = vjp((dc, dy))
            for c, v in zip(dc_refs, dcin):
                c[...] = v
        else:
            _, vjp = jax.vjp(lambda p_, x_: f(blk, p_, x_), p, x)
            dp, dx = vjp(dy)

        @pl.when(i == 0)
        def _():
            for r, v in zip(dp_refs, dp):
                r[...] = v

        @pl.when(i > 0)
        def _():
            for r, v in zip(dp_refs, dp):
                r[...] += v
        for r, v in zip(dx_refs, dx):
            r[...] = v.astype(r.dtype)

    in_specs = [pl.BlockSpec(p.shape, lambda i, nd=p.ndim: (0,) * nd) for p in params]
    in_specs += [pl.BlockSpec((rb, x.shape[1]), _row(o, n)) for x, o in zip(xs, xo)]
    in_specs += [pl.BlockSpec((1,) + s, lambda i: (n - 1 - i, 0, 0)) for s in carry_sds]
    in_specs += [pl.BlockSpec((rb, d.shape[1]), _row(o, n)) for d, o in zip(dys, oo)]
    out_specs = [pl.BlockSpec(p.shape, lambda i, nd=p.ndim: (0,) * nd) for p in params]
    out_specs += [pl.BlockSpec((rb, x.shape[1]), _row(o, n)) for x, o in zip(xs, xo)]
    out_shape = [jax.ShapeDtypeStruct(p.shape, f32) for p in params]
    out_shape += [jax.ShapeDtypeStruct(x.shape, x.dtype) for x in xs]
    res = pl.pallas_call(
        body, name=name + "_bwd", grid=(n,), in_specs=in_specs, out_specs=out_specs, out_shape=out_shape,
        scratch_shapes=[pltpu.VMEM(s, f32) for s in carry_sds], compiler_params=_cparams(1),
    )(*params, *xs, *states, *dys)
    return list(res[:n_p]), list(res[n_p:])


def blocked_op(name, f, params, xs, out_sds, order="asc", carry_sds=(), rb=RB):
    carry_sds = tuple(carry_sds)

    @jax.custom_vjp
    def op(params, xs):
        return tuple(_blocked_fwd(name, f, order, rb, params, xs, out_sds, carry_sds)[0])

    def fwd(params, xs):
        ys, states = _blocked_fwd(name, f, order, rb, params, xs, out_sds, carry_sds)
        return tuple(ys), (params, xs, states)

    def bwd(res, dys):
        params, xs, states = res
        dp, dx = _blocked_bwd(name, f, order, rb, params, xs, states, list(dys), carry_sds)
        return list(dp), list(dx)

    op.defvjp(fwd, bwd)
    return op(list(params), list(xs))


def _cscan_call(name, order, asc, a, xr, xi, sr=None, si=None):
    t = xr.shape[0]
    n = t // RB
    tile = xr.shape[1:]
    xspec = pl.BlockSpec((RB,) + tile, lambda i: (_blk(order, i, n), 0, 0))
    aspec = pl.BlockSpec(a.shape, lambda i: (0, 0, 0))
    plane = jax.ShapeDtypeStruct(xr.shape, f32)

    def rowidx(tt):
        return tt if asc else RB - 1 - tt

    if sr is None:
        def body(a_ref, xr_ref, xi_ref, sr_ref, si_ref, c_ref):
            i = pl.program_id(0)

            @pl.when(i == 0)
            def _():
                c_ref[...] = jnp.zeros_like(c_ref)
            ar = a_ref[0]
            ai = a_ref[1]
            a2r = ar * ar - ai * ai
            a2i = 2.0 * ar * ai

            def step(tt, carry):
                cr, ci = carry
                r1 = rowidx(2 * tt)
                r2 = rowidx(2 * tt + 1)
                x1r, x1i, x2r, x2i = xr_ref[r1], xi_ref[r1], xr_ref[r2], xi_ref[r2]
                s1r = ar * cr - ai * ci + x1r
                s1i = ar * ci + ai * cr + x1i
                kr = ar * x1r - ai * x1i + x2r
                ki = ar * x1i + ai * x1r + x2i
                s2r = a2r * cr - a2i * ci + kr
                s2i = a2r * ci + a2i * cr + ki
                sr_ref[r1] = s1r
                si_ref[r1] = s1i
                sr_ref[r2] = s2r
                si_ref[r2] = s2i
                return s2r, s2i
            cr, ci = lax.fori_loop(0, RB // 2, step, (c_ref[0], c_ref[1]), unroll=4)
            c_ref[0] = cr
            c_ref[1] = ci

        return pl.pallas_call(
            body, name=name, grid=(n,), in_specs=[aspec, xspec, xspec], out_specs=[xspec, xspec],
            out_shape=[plane, plane], scratch_shapes=[pltpu.VMEM(a.shape, f32)], compiler_params=_cparams(1),
        )(a, xr, xi)

    def body(a_ref, xr_ref, xi_ref, sr_ref, si_ref, gr_ref, gi_ref, da_ref, c_ref):
        i = pl.program_id(0)

        @pl.when(i == 0)
        def _():
            c_ref[...] = jnp.zeros_like(c_ref)
            da_ref[...] = jnp.zeros_like(da_ref)
        ar = a_ref[0]
        ai = a_ref[1]
        a2r = ar * ar - ai * ai
        a2i = 2.0 * ar * ai

        def step(tt, carry):
            gr, gi, dar, dai, dbr, dbi = carry
            r1 = rowidx(2 * tt)
            r2 = rowidx(2 * tt + 1)
            x1r, x1i, x2r, x2i = xr_ref[r1], xi_ref[r1], xr_ref[r2], xi_ref[r2]
            v1r, v1i, v2r, v2i = sr_ref[r1], si_ref[r1], sr_ref[r2], si_ref[r2]
            g1r = x1r + ar * gr + ai * gi
            g1i = x1i + ar * gi - ai * gr
            kr = x2r + ar * x1r + ai * x1i
            ki = x2i + ar * x1i - ai * x1r
            g2r = kr + a2r * gr + a2i * gi
            g2i = ki + a2r * gi - a2i * gr
            dar = dar + gr * v1r + gi * v1i
            dai = dai + gi * v1r - gr * v1i
            dbr = dbr + g1r * v2r + g1i * v2i
            dbi = dbi + g1i * v2r - g1r * v2i
            gr_ref[r1] = g1r
            gi_ref[r1] = g1i
            gr_ref[r2] = g2r
            gi_ref[r2] = g2i
            return g2r, g2i, dar, dai, dbr, dbi
        z = jnp.zeros(tile, f32)
        gr, gi, dar, dai, dbr, dbi = lax.fori_loop(0, RB // 2, step, (c_ref[0], c_ref[1], z, z, z, z), unroll=4)
        c_ref[0] = gr
        c_ref[1] = gi
        da_ref[0] += dar + dbr
        da_ref[1] += dai + dbi

    return pl.pallas_call(
        body, name=name, grid=(n,), in_specs=[aspec] + [xspec] * 4, out_specs=[xspec, xspec, aspec],
        out_shape=[plane, plane, jax.ShapeDtypeStruct(a.shape, f32)],
        scratch_shapes=[pltpu.VMEM(a.shape, f32)], compiler_params=_cparams(1),
    )(a, xr, xi, sr, si)


def s5_states(tag, u, bd, a0, a1):
    t = u.shape[0]
    tile = a0.shape[1:]

    def run(u, bd, a0, a1):
        bu = [b.reshape((t,) + tile) for b in _bd_call(tag + "s5_bu", [u], [bd], "nn")]
        s0 = _cscan_call(tag + "s5_scan0", "asc", True, a0, bu[0], bu[1])
        s1 = _cscan_call(tag + "s5_scan1", "d1", False, a1, bu[0], bu[1])
        return (s0[0], s0[1], s1[0], s1[1])

    @jax.custom_vjp
    def op(u, bd, a0, a1):
        return run(u, bd, a0, a1)

    def fwd(u, bd, a0, a1):
        s = run(u, bd, a0, a1)
        return s, (u, bd, a0, a1, s)

    def bwd(res, ds):
        u, bd, a0, a1, s = res
        g0r, g0i, da0 = _cscan_call(tag + "s5_scan0_bwd", "desc", False, a0, ds[0], ds[1], s[0], s[1])
        g1r, g1i, da1 = _cscan_call(tag + "s5_scan1_bwd", "d1r", True, a1, ds[2], ds[3], s[2], s[3])
        gs = [g.reshape(t, -1) for g in (g0r, g0i, g1r, g1i)]
        du, = _bd_call(tag + "s5_bu_da", gs, [[bd[0], bd[1], bd[0], bd[1]]], "nt")
        k = bd[0].shape[0]
        dbd = [_bd_call(tag + "s5_bu_db%d" % j, u, gs[j], "tn", k) + _bd_call(tag + "s5_bu_db%d" % (j + 2), u, gs[j + 2], "tn", k)
               for j in range(2)]
        return du, dbd, da0, da1

    op.defvjp(fwd, bwd)
    return op(u, list(bd), a0, a1)


def _bd_call(name, a, b, mode, k=None):
    if mode == "tn":
        t = a.shape[0]
        ck, cn = a.shape[1] // k, b.shape[1] // k

        def body(a_ref, b_ref, o_ref):
            o_ref[0] = _dot(a_ref[...], b_ref[...], _TN, _MAP_PREC)

        return pl.pallas_call(
            body, name=name, grid=(k,),
            in_specs=[pl.BlockSpec((t, ck), lambda j: (0, j)), pl.BlockSpec((t, cn), lambda j: (0, j))],
            out_specs=pl.BlockSpec((1, ck, cn), lambda j: (j, 0, 0)),
            out_shape=jax.ShapeDtypeStruct((k, ck, cn), f32), compiler_params=_cparams(1),
        )(a, b)
    k, ck, cn = b[0][0].shape
    n_i, n_o = len(b), len(b[0])
    n_x = len(a)
    t = a[0].shape[0]
    tm = _pick(t, 1088, _SUB)
    flat = [w for row in b for w in row]
    win, wout, n_out, dn = (ck, cn, n_o, _NN) if mode == "nn" else (cn, ck, n_i, _NT)

    def body(*refs):
        xv = [r[...] for r in refs[:n_x]]
        w_refs = refs[n_x:n_x + len(flat)]
        o_refs = refs[n_x + len(flat):]
        for q in range(n_out):
            acc = None
            for s in range(n_x):
                w = w_refs[s * n_o + q] if mode == "nn" else w_refs[q * n_o + s]
                part = _dot(xv[s], w[0], dn, _MAP_PREC)
                acc = part if acc is None else acc + part
            o_refs[q][...] = acc

    return pl.pallas_call(
        body, name=name, grid=(t // tm, k),
        in_specs=[pl.BlockSpec((tm, win), lambda i, j: (i, j))] * n_x
        + [pl.BlockSpec((1, ck, cn), lambda i, j: (j, 0, 0))] * len(flat),
        out_specs=[pl.BlockSpec((tm, wout), lambda i, j: (i, j))] * n_out,
        out_shape=[jax.ShapeDtypeStruct((t, k * wout), f32)] * n_out, compiler_params=_cparams(2),
    )(*a, *flat)


def bd_mm(name, xs, ws):
    k = ws[0][0].shape[0]

    @jax.custom_vjp
    def op(xs, ws):
        return tuple(_bd_call(name, xs, ws, "nn"))

    def fwd(xs, ws):
        return op(xs, ws), (xs, ws)

    def bwd(res, gs):
        xs, ws = res
        dws = [[_bd_call(name + "_db%d%d" % (i, o), x, g, "tn", k) for o, g in enumerate(gs)] for i, x in enumerate(xs)]
        return list(_bd_call(name + "_da", list(gs), ws, "nt")), dws

    op.defvjp(fwd, bwd)
    return op(list(xs), [list(row) for row in ws])


def _rscan_call(name, order, asc, a, x, hp=None):
    t = x.shape[0]
    n = t // RB
    cshape = (1, x.shape[1])
    xspec = pl.BlockSpec((RB, x.shape[1]), lambda i: (_blk(order, i, n), 0))

    def rd(ref, r):
        return ref[pl.ds(r, 1), :]

    def wr(ref, r, v):
        ref[pl.ds(r, 1), :] = v

    def rowidx(tt):
        return tt if asc else RB - 1 - tt

    if hp is None:
        def body(a_ref, x_ref, h_ref, hp_ref, c_ref):
            i = pl.program_id(0)

            @pl.when(i == 0)
            def _():
                c_ref[...] = jnp.zeros_like(c_ref)

            def step(tt, h):
                r1 = rowidx(2 * tt)
                r2 = rowidx(2 * tt + 1)
                a1, a2, x1, x2 = rd(a_ref, r1), rd(a_ref, r2), rd(x_ref, r1), rd(x_ref, r2)
                h1 = a1 * h + x1
                h2 = (a2 * a1) * h + (a2 * x1 + x2)
                wr(hp_ref, r1, h)
                wr(h_ref, r1, h1)
                wr(hp_ref, r2, h1)
                wr(h_ref, r2, h2)
                return h2
            c_ref[...] = lax.fori_loop(0, RB // 2, step, c_ref[...], unroll=4)

        return pl.pallas_call(
            body, name=name, grid=(n,), in_specs=[xspec, xspec], out_specs=[xspec, xspec],
            out_shape=[jax.ShapeDtypeStruct(x.shape, f32)] * 2, scratch_shapes=[pltpu.VMEM(cshape, f32)],
            compiler_params=_cparams(1),
        )(a, x)

    def body(a_ref, x_ref, hp_ref, da_ref, db_ref, c_ref):
        i = pl.program_id(0)

        @pl.when(i == 0)
        def _():
            c_ref[...] = jnp.zeros_like(c_ref)

        def step(tt, c):
            r1 = rowidx(2 * tt)
            r2 = rowidx(2 * tt + 1)
            a1, a2, x1, x2 = rd(a_ref, r1), rd(a_ref, r2), rd(x_ref, r1), rd(x_ref, r2)
            g1 = x1 + c
            k = x2 + a1 * x1
            g2 = k + a1 * c
            wr(db_ref, r1, g1)
            wr(da_ref, r1, g1 * rd(hp_ref, r1))
            wr(db_ref, r2, g2)
            wr(da_ref, r2, g2 * rd(hp_ref, r2))
            return a2 * k + (a2 * a1) * c
        c_ref[...] = lax.fori_loop(0, RB // 2, step, c_ref[...], unroll=4)

    return pl.pallas_call(
        body, name=name, grid=(n,), in_specs=[xspec, xspec, xspec], out_specs=[xspec, xspec],
        out_shape=[jax.ShapeDtypeStruct(x.shape, f32)] * 2, scratch_shapes=[pltpu.VMEM(cshape, f32)],
        compiler_params=_cparams(1),
    )(a, x, hp)


def rscan(name, d, a, x):
    order = "d1" if d else "asc"

    @jax.custom_vjp
    def op(a, x):
        return _rscan_call(name, order, d == 0, a, x)[0]

    def fwd(a, x):
        h, hp = _rscan_call(name, order, d == 0, a, x)
        return h, (a, hp)

    def bwd(res, dh):
        a, hp = res
        da, db = _rscan_call(name + "_bwd", _REV[order], d != 0, a, dh, hp)
        return da, db

    op.defvjp(fwd, bwd)
    return op(a, x)


def _wide_rows(t):
    return _pick(t, 544, 16)


def _mod_part(blk, rows, mod, bm, lo, hi):
    is_ctx = blk * rows + lax.broadcasted_iota(jnp.int32, (rows, 1), 0) < RB
    r = mod[:, lo:hi] + bm[:, lo:hi]
    return jnp.where(is_ctx, r[1:2], r[0:1])


def _f_silu(blk, p, x):
    return [_silu(x[0]).astype(bf16)]


def _f_normmod(blk, p, x):
    nw, mod, bm = p
    rows, d = x[0].shape
    shift = _mod_part(blk, rows, mod, bm, 0, d)
    scale = _mod_part(blk, rows, mod, bm, d, 2 * d)
    return [(_rms(x[0], nw) * (1.0 + scale) + shift).astype(bf16)]


def _f_resid(blk, p, x):
    mod, bm = p
    rows, d = x[0].shape
    return [x[0] + _mod_part(blk, rows, mod, bm, 2 * d, 3 * d) * x[1]]


def _f_mix(blk, p, x):
    bg, = p
    gp = x[0]
    d = x[1].shape[1]
    acc = None
    for k in range(4):
        t = jax.nn.sigmoid(gp[:, k * d:(k + 1) * d] + bg[:, k * d:(k + 1) * d]) * x[1 + k]
        acc = t if acc is None else acc + t
    return [acc.astype(bf16)]


def _tri(rev):
    row = lax.broadcasted_iota(jnp.int32, (CHUNK, CHUNK), 0)
    col = lax.broadcasted_iota(jnp.int32, (CHUNK, CHUNK), 1)
    return (col >= row) if rev else (col <= row)


def _chunk_ids(rev):
    ids = list(range(RB // CHUNK))
    return ids[::-1] if rev else ids


def _f_hg(rev):
    def f(blk, p, c, x):
        lb, = p
        st, = c
        qi, fr = x
        q = _silu(qi[:, :BR_W])
        v = qi[:, BR_W:]
        fg = lb + (1.0 - lb) * jax.nn.sigmoid(fr)
        logf = jnp.log(fg)
        k = 1.0 - fg
        m = _tri(rev)
        mf = m.astype(f32)
        outs = [None] * (RB // CHUNK)
        for ci in _chunk_ids(rev):
            sl = slice(CHUNK * ci, CHUNK * ci + CHUNK)
            lf = logf[sl]
            b = _dot(mf, lf, hi=True)
            bend = jnp.sum(lf, axis=0, keepdims=True)
            mid = 0.5 * bend
            qe = q[sl] * jnp.exp(b - mid)
            ke = k[sl] * jnp.exp(mid - b)
            kd = k[sl] * jnp.exp(bend - b)
            qb = q[sl] * jnp.exp(b)
            dec = jnp.exp(bend)
            vc = v[sl]
            oh, ns = [], []
            for hh in range(HG_HEADS):
                cs = slice(HG_DK * hh, HG_DK * hh + HG_DK)
                sth = st[cs]
                att = jnp.where(m, _dot(qe[:, cs], ke[:, cs], _NT), 0.0)
                oh.append(_dot(att, vc[:, cs]) + _dot(qb[:, cs], sth, _NT))
                ns.append(sth * dec[:, cs] + _dot(vc[:, cs], kd[:, cs], _TN))
            st = jnp.concatenate(ns, axis=0)
            outs[ci] = jnp.concatenate(oh, axis=1)
        return [st], [jnp.concatenate(outs, axis=0)]
    return f


def _both(f0, f1, n_p, n_x):
    def f(blk, p, c, x):
        c0, y0 = f0(blk, p[:n_p], c[:1], x[:n_x])
        c1, y1 = f1(blk, p[n_p:], c[1:], x[n_x:])
        return c0 + c1, y0 + y1
    return f


_BOTH_ORDERS = (["asc", "asc", "d1", "d1"], ["asc", "d1"])


def _f_hg_final(blk, p, x):
    nw, = p
    o = x[0] + x[1]
    parts = []
    for hh in range(HG_HEADS):
        cs = slice(HG_DK * hh, HG_DK * hh + HG_DK)
        parts.append(_rms(o[:, cs], nw[:, cs]))
    return [(jnp.concatenate(parts, axis=1) * _silu(x[2])).astype(bf16)]


def _conv(x, cw, cb, blk):
    rows = x.shape[0]
    r = blk * rows + lax.broadcasted_iota(jnp.int32, (rows, 1), 0)
    rm = jnp.where(r < RB, r, r % CHUNK)
    seg = jnp.where(r < RB, RB, CHUNK)

    def vmask(o):
        return ((rm + o >= 0) & (rm + o < seg)).astype(f32)

    def shifted(o):
        @jax.custom_vjp
        def sh(x, mo, mn):
            return pltpu.roll(x, (-o) % rows, 0) * mo

        def fwd(x, mo, mn):
            return sh(x, mo, mn), (mo, mn)

        def bwd(res, g):
            mo, mn = res
            return pltpu.roll(g, o % rows, 0) * mn, jnp.zeros_like(mo), jnp.zeros_like(mn)
        sh.defvjp(fwd, bwd)
        return sh(x, vmask(o), vmask(-o))

    lo = (CONV_W - 1) // 2
    out = cb
    for k in range(CONV_W):
        o = k - lo
        out = out + cw[k:k + 1] * (x if o == 0 else shifted(o))
    return out


def _f_lru_a(blk, p, x):
    cw, cb, wg, gb, lam = p
    xc = _conv(x[0], cw, cb, blk)
    n_chunks = BR_W // _LANE
    xk = [xc[:, _LANE * k:_LANE * (k + 1)] for k in range(n_chunks)]

    def gate(j):
        pre = jnp.concatenate([_dot(xk[k], wg[j * n_chunks + k], hi=_MAP_PREC) for k in range(n_chunks)], axis=1)
        return jax.nn.sigmoid(pre + gb[:, BR_W * j:BR_W * (j + 1)])

    outs = []
    for d in range(2):
        r = gate(2 * d)
        ig = gate(2 * d + 1)
        log_a = -LRU_C * r * _softplus(-lam[d:d + 1])
        outs.append(jnp.exp(log_a))
        outs.append(jnp.sqrt(_one_minus_exp(2.0 * log_a)) * (ig * xc))
    return outs


def _f_lru_c(blk, p, x):
    return [((x[0] + x[1]) * _silu(x[2])).astype(bf16)]


def _f_s5_c1(blk, p, x):
    dsk, = p
    return [jax.nn.gelu(x[0] + dsk * x[1])]


def _f_s5_c2(blk, p, x):
    bglu, = p
    return [(x[0] * jax.nn.sigmoid(x[1] + bglu) * _silu(x[2])).astype(bf16)]


def _f_m2_a(blk, p, x):
    cw, cb, dtb = p
    return [_silu(_conv(x[0], cw, cb, blk)), _softplus(x[1] + dtb)]


def _f_ssd(d):
    rev = d == 1
    hpg = M2_HEADS // M2_GROUPS

    def f(blk, p, c, x):
        alog, = p
        st, = c
        xbc, dtp = x
        a = -jnp.exp(alog[:, M2_HEADS * d:M2_HEADS * (d + 1)])
        dt = dtp[:, M2_HEADS * d:M2_HEADS * (d + 1)]
        xs = xbc[:, :BR_W]
        bm = xbc[:, BR_W:BR_W + M2_GROUPS * M2_STATE]
        cm = xbc[:, BR_W + M2_GROUPS * M2_STATE:]
        gw = hpg * M2_HEADDIM
        mf = _tri(rev).astype(f32)
        row = lax.broadcasted_iota(jnp.int32, (CHUNK, gw), 0)
        col = lax.broadcasted_iota(jnp.int32, (CHUNK, gw), 1)
        m4 = (col % CHUNK >= row) if rev else (col % CHUNK <= row)
        spread = (lax.broadcasted_iota(jnp.int32, (hpg, gw), 0)
                  == lax.div(lax.broadcasted_iota(jnp.int32, (hpg, gw), 1), M2_HEADDIM)).astype(f32)
        own = [lax.div(lax.broadcasted_iota(jnp.int32, (1, gw), 1), M2_HEADDIM) == r for r in range(hpg)]
        outs = [None] * (RB // CHUNK)
        for ci in _chunk_ids(rev):
            sl = slice(CHUNK * ci, CHUNK * ci + CHUNK)
            dtc = dt[sl]
            dta = dtc * a
            cum = _dot(mf, dta, hi=True)
            cum_t = cum.T
            dt_t = dtc.T
            ys, ns = [], []
            for g in range(M2_GROUPS):
                hs = slice(hpg * g, hpg * (g + 1))
                bmg = bm[sl, M2_STATE * g:M2_STATE * (g + 1)]
                cmg = cm[sl, M2_STATE * g:M2_STATE * (g + 1)]
                xg = xs[sl, gw * g:gw * (g + 1)]
                stg = st[M2_STATE * g:M2_STATE * (g + 1)]
                cum_i = _dot(cum[:, hs], spread, hi=True)
                cum_j = jnp.concatenate([cum_t[hpg * g + r:hpg * g + r + 1] for r in range(hpg)], axis=1)
                dt_j = jnp.concatenate([dt_t[hpg * g + r:hpg * g + r + 1] for r in range(hpg)], axis=1)
                dt_i = _dot(dtc[:, hs], spread, hi=True)
                cend_g = jnp.sum(_dot(dta[:, hs], spread, hi=True), axis=0, keepdims=True)
                decay = jnp.exp(jnp.where(m4, cum_i - cum_j, -1e30))
                scores = _dot(cmg, jnp.concatenate([bmg] * hpg, axis=0), _NT)
                w = scores * decay * dt_j
                xdiag = jnp.concatenate([jnp.where(own[r], xg, 0.0) for r in range(hpg)], axis=0)
                ys.append(_dot(w, xdiag) + _dot(cmg, stg) * jnp.exp(cum_i))
                wx = jnp.exp(cend_g - cum_i) * dt_i * xg
                ns.append(jnp.exp(cend_g) * stg + _dot(bmg, wx, _TN))
            st = jnp.concatenate(ns, axis=0)
            outs[ci] = jnp.concatenate(ys, axis=1)
        return [st], [jnp.concatenate(outs, axis=0)]
    return f


def _f_m2_c(blk, p, x):
    dsk, nw = p
    y = x[0] + x[1] + dsk * x[2][:, :BR_W]
    return [_rms(y * _silu(x[3]), nw).astype(bf16)]


def _f_loss(blk, p, x):
    fnw, = p
    err = _rms(x[0], fnw) - x[1]
    return [0.5 * jnp.mean(err * err, axis=-1, keepdims=True)]


def _blockdiag(w):
    g, a, b = w.shape
    return jnp.einsum("gab,gh->gahb", w, jnp.eye(g, dtype=w.dtype)).reshape(g * a, g * b)


def _s5_params(l, w):
    a_scan, cds = [], []
    per = _LANE // S5_GROUP

    def chunks(m):
        return jnp.stack([_blockdiag(m[k * per:(k + 1) * per]) for k in range(S5_GROUPS // per)])

    b_re = jnp.transpose(w["s5_b_re"][l], (0, 2, 1))
    b_im = jnp.transpose(w["s5_b_im"][l], (0, 2, 1))
    bd = [chunks(b_re), chunks(b_im)]
    c_re = jnp.transpose(w["s5_c_re"][l], (0, 2, 1))
    c_im = jnp.transpose(w["s5_c_im"][l], (0, 2, 1))
    for d in range(2):
        lam_re = w["s5_a_re"][l, d]
        lam_im = w["s5_a_im"][l, d]
        step = jnp.exp(w["s5_log_step"][l, d])[:, None]
        mag = jnp.exp(lam_re * step)
        ab_re = mag * jnp.cos(lam_im * step)
        ab_im = mag * jnp.sin(lam_im * step)
        den = lam_re * lam_re + lam_im * lam_im
        nr = ab_re - 1.0
        co_re = (nr * lam_re + ab_im * lam_im) / den
        co_im = (ab_im * lam_re - nr * lam_im) / den
        n_state = S5_GROUPS * S5_STATE
        a_scan.append(jnp.stack([ab_re.reshape(_SUB, n_state // _SUB), ab_im.reshape(_SUB, n_state // _SUB)]))
        cp_re = c_re * co_re[:, :, None] - c_im * co_im[:, :, None]
        cp_im = c_re * co_im[:, :, None] + c_im * co_re[:, :, None]
        cds.append([chunks(cp_re), -chunks(cp_im)])
    return a_scan, bd, cds


def _lru_gate(l, w):
    gw = w["lru_gate_w"][l]
    per = _LANE // (BR_W // LRU_BLOCKS)
    chunks = [_blockdiag(gw[d, g, k * per:(k + 1) * per])
              for d in range(2) for g in range(2) for k in range(LRU_BLOCKS // per)]
    return jnp.stack(chunks), w["lru_gate_b"][l].reshape(1, -1)


def _pad_cols(a, n):
    return jnp.pad(a, ((0, 0), (0, n - a.shape[1])))


IN_SIZES = (BR_W,) * 9 + (M2_XBC, 2 * M2_HEADS, BR_W)
IN_OFFS = tuple(sum(IN_SIZES[:i]) for i in range(len(IN_SIZES) + 1))
IN_GROUPS = (("hg_qi", 0, 2, 1024), ("hg_ff", 2, 1, 512), ("hg_fb", 3, 1, 512), ("hg_z", 4, 1, 512),
             ("s5_u", 5, 1, 512), ("s5_z", 6, 1, 512), ("lru_x", 7, 1, 512), ("lru_z", 8, 1, 512),
             ("m2_xbc", 9, 1, 768), ("m2_dt", 10, 1, 128), ("m2_z", 11, 1, 512))


def _new_slots(big):
    slots = {n: [jnp.zeros(w.shape, f32) for w in ws] for n, ws in big.items() if n not in ("w_in", "w_gate", "w_branch")}
    n_layers, d_model = len(big["w_in"]), big["w_in"][0].shape[0]
    slots["w_in"] = [{name: jnp.zeros((d_model, width), f32) for name, _, _, width in IN_GROUPS} for _ in range(n_layers)]
    slots["w_gate"] = [jnp.zeros((d_model, 4 * d_model), f32) for _ in range(n_layers)]
    slots["w_branch"] = [[jnp.zeros(w.shape[1:], f32) for _ in range(w.shape[0])] for w in big["w_branch"]]
    return slots


def _slot_grads(g):
    out = {n: jnp.stack(v) for n, v in g.items() if n not in ("w_in", "w_gate", "w_branch")}
    out["w_branch"] = jnp.stack([jnp.stack(gl) for gl in g["w_branch"]])
    out["w_in"] = jnp.stack([
        jnp.concatenate([gl[name][:, :IN_OFFS[s0 + ns] - IN_OFFS[s0]] for name, s0, ns, _ in IN_GROUPS], axis=1)
        for gl in g["w_in"]])
    d_model = g["w_gate"][0].shape[0]
    out["w_gate"] = jnp.stack([jnp.transpose(gl.reshape(d_model, 4, d_model), (1, 0, 2)) for gl in g["w_gate"]])
    return out


def _forward(p, big, slots, x, ctx, c, target):
    n_layers = p["norm_w"].shape[0]
    d_model = x.shape[-1]
    xa = jnp.concatenate([ctx, x], axis=0)
    t = xa.shape[0]
    cc = jnp.concatenate([c, p["c_ctx"][None], jnp.zeros((_SUB - 2, d_model), f32)], axis=0)
    lb_all = jnp.cumsum(jax.nn.softmax(p["hg_lb_logits"], axis=0), axis=0)
    scc, = blocked_op("silu_c", _f_silu, [], [cc], [(d_model, bf16)], rb=_SUB)
    wide = _wide_rows(t)
    conv_rows = _pick(t, 1088, CHUNK)

    for l in range(n_layers):
        tag = "l%d_" % l
        mod = mm(tag + "mod", scc, big["w_mod"][l], slots["w_mod"][l])
        bm = p["b_mod"][l][None]
        h, = blocked_op(tag + "normmod", _f_normmod, [p["norm_w"][l][None], mod, bm], [xa], [(d_model, bf16)], rb=wide)
        gnames = [g[0] for g in IN_GROUPS]
        wvs = [_pad_cols(big["w_in"][l][:, IN_OFFS[s0]:IN_OFFS[s0 + ns]], width) for _, s0, ns, width in IN_GROUPS]
        u = dict(zip(gnames, multi_mm(tag + "in_", gnames, h, wvs, [slots["w_in"][l][g] for g in gnames])))

        o_dirs = blocked_op(tag + "hg", _both(_f_hg(False), _f_hg(True), 1, 2), [lb_all[l, 0][None], lb_all[l, 1][None]],
                            [u["hg_qi"], u["hg_ff"], u["hg_qi"], u["hg_fb"]], [(BR_W, f32)] * 2,
                            order=_BOTH_ORDERS, carry_sds=[(BR_W, HG_DK)] * 2)
        y_hg, = blocked_op(tag + "hg_fin", _f_hg_final, [p["hg_norm"][l][None]], list(o_dirs) + [u["hg_z"]], [(BR_W, bf16)],
                           rb=wide)

        a_scan, bd, cds = _s5_params(l, p)
        n_state = S5_GROUPS * S5_STATE
        planes = [s.reshape(t, n_state) for s in s5_states(tag, u["s5_u"], bd, a_scan[0], a_scan[1])]
        ysum, = bd_mm(tag + "s5_c", planes, [[cds[d][part]] for d in range(2) for part in range(2)])
        g5, = blocked_op(tag + "s5_c1", _f_s5_c1, [p["s5_d"][l][None]], [ysum, u["s5_u"]], [(BR_W, f32)], rb=wide)
        gl = mm(tag + "s5_glu", g5, big["s5_w_glu"][l], slots["s5_w_glu"][l])
        y_s5, = blocked_op(tag + "s5_c2", _f_s5_c2, [p["s5_b_glu"][l][None]], [g5, gl, u["s5_z"]], [(BR_W, bf16)],
                           rb=wide)

        wg, gb = _lru_gate(l, p)
        ab = blocked_op(tag + "lru_a", _f_lru_a,
                        [p["lru_conv_w"][l], p["lru_conv_b"][l][None], wg, gb, p["lru_lam"][l]],
                        [u["lru_x"]], [(BR_W, f32)] * 4)
        hs = []
        for d in range(2):
            hs.append(rscan(tag + "lru_scan%d" % d, d, ab[2 * d], ab[2 * d + 1]))
        y_lru, = blocked_op(tag + "lru_c", _f_lru_c, [], hs + [u["lru_z"]], [(BR_W, bf16)], rb=wide)

        dtb = _pad_cols(p["m2_dt_bias"][l].reshape(1, -1), _LANE)
        xbc, dtp = blocked_op(tag + "m2_a", _f_m2_a, [p["m2_conv_w"][l], p["m2_conv_b"][l][None], dtb],
                              [u["m2_xbc"], u["m2_dt"]], [(M2_XBC, f32), (_LANE, f32)], rb=conv_rows)
        alog = _pad_cols(p["m2_a_log"][l].reshape(1, -1), _LANE)
        y_dirs = blocked_op(tag + "ssd", _both(_f_ssd(0), _f_ssd(1), 1, 2), [alog, alog], [xbc, dtp, xbc, dtp],
                            [(BR_W, f32)] * 2, order=_BOTH_ORDERS,
                            carry_sds=[(M2_GROUPS * M2_STATE, BR_W // M2_GROUPS)] * 2)
        dsk = jnp.repeat(p["m2_d"][l], M2_HEADDIM)[None]
        y_m2, = blocked_op(tag + "m2_c", _f_m2_c, [dsk, p["m2_norm"][l][None]], list(y_dirs) + [xbc, u["m2_z"]], [(BR_W, bf16)],
                           rb=wide)

        wg_all = jnp.transpose(big["w_gate"][l], (1, 0, 2)).reshape(d_model, 4 * d_model)
        gp = mm(tag + "gate", h, wg_all, slots["w_gate"][l], out_dtype=bf16)
        bs = [mm(tag + "br%d" % k, yk, big["w_branch"][l][k], slots["w_branch"][l][k], out_dtype=bf16)
              for k, yk in enumerate((y_hg, y_s5, y_lru, y_m2))]
        mix, = blocked_op(tag + "mix", _f_mix, [p["b_gate"][l].reshape(1, -1)], [gp] + bs, [(d_model, bf16)])
        o = mm(tag + "out", mix, big["w_out"][l], slots["w_out"][l])
        xa, = blocked_op(tag + "resid", _f_resid, [mod, bm], [xa, o], [(d_model, f32)], rb=wide)

    rl, = blocked_op("loss", _f_loss, [p["final_norm"][None]], [xa[ctx.shape[0]:], target], [(1, f32)],
                     rb=_wide_rows(target.shape[0]))
    return jnp.sum(rl)


_MESH = pl.DeviceIdType.MESH
_ANY = pl.BlockSpec(memory_space=pl.ANY)
W_PACK = 1024


def _place():
    x, y, c = lax.axis_index("x"), lax.axis_index("y"), lax.axis_index("c")
    chips = [(x, 1 - y), (1 - x, y), (1 - x, 1 - y)]
    return x, y, c, chips


def _rcopy(src, dst, ssem, rsem, k, to):
    return pltpu.make_async_remote_copy(src_ref=src, dst_ref=dst, send_sem=ssem.at[k], recv_sem=rsem.at[k],
                                        device_id=to, device_id_type=_MESH)


def gather_shards(xs):
    n = len(xs)

    def body(*refs):
        x_refs, o_refs = refs[:n], refs[n:2 * n]
        ssem, rsem, lsem = refs[2 * n:]
        x, y, c, chips = _place()
        j = 2 * x + y
        sib = (x, y, 1 - c)
        mine = [pltpu.make_async_copy(x_refs[a], o_refs[a].at[j], lsem.at[a]) for a in range(n)]
        for cp in mine:
            cp.start()
        first = [_rcopy(x_refs[a].at[c], o_refs[a].at[j, c], ssem, rsem, 6 * a + r, (*chips[r], c))
                 for r in range(3) for a in range(n)]
        for cp in first:
            cp.start()
        passed = []
        for r in range(3):
            jr = j ^ (r + 1)
            for a in range(n):
                _rcopy(x_refs[a].at[c], o_refs[a].at[jr, c], ssem, rsem, 6 * a + r, sib).wait_recv()
                cp = _rcopy(o_refs[a].at[jr, c], o_refs[a].at[jr, c], ssem, rsem, 6 * a + 3 + r, sib)
                cp.start()
                passed.append(cp)
        for r in range(3):
            jr = j ^ (r + 1)
            for a in range(n):
                _rcopy(x_refs[a].at[c], o_refs[a].at[jr, 1 - c], ssem, rsem, 6 * a + 3 + r, sib).wait_recv()
        for cp in first + passed:
            cp.wait_send()
        for cp in mine:
            cp.wait()

    return pl.pallas_call(
        body, name="gather_shards", out_shape=[jax.ShapeDtypeStruct((4,) + x.shape, x.dtype) for x in xs],
        in_specs=[_ANY] * n, out_specs=[_ANY] * n,
        scratch_shapes=[pltpu.SemaphoreType.DMA((6 * n,)), pltpu.SemaphoreType.DMA((6 * n,)), pltpu.SemaphoreType.DMA((n,))],
    )(*xs)


def sibling_halves(gs):
    n = len(gs)

    def body(*refs):
        g_refs, o_refs = refs[:n], refs[n:2 * n]
        ssem, rsem = refs[2 * n:]
        x, y, c, _ = _place()
        sib = (x, y, 1 - c)
        cps = [_rcopy(g_refs[a].at[k, 1 - c], o_refs[a].at[k], ssem, rsem, 4 * a + k, sib)
               for k in range(4) for a in range(n)]
        for cp in cps:
            cp.start()
        for cp in cps:
            cp.wait()

    return pl.pallas_call(
        body, name="sibling_halves", out_shape=[jax.ShapeDtypeStruct((4,) + g.shape[2:], g.dtype) for g in gs],
        in_specs=[_ANY] * n, out_specs=[_ANY] * n,
        scratch_shapes=[pltpu.SemaphoreType.DMA((4 * n,)), pltpu.SemaphoreType.DMA((4 * n,))],
    )(*gs)


def scatter_chips(ps):
    n = len(ps)

    def body(*refs):
        p_refs, o_refs = refs[:n], refs[n:2 * n]
        ssem, rsem = refs[2 * n:]
        x, y, c, chips = _place()
        j = 2 * x + y
        cps = [_rcopy(p_refs[a].at[j ^ (r + 1)], o_refs[a].at[r], ssem, rsem, 3 * a + r, (*chips[r], c))
               for r in range(3) for a in range(n)]
        for cp in cps:
            cp.start()
        for cp in cps:
            cp.wait()

    return pl.pallas_call(
        body, name="scatter_chips", out_shape=[jax.ShapeDtypeStruct((3,) + p.shape[1:], p.dtype) for p in ps],
        in_specs=[_ANY] * n, out_specs=[_ANY] * n,
        scratch_shapes=[pltpu.SemaphoreType.DMA((3 * n,)), pltpu.SemaphoreType.DMA((3 * n,))],
    )(*ps)


def join_halves(qs):
    n = len(qs)

    def body(*refs):
        o_refs = refs[n:2 * n]
        ssem, rsem = refs[2 * n:]
        x, y, c, _ = _place()
        sib = (x, y, 1 - c)
        cps = [_rcopy(o_refs[a].at[c], o_refs[a].at[c], ssem, rsem, a, sib) for a in range(n)]
        for cp in cps:
            cp.start()
        for a in range(n):
            _rcopy(o_refs[a].at[c], o_refs[a].at[1 - c], ssem, rsem, a, sib).wait_recv()
        for cp in cps:
            cp.wait_send()

    return pl.pallas_call(
        body, name="join_halves", out_shape=[jax.ShapeDtypeStruct(q.shape, q.dtype) for q in qs],
        in_specs=[_ANY] * n, out_specs=[_ANY] * n, input_output_aliases={a: a for a in range(n)},
        scratch_shapes=[pltpu.SemaphoreType.DMA((n,)), pltpu.SemaphoreType.DMA((n,))],
    )(*qs)


def _rows_block(r):
    return _pick(r, 256, _SUB)


def add_sibling(tag, g, r1, place, out_dtype):
    _, _, rows, w = g.shape
    rb = _rows_block(rows)

    def body(pl_ref, g_ref, r_ref, o_ref):
        o_ref[...] = (g_ref[0] + r_ref[...]).astype(out_dtype)

    return pl.pallas_call(
        body, name="add_sibling_" + tag, out_shape=jax.ShapeDtypeStruct((4, rows, w), out_dtype),
        grid_spec=pltpu.PrefetchScalarGridSpec(
            num_scalar_prefetch=1, grid=(4, rows // rb),
            in_specs=[pl.BlockSpec((1, 1, rb, w), lambda k, i, s: (k, s[1], i, 0)),
                      pl.BlockSpec((1, rb, w), lambda k, i, s: (k, i, 0))],
            out_specs=pl.BlockSpec((1, rb, w), lambda k, i, s: (k, i, 0))),
        compiler_params=_cparams(2),
    )(place, g, r1)


def add_chips(tag, p, r2, place):
    _, rows, w = p.shape
    rb = _rows_block(rows)

    def body(pl_ref, p_ref, r_ref, o_ref):
        j = pl_ref[0]
        own = p_ref[0].astype(f32)
        others = [r_ref[0].astype(f32), r_ref[1].astype(f32), r_ref[2].astype(f32)]
        acc = None
        for k in range(4):
            rel = k ^ j
            t = jnp.where(rel == 0, own, jnp.where(rel == 1, others[0], jnp.where(rel == 2, others[1], others[2])))
            acc = t if acc is None else acc + t
        o_ref[0] = acc

    return pl.pallas_call(
        body, name="add_chips_" + tag, out_shape=jax.ShapeDtypeStruct((2, rows, w), f32),
        grid_spec=pltpu.PrefetchScalarGridSpec(
            num_scalar_prefetch=1, grid=(rows // rb,),
            in_specs=[pl.BlockSpec((1, rb, w), lambda i, s: (s[0], i, 0)),
                      pl.BlockSpec((3, rb, w), lambda i, s: (0, i, 0))],
            out_specs=pl.BlockSpec((1, rb, w), lambda i, s: (s[1], i, 0))),
        compiler_params=_cparams(1),
    )(place, p, r2)


def adamw(tag, g, w, m, v):
    rows, wd = g.shape
    rb = _rows_block(rows)

    def body(g_ref, w_ref, m_ref, v_ref, d_ref, nm_ref, nv_ref):
        gv = g_ref[...]
        nm = ADAM_B1 * m_ref[...] + (1.0 - ADAM_B1) * gv
        nv = ADAM_B2 * v_ref[...] + (1.0 - ADAM_B2) * (gv * gv)
        m_hat = nm / (1.0 - ADAM_B1 ** ADAM_STEP)
        v_hat = nv / (1.0 - ADAM_B2 ** ADAM_STEP)
        d_ref[...] = -ADAM_LR * (m_hat / (jnp.sqrt(v_hat) + ADAM_EPS) + ADAM_WD * w_ref[...])
        nm_ref[...] = nm
        nv_ref[...] = nv

    spec = pl.BlockSpec((rb, wd), lambda i: (i, 0))
    return pl.pallas_call(
        body, name="adamw_" + tag, grid=(rows // rb,), in_specs=[spec] * 4, out_specs=[spec] * 3,
        out_shape=[jax.ShapeDtypeStruct(g.shape, f32)] * 3, compiler_params=_cparams(1),
    )(g, w, m, v)


WEIGHTS = ("c_ctx", "norm_w", "w_mod", "b_mod", "w_in", "hg_lb_logits", "hg_norm", "s5_a_re", "s5_a_im", "s5_log_step",
           "s5_b_re", "s5_b_im", "s5_c_re", "s5_c_im", "s5_d", "s5_w_glu", "s5_b_glu", "lru_conv_w", "lru_conv_b",
           "lru_gate_w", "lru_gate_b", "lru_lam", "m2_conv_w", "m2_conv_b", "m2_dt_bias", "m2_a_log", "m2_d", "m2_norm",
           "w_branch", "w_gate", "b_gate", "w_out", "final_norm")
SHARD_AXIS = {"w_mod": 2, "w_in": 2, "hg_lb_logits": 2, "s5_w_glu": 1, "lru_conv_w": 2, "lru_lam": 2, "m2_conv_w": 2,
              "w_branch": 3, "w_gate": 2, "b_gate": 2, "w_out": 1}
BIG = ("w_mod", "w_in", "s5_w_glu", "w_branch", "w_gate", "w_out")
N_CHIPS = 4


def _to_rows(flat, row_unit):
    n = flat.shape[-1]
    per = 2 * row_unit * W_PACK
    total = -(-n // per) * per
    flat = jnp.pad(flat, [(0, 0)] * (flat.ndim - 1) + [(0, total - n)])
    return flat.reshape(flat.shape[:-1] + (2, total // (2 * W_PACK), W_PACK))


SMALL_SHARDED = tuple(n for n in WEIGHTS if n in SHARD_AXIS and n not in BIG)
SMALL_REPLICATED = tuple(n for n in WEIGHTS if n not in SHARD_AXIS)


def _chip_slices(a, axis):
    width = a.shape[axis] // N_CHIPS
    return jnp.stack([lax.slice_in_dim(a, k * width, (k + 1) * width, axis=axis) for k in range(N_CHIPS)])


def _gather_weights(local):
    small = jnp.concatenate([lax.bitcast_convert_type(local[n], bf16).reshape(-1) for n in SMALL_SHARDED])
    got = gather_shards([local[n].astype(bf16) for n in BIG] + [_to_rows(small, 16)])
    full = {}
    for n, g in zip(BIG, got):
        full[n] = [jnp.concatenate([g[j, l] for j in range(N_CHIPS)], axis=SHARD_AXIS[n] - 1) for l in range(g.shape[1])]
    flat, off = got[-1].reshape(N_CHIPS, -1), 0
    for n in SMALL_SHARDED:
        shp = local[n].shape
        size = 2 * math.prod(shp)
        part = lax.bitcast_convert_type(flat[:, off:off + size].reshape((N_CHIPS,) + shp + (2,)), f32)
        off += size
        full[n] = jnp.concatenate([part[j] for j in range(N_CHIPS)], axis=SHARD_AXIS[n])
    return full


def _whole_rows(v):
    n = v.shape[-1]
    return jnp.pad(v, [(0, 0)] * (v.ndim - 1) + [(0, -n % W_PACK)])


def _pack_small(vals, extra):
    return jnp.concatenate([_whole_rows(vals[n].reshape(-1)) for n in SMALL_SHARDED + SMALL_REPLICATED]
                           + [_whole_rows(extra.reshape(1))])


def _pack_small_grads(grads, loss):
    rep = [grads[n].reshape(-1) for n in SMALL_REPLICATED] + [loss.reshape(1)]
    sh = [_chip_slices(grads[n], SHARD_AXIS[n]).reshape(N_CHIPS, -1) for n in SMALL_SHARDED]
    return jnp.concatenate([_whole_rows(a) for a in sh]
                           + [_whole_rows(jnp.broadcast_to(r, (N_CHIPS,) + r.shape)) for r in rep], axis=1)


def _unpack_small(rows, like):
    out, r0 = {}, 0
    for n in SMALL_SHARDED + SMALL_REPLICATED:
        size = math.prod(like[n].shape)
        nr = -(-size // W_PACK)
        piece = lax.optimization_barrier(rows[r0:r0 + nr])
        out[n] = piece.reshape(-1)[:size].reshape(like[n].shape)
        r0 += nr
    return out, lax.optimization_barrier(rows[r0:r0 + 1])[0, 0]


def _reduce_grads(tags, gs):
    place = jnp.stack([2 * lax.axis_index("x") + lax.axis_index("y"), lax.axis_index("c")]).astype(jnp.int32)
    pairs = [add_sibling(t, g, r, place, bf16 if t in BIG else f32) for t, g, r in zip(tags, gs, sibling_halves(gs))]
    quads = [add_chips(t, p, r, place) for t, p, r in zip(tags, pairs, scatter_chips(pairs))]
    return join_halves(quads)


def kernel(x, c, ctx, c_ctx, norm_w, w_mod, b_mod, w_in, hg_lb_logits, hg_norm, s5_a_re, s5_a_im, s5_log_step, s5_b_re, s5_b_im, s5_c_re, s5_c_im, s5_d, s5_w_glu, s5_b_glu, lru_conv_w, lru_conv_b, lru_gate_w, lru_gate_b, lru_lam, m2_conv_w, m2_conv_b, m2_dt_bias, m2_a_log, m2_d, m2_norm, w_branch, w_gate, b_gate, w_out, final_norm, loss_target, m_c_ctx, m_norm_w, m_w_mod, m_b_mod, m_w_in, m_hg_lb_logits, m_hg_norm, m_s5_a_re, m_s5_a_im, m_s5_log_step, m_s5_b_re, m_s5_b_im, m_s5_c_re, m_s5_c_im, m_s5_d, m_s5_w_glu, m_s5_b_glu, m_lru_conv_w, m_lru_conv_b, m_lru_gate_w, m_lru_gate_b, m_lru_lam, m_m2_conv_w, m_m2_conv_b, m_m2_dt_bias, m_m2_a_log, m_m2_d, m_m2_norm, m_w_branch, m_w_gate, m_b_gate, m_w_out, m_final_norm, v_c_ctx, v_norm_w, v_w_mod, v_b_mod, v_w_in, v_hg_lb_logits, v_hg_norm, v_s5_a_re, v_s5_a_im, v_s5_log_step, v_s5_b_re, v_s5_b_im, v_s5_c_re, v_s5_c_im, v_s5_d, v_s5_w_glu, v_s5_b_glu, v_lru_conv_w, v_lru_conv_b, v_lru_gate_w, v_lru_gate_b, v_lru_lam, v_m2_conv_w, v_m2_conv_b, v_m2_dt_bias, v_m2_a_log, v_m2_d, v_m2_norm, v_w_branch, v_w_gate, v_b_gate, v_w_out, v_final_norm):
    given = dict(locals())
    w_loc = {n: given[n] for n in WEIGHTS}
    m_loc = {n: given["m_" + n] for n in WEIGHTS}
    v_loc = {n: given["v_" + n] for n in WEIGHTS}

    full = _gather_weights(w_loc)
    params = {n: (full[n] if n in SHARD_AXIS else w_loc[n]) for n in WEIGHTS if n not in BIG}
    big = {n: full[n] for n in BIG}
    def loss_fn(p, s, xx):
        return _forward(p, big, s, xx, ctx[0], c, loss_target[0])

    loss, (g_p, g_s, g_x) = jax.value_and_grad(loss_fn, argnums=(0, 1, 2))(params, _new_slots(big), x[0])
    grads = {**g_p, **_slot_grads(g_s)}

    def rows4(a):
        return a.reshape(a.shape[:2] + (-1, a.shape[-1]))

    g_big = [rows4(_chip_slices(grads[n], SHARD_AXIS[n])) for n in BIG]
    g_small = _to_rows(_pack_small_grads(grads, loss), 64)
    summed = _reduce_grads(list(BIG) + ["small"], g_big + [g_small])

    g_out, d_out, m_out, v_out = {}, {}, {}, {}
    for n, g in zip(BIG, summed):
        shp = w_loc[n].shape
        flat2 = lambda a: a.reshape(-1, shp[-1])
        g_out[n] = g.reshape(shp)
        d, nm, nv = adamw(n, flat2(g), flat2(w_loc[n]), flat2(m_loc[n]), flat2(v_loc[n]))
        d_out[n], m_out[n], v_out[n] = d.reshape(shp), nm.reshape(shp), nv.reshape(shp)
    zero = jnp.zeros((), f32)
    flat = lambda vals: _to_rows(_pack_small(vals, zero), 64).reshape(-1, W_PACK)
    gs = summed[-1].reshape(-1, W_PACK)
    d, nm, nv = adamw("small", gs, flat(w_loc), flat(m_loc), flat(v_loc))
    gsm, loss_out = _unpack_small(gs, w_loc)
    g_out.update(gsm)
    d_out.update(_unpack_small(d, w_loc)[0])
    m_out.update(_unpack_small(nm, w_loc)[0])
    v_out.update(_unpack_small(nv, w_loc)[0])
    outs = [loss_out, g_x[None]]
    for group in (g_out, d_out, m_out, v_out):
        outs += [group[n] for n in WEIGHTS]
    return tuple(outs)
```

```python
import functools
import math

import jax
import jax.numpy as jnp
from jax import lax
from jax.experimental import pallas as pl
from jax.experimental.pallas import tpu as pltpu

f32 = jnp.float32
bf16 = jnp.bfloat16
_MM_DTYPE = bf16
_HI = lax.Precision.HIGHEST
_MAP_PREC = lax.Precision.HIGH
_VMEM_LIMIT = 56 * 1024 * 1024
_LANE = 128
_SUB = 8

EPS = 1e-6
CONV_W = 4
CHUNK = 64
RB = 256
BR_W = 512
HG_HEADS = 4
HG_DK = 128
S5_GROUPS = 32
S5_GROUP = 16
S5_STATE = 64
LRU_BLOCKS = 8
LRU_C = 8.0
M2_HEADS = 8
M2_HEADDIM = 64
M2_GROUPS = 2
M2_STATE = 64
M2_XBC = BR_W + 2 * M2_GROUPS * M2_STATE
ADAM_LR = 0.001
ADAM_B1 = 0.9
ADAM_B2 = 0.999
ADAM_EPS = 1e-08
ADAM_WD = 0.01
ADAM_STEP = 10

_NN = (((1,), (0,)), ((), ()))
_NT = (((1,), (1,)), ((), ()))
_TN = (((0,), (0,)), ((), ()))


def _silu(x):
    return x * jax.nn.sigmoid(x)


def _softplus(x):
    return jnp.maximum(x, 0.0) + jnp.log1p(jnp.exp(-jnp.abs(x)))


def _one_minus_exp(z):
    series = -z * (1.0 + z * 0.5 * (1.0 + z * (1.0 / 3.0) * (1.0 + z * 0.25 * (1.0 + z * 0.2))))
    return jnp.where(z > -0.05, series, 1.0 - jnp.exp(z))


def _rms(x, w):
    return x * lax.rsqrt(jnp.mean(x * x, axis=-1, keepdims=True) + EPS) * w


def _dot(a, b, dn=_NN, hi=False):
    prec = hi if isinstance(hi, lax.Precision) else (_HI if hi else None)
    return lax.dot_general(a, b, dn, precision=prec, preferred_element_type=f32)


def _cparams(n_grid):
    return pltpu.CompilerParams(dimension_semantics=("arbitrary",) * n_grid, vmem_limit_bytes=_VMEM_LIMIT)


_REV = {"asc": "desc", "d1": "d1r", "desc": "asc", "d1r": "d1"}


def _blk(order, i, n):
    if order == "asc":
        return i
    if order == "desc":
        return n - 1 - i
    if order == "d1":
        return jnp.where(i == 0, 0, n - i)
    return jnp.where(i == n - 1, 0, i + 1)


def _pick(n, cap, unit):
    if n <= cap:
        return n
    best = None
    d = unit
    while d <= cap:
        if n % d == 0:
            best = d
        d += unit
    return n if best is None else best


def _mm_call(name, a, b, mode, hi, out_dtype):
    if mode == "tn":
        k, m = a.shape
        n = b.shape[1]
        tm = _pick(m, 512, _LANE)
        tn = _pick(n, 512, _LANE)
        a_spec = pl.BlockSpec((k, tm), lambda i, j: (0, i))
        b_spec = pl.BlockSpec((k, tn), lambda i, j: (0, j))
    else:
        m, k = a.shape
        tm = _pick(m, max(256, min(1088, 4 * 1024 * 1024 // (k * a.dtype.itemsize))), _SUB)
        a_spec = pl.BlockSpec((tm, k), lambda i, j: (i, 0))
        if mode == "nn":
            n = b.shape[1]
            tn = _pick(n, max(_LANE, (8 * 1024 * 1024 // (k * b.dtype.itemsize)) // _LANE * _LANE), _LANE)
            b_spec = pl.BlockSpec((k, tn), lambda i, j: (0, j))
        else:
            n = b.shape[0]
            tn = _pick(n, max(_LANE, (8 * 1024 * 1024 // (k * b.dtype.itemsize)) // _LANE * _LANE), _LANE)
            b_spec = pl.BlockSpec((tn, k), lambda i, j: (j, 0))
    dn = {"nn": _NN, "nt": _NT, "tn": _TN}[mode]

    def body(a_ref, b_ref, o_ref):
        av = a_ref[...]
        bv = b_ref[...]
        if hi:
            av = av.astype(f32)
            bv = bv.astype(f32)
        else:
            av = av.astype(_MM_DTYPE)
            bv = bv.astype(_MM_DTYPE)
        o_ref[...] = _dot(av, bv, dn, hi).astype(o_ref.dtype)

    return pl.pallas_call(
        body, name=name, grid=(m // tm, n // tn), in_specs=[a_spec, b_spec],
        out_specs=pl.BlockSpec((tm, tn), lambda i, j: (i, j)),
        out_shape=jax.ShapeDtypeStruct((m, n), out_dtype), compiler_params=_cparams(2),
    )(a, b)


def mm(name, a, b, slot=None, hi=False, out_dtype=f32):
    @jax.custom_vjp
    def op(a, b, slot):
        return _mm_call(name, a, b, "nn", hi, out_dtype)

    def fwd(a, b, slot):
        return op(a, b, slot), (a, b)

    def bwd(res, g):
        a, b = res
        da = _mm_call(name + "_da", g, b, "nt", hi, a.dtype)
        db = _mm_call(name + "_db", a, g, "tn", hi, f32)
        if slot is None:
            return da, db.astype(b.dtype), None
        return da, jnp.zeros_like(b), db

    op.defvjp(fwd, bwd)
    return op(a, b, slot)


def _sum_nt_call(name, gs, ws, out_dtype):
    m = gs[0].shape[0]
    kdim = ws[0].shape[0]
    tm = _pick(m, 256, _SUB)
    n = len(gs)

    def body(*refs):
        acc = None
        for g_ref, w_ref in zip(refs[:n], refs[n:2 * n]):
            part = _dot(g_ref[...].astype(_MM_DTYPE), w_ref[...].astype(_MM_DTYPE), _NT)
            acc = part if acc is None else acc + part
        refs[2 * n][...] = acc.astype(out_dtype)

    in_specs = [pl.BlockSpec((tm, g.shape[1]), lambda i: (i, 0)) for g in gs]
    in_specs += [pl.BlockSpec(w.shape, lambda i: (0, 0)) for w in ws]
    return pl.pallas_call(
        body, name=name, grid=(m // tm,), in_specs=in_specs, out_specs=pl.BlockSpec((tm, kdim), lambda i: (i, 0)),
        out_shape=jax.ShapeDtypeStruct((m, kdim), out_dtype), compiler_params=_cparams(1),
    )(*gs, *ws)


def multi_mm(tag, names, a, ws, slots):
    @jax.custom_vjp
    def op(a, ws, slots):
        return tuple(_mm_call(tag + n, a, w, "nn", False, f32) for n, w in zip(names, ws))

    def fwd(a, ws, slots):
        return op(a, ws, slots), (a, ws)

    def bwd(res, gs):
        a, ws = res
        dws = [_mm_call(tag + n + "_db", a, g, "tn", False, f32) for n, g in zip(names, gs)]
        da = _sum_nt_call(tag + "da", list(gs), ws, a.dtype)
        return da, [jnp.zeros_like(w) for w in ws], dws

    op.defvjp(fwd, bwd)
    return op(a, list(ws), list(slots))


def _orders(order, n_x, n_o):
    if isinstance(order, str):
        return [order] * n_x, [order] * n_o
    return list(order[0]), list(order[1])


def _row(o, n):
    return lambda i: (_blk(o, i, n), 0)


def _blocked_fwd(name, f, order, rb, params, xs, out_sds, carry_sds):
    t = xs[0].shape[0]
    n = t // rb
    n_p, n_x, n_o, n_c = len(params), len(xs), len(out_sds), len(carry_sds)
    xo, oo = _orders(order, n_x, n_o)

    def body(*refs):
        p_refs = refs[:n_p]
        x_refs = refs[n_p:n_p + n_x]
        o_refs = refs[n_p + n_x:n_p + n_x + n_o]
        st_refs = refs[n_p + n_x + n_o:n_p + n_x + n_o + n_c]
        c_refs = refs[n_p + n_x + n_o + n_c:]
        i = pl.program_id(0)
        blk = _blk(xo[0], i, n)
        p = [r[...] for r in p_refs]
        x = [r[...] for r in x_refs]
        if n_c:
            @pl.when(i == 0)
            def _():
                for c in c_refs:
                    c[...] = jnp.zeros_like(c)
            c_in = [c[...] for c in c_refs]
            for sr, c in zip(st_refs, c_in):
                sr[0] = c
            c_out, ys = f(blk, p, c_in, x)
            for c, v in zip(c_refs, c_out):
                c[...] = v
        else:
            ys = f(blk, p, x)
        for o, y in zip(o_refs, ys):
            o[...] = y.astype(o.dtype)

    in_specs = [pl.BlockSpec(p.shape, lambda i, nd=p.ndim: (0,) * nd) for p in params]
    in_specs += [pl.BlockSpec((rb, x.shape[1]), _row(o, n)) for x, o in zip(xs, xo)]
    out_specs = [pl.BlockSpec((rb, c), _row(o, n)) for (c, _), o in zip(out_sds, oo)]
    out_specs += [pl.BlockSpec((1,) + s, lambda i: (i, 0, 0)) for s in carry_sds]
    out_shape = [jax.ShapeDtypeStruct((t, c), d) for c, d in out_sds]
    out_shape += [jax.ShapeDtypeStruct((n,) + s, f32) for s in carry_sds]
    res = pl.pallas_call(
        body, name=name, grid=(n,), in_specs=in_specs, out_specs=out_specs, out_shape=out_shape,
        scratch_shapes=[pltpu.VMEM(s, f32) for s in carry_sds], compiler_params=_cparams(1),
    )(*params, *xs)
    return list(res[:n_o]), list(res[n_o:])


def _blocked_bwd(name, f, order, rb, params, xs, states, dys, carry_sds):
    t = xs[0].shape[0]
    n = t // rb
    n_p, n_x, n_o, n_c = len(params), len(xs), len(dys), len(carry_sds)
    xo, oo = _orders(order, n_x, n_o)
    xo, oo = [_REV[o] for o in xo], [_REV[o] for o in oo]

    def body(*refs):
        k = 0
        p_refs = refs[k:k + n_p]; k += n_p
        x_refs = refs[k:k + n_x]; k += n_x
        st_refs = refs[k:k + n_c]; k += n_c
        dy_refs = refs[k:k + n_o]; k += n_o
        dp_refs = refs[k:k + n_p]; k += n_p
        dx_refs = refs[k:k + n_x]; k += n_x
        dc_refs = refs[k:]
        i = pl.program_id(0)
        blk = _blk(xo[0], i, n)
        p = [r[...] for r in p_refs]
        x = [r[...] for r in x_refs]
        dy = [r[...] for r in dy_refs]
        if n_c:
            @pl.when(i == 0)
            def _():
                for c in dc_refs:
                    c[...] = jnp.zeros_like(c)
            c_in = [r[0] for r in st_refs]
            dc = [c[...] for c in dc_refs]
            _, vjp = jax.vjp(lambda p_, c_, x_: f(blk, p_, c_, x_), p, c_in, x)
            dp, dcin, dx = vjp((dc, dy))
            for c, v in zip(dc_refs, dcin):
                c[...] = v
        else:
            _, vjp = jax.vjp(lambda p_, x_: f(blk, p_, x_), p, x)
            dp, dx = vjp(dy)

        @pl.when(i == 0)
        def _():
            for r, v in zip(dp_refs, dp):
                r[...] = v

        @pl.when(i > 0)
        def _():
            for r, v in zip(dp_refs, dp):
                r[...] += v
        for r, v in zip(dx_refs, dx):
            r[...] = v.astype(r.dtype)

    in_specs = [pl.BlockSpec(p.shape, lambda i, nd=p.ndim: (0,) * nd) for p in params]
    in_specs += [pl.BlockSpec((rb, x.shape[1]), _row(o, n)) for x, o in zip(xs, xo)]
    in_specs += [pl.BlockSpec((1,) + s, lambda i: (n - 1 - i, 0, 0)) for s in carry_sds]
    in_specs += [pl.BlockSpec((rb, d.shape[1]), _row(o, n)) for d, o in zip(dys, oo)]
    out_specs = [pl.BlockSpec(p.shape, lambda i, nd=p.ndim: (0,) * nd) for p in params]
    out_specs += [pl.BlockSpec((rb, x.shape[1]), _row(o, n)) for x, o in zip(xs, xo)]
    out_shape = [jax.ShapeDtypeStruct(p.shape, f32) for p in params]
    out_shape += [jax.ShapeDtypeStruct(x.shape, x.dtype) for x in xs]
    res = pl.pallas_call(
        body, name=name + "_bwd", grid=(n,), in_specs=in_specs, out_specs=out_specs, out_shape=out_shape,
        scratch_shapes=[pltpu.VMEM(s, f32) for s in carry_sds], compiler_params=_cparams(1),
    )(*params, *xs, *states, *dys)
    return list(res[:n_p]), list(res[n_p:])


def blocked_op(name, f, params, xs, out_sds, order="asc", carry_sds=(), rb=RB):
    carry_sds = tuple(carry_sds)

    @jax.custom_vjp
    def op(params, xs):
        return tuple(_blocked_fwd(name, f, order, rb, params, xs, out_sds, carry_sds)[0])

    def fwd(params, xs):
        ys, states = _blocked_fwd(name, f, order, rb, params, xs, out_sds, carry_sds)
        return tuple(ys), (params, xs, states)

    def bwd(res, dys):
        params, xs, states = res
        dp, dx = _blocked_bwd(name, f, order, rb, params, xs, states, list(dys), carry_sds)
        return list(dp), list(dx)

    op.defvjp(fwd, bwd)
    return op(list(params), list(xs))


def _cscan_call(name, order, asc, a, xr, xi, sr=None, si=None):
    t = xr.shape[0]
    n = t // RB
    tile = xr.shape[1:]
    xspec = pl.BlockSpec((RB,) + tile, lambda i: (_blk(order, i, n), 0, 0))
    aspec = pl.BlockSpec(a.shape, lambda i: (0, 0, 0))
    plane = jax.ShapeDtypeStruct(xr.shape, f32)

    def rowidx(tt):
        return tt if asc else RB - 1 - tt

    if sr is None:
        def body(a_ref, xr_ref, xi_ref, sr_ref, si_ref, c_ref):
            i = pl.program_id(0)

            @pl.when(i == 0)
            def _():
                c_ref[...] = jnp.zeros_like(c_ref)
            ar = a_ref[0]
            ai = a_ref[1]
            a2r = ar * ar - ai * ai
            a2i = 2.0 * ar * ai

            def step(tt, carry):
                cr, ci = carry
                r1 = rowidx(2 * tt)
                r2 = rowidx(2 * tt + 1)
                x1r, x1i, x2r, x2i = xr_ref[r1], xi_ref[r1], xr_ref[r2], xi_ref[r2]
                s1r = ar * cr - ai * ci + x1r
                s1i = ar * ci + ai * cr + x1i
                kr = ar * x1r - ai * x1i + x2r
                ki = ar * x1i + ai * x1r + x2i
                s2r = a2r * cr - a2i * ci + kr
                s2i = a2r * ci + a2i * cr + ki
                sr_ref[r1] = s1r
                si_ref[r1] = s1i
                sr_ref[r2] = s2r
                si_ref[r2] = s2i
                return s2r, s2i
            cr, ci = lax.fori_loop(0, RB // 2, step, (c_ref[0], c_ref[1]), unroll=4)
            c_ref[0] = cr
            c_ref[1] = ci

        return pl.pallas_call(
            body, name=name, grid=(n,), in_specs=[aspec, xspec, xspec], out_specs=[xspec, xspec],
            out_shape=[plane, plane], scratch_shapes=[pltpu.VMEM(a.shape, f32)], compiler_params=_cparams(1),
        )(a, xr, xi)

    def body(a_ref, xr_ref, xi_ref, sr_ref, si_ref, gr_ref, gi_ref, da_ref, c_ref):
        i = pl.program_id(0)

        @pl.when(i == 0)
        def _():
            c_ref[...] = jnp.zeros_like(c_ref)
            da_ref[...] = jnp.zeros_like(da_ref)
        ar = a_ref[0]
        ai = a_ref[1]
        a2r = ar * ar - ai * ai
        a2i = 2.0 * ar * ai

        def step(tt, carry):
            gr, gi, dar, dai, dbr, dbi = carry
            r1 = rowidx(2 * tt)
            r2 = rowidx(2 * tt + 1)
            x1r, x1i, x2r, x2i = xr_ref[r1], xi_ref[r1], xr_ref[r2], xi_ref[r2]
            v1r, v1i, v2r, v2i = sr_ref[r1], si_ref[r1], sr_ref[r2], si_ref[r2]
            g1r = x1r + ar * gr + ai * gi
            g1i = x1i + ar * gi - ai * gr
            kr = x2r + ar * x1r + ai * x1i
            ki = x2i + ar * x1i - ai * x1r
            g2r = kr + a2r * gr + a2i * gi
            g2i = ki + a2r * gi - a2i * gr
            dar = dar + gr * v1r + gi * v1i
            dai = dai + gi * v1r - gr * v1i
            dbr = dbr + g1r * v2r + g1i * v2i
            dbi = dbi + g1i * v2r - g1r * v2i
            gr_ref[r1] = g1r
            gi_ref[r1] = g1i
            gr_ref[r2] = g2r
            gi_ref[r2] = g2i
            return g2r, g2i, dar, dai, dbr, dbi
        z = jnp.zeros(tile, f32)
        gr, gi, dar, dai, dbr, dbi = lax.fori_loop(0, RB // 2, step, (c_ref[0], c_ref[1], z, z, z, z), unroll=4)
        c_ref[0] = gr
        c_ref[1] = gi
        da_ref[0] += dar + dbr
        da_ref[1] += dai + dbi

    return pl.pallas_call(
        body, name=name, grid=(n,), in_specs=[aspec] + [xspec] * 4, out_specs=[xspec, xspec, aspec],
        out_shape=[plane, plane, jax.ShapeDtypeStruct(a.shape, f32)],
        scratch_shapes=[pltpu.VMEM(a.shape, f32)], compiler_params=_cparams(1),
    )(a, xr, xi, sr, si)


def s5_states(tag, u, bd, a0, a1):
    t = u.shape[0]
    tile = a0.shape[1:]

    def run(u, bd, a0, a1):
        bu = [b.reshape((t,) + tile) for b in _bd_call(tag + "s5_bu", [u], [bd], "nn")]
        s0 = _cscan_call(tag + "s5_scan0", "asc", True, a0, bu[0], bu[1])
        s1 = _cscan_call(tag + "s5_scan1", "d1", False, a1, bu[0], bu[1])
        return (s0[0], s0[1], s1[0], s1[1])

    @jax.custom_vjp
    def op(u, bd, a0, a1):
        return run(u, bd, a0, a1)

    def fwd(u, bd, a0, a1):
        s = run(u, bd, a0, a1)
        return s, (u, bd, a0, a1, s)

    def bwd(res, ds):
        u, bd, a0, a1, s = res
        g0r, g0i, da0 = _cscan_call(tag + "s5_scan0_bwd", "desc", False, a0, ds[0], ds[1], s[0], s[1])
        g1r, g1i, da1 = _cscan_call(tag + "s5_scan1_bwd", "d1r", True, a1, ds[2], ds[3], s[2], s[3])
        gs = [g.reshape(t, -1) for g in (g0r, g0i, g1r, g1i)]
        du, = _bd_call(tag + "s5_bu_da", gs, [[bd[0], bd[1], bd[0], bd[1]]], "nt")
        k = bd[0].shape[0]
        dbd = [_bd_call(tag + "s5_bu_db%d" % j, u, gs[j], "tn", k) + _bd_call(tag + "s5_bu_db%d" % (j + 2), u, gs[j + 2], "tn", k)
               for j in range(2)]
        return du, dbd, da0, da1

    op.defvjp(fwd, bwd)
    return op(u, list(bd), a0, a1)


def _bd_call(name, a, b, mode, k=None):
    if mode == "tn":
        t = a.shape[0]
        ck, cn = a.shape[1] // k, b.shape[1] // k

        def body(a_ref, b_ref, o_ref):
            o_ref[0] = _dot(a_ref[...], b_ref[...], _TN, _MAP_PREC)

        return pl.pallas_call(
            body, name=name, grid=(k,),
            in_specs=[pl.BlockSpec((t, ck), lambda j: (0, j)), pl.BlockSpec((t, cn), lambda j: (0, j))],
            out_specs=pl.BlockSpec((1, ck, cn), lambda j: (j, 0, 0)),
            out_shape=jax.ShapeDtypeStruct((k, ck, cn), f32), compiler_params=_cparams(1),
        )(a, b)
    k, ck, cn = b[0][0].shape
    n_i, n_o = len(b), len(b[0])
    n_x = len(a)
    t = a[0].shape[0]
    tm = _pick(t, 2176, _SUB)
    flat = [w for row in b for w in row]
    win, wout, n_out, dn = (ck, cn, n_o, _NN) if mode == "nn" else (cn, ck, n_i, _NT)

    def body(*refs):
        xv = [r[...] for r in refs[:n_x]]
        w_refs = refs[n_x:n_x + len(flat)]
        o_refs = refs[n_x + len(flat):]
        for q in range(n_out):
            acc = None
            for s in range(n_x):
                w = w_refs[s * n_o + q] if mode == "nn" else w_refs[q * n_o + s]
                part = _dot(xv[s], w[0], dn, _MAP_PREC)
                acc = part if acc is None else acc + part
            o_refs[q][...] = acc

    return pl.pallas_call(
        body, name=name, grid=(t // tm, k),
        in_specs=[pl.BlockSpec((tm, win), lambda i, j: (i, j))] * n_x
        + [pl.BlockSpec((1, ck, cn), lambda i, j: (j, 0, 0))] * len(flat),
        out_specs=[pl.BlockSpec((tm, wout), lambda i, j: (i, j))] * n_out,
        out_shape=[jax.ShapeDtypeStruct((t, k * wout), f32)] * n_out, compiler_params=_cparams(2),
    )(*a, *flat)


def bd_mm(name, xs, ws):
    k = ws[0][0].shape[0]

    @jax.custom_vjp
    def op(xs, ws):
        return tuple(_bd_call(name, xs, ws, "nn"))

    def fwd(xs, ws):
        return op(xs, ws), (xs, ws)

    def bwd(res, gs):
        xs, ws = res
        dws = [[_bd_call(name + "_db%d%d" % (i, o), x, g, "tn", k) for o, g in enumerate(gs)] for i, x in enumerate(xs)]
        return list(_bd_call(name + "_da", list(gs), ws, "nt")), dws

    op.defvjp(fwd, bwd)
    return op(list(xs), [list(row) for row in ws])


def _rscan_call(name, order, asc, a, x, hp=None):
    t = x.shape[0]
    n = t // RB
    cshape = (1, x.shape[1])
    xspec = pl.BlockSpec((RB, x.shape[1]), lambda i: (_blk(order, i, n), 0))

    def rd(ref, r):
        return ref[pl.ds(r, 1), :]

    def wr(ref, r, v):
        ref[pl.ds(r, 1), :] = v

    def rowidx(tt):
        return tt if asc else RB - 1 - tt

    if hp is None:
        def body(a_ref, x_ref, h_ref, hp_ref, c_ref):
            i = pl.program_id(0)

            @pl.when(i == 0)
            def _():
                c_ref[...] = jnp.zeros_like(c_ref)

            def step(tt, h):
                r1 = rowidx(2 * tt)
                r2 = rowidx(2 * tt + 1)
                a1, a2, x1, x2 = rd(a_ref, r1), rd(a_ref, r2), rd(x_ref, r1), rd(x_ref, r2)
                h1 = a1 * h + x1
                h2 = (a2 * a1) * h + (a2 * x1 + x2)
                wr(hp_ref, r1, h)
                wr(h_ref, r1, h1)
                wr(hp_ref, r2, h1)
                wr(h_ref, r2, h2)
                return h2
            c_ref[...] = lax.fori_loop(0, RB // 2, step, c_ref[...], unroll=4)

        return pl.pallas_call(
            body, name=name, grid=(n,), in_specs=[xspec, xspec], out_specs=[xspec, xspec],
            out_shape=[jax.ShapeDtypeStruct(x.shape, f32)] * 2, scratch_shapes=[pltpu.VMEM(cshape, f32)],
            compiler_params=_cparams(1),
        )(a, x)

    def body(a_ref, x_ref, hp_ref, da_ref, db_ref, c_ref):
        i = pl.program_id(0)

        @pl.when(i == 0)
        def _():
            c_ref[...] = jnp.zeros_like(c_ref)

        def step(tt, c):
            r1 = rowidx(2 * tt)
            r2 = rowidx(2 * tt + 1)
            a1, a2, x1, x2 = rd(a_ref, r1), rd(a_ref, r2), rd(x_ref, r1), rd(x_ref, r2)
            g1 = x1 + c
            k = x2 + a1 * x1
            g2 = k + a1 * c
            wr(db_ref, r1, g1)
            wr(da_ref, r1, g1 * rd(hp_ref, r1))
            wr(db_ref, r2, g2)
            wr(da_ref, r2, g2 * rd(hp_ref, r2))
            return a2 * k + (a2 * a1) * c
        c_ref[...] = lax.fori_loop(0, RB // 2, step, c_ref[...], unroll=4)

    return pl.pallas_call(
        body, name=name, grid=(n,), in_specs=[xspec, xspec, xspec], out_specs=[xspec, xspec],
        out_shape=[jax.ShapeDtypeStruct(x.shape, f32)] * 2, scratch_shapes=[pltpu.VMEM(cshape, f32)],
        compiler_params=_cparams(1),
    )(a, x, hp)


def rscan(name, d, a, x):
    order = "d1" if d else "asc"

    @jax.custom_vjp
    def op(a, x):
        return _rscan_call(name, order, d == 0, a, x)[0]

    def fwd(a, x):
        h, hp = _rscan_call(name, order, d == 0, a, x)
        return h, (a, hp)

    def bwd(res, dh):
        a, hp = res
        da, db = _rscan_call(name + "_bwd", _REV[order], d != 0, a, dh, hp)
        return da, db

    op.defvjp(fwd, bwd)
    return op(a, x)


def _wide_rows(t):
    return _pick(t, 544, 16)


def _mod_part(blk, rows, mod, bm, lo, hi):
    is_ctx = blk * rows + lax.broadcasted_iota(jnp.int32, (rows, 1), 0) < RB
    r = mod[:, lo:hi] + bm[:, lo:hi]
    return jnp.where(is_ctx, r[1:2], r[0:1])


def _f_silu(blk, p, x):
    return [_silu(x[0]).astype(bf16)]


def _f_normmod(blk, p, x):
    nw, mod, bm = p
    rows, d = x[0].shape
    shift = _mod_part(blk, rows, mod, bm, 0, d)
    scale = _mod_part(blk, rows, mod, bm, d, 2 * d)
    return [(_rms(x[0], nw) * (1.0 + scale) + shift).astype(bf16)]


def _f_resid(blk, p, x):
    mod, bm = p
    rows, d = x[0].shape
    return [x[0] + _mod_part(blk, rows, mod, bm, 2 * d, 3 * d) * x[1]]


def _f_mix(blk, p, x):
    bg, = p
    gp = x[0]
    d = x[1].shape[1]
    acc = None
    for k in range(4):
        t = jax.nn.sigmoid(gp[:, k * d:(k + 1) * d] + bg[:, k * d:(k + 1) * d]) * x[1 + k]
        acc = t if acc is None else acc + t
    return [acc.astype(bf16)]


def _tri(rev):
    row = lax.broadcasted_iota(jnp.int32, (CHUNK, CHUNK), 0)
    col = lax.broadcasted_iota(jnp.int32, (CHUNK, CHUNK), 1)
    return (col >= row) if rev else (col <= row)


def _chunk_ids(rev):
    ids = list(range(RB // CHUNK))
    return ids[::-1] if rev else ids


def _f_hg(rev):
    def f(blk, p, c, x):
        lb, = p
        st, = c
        qi, fr = x
        q = _silu(qi[:, :BR_W])
        v = qi[:, BR_W:]
        fg = lb + (1.0 - lb) * jax.nn.sigmoid(fr)
        logf = jnp.log(fg)
        k = 1.0 - fg
        m = _tri(rev)
        mf = m.astype(f32)
        outs = [None] * (RB // CHUNK)
        for ci in _chunk_ids(rev):
            sl = slice(CHUNK * ci, CHUNK * ci + CHUNK)
            lf = logf[sl]
            b = _dot(mf, lf, hi=True)
            bend = jnp.sum(lf, axis=0, keepdims=True)
            mid = 0.5 * bend
            qe = q[sl] * jnp.exp(b - mid)
            ke = k[sl] * jnp.exp(mid - b)
            kd = k[sl] * jnp.exp(bend - b)
            qb = q[sl] * jnp.exp(b)
            dec = jnp.exp(bend)
            vc = v[sl]
            oh, ns = [], []
            for hh in range(HG_HEADS):
                cs = slice(HG_DK * hh, HG_DK * hh + HG_DK)
                sth = st[cs]
                att = jnp.where(m, _dot(qe[:, cs], ke[:, cs], _NT), 0.0)
                oh.append(_dot(att, vc[:, cs]) + _dot(qb[:, cs], sth, _NT))
                ns.append(sth * dec[:, cs] + _dot(vc[:, cs], kd[:, cs], _TN))
            st = jnp.concatenate(ns, axis=0)
            outs[ci] = jnp.concatenate(oh, axis=1)
        return [st], [jnp.concatenate(outs, axis=0)]
    return f


def _both(f0, f1, n_p, n_x):
    def f(blk, p, c, x):
        c0, y0 = f0(blk, p[:n_p], c[:1], x[:n_x])
        c1, y1 = f1(blk, p[n_p:], c[1:], x[n_x:])
        return c0 + c1, y0 + y1
    return f


_BOTH_ORDERS = (["asc", "asc", "d1", "d1"], ["asc", "d1"])


def _f_hg_final(blk, p, x):
    nw, = p
    o = x[0] + x[1]
    parts = []
    for hh in range(HG_HEADS):
        cs = slice(HG_DK * hh, HG_DK * hh + HG_DK)
        parts.append(_rms(o[:, cs], nw[:, cs]))
    return [(jnp.concatenate(parts, axis=1) * _silu(x[2])).astype(bf16)]


def _conv(x, cw, cb, blk):
    rows = x.shape[0]
    r = lax.broadcasted_iota(jnp.int32, (rows, 1), 0)
    rm = jnp.where(blk == 0, r, r % CHUNK)
    seg = jnp.where(blk == 0, rows, CHUNK)

    def vmask(o):
        return ((rm + o >= 0) & (rm + o < seg)).astype(f32)

    def shifted(o):
        @jax.custom_vjp
        def sh(x, mo, mn):
            return pltpu.roll(x, (-o) % rows, 0) * mo

        def fwd(x, mo, mn):
            return sh(x, mo, mn), (mo, mn)

        def bwd(res, g):
            mo, mn = res
            return pltpu.roll(g, o % rows, 0) * mn, jnp.zeros_like(mo), jnp.zeros_like(mn)
        sh.defvjp(fwd, bwd)
        return sh(x, vmask(o), vmask(-o))

    lo = (CONV_W - 1) // 2
    out = cb
    for k in range(CONV_W):
        o = k - lo
        out = out + cw[k:k + 1] * (x if o == 0 else shifted(o))
    return out


def _f_lru_a(blk, p, x):
    cw, cb, wg, gb, lam = p
    xc = _conv(x[0], cw, cb, blk)
    n_chunks = BR_W // _LANE
    xk = [xc[:, _LANE * k:_LANE * (k + 1)] for k in range(n_chunks)]

    def gate(j):
        pre = jnp.concatenate([_dot(xk[k], wg[j * n_chunks + k], hi=_MAP_PREC) for k in range(n_chunks)], axis=1)
        return jax.nn.sigmoid(pre + gb[:, BR_W * j:BR_W * (j + 1)])

    outs = []
    for d in range(2):
        r = gate(2 * d)
        ig = gate(2 * d + 1)
        log_a = -LRU_C * r * _softplus(-lam[d:d + 1])
        outs.append(jnp.exp(log_a))
        outs.append(jnp.sqrt(_one_minus_exp(2.0 * log_a)) * (ig * xc))
    return outs


def _f_lru_c(blk, p, x):
    return [((x[0] + x[1]) * _silu(x[2])).astype(bf16)]


def _f_s5_c1(blk, p, x):
    dsk, = p
    return [jax.nn.gelu(x[0] + dsk * x[1])]


def _f_s5_c2(blk, p, x):
    bglu, = p
    return [(x[0] * jax.nn.sigmoid(x[1] + bglu) * _silu(x[2])).astype(bf16)]


def _f_m2_a(blk, p, x):
    cw, cb, dtb = p
    return [_silu(_conv(x[0], cw, cb, blk)), _softplus(x[1] + dtb)]


def _f_ssd(d):
    rev = d == 1
    hpg = M2_HEADS // M2_GROUPS

    def f(blk, p, c, x):
        alog, = p
        st, = c
        xbc, dtp = x
        a = -jnp.exp(alog[:, M2_HEADS * d:M2_HEADS * (d + 1)])
        dt = dtp[:, M2_HEADS * d:M2_HEADS * (d + 1)]
        xs = xbc[:, :BR_W]
        bm = xbc[:, BR_W:BR_W + M2_GROUPS * M2_STATE]
        cm = xbc[:, BR_W + M2_GROUPS * M2_STATE:]
        gw = hpg * M2_HEADDIM
        mf = _tri(rev).astype(f32)
        row = lax.broadcasted_iota(jnp.int32, (CHUNK, gw), 0)
        col = lax.broadcasted_iota(jnp.int32, (CHUNK, gw), 1)
        m4 = (col % CHUNK >= row) if rev else (col % CHUNK <= row)
        spread = (lax.broadcasted_iota(jnp.int32, (hpg, gw), 0)
                  == lax.div(lax.broadcasted_iota(jnp.int32, (hpg, gw), 1), M2_HEADDIM)).astype(f32)
        own = [lax.div(lax.broadcasted_iota(jnp.int32, (1, gw), 1), M2_HEADDIM) == r for r in range(hpg)]
        outs = [None] * (RB // CHUNK)
        for ci in _chunk_ids(rev):
            sl = slice(CHUNK * ci, CHUNK * ci + CHUNK)
            dtc = dt[sl]
            dta = dtc * a
            cum = _dot(mf, dta, hi=True)
            cum_t = cum.T
            dt_t = dtc.T
            ys, ns = [], []
            for g in range(M2_GROUPS):
                hs = slice(hpg * g, hpg * (g + 1))
                bmg = bm[sl, M2_STATE * g:M2_STATE * (g + 1)]
                cmg = cm[sl, M2_STATE * g:M2_STATE * (g + 1)]
                xg = xs[sl, gw * g:gw * (g + 1)]
                stg = st[M2_STATE * g:M2_STATE * (g + 1)]
                cum_i = _dot(cum[:, hs], spread, hi=True)
                cum_j = jnp.concatenate([cum_t[hpg * g + r:hpg * g + r + 1] for r in range(hpg)], axis=1)
                dt_j = jnp.concatenate([dt_t[hpg * g + r:hpg * g + r + 1] for r in range(hpg)], axis=1)
                dt_i = _dot(dtc[:, hs], spread, hi=True)
                cend_g = jnp.sum(_dot(dta[:, hs], spread, hi=True), axis=0, keepdims=True)
                decay = jnp.exp(jnp.where(m4, cum_i - cum_j, -1e30))
                scores = _dot(cmg, jnp.concatenate([bmg] * hpg, axis=0), _NT)
                w = scores * decay * dt_j
                xdiag = jnp.concatenate([jnp.where(own[r], xg, 0.0) for r in range(hpg)], axis=0)
                ys.append(_dot(w, xdiag) + _dot(cmg, stg) * jnp.exp(cum_i))
                wx = jnp.exp(cend_g - cum_i) * dt_i * xg
                ns.append(jnp.exp(cend_g) * stg + _dot(bmg, wx, _TN))
            st = jnp.concatenate(ns, axis=0)
            outs[ci] = jnp.concatenate(ys, axis=1)
        return [st], [jnp.concatenate(outs, axis=0)]
    return f


def _f_m2_c(blk, p, x):
    dsk, nw = p
    y = x[0] + x[1] + dsk * x[2][:, :BR_W]
    return [_rms(y * _silu(x[3]), nw).astype(bf16)]


def _f_loss(blk, p, x):
    fnw, = p
    err = _rms(x[0], fnw) - x[1]
    return [0.5 * jnp.mean(err * err, axis=-1, keepdims=True)]


def _blockdiag(w):
    g, a, b = w.shape
    return jnp.einsum("gab,gh->gahb", w, jnp.eye(g, dtype=w.dtype)).reshape(g * a, g * b)


def _s5_params(l, w):
    a_scan, cds = [], []
    per = _LANE // S5_GROUP

    def chunks(m):
        return jnp.stack([_blockdiag(m[k * per:(k + 1) * per]) for k in range(S5_GROUPS // per)])

    b_re = jnp.transpose(w["s5_b_re"][l], (0, 2, 1))
    b_im = jnp.transpose(w["s5_b_im"][l], (0, 2, 1))
    bd = [chunks(b_re), chunks(b_im)]
    c_re = jnp.transpose(w["s5_c_re"][l], (0, 2, 1))
    c_im = jnp.transpose(w["s5_c_im"][l], (0, 2, 1))
    for d in range(2):
        lam_re = w["s5_a_re"][l, d]
        lam_im = w["s5_a_im"][l, d]
        step = jnp.exp(w["s5_log_step"][l, d])[:, None]
        mag = jnp.exp(lam_re * step)
        ab_re = mag * jnp.cos(lam_im * step)
        ab_im = mag * jnp.sin(lam_im * step)
        den = lam_re * lam_re + lam_im * lam_im
        nr = ab_re - 1.0
        co_re = (nr * lam_re + ab_im * lam_im) / den
        co_im = (ab_im * lam_re - nr * lam_im) / den
        n_state = S5_GROUPS * S5_STATE
        a_scan.append(jnp.stack([ab_re.reshape(_SUB, n_state // _SUB), ab_im.reshape(_SUB, n_state // _SUB)]))
        cp_re = c_re * co_re[:, :, None] - c_im * co_im[:, :, None]
        cp_im = c_re * co_im[:, :, None] + c_im * co_re[:, :, None]
        cds.append([chunks(cp_re), -chunks(cp_im)])
    return a_scan, bd, cds


def _lru_gate(l, w):
    gw = w["lru_gate_w"][l]
    per = _LANE // (BR_W // LRU_BLOCKS)
    chunks = [_blockdiag(gw[d, g, k * per:(k + 1) * per])
              for d in range(2) for g in range(2) for k in range(LRU_BLOCKS // per)]
    return jnp.stack(chunks), w["lru_gate_b"][l].reshape(1, -1)


def _pad_cols(a, n):
    return jnp.pad(a, ((0, 0), (0, n - a.shape[1])))


IN_SIZES = (BR_W,) * 9 + (M2_XBC, 2 * M2_HEADS, BR_W)
IN_OFFS = tuple(sum(IN_SIZES[:i]) for i in range(len(IN_SIZES) + 1))
IN_GROUPS = (("hg_qi", 0, 2, 1024), ("hg_ff", 2, 1, 512), ("hg_fb", 3, 1, 512), ("hg_z", 4, 1, 512),
             ("s5_u", 5, 1, 512), ("s5_z", 6, 1, 512), ("lru_x", 7, 1, 512), ("lru_z", 8, 1, 512),
             ("m2_xbc", 9, 1, 768), ("m2_dt", 10, 1, 128), ("m2_z", 11, 1, 512))


def _new_slots(big):
    slots = {n: [jnp.zeros(w.shape, f32) for w in ws] for n, ws in big.items() if n not in ("w_in", "w_gate", "w_branch")}
    n_layers, d_model = len(big["w_in"]), big["w_in"][0].shape[0]
    slots["w_in"] = [{name: jnp.zeros((d_model, width), f32) for name, _, _, width in IN_GROUPS} for _ in range(n_layers)]
    slots["w_gate"] = [jnp.zeros((d_model, 4 * d_model), f32) for _ in range(n_layers)]
    slots["w_branch"] = [[jnp.zeros(w.shape[1:], f32) for _ in range(w.shape[0])] for w in big["w_branch"]]
    return slots


def _slot_grads(g):
    out = {n: jnp.stack(v) for n, v in g.items() if n not in ("w_in", "w_gate", "w_branch")}
    out["w_branch"] = jnp.stack([jnp.stack(gl) for gl in g["w_branch"]])
    out["w_in"] = jnp.stack([
        jnp.concatenate([gl[name][:, :IN_OFFS[s0 + ns] - IN_OFFS[s0]] for name, s0, ns, _ in IN_GROUPS], axis=1)
        for gl in g["w_in"]])
    d_model = g["w_gate"][0].shape[0]
    out["w_gate"] = jnp.stack([jnp.transpose(gl.reshape(d_model, 4, d_model), (1, 0, 2)) for gl in g["w_gate"]])
    return out


def _forward(p, big, slots, x, ctx, c, target):
    n_layers = p["norm_w"].shape[0]
    d_model = x.shape[-1]
    xa = jnp.concatenate([ctx, x], axis=0)
    t = xa.shape[0]
    cc = jnp.concatenate([c, p["c_ctx"][None], jnp.zeros((_SUB - 2, d_model), f32)], axis=0)
    lb_all = jnp.cumsum(jax.nn.softmax(p["hg_lb_logits"], axis=0), axis=0)
    scc, = blocked_op("silu_c", _f_silu, [], [cc], [(d_model, bf16)], rb=_SUB)
    wide = _wide_rows(t)

    for l in range(n_layers):
        tag = "l%d_" % l
        mod = mm(tag + "mod", scc, big["w_mod"][l], slots["w_mod"][l])
        bm = p["b_mod"][l][None]
        h, = blocked_op(tag + "normmod", _f_normmod, [p["norm_w"][l][None], mod, bm], [xa], [(d_model, bf16)], rb=wide)
        gnames = [g[0] for g in IN_GROUPS]
        wvs = [_pad_cols(big["w_in"][l][:, IN_OFFS[s0]:IN_OFFS[s0 + ns]], width) for _, s0, ns, width in IN_GROUPS]
        u = dict(zip(gnames, multi_mm(tag + "in_", gnames, h, wvs, [slots["w_in"][l][g] for g in gnames])))

        o_dirs = blocked_op(tag + "hg", _both(_f_hg(False), _f_hg(True), 1, 2), [lb_all[l, 0][None], lb_all[l, 1][None]],
                            [u["hg_qi"], u["hg_ff"], u["hg_qi"], u["hg_fb"]], [(BR_W, f32)] * 2,
                            order=_BOTH_ORDERS, carry_sds=[(BR_W, HG_DK)] * 2)
        y_hg, = blocked_op(tag + "hg_fin", _f_hg_final, [p["hg_norm"][l][None]], list(o_dirs) + [u["hg_z"]], [(BR_W, bf16)],
                           rb=wide)

        a_scan, bd, cds = _s5_params(l, p)
        n_state = S5_GROUPS * S5_STATE
        planes = [s.reshape(t, n_state) for s in s5_states(tag, u["s5_u"], bd, a_scan[0], a_scan[1])]
        ysum, = bd_mm(tag + "s5_c", planes, [[cds[d][part]] for d in range(2) for part in range(2)])
        g5, = blocked_op(tag + "s5_c1", _f_s5_c1, [p["s5_d"][l][None]], [ysum, u["s5_u"]], [(BR_W, f32)], rb=wide)
        gl = mm(tag + "s5_glu", g5, big["s5_w_glu"][l], slots["s5_w_glu"][l])
        y_s5, = blocked_op(tag + "s5_c2", _f_s5_c2, [p["s5_b_glu"][l][None]], [g5, gl, u["s5_z"]], [(BR_W, bf16)],
                           rb=wide)

        wg, gb = _lru_gate(l, p)
        ab = blocked_op(tag + "lru_a", _f_lru_a,
                        [p["lru_conv_w"][l], p["lru_conv_b"][l][None], wg, gb, p["lru_lam"][l]],
                        [u["lru_x"]], [(BR_W, f32)] * 4)
        hs = []
        for d in range(2):
            hs.append(rscan(tag + "lru_scan%d" % d, d, ab[2 * d], ab[2 * d + 1]))
        y_lru, = blocked_op(tag + "lru_c", _f_lru_c, [], hs + [u["lru_z"]], [(BR_W, bf16)], rb=wide)

        dtb = _pad_cols(p["m2_dt_bias"][l].reshape(1, -1), _LANE)
        xbc, dtp = blocked_op(tag + "m2_a", _f_m2_a, [p["m2_conv_w"][l], p["m2_conv_b"][l][None], dtb],
                              [u["m2_xbc"], u["m2_dt"]], [(M2_XBC, f32), (_LANE, f32)])
        alog = _pad_cols(p["m2_a_log"][l].reshape(1, -1), _LANE)
        y_dirs = blocked_op(tag + "ssd", _both(_f_ssd(0), _f_ssd(1), 1, 2), [alog, alog], [xbc, dtp, xbc, dtp],
                            [(BR_W, f32)] * 2, order=_BOTH_ORDERS,
                            carry_sds=[(M2_GROUPS * M2_STATE, BR_W // M2_GROUPS)] * 2)
        dsk = jnp.repeat(p["m2_d"][l], M2_HEADDIM)[None]
        y_m2, = blocked_op(tag + "m2_c", _f_m2_c, [dsk, p["m2_norm"][l][None]], list(y_dirs) + [xbc, u["m2_z"]], [(BR_W, bf16)],
                           rb=wide)

        wg_all = jnp.transpose(big["w_gate"][l], (1, 0, 2)).reshape(d_model, 4 * d_model)
        gp = mm(tag + "gate", h, wg_all, slots["w_gate"][l], out_dtype=bf16)
        bs = [mm(tag + "br%d" % k, yk, big["w_branch"][l][k], slots["w_branch"][l][k], out_dtype=bf16)
              for k, yk in enumerate((y_hg, y_s5, y_lru, y_m2))]
        mix, = blocked_op(tag + "mix", _f_mix, [p["b_gate"][l].reshape(1, -1)], [gp] + bs, [(d_model, bf16)])
        o = mm(tag + "out", mix, big["w_out"][l], slots["w_out"][l])
        xa, = blocked_op(tag + "resid", _f_resid, [mod, bm], [xa, o], [(d_model, f32)], rb=wide)

    rl, = blocked_op("loss", _f_loss, [p["final_norm"][None]], [xa[ctx.shape[0]:], target], [(1, f32)],
                     rb=_wide_rows(target.shape[0]))
    return jnp.sum(rl)


_MESH = pl.DeviceIdType.MESH
_ANY = pl.BlockSpec(memory_space=pl.ANY)
W_PACK = 1024


def _place():
    x, y, c = lax.axis_index("x"), lax.axis_index("y"), lax.axis_index("c")
    chips = [(x, 1 - y), (1 - x, y), (1 - x, 1 - y)]
    return x, y, c, chips


def _rcopy(src, dst, ssem, rsem, k, to):
    return pltpu.make_async_remote_copy(src_ref=src, dst_ref=dst, send_sem=ssem.at[k], recv_sem=rsem.at[k],
                                        device_id=to, device_id_type=_MESH)


def gather_shards(xs):
    n = len(xs)

    def body(*refs):
        x_refs, o_refs = refs[:n], refs[n:2 * n]
        ssem, rsem, lsem = refs[2 * n:]
        x, y, c, chips = _place()
        j = 2 * x + y
        sib = (x, y, 1 - c)
        mine = [pltpu.make_async_copy(x_refs[a], o_refs[a].at[j], lsem.at[a]) for a in range(n)]
        for cp in mine:
            cp.start()
        first = [_rcopy(x_refs[a].at[c], o_refs[a].at[j, c], ssem, rsem, 6 * a + r, (*chips[r], c))
                 for r in range(3) for a in range(n)]
        for cp in first:
            cp.start()
        passed = []
        for r in range(3):
            jr = j ^ (r + 1)
            for a in range(n):
                _rcopy(x_refs[a].at[c], o_refs[a].at[jr, c], ssem, rsem, 6 * a + r, sib).wait_recv()
                cp = _rcopy(o_refs[a].at[jr, c], o_refs[a].at[jr, c], ssem, rsem, 6 * a + 3 + r, sib)
                cp.start()
                passed.append(cp)
        for r in range(3):
            jr = j ^ (r + 1)
            for a in range(n):
                _rcopy(x_refs[a].at[c], o_refs[a].at[jr, 1 - c], ssem, rsem, 6 * a + 3 + r, sib).wait_recv()
        for cp in first + passed:
            cp.wait_send()
        for cp in mine:
            cp.wait()

    return pl.pallas_call(
        body, name="gather_shards", out_shape=[jax.ShapeDtypeStruct((4,) + x.shape, x.dtype) for x in xs],
        in_specs=[_ANY] * n, out_specs=[_ANY] * n,
        scratch_shapes=[pltpu.SemaphoreType.DMA((6 * n,)), pltpu.SemaphoreType.DMA((6 * n,)), pltpu.SemaphoreType.DMA((n,))],
    )(*xs)


def sibling_halves(gs):
    n = len(gs)

    def body(*refs):
        g_refs, o_refs = refs[:n], refs[n:2 * n]
        ssem, rsem = refs[2 * n:]
        x, y, c, _ = _place()
        sib = (x, y, 1 - c)
        cps = [_rcopy(g_refs[a].at[k, 1 - c], o_refs[a].at[k], ssem, rsem, 4 * a + k, sib)
               for k in range(4) for a in range(n)]
        for cp in cps:
            cp.start()
        for cp in cps:
            cp.wait()

    return pl.pallas_call(
        body, name="sibling_halves", out_shape=[jax.ShapeDtypeStruct((4,) + g.shape[2:], g.dtype) for g in gs],
        in_specs=[_ANY] * n, out_specs=[_ANY] * n,
        scratch_shapes=[pltpu.SemaphoreType.DMA((4 * n,)), pltpu.SemaphoreType.DMA((4 * n,))],
    )(*gs)


def scatter_chips(ps):
    n = len(ps)

    def body(*refs):
        p_refs, o_refs = refs[:n], refs[n:2 * n]
        ssem, rsem = refs[2 * n:]
        x, y, c, chips = _place()
        j = 2 * x + y
        cps = [_rcopy(p_refs[a].at[j ^ (r + 1)], o_refs[a].at[r], ssem, rsem, 3 * a + r, (*chips[r], c))
               for r in range(3) for a in range(n)]
        for cp in cps:
            cp.start()
        for cp in cps:
            cp.wait()

    return pl.pallas_call(
        body, name="scatter_chips", out_shape=[jax.ShapeDtypeStruct((3,) + p.shape[1:], p.dtype) for p in ps],
        in_specs=[_ANY] * n, out_specs=[_ANY] * n,
        scratch_shapes=[pltpu.SemaphoreType.DMA((3 * n,)), pltpu.SemaphoreType.DMA((3 * n,))],
    )(*ps)


def join_halves(qs):
    n = len(qs)

    def body(*refs):
        o_refs = refs[n:2 * n]
        ssem, rsem = refs[2 * n:]
        x, y, c, _ = _place()
        sib = (x, y, 1 - c)
        cps = [_rcopy(o_refs[a].at[c], o_refs[a].at[c], ssem, rsem, a, sib) for a in range(n)]
        for cp in cps:
            cp.start()
        for a in range(n):
            _rcopy(o_refs[a].at[c], o_refs[a].at[1 - c], ssem, rsem, a, sib).wait_recv()
        for cp in cps:
            cp.wait_send()

    return pl.pallas_call(
        body, name="join_halves", out_shape=[jax.ShapeDtypeStruct(q.shape, q.dtype) for q in qs],
        in_specs=[_ANY] * n, out_specs=[_ANY] * n, input_output_aliases={a: a for a in range(n)},
        scratch_shapes=[pltpu.SemaphoreType.DMA((n,)), pltpu.SemaphoreType.DMA((n,))],
    )(*qs)


def _rows_block(r):
    return _pick(r, 256, _SUB)


def add_sibling(tag, g, r1, place, out_dtype):
    _, _, rows, w = g.shape
    rb = _rows_block(rows)

    def body(pl_ref, g_ref, r_ref, o_ref):
        o_ref[...] = (g_ref[0] + r_ref[...]).astype(out_dtype)

    return pl.pallas_call(
        body, name="add_sibling_" + tag, out_shape=jax.ShapeDtypeStruct((4, rows, w), out_dtype),
        grid_spec=pltpu.PrefetchScalarGridSpec(
            num_scalar_prefetch=1, grid=(4, rows // rb),
            in_specs=[pl.BlockSpec((1, 1, rb, w), lambda k, i, s: (k, s[1], i, 0)),
                      pl.BlockSpec((1, rb, w), lambda k, i, s: (k, i, 0))],
            out_specs=pl.BlockSpec((1, rb, w), lambda k, i, s: (k, i, 0))),
        compiler_params=_cparams(2),
    )(place, g, r1)


def add_chips(tag, p, r2, place):
    _, rows, w = p.shape
    rb = _rows_block(rows)

    def body(pl_ref, p_ref, r_ref, o_ref):
        j = pl_ref[0]
        own = p_ref[0].astype(f32)
        others = [r_ref[0].astype(f32), r_ref[1].astype(f32), r_ref[2].astype(f32)]
        acc = None
        for k in range(4):
            rel = k ^ j
            t = jnp.where(rel == 0, own, jnp.where(rel == 1, others[0], jnp.where(rel == 2, others[1], others[2])))
            acc = t if acc is None else acc + t
        o_ref[0] = acc

    return pl.pallas_call(
        body, name="add_chips_" + tag, out_shape=jax.ShapeDtypeStruct((2, rows, w), f32),
        grid_spec=pltpu.PrefetchScalarGridSpec(
            num_scalar_prefetch=1, grid=(rows // rb,),
            in_specs=[pl.BlockSpec((1, rb, w), lambda i, s: (s[0], i, 0)),
                      pl.BlockSpec((3, rb, w), lambda i, s: (0, i, 0))],
            out_specs=pl.BlockSpec((1, rb, w), lambda i, s: (s[1], i, 0))),
        compiler_params=_cparams(1),
    )(place, p, r2)


def adamw(tag, g, w, m, v):
    rows, wd = g.shape
    rb = _rows_block(rows)

    def body(g_ref, w_ref, m_ref, v_ref, d_ref, nm_ref, nv_ref):
        gv = g_ref[...]
        nm = ADAM_B1 * m_ref[...] + (1.0 - ADAM_B1) * gv
        nv = ADAM_B2 * v_ref[...] + (1.0 - ADAM_B2) * (gv * gv)
        m_hat = nm / (1.0 - ADAM_B1 ** ADAM_STEP)
        v_hat = nv / (1.0 - ADAM_B2 ** ADAM_STEP)
        d_ref[...] = -ADAM_LR * (m_hat / (jnp.sqrt(v_hat) + ADAM_EPS) + ADAM_WD * w_ref[...])
        nm_ref[...] = nm
        nv_ref[...] = nv

    spec = pl.BlockSpec((rb, wd), lambda i: (i, 0))
    return pl.pallas_call(
        body, name="adamw_" + tag, grid=(rows // rb,), in_specs=[spec] * 4, out_specs=[spec] * 3,
        out_shape=[jax.ShapeDtypeStruct(g.shape, f32)] * 3, compiler_params=_cparams(1),
    )(g, w, m, v)


WEIGHTS = ("c_ctx", "norm_w", "w_mod", "b_mod", "w_in", "hg_lb_logits", "hg_norm", "s5_a_re", "s5_a_im", "s5_log_step",
           "s5_b_re", "s5_b_im", "s5_c_re", "s5_c_im", "s5_d", "s5_w_glu", "s5_b_glu", "lru_conv_w", "lru_conv_b",
           "lru_gate_w", "lru_gate_b", "lru_lam", "m2_conv_w", "m2_conv_b", "m2_dt_bias", "m2_a_log", "m2_d", "m2_norm",
           "w_branch", "w_gate", "b_gate", "w_out", "final_norm")
SHARD_AXIS = {"w_mod": 2, "w_in": 2, "hg_lb_logits": 2, "s5_w_glu": 1, "lru_conv_w": 2, "lru_lam": 2, "m2_conv_w": 2,
              "w_branch": 3, "w_gate": 2, "b_gate": 2, "w_out": 1}
BIG = ("w_mod", "w_in", "s5_w_glu", "w_branch", "w_gate", "w_out")
N_CHIPS = 4


def _to_rows(flat, row_unit):
    n = flat.shape[-1]
    per = 2 * row_unit * W_PACK
    total = -(-n // per) * per
    flat = jnp.pad(flat, [(0, 0)] * (flat.ndim - 1) + [(0, total - n)])
    return flat.reshape(flat.shape[:-1] + (2, total // (2 * W_PACK), W_PACK))


SMALL_SHARDED = tuple(n for n in WEIGHTS if n in SHARD_AXIS and n not in BIG)
SMALL_REPLICATED = tuple(n for n in WEIGHTS if n not in SHARD_AXIS)


def _chip_slices(a, axis):
    width = a.shape[axis] // N_CHIPS
    return jnp.stack([lax.slice_in_dim(a, k * width, (k + 1) * width, axis=axis) for k in range(N_CHIPS)])


def _gather_weights(local):
    small = jnp.concatenate([lax.bitcast_convert_type(local[n], bf16).reshape(-1) for n in SMALL_SHARDED])
    got = gather_shards([local[n].astype(bf16) for n in BIG] + [_to_rows(small, 16)])
    full = {}
    for n, g in zip(BIG, got):
        full[n] = [jnp.concatenate([g[j, l] for j in range(N_CHIPS)], axis=SHARD_AXIS[n] - 1) for l in range(g.shape[1])]
    flat, off = got[-1].reshape(N_CHIPS, -1), 0
    for n in SMALL_SHARDED:
        shp = local[n].shape
        size = 2 * math.prod(shp)
        part = lax.bitcast_convert_type(flat[:, off:off + size].reshape((N_CHIPS,) + shp + (2,)), f32)
        off += size
        full[n] = jnp.concatenate([part[j] for j in range(N_CHIPS)], axis=SHARD_AXIS[n])
    return full


def _whole_rows(v):
    n = v.shape[-1]
    return jnp.pad(v, [(0, 0)] * (v.ndim - 1) + [(0, -n % W_PACK)])


def _pack_small(vals, extra):
    return jnp.concatenate([_whole_rows(vals[n].reshape(-1)) for n in SMALL_SHARDED + SMALL_REPLICATED]
                           + [_whole_rows(extra.reshape(1))])


def _pack_small_grads(grads, loss):
    rep = [grads[n].reshape(-1) for n in SMALL_REPLICATED] + [loss.reshape(1)]
    sh = [_chip_slices(grads[n], SHARD_AXIS[n]).reshape(N_CHIPS, -1) for n in SMALL_SHARDED]
    return jnp.concatenate([_whole_rows(a) for a in sh]
                           + [_whole_rows(jnp.broadcast_to(r, (N_CHIPS,) + r.shape)) for r in rep], axis=1)


def _unpack_small(rows, like):
    out, r0 = {}, 0
    for n in SMALL_SHARDED + SMALL_REPLICATED:
        size = math.prod(like[n].shape)
        nr = -(-size // W_PACK)
        piece = lax.optimization_barrier(rows[r0:r0 + nr])
        out[n] = piece.reshape(-1)[:size].reshape(like[n].shape)
        r0 += nr
    return out, lax.optimization_barrier(rows[r0:r0 + 1])[0, 0]


def _reduce_grads(tags, gs):
    place = jnp.stack([2 * lax.axis_index("x") + lax.axis_index("y"), lax.axis_index("c")]).astype(jnp.int32)
    pairs = [add_sibling(t, g, r, place, bf16 if t in BIG else f32) for t, g, r in zip(tags, gs, sibling_halves(gs))]
    quads = [add_chips(t, p, r, place) for t, p, r in zip(tags, pairs, scatter_chips(pairs))]
    return join_halves(quads)


def kernel(x, c, ctx, c_ctx, norm_w, w_mod, b_mod, w_in, hg_lb_logits, hg_norm, s5_a_re, s5_a_im, s5_log_step, s5_b_re, s5_b_im, s5_c_re, s5_c_im, s5_d, s5_w_glu, s5_b_glu, lru_conv_w, lru_conv_b, lru_gate_w, lru_gate_b, lru_lam, m2_conv_w, m2_conv_b, m2_dt_bias, m2_a_log, m2_d, m2_norm, w_branch, w_gate, b_gate, w_out, final_norm, loss_target, m_c_ctx, m_norm_w, m_w_mod, m_b_mod, m_w_in, m_hg_lb_logits, m_hg_norm, m_s5_a_re, m_s5_a_im, m_s5_log_step, m_s5_b_re, m_s5_b_im, m_s5_c_re, m_s5_c_im, m_s5_d, m_s5_w_glu, m_s5_b_glu, m_lru_conv_w, m_lru_conv_b, m_lru_gate_w, m_lru_gate_b, m_lru_lam, m_m2_conv_w, m_m2_conv_b, m_m2_dt_bias, m_m2_a_log, m_m2_d, m_m2_norm, m_w_branch, m_w_gate, m_b_gate, m_w_out, m_final_norm, v_c_ctx, v_norm_w, v_w_mod, v_b_mod, v_w_in, v_hg_lb_logits, v_hg_norm, v_s5_a_re, v_s5_a_im, v_s5_log_step, v_s5_b_re, v_s5_b_im, v_s5_c_re, v_s5_c_im, v_s5_d, v_s5_w_glu, v_s5_b_glu, v_lru_conv_w, v_lru_conv_b, v_lru_gate_w, v_lru_gate_b, v_lru_lam, v_m2_conv_w, v_m2_conv_b, v_m2_dt_bias, v_m2_a_log, v_m2_d, v_m2_norm, v_w_branch, v_w_gate, v_b_gate, v_w_out, v_final_norm):
    given = dict(locals())
    w_loc = {n: given[n] for n in WEIGHTS}
    m_loc = {n: given["m_" + n] for n in WEIGHTS}
    v_loc = {n: given["v_" + n] for n in WEIGHTS}

    full = _gather_weights(w_loc)
    params = {n: (full[n] if n in SHARD_AXIS else w_loc[n]) for n in WEIGHTS if n not in BIG}
    big = {n: full[n] for n in BIG}
    def loss_fn(p, s, xx):
        return _forward(p, big, s, xx, ctx[0], c, loss_target[0])

    loss, (g_p, g_s, g_x) = jax.value_and_grad(loss_fn, argnums=(0, 1, 2))(params, _new_slots(big), x[0])
    grads = {**g_p, **_slot_grads(g_s)}

    def rows4(a):
        return a.reshape(a.shape[:2] + (-1, a.shape[-1]))

    g_big = [rows4(_chip_slices(grads[n], SHARD_AXIS[n])) for n in BIG]
    g_small = _to_rows(_pack_small_grads(grads, loss), 64)
    summed = _reduce_grads(list(BIG) + ["small"], g_big + [g_small])

    g_out, d_out, m_out, v_out = {}, {}, {}, {}
    for n, g in zip(BIG, summed):
        shp = w_loc[n].shape
        flat2 = lambda a: a.reshape(-1, shp[-1])
        g_out[n] = g.reshape(shp)
        d, nm, nv = adamw(n, flat2(g), flat2(w_loc[n]), flat2(m_loc[n]), flat2(v_loc[n]))
        d_out[n], m_out[n], v_out[n] = d.reshape(shp), nm.reshape(shp), nv.reshape(shp)
    zero = jnp.zeros((), f32)
    flat = lambda vals: _to_rows(_pack_small(vals, zero), 64).reshape(-1, W_PACK)
    gs = summed[-1].reshape(-1, W_PACK)
    d, nm, nv = adamw("small", gs, flat(w_loc), flat(m_loc), flat(v_loc))
    gsm, loss_out = _unpack_small(gs, w_loc)
    g_out.update(gsm)
    d_out.update(_unpack_small(d, w_loc)[0])
    m_out.update(_unpack_small(nm, w_loc)[0])
    v_out.update(_unpack_small(nv, w_loc)[0])
    outs = [loss_out, g_x[None]]
    for group in (g_out, d_out, m_out, v_out):
        outs += [group[n] for n in WEIGHTS]
    return tuple(outs)
```

```python
import functools
import math

import jax
import jax.numpy as jnp
from jax import lax
from jax.experimental import pallas as pl
from jax.experimental.pallas import tpu as pltpu

f32 = jnp.float32
bf16 = jnp.bfloat16
_MM_DTYPE = bf16
_HI = lax.Precision.HIGHEST
_MAP_PREC = lax.Precision.HIGH
_VMEM_LIMIT = 56 * 1024 * 1024
_LANE = 128
_SUB = 8

EPS = 1e-6
CONV_W = 4
CHUNK = 64
RB = 256
BR_W = 512
HG_HEADS = 4
HG_DK = 128
S5_GROUPS = 32
S5_GROUP = 16
S5_STATE = 64
LRU_BLOCKS = 8
LRU_C = 8.0
M2_HEADS = 8
M2_HEADDIM = 64
M2_GROUPS = 2
M2_STATE = 64
M2_XBC = BR_W + 2 * M2_GROUPS * M2_STATE
ADAM_LR = 0.001
ADAM_B1 = 0.9
ADAM_B2 = 0.999
ADAM_EPS = 1e-08
ADAM_WD = 0.01
ADAM_STEP = 10

_NN = (((1,), (0,)), ((), ()))
_NT = (((1,), (1,)), ((), ()))
_TN = (((0,), (0,)), ((), ()))


def _silu(x):
    return x * jax.nn.sigmoid(x)


def _softplus(x):
    return jnp.maximum(x, 0.0) + jnp.log1p(jnp.exp(-jnp.abs(x)))


def _one_minus_exp(z):
    series = -z * (1.0 + z * 0.5 * (1.0 + z * (1.0 / 3.0) * (1.0 + z * 0.25 * (1.0 + z * 0.2))))
    return jnp.where(z > -0.05, series, 1.0 - jnp.exp(z))


def _rms(x, w):
    return x * lax.rsqrt(jnp.mean(x * x, axis=-1, keepdims=True) + EPS) * w


def _dot(a, b, dn=_NN, hi=False):
    prec = hi if isinstance(hi, lax.Precision) else (_HI if hi else None)
    return lax.dot_general(a, b, dn, precision=prec, preferred_element_type=f32)


def _cparams(n_grid):
    return pltpu.CompilerParams(dimension_semantics=("arbitrary",) * n_grid, vmem_limit_bytes=_VMEM_LIMIT)


_REV = {"asc": "desc", "d1": "d1r", "desc": "asc", "d1r": "d1"}


def _blk(order, i, n):
    if order == "asc":
        return i
    if order == "desc":
        return n - 1 - i
    if order == "d1":
        return jnp.where(i == 0, 0, n - i)
    return jnp.where(i == n - 1, 0, i + 1)


def _pick(n, cap, unit):
    if n <= cap:
        return n
    best = None
    d = unit
    while d <= cap:
        if n % d == 0:
            best = d
        d += unit
    return n if best is None else best


def _mm_call(name, a, b, mode, hi, out_dtype):
    if mode == "tn":
        k, m = a.shape
        n = b.shape[1]
        tm = _pick(m, 512, _LANE)
        tn = _pick(n, 512, _LANE)
        a_spec = pl.BlockSpec((k, tm), lambda i, j: (0, i))
        b_spec = pl.BlockSpec((k, tn), lambda i, j: (0, j))
    else:
        m, k = a.shape
        n = b.shape[1] if mode == "nn" else b.shape[0]
        tn = _pick(n, max(_LANE, (8 * 1024 * 1024 // (k * b.dtype.itemsize)) // _LANE * _LANE), _LANE)
        b_spec = pl.BlockSpec((k, tn), lambda i, j: (0, j)) if mode == "nn" else pl.BlockSpec((tn, k), lambda i, j: (j, 0))
        rows_in = 9 * 512 * 1024 // (k * a.dtype.itemsize)
        rows_out = 9 * 1024 * 1024 // (tn * jnp.dtype(out_dtype).itemsize)
        tm = _pick(m, max(256, min(2176, rows_in, rows_out)), _SUB)
        a_spec = pl.BlockSpec((tm, k), lambda i, j: (i, 0))
    dn = {"nn": _NN, "nt": _NT, "tn": _TN}[mode]

    def body(a_ref, b_ref, o_ref):
        av = a_ref[...]
        bv = b_ref[...]
        if hi:
            av = av.astype(f32)
            bv = bv.astype(f32)
        else:
            av = av.astype(_MM_DTYPE)
            bv = bv.astype(_MM_DTYPE)
        o_ref[...] = _dot(av, bv, dn, hi).astype(o_ref.dtype)

    return pl.pallas_call(
        body, name=name, grid=(m // tm, n // tn), in_specs=[a_spec, b_spec],
        out_specs=pl.BlockSpec((tm, tn), lambda i, j: (i, j)),
        out_shape=jax.ShapeDtypeStruct((m, n), out_dtype), compiler_params=_cparams(2),
    )(a, b)


def mm(name, a, b, slot=None, hi=False, out_dtype=f32):
    @jax.custom_vjp
    def op(a, b, slot):
        return _mm_call(name, a, b, "nn", hi, out_dtype)

    def fwd(a, b, slot):
        return op(a, b, slot), (a, b)

    def bwd(res, g):
        a, b = res
        da = _mm_call(name + "_da", g, b, "nt", hi, a.dtype)
        db = _mm_call(name + "_db", a, g, "tn", hi, f32)
        if slot is None:
            return da, db.astype(b.dtype), None
        return da, jnp.zeros_like(b), db

    op.defvjp(fwd, bwd)
    return op(a, b, slot)


def _sum_nt_call(name, gs, ws, out_dtype):
    m = gs[0].shape[0]
    kdim = ws[0].shape[0]
    tm = _pick(m, 256, _SUB)
    n = len(gs)

    def body(*refs):
        acc = None
        for g_ref, w_ref in zip(refs[:n], refs[n:2 * n]):
            part = _dot(g_ref[...].astype(_MM_DTYPE), w_ref[...].astype(_MM_DTYPE), _NT)
            acc = part if acc is None else acc + part
        refs[2 * n][...] = acc.astype(out_dtype)

    in_specs = [pl.BlockSpec((tm, g.shape[1]), lambda i: (i, 0)) for g in gs]
    in_specs += [pl.BlockSpec(w.shape, lambda i: (0, 0)) for w in ws]
    return pl.pallas_call(
        body, name=name, grid=(m // tm,), in_specs=in_specs, out_specs=pl.BlockSpec((tm, kdim), lambda i: (i, 0)),
        out_shape=jax.ShapeDtypeStruct((m, kdim), out_dtype), compiler_params=_cparams(1),
    )(*gs, *ws)


def multi_mm(tag, names, a, ws, slots):
    @jax.custom_vjp
    def op(a, ws, slots):
        return tuple(_mm_call(tag + n, a, w, "nn", False, f32) for n, w in zip(names, ws))

    def fwd(a, ws, slots):
        return op(a, ws, slots), (a, ws)

    def bwd(res, gs):
        a, ws = res
        dws = [_mm_call(tag + n + "_db", a, g, "tn", False, f32) for n, g in zip(names, gs)]
        da = _sum_nt_call(tag + "da", list(gs), ws, a.dtype)
        return da, [jnp.zeros_like(w) for w in ws], dws

    op.defvjp(fwd, bwd)
    return op(a, list(ws), list(slots))


def _orders(order, n_x, n_o):
    if isinstance(order, str):
        return [order] * n_x, [order] * n_o
    return list(order[0]), list(order[1])


def _row(o, n):
    return lambda i: (_blk(o, i, n), 0)


def _blocked_fwd(name, f, order, rb, params, xs, out_sds, carry_sds):
    t = xs[0].shape[0]
    n = t // rb
    n_p, n_x, n_o, n_c = len(params), len(xs), len(out_sds), len(carry_sds)
    xo, oo = _orders(order, n_x, n_o)

    def body(*refs):
        p_refs = refs[:n_p]
        x_refs = refs[n_p:n_p + n_x]
        o_refs = refs[n_p + n_x:n_p + n_x + n_o]
        st_refs = refs[n_p + n_x + n_o:n_p + n_x + n_o + n_c]
        c_refs = refs[n_p + n_x + n_o + n_c:]
        i = pl.program_id(0)
        blk = _blk(xo[0], i, n)
        p = [r[...] for r in p_refs]
        x = [r[...] for r in x_refs]
        if n_c:
            @pl.when(i == 0)
            def _():
                for c in c_refs:
                    c[...] = jnp.zeros_like(c)
            c_in = [c[...] for c in c_refs]
            for sr, c in zip(st_refs, c_in):
                sr[0] = c
            c_out, ys = f(blk, p, c_in, x)
            for c, v in zip(c_refs, c_out):
                c[...] = v
        else:
            ys = f(blk, p, x)
        for o, y in zip(o_refs, ys):
            o[...] = y.astype(o.dtype)

    in_specs = [pl.BlockSpec(p.shape, lambda i, nd=p.ndim: (0,) * nd) for p in params]
    in_specs += [pl.BlockSpec((rb, x.shape[1]), _row(o, n)) for x, o in zip(xs, xo)]
    out_specs = [pl.BlockSpec((rb, c), _row(o, n)) for (c, _), o in zip(out_sds, oo)]
    out_specs += [pl.BlockSpec((1,) + s, lambda i: (i, 0, 0)) for s in carry_sds]
    out_shape = [jax.ShapeDtypeStruct((t, c), d) for c, d in out_sds]
    out_shape += [jax.ShapeDtypeStruct((n,) + s, f32) for s in carry_sds]
    res = pl.pallas_call(
        body, name=name, grid=(n,), in_specs=in_specs, out_specs=out_specs, out_shape=out_shape,
        scratch_shapes=[pltpu.VMEM(s, f32) for s in carry_sds], compiler_params=_cparams(1),
    )(*params, *xs)
    return list(res[:n_o]), list(res[n_o:])


def _blocked_bwd(name, f, order, rb, params, xs, states, dys, carry_sds):
    t = xs[0].shape[0]
    n = t // rb
    n_p, n_x, n_o, n_c = len(params), len(xs), len(dys), len(carry_sds)
    xo, oo = _orders(order, n_x, n_o)
    xo, oo = [_REV[o] for o in xo], [_REV[o] for o in oo]

    def body(*refs):
        k = 0
        p_refs = refs[k:k + n_p]; k += n_p
        x_refs = refs[k:k + n_x]; k += n_x
        st_refs = refs[k:k + n_c]; k += n_c
        dy_refs = refs[k:k + n_o]; k += n_o
        dp_refs = refs[k:k + n_p]; k += n_p
        dx_refs = refs[k:k + n_x]; k += n_x
        dc_refs = refs[k:]
        i = pl.program_id(0)
        blk = _blk(xo[0], i, n)
        p = [r[...] for r in p_refs]
        x = [r[...] for r in x_refs]
        dy = [r[...] for r in dy_refs]
        if n_c:
            @pl.when(i == 0)
            def _():
                for c in dc_refs:
                    c[...] = jnp.zeros_like(c)
            c_in = [r[0] for r in st_refs]
            dc = [c[...] for c in dc_refs]
            _, vjp = jax.vjp(lambda p_, c_, x_: f(blk, p_, c_, x_), p, c_in, x)
            dp, dcin, dx = vjp((dc, dy))
            for c, v in zip(dc_refs, dcin):
                c[...] = v
        else:
            _, vjp = jax.vjp(lambda p_, x_: f(blk, p_, x_), p, x)
            dp, dx = vjp(dy)

        @pl.when(i == 0)
        def _():
            for r, v in zip(dp_refs, dp):
                r[...] = v

        @pl.when(i > 0)
        def _():
            for r, v in zip(dp_refs, dp):
                r[...] += v
        for r, v in zip(dx_refs, dx):
            r[...] = v.astype(r.dtype)

    in_specs = [pl.BlockSpec(p.shape, lambda i, nd=p.ndim: (0,) * nd) for p in params]
    in_specs += [pl.BlockSpec((rb, x.shape[1]), _row(o, n)) for x, o in zip(xs, xo)]
    in_specs += [pl.BlockSpec((1,) + s, lambda i: (n - 1 - i, 0, 0)) for s in carry_sds]
    in_specs += [pl.BlockSpec((rb, d.shape[1]), _row(o, n)) for d, o in zip(dys, oo)]
    out_specs = [pl.BlockSpec(p.shape, lambda i, nd=p.ndim: (0,) * nd) for p in params]
    out_specs += [pl.BlockSpec((rb, x.shape[1]), _row(o, n)) for x, o in zip(xs, xo)]
    out_shape = [jax.ShapeDtypeStruct(p.shape, f32) for p in params]
    out_shape += [jax.ShapeDtypeStruct(x.shape, x.dtype) for x in xs]
    res = pl.pallas_call(
        body, name=name + "_bwd", grid=(n,), in_specs=in_specs, out_specs=out_specs, out_shape=out_shape,
        scratch_shapes=[pltpu.VMEM(s, f32) for s in carry_sds], compiler_params=_cparams(1),
    )(*params, *xs, *states, *dys)
    return list(res[:n_p]), list(res[n_p:])


def blocked_op(name, f, params, xs, out_sds, order="asc", carry_sds=(), rb=RB):
    carry_sds = tuple(carry_sds)

    @jax.custom_vjp
    def op(params, xs):
        return tuple(_blocked_fwd(name, f, order, rb, params, xs, out_sds, carry_sds)[0])

    def fwd(params, xs):
        ys, states = _blocked_fwd(name, f, order, rb, params, xs, out_sds, carry_sds)
        return tuple(ys), (params, xs, states)

    def bwd(res, dys):
        params, xs, states = res
        dp, dx = _blocked_bwd(name, f, order, rb, params, xs, states, list(dys), carry_sds)
        return list(dp), list(dx)

    op.defvjp(fwd, bwd)
    return op(list(params), list(xs))


def _cscan_call(name, order, asc, a, xr, xi, sr=None, si=None):
    t = xr.shape[0]
    n = t // RB
    tile = xr.shape[1:]
    xspec = pl.BlockSpec((RB,) + tile, lambda i: (_blk(order, i, n), 0, 0))
    aspec = pl.BlockSpec(a.shape, lambda i: (0, 0, 0))
    plane = jax.ShapeDtypeStruct(xr.shape, f32)

    def rowidx(tt):
        return tt if asc else RB - 1 - tt

    if sr is None:
        def body(a_ref, xr_ref, xi_ref, sr_ref, si_ref, c_ref):
            i = pl.program_id(0)

            @pl.when(i == 0)
            def _():
                c_ref[...] = jnp.zeros_like(c_ref)
            ar = a_ref[0]
            ai = a_ref[1]
            a2r = ar * ar - ai * ai
            a2i = 2.0 * ar * ai

            def step(tt, carry):
                cr, ci = carry
                r1 = rowidx(2 * tt)
                r2 = rowidx(2 * tt + 1)
                x1r, x1i, x2r, x2i = xr_ref[r1], xi_ref[r1], xr_ref[r2], xi_ref[r2]
                s1r = ar * cr - ai * ci + x1r
                s1i = ar * ci + ai * cr + x1i
                kr = ar * x1r - ai * x1i + x2r
                ki = ar * x1i + ai * x1r + x2i
                s2r = a2r * cr - a2i * ci + kr
                s2i = a2r * ci + a2i * cr + ki
                sr_ref[r1] = s1r
                si_ref[r1] = s1i
                sr_ref[r2] = s2r
                si_ref[r2] = s2i
                return s2r, s2i
            cr, ci = lax.fori_loop(0, RB // 2, step, (c_ref[0], c_ref[1]), unroll=4)
            c_ref[0] = cr
            c_ref[1] = ci

        return pl.pallas_call(
            body, name=name, grid=(n,), in_specs=[aspec, xspec, xspec], out_specs=[xspec, xspec],
            out_shape=[plane, plane], scratch_shapes=[pltpu.VMEM(a.shape, f32)], compiler_params=_cparams(1),
        )(a, xr, xi)

    def body(a_ref, xr_ref, xi_ref, sr_ref, si_ref, gr_ref, gi_ref, da_ref, c_ref):
        i = pl.program_id(0)

        @pl.when(i == 0)
        def _():
            c_ref[...] = jnp.zeros_like(c_ref)
            da_ref[...] = jnp.zeros_like(da_ref)
        ar = a_ref[0]
        ai = a_ref[1]
        a2r = ar * ar - ai * ai
        a2i = 2.0 * ar * ai

        def step(tt, carry):
            gr, gi, dar, dai, dbr, dbi = carry
            r1 = rowidx(2 * tt)
            r2 = rowidx(2 * tt + 1)
            x1r, x1i, x2r, x2i = xr_ref[r1], xi_ref[r1], xr_ref[r2], xi_ref[r2]
            v1r, v1i, v2r, v2i = sr_ref[r1], si_ref[r1], sr_ref[r2], si_ref[r2]
            g1r = x1r + ar * gr + ai * gi
            g1i = x1i + ar * gi - ai * gr
            kr = x2r + ar * x1r + ai * x1i
            ki = x2i + ar * x1i - ai * x1r
            g2r = kr + a2r * gr + a2i * gi
            g2i = ki + a2r * gi - a2i * gr
            dar = dar + gr * v1r + gi * v1i
            dai = dai + gi * v1r - gr * v1i
            dbr = dbr + g1r * v2r + g1i * v2i
            dbi = dbi + g1i * v2r - g1r * v2i
            gr_ref[r1] = g1r
            gi_ref[r1] = g1i
            gr_ref[r2] = g2r
            gi_ref[r2] = g2i
            return g2r, g2i, dar, dai, dbr, dbi
        z = jnp.zeros(tile, f32)
        gr, gi, dar, dai, dbr, dbi = lax.fori_loop(0, RB // 2, step, (c_ref[0], c_ref[1], z, z, z, z), unroll=4)
        c_ref[0] = gr
        c_ref[1] = gi
        da_ref[0] += dar + dbr
        da_ref[1] += dai + dbi

    return pl.pallas_call(
        body, name=name, grid=(n,), in_specs=[aspec] + [xspec] * 4, out_specs=[xspec, xspec, aspec],
        out_shape=[plane, plane, jax.ShapeDtypeStruct(a.shape, f32)],
        scratch_shapes=[pltpu.VMEM(a.shape, f32)], compiler_params=_cparams(1),
    )(a, xr, xi, sr, si)


def s5_states(tag, u, bd, a0, a1):
    t = u.shape[0]
    tile = a0.shape[1:]

    def run(u, bd, a0, a1):
        bu = [b.reshape((t,) + tile) for b in _bd_call(tag + "s5_bu", [u], [bd], "nn")]
        s0 = _cscan_call(tag + "s5_scan0", "asc", True, a0, bu[0], bu[1])
        s1 = _cscan_call(tag + "s5_scan1", "d1", False, a1, bu[0], bu[1])
        return (s0[0], s0[1], s1[0], s1[1])

    @jax.custom_vjp
    def op(u, bd, a0, a1):
        return run(u, bd, a0, a1)

    def fwd(u, bd, a0, a1):
        s = run(u, bd, a0, a1)
        return s, (u, bd, a0, a1, s)

    def bwd(res, ds):
        u, bd, a0, a1, s = res
        g0r, g0i, da0 = _cscan_call(tag + "s5_scan0_bwd", "desc", False, a0, ds[0], ds[1], s[0], s[1])
        g1r, g1i, da1 = _cscan_call(tag + "s5_scan1_bwd", "d1r", True, a1, ds[2], ds[3], s[2], s[3])
        gs = [g.reshape(t, -1) for g in (g0r, g0i, g1r, g1i)]
        du, = _bd_call(tag + "s5_bu_da", gs, [[bd[0], bd[1], bd[0], bd[1]]], "nt")
        k = bd[0].shape[0]
        dbd = [_bd_call(tag + "s5_bu_db%d" % j, u, gs[j], "tn", k) + _bd_call(tag + "s5_bu_db%d" % (j + 2), u, gs[j + 2], "tn", k)
               for j in range(2)]
        return du, dbd, da0, da1

    op.defvjp(fwd, bwd)
    return op(u, list(bd), a0, a1)


def _bd_call(name, a, b, mode, k=None):
    if mode == "tn":
        t = a.shape[0]
        ck, cn = a.shape[1] // k, b.shape[1] // k

        def body(a_ref, b_ref, o_ref):
            o_ref[0] = _dot(a_ref[...], b_ref[...], _TN, _MAP_PREC)

        return pl.pallas_call(
            body, name=name, grid=(k,),
            in_specs=[pl.BlockSpec((t, ck), lambda j: (0, j)), pl.BlockSpec((t, cn), lambda j: (0, j))],
            out_specs=pl.BlockSpec((1, ck, cn), lambda j: (j, 0, 0)),
            out_shape=jax.ShapeDtypeStruct((k, ck, cn), f32), compiler_params=_cparams(1),
        )(a, b)
    k, ck, cn = b[0][0].shape
    n_i, n_o = len(b), len(b[0])
    n_x = len(a)
    t = a[0].shape[0]
    tm = _pick(t, 2176, _SUB)
    flat = [w for row in b for w in row]
    win, wout, n_out, dn = (ck, cn, n_o, _NN) if mode == "nn" else (cn, ck, n_i, _NT)

    def body(*refs):
        xv = [r[...] for r in refs[:n_x]]
        w_refs = refs[n_x:n_x + len(flat)]
        o_refs = refs[n_x + len(flat):]
        for q in range(n_out):
            acc = None
            for s in range(n_x):
                w = w_refs[s * n_o + q] if mode == "nn" else w_refs[q * n_o + s]
                part = _dot(xv[s], w[0], dn, _MAP_PREC)
                acc = part if acc is None else acc + part
            o_refs[q][...] = acc

    return pl.pallas_call(
        body, name=name, grid=(t // tm, k),
        in_specs=[pl.BlockSpec((tm, win), lambda i, j: (i, j))] * n_x
        + [pl.BlockSpec((1, ck, cn), lambda i, j: (j, 0, 0))] * len(flat),
        out_specs=[pl.BlockSpec((tm, wout), lambda i, j: (i, j))] * n_out,
        out_shape=[jax.ShapeDtypeStruct((t, k * wout), f32)] * n_out, compiler_params=_cparams(2),
    )(*a, *flat)


def bd_mm(name, xs, ws):
    k = ws[0][0].shape[0]

    @jax.custom_vjp
    def op(xs, ws):
        return tuple(_bd_call(name, xs, ws, "nn"))

    def fwd(xs, ws):
        return op(xs, ws), (xs, ws)

    def bwd(res, gs):
        xs, ws = res
        dws = [[_bd_call(name + "_db%d%d" % (i, o), x, g, "tn", k) for o, g in enumerate(gs)] for i, x in enumerate(xs)]
        return list(_bd_call(name + "_da", list(gs), ws, "nt")), dws

    op.defvjp(fwd, bwd)
    return op(list(xs), [list(row) for row in ws])


def _rscan_call(name, order, asc, a, x, hp=None):
    t = x.shape[0]
    n = t // RB
    cshape = (1, x.shape[1])
    xspec = pl.BlockSpec((RB, x.shape[1]), lambda i: (_blk(order, i, n), 0))

    def rd(ref, r):
        return ref[pl.ds(r, 1), :]

    def wr(ref, r, v):
        ref[pl.ds(r, 1), :] = v

    def rowidx(tt):
        return tt if asc else RB - 1 - tt

    if hp is None:
        def body(a_ref, x_ref, h_ref, hp_ref, c_ref):
            i = pl.program_id(0)

            @pl.when(i == 0)
            def _():
                c_ref[...] = jnp.zeros_like(c_ref)

            def step(tt, h):
                r1 = rowidx(2 * tt)
                r2 = rowidx(2 * tt + 1)
                a1, a2, x1, x2 = rd(a_ref, r1), rd(a_ref, r2), rd(x_ref, r1), rd(x_ref, r2)
                h1 = a1 * h + x1
                h2 = (a2 * a1) * h + (a2 * x1 + x2)
                wr(hp_ref, r1, h)
                wr(h_ref, r1, h1)
                wr(hp_ref, r2, h1)
                wr(h_ref, r2, h2)
                return h2
            c_ref[...] = lax.fori_loop(0, RB // 2, step, c_ref[...], unroll=4)

        return pl.pallas_call(
            body, name=name, grid=(n,), in_specs=[xspec, xspec], out_specs=[xspec, xspec],
            out_shape=[jax.ShapeDtypeStruct(x.shape, f32)] * 2, scratch_shapes=[pltpu.VMEM(cshape, f32)],
            compiler_params=_cparams(1),
        )(a, x)

    def body(a_ref, x_ref, hp_ref, da_ref, db_ref, c_ref):
        i = pl.program_id(0)

        @pl.when(i == 0)
        def _():
            c_ref[...] = jnp.zeros_like(c_ref)

        def step(tt, c):
            r1 = rowidx(2 * tt)
            r2 = rowidx(2 * tt + 1)
            a1, a2, x1, x2 = rd(a_ref, r1), rd(a_ref, r2), rd(x_ref, r1), rd(x_ref, r2)
            g1 = x1 + c
            k = x2 + a1 * x1
            g2 = k + a1 * c
            wr(db_ref, r1, g1)
            wr(da_ref, r1, g1 * rd(hp_ref, r1))
            wr(db_ref, r2, g2)
            wr(da_ref, r2, g2 * rd(hp_ref, r2))
            return a2 * k + (a2 * a1) * c
        c_ref[...] = lax.fori_loop(0, RB // 2, step, c_ref[...], unroll=4)

    return pl.pallas_call(
        body, name=name, grid=(n,), in_specs=[xspec, xspec, xspec], out_specs=[xspec, xspec],
        out_shape=[jax.ShapeDtypeStruct(x.shape, f32)] * 2, scratch_shapes=[pltpu.VMEM(cshape, f32)],
        compiler_params=_cparams(1),
    )(a, x, hp)


def rscan(name, d, a, x):
    order = "d1" if d else "asc"

    @jax.custom_vjp
    def op(a, x):
        return _rscan_call(name, order, d == 0, a, x)[0]

    def fwd(a, x):
        h, hp = _rscan_call(name, order, d == 0, a, x)
        return h, (a, hp)

    def bwd(res, dh):
        a, hp = res
        da, db = _rscan_call(name + "_bwd", _REV[order], d != 0, a, dh, hp)
        return da, db

    op.defvjp(fwd, bwd)
    return op(a, x)


def _wide_rows(t):
    return _pick(t, 544, 16)


def _mod_part(blk, rows, mod, bm, lo, hi):
    is_ctx = blk * rows + lax.broadcasted_iota(jnp.int32, (rows, 1), 0) < RB
    r = mod[:, lo:hi] + bm[:, lo:hi]
    return jnp.where(is_ctx, r[1:2], r[0:1])


def _f_silu(blk, p, x):
    return [_silu(x[0]).astype(bf16)]


def _f_normmod(blk, p, x):
    nw, mod, bm = p
    rows, d = x[0].shape
    shift = _mod_part(blk, rows, mod, bm, 0, d)
    scale = _mod_part(blk, rows, mod, bm, d, 2 * d)
    return [(_rms(x[0], nw) * (1.0 + scale) + shift).astype(bf16)]


def _f_resid(blk, p, x):
    mod, bm = p
    rows, d = x[0].shape
    return [x[0] + _mod_part(blk, rows, mod, bm, 2 * d, 3 * d) * x[1]]


def _f_mix(blk, p, x):
    bg, = p
    gp = x[0]
    d = x[1].shape[1]
    acc = None
    for k in range(4):
        t = jax.nn.sigmoid(gp[:, k * d:(k + 1) * d] + bg[:, k * d:(k + 1) * d]) * x[1 + k]
        acc = t if acc is None else acc + t
    return [acc.astype(bf16)]


def _tri(rev):
    row = lax.broadcasted_iota(jnp.int32, (CHUNK, CHUNK), 0)
    col = lax.broadcasted_iota(jnp.int32, (CHUNK, CHUNK), 1)
    return (col >= row) if rev else (col <= row)


def _chunk_ids(rev):
    ids = list(range(RB // CHUNK))
    return ids[::-1] if rev else ids


def _f_hg(rev):
    def f(blk, p, c, x):
        lb, = p
        st, = c
        qi, fr = x
        q = _silu(qi[:, :BR_W])
        v = qi[:, BR_W:]
        fg = lb + (1.0 - lb) * jax.nn.sigmoid(fr)
        logf = jnp.log(fg)
        k = 1.0 - fg
        m = _tri(rev)
        mf = m.astype(f32)
        outs = [None] * (RB // CHUNK)
        for ci in _chunk_ids(rev):
            sl = slice(CHUNK * ci, CHUNK * ci + CHUNK)
            lf = logf[sl]
            b = _dot(mf, lf, hi=True)
            bend = jnp.sum(lf, axis=0, keepdims=True)
            mid = 0.5 * bend
            qe = q[sl] * jnp.exp(b - mid)
            ke = k[sl] * jnp.exp(mid - b)
            kd = k[sl] * jnp.exp(bend - b)
            qb = q[sl] * jnp.exp(b)
            dec = jnp.exp(bend)
            vc = v[sl]
            oh, ns = [], []
            for hh in range(HG_HEADS):
                cs = slice(HG_DK * hh, HG_DK * hh + HG_DK)
                sth = st[cs]
                att = jnp.where(m, _dot(qe[:, cs], ke[:, cs], _NT), 0.0)
                oh.append(_dot(att, vc[:, cs]) + _dot(qb[:, cs], sth, _NT))
                ns.append(sth * dec[:, cs] + _dot(vc[:, cs], kd[:, cs], _TN))
            st = jnp.concatenate(ns, axis=0)
            outs[ci] = jnp.concatenate(oh, axis=1)
        return [st], [jnp.concatenate(outs, axis=0)]
    return f


def _both(f0, f1, n_p, n_x):
    def f(blk, p, c, x):
        c0, y0 = f0(blk, p[:n_p], c[:1], x[:n_x])
        c1, y1 = f1(blk, p[n_p:], c[1:], x[n_x:])
        return c0 + c1, y0 + y1
    return f


_BOTH_ORDERS = (["asc", "asc", "d1", "d1"], ["asc", "d1"])


def _f_hg_final(blk, p, x):
    nw, = p
    o = x[0] + x[1]
    parts = []
    for hh in range(HG_HEADS):
        cs = slice(HG_DK * hh, HG_DK * hh + HG_DK)
        parts.append(_rms(o[:, cs], nw[:, cs]))
    return [(jnp.concatenate(parts, axis=1) * _silu(x[2])).astype(bf16)]


def _conv(x, cw, cb, blk):
    rows = x.shape[0]
    r = lax.broadcasted_iota(jnp.int32, (rows, 1), 0)
    rm = jnp.where(blk == 0, r, r % CHUNK)
    seg = jnp.where(blk == 0, rows, CHUNK)

    def vmask(o):
        return ((rm + o >= 0) & (rm + o < seg)).astype(f32)

    def shifted(o):
        @jax.custom_vjp
        def sh(x, mo, mn):
            return pltpu.roll(x, (-o) % rows, 0) * mo

        def fwd(x, mo, mn):
            return sh(x, mo, mn), (mo, mn)

        def bwd(res, g):
            mo, mn = res
            return pltpu.roll(g, o % rows, 0) * mn, jnp.zeros_like(mo), jnp.zeros_like(mn)
        sh.defvjp(fwd, bwd)
        return sh(x, vmask(o), vmask(-o))

    lo = (CONV_W - 1) // 2
    out = cb
    for k in range(CONV_W):
        o = k - lo
        out = out + cw[k:k + 1] * (x if o == 0 else shifted(o))
    return out


def _f_lru_a(blk, p, x):
    cw, cb, wg, gb, lam = p
    xc = _conv(x[0], cw, cb, blk)
    n_chunks = BR_W // _LANE
    xk = [xc[:, _LANE * k:_LANE * (k + 1)] for k in range(n_chunks)]

    def gate(j):
        pre = jnp.concatenate([_dot(xk[k], wg[j * n_chunks + k], hi=_MAP_PREC) for k in range(n_chunks)], axis=1)
        return jax.nn.sigmoid(pre + gb[:, BR_W * j:BR_W * (j + 1)])

    outs = []
    for d in range(2):
        r = gate(2 * d)
        ig = gate(2 * d + 1)
        log_a = -LRU_C * r * _softplus(-lam[d:d + 1])
        outs.append(jnp.exp(log_a))
        outs.append(jnp.sqrt(_one_minus_exp(2.0 * log_a)) * (ig * xc))
    return outs


def _f_lru_c(blk, p, x):
    return [((x[0] + x[1]) * _silu(x[2])).astype(bf16)]


def _f_s5_c1(blk, p, x):
    dsk, = p
    return [jax.nn.gelu(x[0] + dsk * x[1])]


def _f_s5_c2(blk, p, x):
    bglu, = p
    return [(x[0] * jax.nn.sigmoid(x[1] + bglu) * _silu(x[2])).astype(bf16)]


def _f_m2_a(blk, p, x):
    cw, cb, dtb = p
    return [_silu(_conv(x[0], cw, cb, blk)), _softplus(x[1] + dtb)]


def _f_ssd(d):
    rev = d == 1
    hpg = M2_HEADS // M2_GROUPS

    def f(blk, p, c, x):
        alog, = p
        st, = c
        xbc, dtp = x
        a = -jnp.exp(alog[:, M2_HEADS * d:M2_HEADS * (d + 1)])
        dt = dtp[:, M2_HEADS * d:M2_HEADS * (d + 1)]
        xs = xbc[:, :BR_W]
        bm = xbc[:, BR_W:BR_W + M2_GROUPS * M2_STATE]
        cm = xbc[:, BR_W + M2_GROUPS * M2_STATE:]
        gw = hpg * M2_HEADDIM
        mf = _tri(rev).astype(f32)
        row = lax.broadcasted_iota(jnp.int32, (CHUNK, gw), 0)
        col = lax.broadcasted_iota(jnp.int32, (CHUNK, gw), 1)
        m4 = (col % CHUNK >= row) if rev else (col % CHUNK <= row)
        spread = (lax.broadcasted_iota(jnp.int32, (hpg, gw), 0)
                  == lax.div(lax.broadcasted_iota(jnp.int32, (hpg, gw), 1), M2_HEADDIM)).astype(f32)
        own = [lax.div(lax.broadcasted_iota(jnp.int32, (1, gw), 1), M2_HEADDIM) == r for r in range(hpg)]
        outs = [None] * (RB // CHUNK)
        for ci in _chunk_ids(rev):
            sl = slice(CHUNK * ci, CHUNK * ci + CHUNK)
            dtc = dt[sl]
            dta = dtc * a
            cum = _dot(mf, dta, hi=True)
            cum_t = cum.T
            dt_t = dtc.T
            ys, ns = [], []
            for g in range(M2_GROUPS):
                hs = slice(hpg * g, hpg * (g + 1))
                bmg = bm[sl, M2_STATE * g:M2_STATE * (g + 1)]
                cmg = cm[sl, M2_STATE * g:M2_STATE * (g + 1)]
                xg = xs[sl, gw * g:gw * (g + 1)]
                stg = st[M2_STATE * g:M2_STATE * (g + 1)]
                cum_i = _dot(cum[:, hs], spread, hi=True)
                cum_j = jnp.concatenate([cum_t[hpg * g + r:hpg * g + r + 1] for r in range(hpg)], axis=1)
                dt_j = jnp.concatenate([dt_t[hpg * g + r:hpg * g + r + 1] for r in range(hpg)], axis=1)
                dt_i = _dot(dtc[:, hs], spread, hi=True)
                cend_g = jnp.sum(_dot(dta[:, hs], spread, hi=True), axis=0, keepdims=True)
                decay = jnp.exp(jnp.where(m4, cum_i - cum_j, -1e30))
                scores = _dot(cmg, jnp.concatenate([bmg] * hpg, axis=0), _NT)
                w = scores * decay * dt_j
                xdiag = jnp.concatenate([jnp.where(own[r], xg, 0.0) for r in range(hpg)], axis=0)
                ys.append(_dot(w, xdiag) + _dot(cmg, stg) * jnp.exp(cum_i))
                wx = jnp.exp(cend_g - cum_i) * dt_i * xg
                ns.append(jnp.exp(cend_g) * stg + _dot(bmg, wx, _TN))
            st = jnp.concatenate(ns, axis=0)
            outs[ci] = jnp.concatenate(ys, axis=1)
        return [st], [jnp.concatenate(outs, axis=0)]
    return f


def _f_m2_c(blk, p, x):
    dsk, nw = p
    y = x[0] + x[1] + dsk * x[2][:, :BR_W]
    return [_rms(y * _silu(x[3]), nw).astype(bf16)]


def _f_loss(blk, p, x):
    fnw, = p
    err = _rms(x[0], fnw) - x[1]
    return [0.5 * jnp.mean(err * err, axis=-1, keepdims=True)]


def _blockdiag(w):
    g, a, b = w.shape
    return jnp.einsum("gab,gh->gahb", w, jnp.eye(g, dtype=w.dtype)).reshape(g * a, g * b)


def _s5_params(l, w):
    a_scan, cds = [], []
    per = _LANE // S5_GROUP

    def chunks(m):
        return jnp.stack([_blockdiag(m[k * per:(k + 1) * per]) for k in range(S5_GROUPS // per)])

    b_re = jnp.transpose(w["s5_b_re"][l], (0, 2, 1))
    b_im = jnp.transpose(w["s5_b_im"][l], (0, 2, 1))
    bd = [chunks(b_re), chunks(b_im)]
    c_re = jnp.transpose(w["s5_c_re"][l], (0, 2, 1))
    c_im = jnp.transpose(w["s5_c_im"][l], (0, 2, 1))
    for d in range(2):
        lam_re = w["s5_a_re"][l, d]
        lam_im = w["s5_a_im"][l, d]
        step = jnp.exp(w["s5_log_step"][l, d])[:, None]
        mag = jnp.exp(lam_re * step)
        ab_re = mag * jnp.cos(lam_im * step)
        ab_im = mag * jnp.sin(lam_im * step)
        den = lam_re * lam_re + lam_im * lam_im
        nr = ab_re - 1.0
        co_re = (nr * lam_re + ab_im * lam_im) / den
        co_im = (ab_im * lam_re - nr * lam_im) / den
        n_state = S5_GROUPS * S5_STATE
        a_scan.append(jnp.stack([ab_re.reshape(_SUB, n_state // _SUB), ab_im.reshape(_SUB, n_state // _SUB)]))
        cp_re = c_re * co_re[:, :, None] - c_im * co_im[:, :, None]
        cp_im = c_re * co_im[:, :, None] + c_im * co_re[:, :, None]
        cds.append([chunks(cp_re), -chunks(cp_im)])
    return a_scan, bd, cds


def _lru_gate(l, w):
    gw = w["lru_gate_w"][l]
    per = _LANE // (BR_W // LRU_BLOCKS)
    chunks = [_blockdiag(gw[d, g, k * per:(k + 1) * per])
              for d in range(2) for g in range(2) for k in range(LRU_BLOCKS // per)]
    return jnp.stack(chunks), w["lru_gate_b"][l].reshape(1, -1)


def _pad_cols(a, n):
    return jnp.pad(a, ((0, 0), (0, n - a.shape[1])))


IN_SIZES = (BR_W,) * 9 + (M2_XBC, 2 * M2_HEADS, BR_W)
IN_OFFS = tuple(sum(IN_SIZES[:i]) for i in range(len(IN_SIZES) + 1))
IN_GROUPS = (("hg_qi", 0, 2, 1024), ("hg_ff", 2, 1, 512), ("hg_fb", 3, 1, 512), ("hg_z", 4, 1, 512),
             ("s5_u", 5, 1, 512), ("s5_z", 6, 1, 512), ("lru_x", 7, 1, 512), ("lru_z", 8, 1, 512),
             ("m2_xbc", 9, 1, 768), ("m2_dt", 10, 1, 128), ("m2_z", 11, 1, 512))


def _new_slots(big):
    slots = {n: [jnp.zeros(w.shape, f32) for w in ws] for n, ws in big.items() if n not in ("w_in", "w_gate", "w_branch")}
    n_layers, d_model = len(big["w_in"]), big["w_in"][0].shape[0]
    slots["w_in"] = [{name: jnp.zeros((d_model, width), f32) for name, _, _, width in IN_GROUPS} for _ in range(n_layers)]
    slots["w_gate"] = [jnp.zeros((d_model, 4 * d_model), f32) for _ in range(n_layers)]
    slots["w_branch"] = [[jnp.zeros(w.shape[1:], f32) for _ in range(w.shape[0])] for w in big["w_branch"]]
    return slots


def _slot_grads(g):
    out = {n: jnp.stack(v) for n, v in g.items() if n not in ("w_in", "w_gate", "w_branch")}
    out["w_branch"] = jnp.stack([jnp.stack(gl) for gl in g["w_branch"]])
    out["w_in"] = jnp.stack([
        jnp.concatenate([gl[name][:, :IN_OFFS[s0 + ns] - IN_OFFS[s0]] for name, s0, ns, _ in IN_GROUPS], axis=1)
        for gl in g["w_in"]])
    d_model = g["w_gate"][0].shape[0]
    out["w_gate"] = jnp.stack([jnp.transpose(gl.reshape(d_model, 4, d_model), (1, 0, 2)) for gl in g["w_gate"]])
    return out


def _forward(p, big, slots, x, ctx, c, target):
    n_layers = p["norm_w"].shape[0]
    d_model = x.shape[-1]
    xa = jnp.concatenate([ctx, x], axis=0)
    t = xa.shape[0]
    cc = jnp.concatenate([c, p["c_ctx"][None], jnp.zeros((_SUB - 2, d_model), f32)], axis=0)
    lb_all = jnp.cumsum(jax.nn.softmax(p["hg_lb_logits"], axis=0), axis=0)
    scc, = blocked_op("silu_c", _f_silu, [], [cc], [(d_model, bf16)], rb=_SUB)
    wide = _wide_rows(t)

    for l in range(n_layers):
        tag = "l%d_" % l
        mod = mm(tag + "mod", scc, big["w_mod"][l], slots["w_mod"][l])
        bm = p["b_mod"][l][None]
        h, = blocked_op(tag + "normmod", _f_normmod, [p["norm_w"][l][None], mod, bm], [xa], [(d_model, bf16)], rb=wide)
        gnames = [g[0] for g in IN_GROUPS]
        wvs = [_pad_cols(big["w_in"][l][:, IN_OFFS[s0]:IN_OFFS[s0 + ns]], width) for _, s0, ns, width in IN_GROUPS]
        u = dict(zip(gnames, multi_mm(tag + "in_", gnames, h, wvs, [slots["w_in"][l][g] for g in gnames])))

        o_dirs = blocked_op(tag + "hg", _both(_f_hg(False), _f_hg(True), 1, 2), [lb_all[l, 0][None], lb_all[l, 1][None]],
                            [u["hg_qi"], u["hg_ff"], u["hg_qi"], u["hg_fb"]], [(BR_W, f32)] * 2,
                            order=_BOTH_ORDERS, carry_sds=[(BR_W, HG_DK)] * 2)
        y_hg, = blocked_op(tag + "hg_fin", _f_hg_final, [p["hg_norm"][l][None]], list(o_dirs) + [u["hg_z"]], [(BR_W, bf16)],
                           rb=wide)

        a_scan, bd, cds = _s5_params(l, p)
        n_state = S5_GROUPS * S5_STATE
        planes = [s.reshape(t, n_state) for s in s5_states(tag, u["s5_u"], bd, a_scan[0], a_scan[1])]
        ysum, = bd_mm(tag + "s5_c", planes, [[cds[d][part]] for d in range(2) for part in range(2)])
        g5, = blocked_op(tag + "s5_c1", _f_s5_c1, [p["s5_d"][l][None]], [ysum, u["s5_u"]], [(BR_W, f32)], rb=wide)
        gl = mm(tag + "s5_glu", g5, big["s5_w_glu"][l], slots["s5_w_glu"][l])
        y_s5, = blocked_op(tag + "s5_c2", _f_s5_c2, [p["s5_b_glu"][l][None]], [g5, gl, u["s5_z"]], [(BR_W, bf16)],
                           rb=wide)

        wg, gb = _lru_gate(l, p)
        ab = blocked_op(tag + "lru_a", _f_lru_a,
                        [p["lru_conv_w"][l], p["lru_conv_b"][l][None], wg, gb, p["lru_lam"][l]],
                        [u["lru_x"]], [(BR_W, f32)] * 4)
        hs = []
        for d in range(2):
            hs.append(rscan(tag + "lru_scan%d" % d, d, ab[2 * d], ab[2 * d + 1]))
        y_lru, = blocked_op(tag + "lru_c", _f_lru_c, [], hs + [u["lru_z"]], [(BR_W, bf16)], rb=wide)

        dtb = _pad_cols(p["m2_dt_bias"][l].reshape(1, -1), _LANE)
        xbc, dtp = blocked_op(tag + "m2_a", _f_m2_a, [p["m2_conv_w"][l], p["m2_conv_b"][l][None], dtb],
                              [u["m2_xbc"], u["m2_dt"]], [(M2_XBC, f32), (_LANE, f32)])
        alog = _pad_cols(p["m2_a_log"][l].reshape(1, -1), _LANE)
        y_dirs = blocked_op(tag + "ssd", _both(_f_ssd(0), _f_ssd(1), 1, 2), [alog, alog], [xbc, dtp, xbc, dtp],
                            [(BR_W, f32)] * 2, order=_BOTH_ORDERS,
                            carry_sds=[(M2_GROUPS * M2_STATE, BR_W // M2_GROUPS)] * 2)
        dsk = jnp.repeat(p["m2_d"][l], M2_HEADDIM)[None]
        y_m2, = blocked_op(tag + "m2_c", _f_m2_c, [dsk, p["m2_norm"][l][None]], list(y_dirs) + [xbc, u["m2_z"]], [(BR_W, bf16)],
                           rb=wide)

        wg_all = jnp.transpose(big["w_gate"][l], (1, 0, 2)).reshape(d_model, 4 * d_model)
        gp = mm(tag + "gate", h, wg_all, slots["w_gate"][l], out_dtype=bf16)
        bs = [mm(tag + "br%d" % k, yk, big["w_branch"][l][k], slots["w_branch"][l][k], out_dtype=bf16)
              for k, yk in enumerate((y_hg, y_s5, y_lru, y_m2))]
        mix, = blocked_op(tag + "mix", _f_mix, [p["b_gate"][l].reshape(1, -1)], [gp] + bs, [(d_model, bf16)])
        o = mm(tag + "out", mix, big["w_out"][l], slots["w_out"][l])
        xa, = blocked_op(tag + "resid", _f_resid, [mod, bm], [xa, o], [(d_model, f32)], rb=wide)

    rl, = blocked_op("loss", _f_loss, [p["final_norm"][None]], [xa[ctx.shape[0]:], target], [(1, f32)],
                     rb=_wide_rows(target.shape[0]))
    return jnp.sum(rl)


_MESH = pl.DeviceIdType.MESH
_ANY = pl.BlockSpec(memory_space=pl.ANY)
W_PACK = 1024


def _place():
    x, y, c = lax.axis_index("x"), lax.axis_index("y"), lax.axis_index("c")
    chips = [(x, 1 - y), (1 - x, y), (1 - x, 1 - y)]
    return x, y, c, chips


def _rcopy(src, dst, ssem, rsem, k, to):
    return pltpu.make_async_remote_copy(src_ref=src, dst_ref=dst, send_sem=ssem.at[k], recv_sem=rsem.at[k],
                                        device_id=to, device_id_type=_MESH)


def gather_shards(xs):
    n = len(xs)

    def body(*refs):
        x_refs, o_refs = refs[:n], refs[n:2 * n]
        ssem, rsem, lsem = refs[2 * n:]
        x, y, c, chips = _place()
        j = 2 * x + y
        sib = (x, y, 1 - c)
        mine = [pltpu.make_async_copy(x_refs[a], o_refs[a].at[j], lsem.at[a]) for a in range(n)]
        for cp in mine:
            cp.start()
        first = [_rcopy(x_refs[a].at[c], o_refs[a].at[j, c], ssem, rsem, 6 * a + r, (*chips[r], c))
                 for r in range(3) for a in range(n)]
        for cp in first:
            cp.start()
        passed = []
        for r in range(3):
            jr = j ^ (r + 1)
            for a in range(n):
                _rcopy(x_refs[a].at[c], o_refs[a].at[jr, c], ssem, rsem, 6 * a + r, sib).wait_recv()
                cp = _rcopy(o_refs[a].at[jr, c], o_refs[a].at[jr, c], ssem, rsem, 6 * a + 3 + r, sib)
                cp.start()
                passed.append(cp)
        for r in range(3):
            jr = j ^ (r + 1)
            for a in range(n):
                _rcopy(x_refs[a].at[c], o_refs[a].at[jr, 1 - c], ssem, rsem, 6 * a + 3 + r, sib).wait_recv()
        for cp in first + passed:
            cp.wait_send()
        for cp in mine:
            cp.wait()

    return pl.pallas_call(
        body, name="gather_shards", out_shape=[jax.ShapeDtypeStruct((4,) + x.shape, x.dtype) for x in xs],
        in_specs=[_ANY] * n, out_specs=[_ANY] * n,
        scratch_shapes=[pltpu.SemaphoreType.DMA((6 * n,)), pltpu.SemaphoreType.DMA((6 * n,)), pltpu.SemaphoreType.DMA((n,))],
    )(*xs)


def sibling_halves(gs):
    n = len(gs)

    def body(*refs):
        g_refs, o_refs = refs[:n], refs[n:2 * n]
        ssem, rsem = refs[2 * n:]
        x, y, c, _ = _place()
        sib = (x, y, 1 - c)
        cps = [_rcopy(g_refs[a].at[k, 1 - c], o_refs[a].at[k], ssem, rsem, 4 * a + k, sib)
               for k in range(4) for a in range(n)]
        for cp in cps:
            cp.start()
        for cp in cps:
            cp.wait()

    return pl.pallas_call(
        body, name="sibling_halves", out_shape=[jax.ShapeDtypeStruct((4,) + g.shape[2:], g.dtype) for g in gs],
        in_specs=[_ANY] * n, out_specs=[_ANY] * n,
        scratch_shapes=[pltpu.SemaphoreType.DMA((4 * n,)), pltpu.SemaphoreType.DMA((4 * n,))],
    )(*gs)


def scatter_chips(ps):
    n = len(ps)

    def body(*refs):
        p_refs, o_refs = refs[:n], refs[n:2 * n]
        ssem, rsem = refs[2 * n:]
        x, y, c, chips = _place()
        j = 2 * x + y
        cps = [_rcopy(p_refs[a].at[j ^ (r + 1)], o_refs[a].at[r], ssem, rsem, 3 * a + r, (*chips[r], c))
               for r in range(3) for a in range(n)]
        for cp in cps:
            cp.start()
        for cp in cps:
            cp.wait()

    return pl.pallas_call(
        body, name="scatter_chips", out_shape=[jax.ShapeDtypeStruct((3,) + p.shape[1:], p.dtype) for p in ps],
        in_specs=[_ANY] * n, out_specs=[_ANY] * n,
        scratch_shapes=[pltpu.SemaphoreType.DMA((3 * n,)), pltpu.SemaphoreType.DMA((3 * n,))],
    )(*ps)


def join_halves(qs):
    n = len(qs)

    def body(*refs):
        o_refs = refs[n:2 * n]
        ssem, rsem = refs[2 * n:]
        x, y, c, _ = _place()
        sib = (x, y, 1 - c)
        cps = [_rcopy(o_refs[a].at[c], o_refs[a].at[c], ssem, rsem, a, sib) for a in range(n)]
        for cp in cps:
            cp.start()
        for a in range(n):
            _rcopy(o_refs[a].at[c], o_refs[a].at[1 - c], ssem, rsem, a, sib).wait_recv()
        for cp in cps:
            cp.wait_send()

    return pl.pallas_call(
        body, name="join_halves", out_shape=[jax.ShapeDtypeStruct(q.shape, q.dtype) for q in qs],
        in_specs=[_ANY] * n, out_specs=[_ANY] * n, input_output_aliases={a: a for a in range(n)},
        scratch_shapes=[pltpu.SemaphoreType.DMA((n,)), pltpu.SemaphoreType.DMA((n,))],
    )(*qs)


def _rows_block(r):
    return _pick(r, 256, _SUB)


def add_sibling(tag, g, r1, place, out_dtype):
    _, _, rows, w = g.shape
    rb = _rows_block(rows)

    def body(pl_ref, g_ref, r_ref, o_ref):
        o_ref[...] = (g_ref[0] + r_ref[...]).astype(out_dtype)

    return pl.pallas_call(
        body, name="add_sibling_" + tag, out_shape=jax.ShapeDtypeStruct((4, rows, w), out_dtype),
        grid_spec=pltpu.PrefetchScalarGridSpec(
            num_scalar_prefetch=1, grid=(4, rows // rb),
            in_specs=[pl.BlockSpec((1, 1, rb, w), lambda k, i, s: (k, s[1], i, 0)),
                      pl.BlockSpec((1, rb, w), lambda k, i, s: (k, i, 0))],
            out_specs=pl.BlockSpec((1, rb, w), lambda k, i, s: (k, i, 0))),
        compiler_params=_cparams(2),
    )(place, g, r1)


def add_chips(tag, p, r2, place):
    _, rows, w = p.shape
    rb = _rows_block(rows)

    def body(pl_ref, p_ref, r_ref, o_ref):
        j = pl_ref[0]
        own = p_ref[0].astype(f32)
        others = [r_ref[0].astype(f32), r_ref[1].astype(f32), r_ref[2].astype(f32)]
        acc = None
        for k in range(4):
            rel = k ^ j
            t = jnp.where(rel == 0, own, jnp.where(rel == 1, others[0], jnp.where(rel == 2, others[1], others[2])))
            acc = t if acc is None else acc + t
        o_ref[0] = acc

    return pl.pallas_call(
        body, name="add_chips_" + tag, out_shape=jax.ShapeDtypeStruct((2, rows, w), f32),
        grid_spec=pltpu.PrefetchScalarGridSpec(
            num_scalar_prefetch=1, grid=(rows // rb,),
            in_specs=[pl.BlockSpec((1, rb, w), lambda i, s: (s[0], i, 0)),
                      pl.BlockSpec((3, rb, w), lambda i, s: (0, i, 0))],
            out_specs=pl.BlockSpec((1, rb, w), lambda i, s: (s[1], i, 0))),
        compiler_params=_cparams(1),
    )(place, p, r2)


def adamw(tag, g, w, m, v):
    rows, wd = g.shape
    rb = _rows_block(rows)

    def body(g_ref, w_ref, m_ref, v_ref, d_ref, nm_ref, nv_ref):
        gv = g_ref[...]
        nm = ADAM_B1 * m_ref[...] + (1.0 - ADAM_B1) * gv
        nv = ADAM_B2 * v_ref[...] + (1.0 - ADAM_B2) * (gv * gv)
        m_hat = nm / (1.0 - ADAM_B1 ** ADAM_STEP)
        v_hat = nv / (1.0 - ADAM_B2 ** ADAM_STEP)
        d_ref[...] = -ADAM_LR * (m_hat / (jnp.sqrt(v_hat) + ADAM_EPS) + ADAM_WD * w_ref[...])
        nm_ref[...] = nm
        nv_ref[...] = nv

    spec = pl.BlockSpec((rb, wd), lambda i: (i, 0))
    return pl.pallas_call(
        body, name="adamw_" + tag, grid=(rows // rb,), in_specs=[spec] * 4, out_specs=[spec] * 3,
        out_shape=[jax.ShapeDtypeStruct(g.shape, f32)] * 3, compiler_params=_cparams(1),
    )(g, w, m, v)


WEIGHTS = ("c_ctx", "norm_w", "w_mod", "b_mod", "w_in", "hg_lb_logits", "hg_norm", "s5_a_re", "s5_a_im", "s5_log_step",
           "s5_b_re", "s5_b_im", "s5_c_re", "s5_c_im", "s5_d", "s5_w_glu", "s5_b_glu", "lru_conv_w", "lru_conv_b",
           "lru_gate_w", "lru_gate_b", "lru_lam", "m2_conv_w", "m2_conv_b", "m2_dt_bias", "m2_a_log", "m2_d", "m2_norm",
           "w_branch", "w_gate", "b_gate", "w_out", "final_norm")
SHARD_AXIS = {"w_mod": 2, "w_in": 2, "hg_lb_logits": 2, "s5_w_glu": 1, "lru_conv_w": 2, "lru_lam": 2, "m2_conv_w": 2,
              "w_branch": 3, "w_gate": 2, "b_gate": 2, "w_out": 1}
BIG = ("w_mod", "w_in", "s5_w_glu", "w_branch", "w_gate", "w_out")
N_CHIPS = 4


def _to_rows(flat, row_unit):
    n = flat.shape[-1]
    per = 2 * row_unit * W_PACK
    total = -(-n // per) * per
    flat = jnp.pad(flat, [(0, 0)] * (flat.ndim - 1) + [(0, total - n)])
    return flat.reshape(flat.shape[:-1] + (2, total // (2 * W_PACK), W_PACK))


SMALL_SHARDED = tuple(n for n in WEIGHTS if n in SHARD_AXIS and n not in BIG)
SMALL_REPLICATED = tuple(n for n in WEIGHTS if n not in SHARD_AXIS)


def _chip_slices(a, axis):
    width = a.shape[axis] // N_CHIPS
    return jnp.stack([lax.slice_in_dim(a, k * width, (k + 1) * width, axis=axis) for k in range(N_CHIPS)])


def _gather_weights(local):
    small = jnp.concatenate([lax.bitcast_convert_type(local[n], bf16).reshape(-1) for n in SMALL_SHARDED])
    got = gather_shards([local[n].astype(bf16) for n in BIG] + [_to_rows(small, 16)])
    full = {}
    for n, g in zip(BIG, got):
        full[n] = [jnp.concatenate([g[j, l] for j in range(N_CHIPS)], axis=SHARD_AXIS[n] - 1) for l in range(g.shape[1])]
    flat, off = got[-1].reshape(N_CHIPS, -1), 0
    for n in SMALL_SHARDED:
        shp = local[n].shape
        size = 2 * math.prod(shp)
        part = lax.bitcast_convert_type(flat[:, off:off + size].reshape((N_CHIPS,) + shp + (2,)), f32)
        off += size
        full[n] = jnp.concatenate([part[j] for j in range(N_CHIPS)], axis=SHARD_AXIS[n])
    return full


def _whole_rows(v):
    n = v.shape[-1]
    return jnp.pad(v, [(0, 0)] * (v.ndim - 1) + [(0, -n % W_PACK)])


def _pack_small(vals, extra):
    return jnp.concatenate([_whole_rows(vals[n].reshape(-1)) for n in SMALL_SHARDED + SMALL_REPLICATED]
                           + [_whole_rows(extra.reshape(1))])


def _pack_small_grads(grads, loss):
    rep = [grads[n].reshape(-1) for n in SMALL_REPLICATED] + [loss.reshape(1)]
    sh = [_chip_slices(grads[n], SHARD_AXIS[n]).reshape(N_CHIPS, -1) for n in SMALL_SHARDED]
    return jnp.concatenate([_whole_rows(a) for a in sh]
                           + [_whole_rows(jnp.broadcast_to(r, (N_CHIPS,) + r.shape)) for r in rep], axis=1)


def _unpack_small(rows, like):
    out, r0 = {}, 0
    for n in SMALL_SHARDED + SMALL_REPLICATED:
        size = math.prod(like[n].shape)
        nr = -(-size // W_PACK)
        piece = lax.optimization_barrier(rows[r0:r0 + nr])
        out[n] = piece.reshape(-1)[:size].reshape(like[n].shape)
        r0 += nr
    return out, lax.optimization_barrier(rows[r0:r0 + 1])[0, 0]


def _reduce_grads(tags, gs):
    place = jnp.stack([2 * lax.axis_index("x") + lax.axis_index("y"), lax.axis_index("c")]).astype(jnp.int32)
    pairs = [add_sibling(t, g, r, place, bf16 if t in BIG else f32) for t, g, r in zip(tags, gs, sibling_halves(gs))]
    quads = [add_chips(t, p, r, place) for t, p, r in zip(tags, pairs, scatter_chips(pairs))]
    return join_halves(quads)


def kernel(x, c, ctx, c_ctx, norm_w, w_mod, b_mod, w_in, hg_lb_logits, hg_norm, s5_a_re, s5_a_im, s5_log_step, s5_b_re, s5_b_im, s5_c_re, s5_c_im, s5_d, s5_w_glu, s5_b_glu, lru_conv_w, lru_conv_b, lru_gate_w, lru_gate_b, lru_lam, m2_conv_w, m2_conv_b, m2_dt_bias, m2_a_log, m2_d, m2_norm, w_branch, w_gate, b_gate, w_out, final_norm, loss_target, m_c_ctx, m_norm_w, m_w_mod, m_b_mod, m_w_in, m_hg_lb_logits, m_hg_norm, m_s5_a_re, m_s5_a_im, m_s5_log_step, m_s5_b_re, m_s5_b_im, m_s5_c_re, m_s5_c_im, m_s5_d, m_s5_w_glu, m_s5_b_glu, m_lru_conv_w, m_lru_conv_b, m_lru_gate_w, m_lru_gate_b, m_lru_lam, m_m2_conv_w, m_m2_conv_b, m_m2_dt_bias, m_m2_a_log, m_m2_d, m_m2_norm, m_w_branch, m_w_gate, m_b_gate, m_w_out, m_final_norm, v_c_ctx, v_norm_w, v_w_mod, v_b_mod, v_w_in, v_hg_lb_logits, v_hg_norm, v_s5_a_re, v_s5_a_im, v_s5_log_step, v_s5_b_re, v_s5_b_im, v_s5_c_re, v_s5_c_im, v_s5_d, v_s5_w_glu, v_s5_b_glu, v_lru_conv_w, v_lru_conv_b, v_lru_gate_w, v_lru_gate_b, v_lru_lam, v_m2_conv_w, v_m2_conv_b, v_m2_dt_bias, v_m2_a_log, v_m2_d, v_m2_norm, v_w_branch, v_w_gate, v_b_gate, v_w_out, v_final_norm):
    given = dict(locals())
    w_loc = {n: given[n] for n in WEIGHTS}
    m_loc = {n: given["m_" + n] for n in WEIGHTS}
    v_loc = {n: given["v_" + n] for n in WEIGHTS}

    full = _gather_weights(w_loc)
    params = {n: (full[n] if n in SHARD_AXIS else w_loc[n]) for n in WEIGHTS if n not in BIG}
    big = {n: full[n] for n in BIG}
    def loss_fn(p, s, xx):
        return _forward(p, big, s, xx, ctx[0], c, loss_target[0])

    loss, (g_p, g_s, g_x) = jax.value_and_grad(loss_fn, argnums=(0, 1, 2))(params, _new_slots(big), x[0])
    grads = {**g_p, **_slot_grads(g_s)}

    def rows4(a):
        return a.reshape(a.shape[:2] + (-1, a.shape[-1]))

    g_big = [rows4(_chip_slices(grads[n], SHARD_AXIS[n])) for n in BIG]
    g_small = _to_rows(_pack_small_grads(grads, loss), 64)
    summed = _reduce_grads(list(BIG) + ["small"], g_big + [g_small])

    g_out, d_out, m_out, v_out = {}, {}, {}, {}
    for n, g in zip(BIG, summed):
        shp = w_loc[n].shape
        flat2 = lambda a: a.reshape(-1, shp[-1])
        g_out[n] = g.reshape(shp)
        d, nm, nv = adamw(n, flat2(g), flat2(w_loc[n]), flat2(m_loc[n]), flat2(v_loc[n]))
        d_out[n], m_out[n], v_out[n] = d.reshape(shp), nm.reshape(shp), nv.reshape(shp)
    zero = jnp.zeros((), f32)
    flat = lambda vals: _to_rows(_pack_small(vals, zero), 64).reshape(-1, W_PACK)
    gs = summed[-1].reshape(-1, W_PACK)
    d, nm, nv = adamw("small", gs, flat(w_loc), flat(m_loc), flat(v_loc))
    gsm, loss_out = _unpack_small(gs, w_loc)
    g_out.update(gsm)
    d_out.update(_unpack_small(d, w_loc)[0])
    m_out.update(_unpack_small(nm, w_loc)[0])
    v_out.update(_unpack_small(nv, w_loc)[0])
    outs = [loss_out, g_x[None]]
    for group in (g_out, d_out, m_out, v_out):
        outs += [group[n] for n in WEIGHTS]
    return tuple(outs)
```

```python
import functools
import math

import jax
import jax.numpy as jnp
from jax import lax
from jax.experimental import pallas as pl
from jax.experimental.pallas import tpu as pltpu

f32 = jnp.float32
bf16 = jnp.bfloat16
_MM_DTYPE = bf16
_HI = lax.Precision.HIGHEST
_MAP_PREC = lax.Precision.HIGH
_VMEM_LIMIT = 56 * 1024 * 1024
_LANE = 128
_SUB = 8

EPS = 1e-6
CONV_W = 4
CHUNK = 64
RB = 256
BR_W = 512
HG_HEADS = 4
HG_DK = 128
S5_GROUPS = 32
S5_GROUP = 16
S5_STATE = 64
LRU_BLOCKS = 8
LRU_C = 8.0
M2_HEADS = 8
M2_HEADDIM = 64
M2_GROUPS = 2
M2_STATE = 64
M2_XBC = BR_W + 2 * M2_GROUPS * M2_STATE
ADAM_LR = 0.001
ADAM_B1 = 0.9
ADAM_B2 = 0.999
ADAM_EPS = 1e-08
ADAM_WD = 0.01
ADAM_STEP = 10

_NN = (((1,), (0,)), ((), ()))
_NT = (((1,), (1,)), ((), ()))
_TN = (((0,), (0,)), ((), ()))


def _silu(x):
    return x * jax.nn.sigmoid(x)


def _softplus(x):
    return jnp.maximum(x, 0.0) + jnp.log1p(jnp.exp(-jnp.abs(x)))


def _one_minus_exp(z):
    series = -z * (1.0 + z * 0.5 * (1.0 + z * (1.0 / 3.0) * (1.0 + z * 0.25 * (1.0 + z * 0.2))))
    return jnp.where(z > -0.05, series, 1.0 - jnp.exp(z))


def _rms(x, w):
    return x * lax.rsqrt(jnp.mean(x * x, axis=-1, keepdims=True) + EPS) * w


def _dot(a, b, dn=_NN, hi=False):
    prec = hi if isinstance(hi, lax.Precision) else (_HI if hi else None)
    return lax.dot_general(a, b, dn, precision=prec, preferred_element_type=f32)


def _cparams(n_grid):
    return pltpu.CompilerParams(dimension_semantics=("arbitrary",) * n_grid, vmem_limit_bytes=_VMEM_LIMIT)


_REV = {"asc": "desc", "d1": "d1r", "desc": "asc", "d1r": "d1"}


def _blk(order, i, n):
    if order == "asc":
        return i
    if order == "desc":
        return n - 1 - i
    if order == "d1":
        return jnp.where(i == 0, 0, n - i)
    return jnp.where(i == n - 1, 0, i + 1)


def _pick(n, cap, unit):
    if n <= cap:
        return n
    best = None
    d = unit
    while d <= cap:
        if n % d == 0:
            best = d
        d += unit
    return n if best is None else best


def _mm_call(name, a, b, mode, hi, out_dtype):
    if mode == "tn":
        k, m = a.shape
        n = b.shape[1]
        tm = _pick(m, 1024 if a.dtype == bf16 else 512, _LANE)
        tn = _pick(n, 512, _LANE)
        a_spec = pl.BlockSpec((k, tm), lambda i, j: (0, i))
        b_spec = pl.BlockSpec((k, tn), lambda i, j: (0, j))
    else:
        m, k = a.shape
        n = b.shape[1] if mode == "nn" else b.shape[0]
        tn = _pick(n, max(_LANE, (8 * 1024 * 1024 // (k * b.dtype.itemsize)) // _LANE * _LANE), _LANE)
        b_spec = pl.BlockSpec((k, tn), lambda i, j: (0, j)) if mode == "nn" else pl.BlockSpec((tn, k), lambda i, j: (j, 0))
        rows_in = 9 * 512 * 1024 // (k * a.dtype.itemsize)
        rows_out = 9 * 1024 * 1024 // (tn * jnp.dtype(out_dtype).itemsize)
        tm = _pick(m, max(256, min(2176, rows_in, rows_out)), _SUB)
        a_spec = pl.BlockSpec((tm, k), lambda i, j: (i, 0))
    dn = {"nn": _NN, "nt": _NT, "tn": _TN}[mode]

    def body(a_ref, b_ref, o_ref):
        av = a_ref[...]
        bv = b_ref[...]
        if hi:
            av = av.astype(f32)
            bv = bv.astype(f32)
        else:
            av = av.astype(_MM_DTYPE)
            bv = bv.astype(_MM_DTYPE)
        o_ref[...] = _dot(av, bv, dn, hi).astype(o_ref.dtype)

    return pl.pallas_call(
        body, name=name, grid=(m // tm, n // tn), in_specs=[a_spec, b_spec],
        out_specs=pl.BlockSpec((tm, tn), lambda i, j: (i, j)),
        out_shape=jax.ShapeDtypeStruct((m, n), out_dtype), compiler_params=_cparams(2),
    )(a, b)


def mm(name, a, b, slot=None, hi=False, out_dtype=f32):
    @jax.custom_vjp
    def op(a, b, slot):
        return _mm_call(name, a, b, "nn", hi, out_dtype)

    def fwd(a, b, slot):
        return op(a, b, slot), (a, b)

    def bwd(res, g):
        a, b = res
        da = _mm_call(name + "_da", g, b, "nt", hi, a.dtype)
        db = _mm_call(name + "_db", a, g, "tn", hi, f32)
        if slot is None:
            return da, db.astype(b.dtype), None
        return da, jnp.zeros_like(b), db

    op.defvjp(fwd, bwd)
    return op(a, b, slot)


def _sum_nt_call(name, gs, ws, out_dtype):
    m = gs[0].shape[0]
    kdim = ws[0].shape[0]
    tm = _pick(m, 256, _SUB)
    n = len(gs)

    def body(*refs):
        acc = None
        for g_ref, w_ref in zip(refs[:n], refs[n:2 * n]):
            part = _dot(g_ref[...].astype(_MM_DTYPE), w_ref[...].astype(_MM_DTYPE), _NT)
            acc = part if acc is None else acc + part
        refs[2 * n][...] = acc.astype(out_dtype)

    in_specs = [pl.BlockSpec((tm, g.shape[1]), lambda i: (i, 0)) for g in gs]
    in_specs += [pl.BlockSpec(w.shape, lambda i: (0, 0)) for w in ws]
    return pl.pallas_call(
        body, name=name, grid=(m // tm,), in_specs=in_specs, out_specs=pl.BlockSpec((tm, kdim), lambda i: (i, 0)),
        out_shape=jax.ShapeDtypeStruct((m, kdim), out_dtype), compiler_params=_cparams(1),
    )(*gs, *ws)


def multi_mm(tag, names, a, ws, slots):
    @jax.custom_vjp
    def op(a, ws, slots):
        return tuple(_mm_call(tag + n, a, w, "nn", False, f32) for n, w in zip(names, ws))

    def fwd(a, ws, slots):
        return op(a, ws, slots), (a, ws)

    def bwd(res, gs):
        a, ws = res
        dws = [_mm_call(tag + n + "_db", a, g, "tn", False, f32) for n, g in zip(names, gs)]
        da = _sum_nt_call(tag + "da", list(gs), ws, a.dtype)
        return da, [jnp.zeros_like(w) for w in ws], dws

    op.defvjp(fwd, bwd)
    return op(a, list(ws), list(slots))


def _orders(order, n_x, n_o):
    if isinstance(order, str):
        return [order] * n_x, [order] * n_o
    return list(order[0]), list(order[1])


def _row(o, n):
    return lambda i: (_blk(o, i, n), 0)


def _blocked_fwd(name, f, order, rb, params, xs, out_sds, carry_sds):
    t = xs[0].shape[0]
    n = t // rb
    n_p, n_x, n_o, n_c = len(params), len(xs), len(out_sds), len(carry_sds)
    xo, oo = _orders(order, n_x, n_o)

    def body(*refs):
        p_refs = refs[:n_p]
        x_refs = refs[n_p:n_p + n_x]
        o_refs = refs[n_p + n_x:n_p + n_x + n_o]
        st_refs = refs[n_p + n_x + n_o:n_p + n_x + n_o + n_c]
        c_refs = refs[n_p + n_x + n_o + n_c:]
        i = pl.program_id(0)
        blk = _blk(xo[0], i, n)
        p = [r[...] for r in p_refs]
        x = [r[...] for r in x_refs]
        if n_c:
            @pl.when(i == 0)
            def _():
                for c in c_refs:
                    c[...] = jnp.zeros_like(c)
            c_in = [c[...] for c in c_refs]
            for sr, c in zip(st_refs, c_in):
                sr[0] = c
            c_out, ys = f(blk, p, c_in, x)
            for c, v in zip(c_refs, c_out):
                c[...] = v
        else:
            ys = f(blk, p, x)
        for o, y in zip(o_refs, ys):
            o[...] = y.astype(o.dtype)

    in_specs = [pl.BlockSpec(p.shape, lambda i, nd=p.ndim: (0,) * nd) for p in params]
    in_specs += [pl.BlockSpec((rb, x.shape[1]), _row(o, n)) for x, o in zip(xs, xo)]
    out_specs = [pl.BlockSpec((rb, c), _row(o, n)) for (c, _), o in zip(out_sds, oo)]
    out_specs += [pl.BlockSpec((1,) + s, lambda i: (i, 0, 0)) for s in carry_sds]
    out_shape = [jax.ShapeDtypeStruct((t, c), d) for c, d in out_sds]
    out_shape += [jax.ShapeDtypeStruct((n,) + s, f32) for s in carry_sds]
    res = pl.pallas_call(
        body, name=name, grid=(n,), in_specs=in_specs, out_specs=out_specs, out_shape=out_shape,
        scratch_shapes=[pltpu.VMEM(s, f32) for s in carry_sds], compiler_params=_cparams(1),
    )(*params, *xs)
    return list(res[:n_o]), list(res[n_o:])


def _blocked_bwd(name, f, order, rb, params, xs, states, dys, carry_sds):
    t = xs[0].shape[0]
    n = t // rb
    n_p, n_x, n_o, n_c = len(params), len(xs), len(dys), len(carry_sds)
    xo, oo = _orders(order, n_x, n_o)
    xo, oo = [_REV[o] for o in xo], [_REV[o] for o in oo]

    def body(*refs):
        k = 0
        p_refs = refs[k:k + n_p]; k += n_p
        x_refs = refs[k:k + n_x]; k += n_x
        st_refs = refs[k:k + n_c]; k += n_c
        dy_refs = refs[k:k + n_o]; k += n_o
        dp_refs = refs[k:k + n_p]; k += n_p
        dx_refs = refs[k:k + n_x]; k += n_x
        dc_refs = refs[k:]
        i = pl.program_id(0)
        blk = _blk(xo[0], i, n)
        p = [r[...] for r in p_refs]
        x = [r[...] for r in x_refs]
        dy = [r[...] for r in dy_refs]
        if n_c:
            @pl.when(i == 0)
            def _():
                for c in dc_refs:
                    c[...] = jnp.zeros_like(c)
            c_in = [r[0] for r in st_refs]
            dc = [c[...] for c in dc_refs]
            _, vjp = jax.vjp(lambda p_, c_, x_: f(blk, p_, c_, x_), p, c_in, x)
            dp, dcin, dx = vjp((dc, dy))
            for c, v in zip(dc_refs, dcin):
                c[...] = v
        else:
            _, vjp = jax.vjp(lambda p_, x_: f(blk, p_, x_), p, x)
            dp, dx = vjp(dy)

        @pl.when(i == 0)
        def _():
            for r, v in zip(dp_refs, dp):
                r[...] = v

        @pl.when(i > 0)
        def _():
            for r, v in zip(dp_refs, dp):
                r[...] += v
        for r, v in zip(dx_refs, dx):
            r[...] = v.astype(r.dtype)

    in_specs = [pl.BlockSpec(p.shape, lambda i, nd=p.ndim: (0,) * nd) for p in params]
    in_specs += [pl.BlockSpec((rb, x.shape[1]), _row(o, n)) for x, o in zip(xs, xo)]
    in_specs += [pl.BlockSpec((1,) + s, lambda i: (n - 1 - i, 0, 0)) for s in carry_sds]
    in_specs += [pl.BlockSpec((rb, d.shape[1]), _row(o, n)) for d, o in zip(dys, oo)]
    out_specs = [pl.BlockSpec(p.shape, lambda i, nd=p.ndim: (0,) * nd) for p in params]
    out_specs += [pl.BlockSpec((rb, x.shape[1]), _row(o, n)) for x, o in zip(xs, xo)]
    out_shape = [jax.ShapeDtypeStruct(p.shape, f32) for p in params]
    out_shape += [jax.ShapeDtypeStruct(x.shape, x.dtype) for x in xs]
    res = pl.pallas_call(
        body, name=name + "_bwd", grid=(n,), in_specs=in_specs, out_specs=out_specs, out_shape=out_shape,
        scratch_shapes=[pltpu.VMEM(s, f32) for s in carry_sds], compiler_params=_cparams(1),
    )(*params, *xs, *states, *dys)
    return list(res[:n_p]), list(res[n_p:])


def blocked_op(name, f, params, xs, out_sds, order="asc", carry_sds=(), rb=RB):
    carry_sds = tuple(carry_sds)

    @jax.custom_vjp
    def op(params, xs):
        return tuple(_blocked_fwd(name, f, order, rb, params, xs, out_sds, carry_sds)[0])

    def fwd(params, xs):
        ys, states = _blocked_fwd(name, f, order, rb, params, xs, out_sds, carry_sds)
        return tuple(ys), (params, xs, states)

    def bwd(res, dys):
        params, xs, states = res
        dp, dx = _blocked_bwd(name, f, order, rb, params, xs, states, list(dys), carry_sds)
        return list(dp), list(dx)

    op.defvjp(fwd, bwd)
    return op(list(params), list(xs))


def _cscan_call(name, order, asc, a, xr, xi, sr=None, si=None):
    t = xr.shape[0]
    n = t // RB
    tile = xr.shape[1:]
    xspec = pl.BlockSpec((RB,) + tile, lambda i: (_blk(order, i, n), 0, 0))
    aspec = pl.BlockSpec(a.shape, lambda i: (0, 0, 0))
    plane = jax.ShapeDtypeStruct(xr.shape, f32)

    def rowidx(tt):
        return tt if asc else RB - 1 - tt

    if sr is None:
        def body(a_ref, xr_ref, xi_ref, sr_ref, si_ref, c_ref):
            i = pl.program_id(0)

            @pl.when(i == 0)
            def _():
                c_ref[...] = jnp.zeros_like(c_ref)
            ar = a_ref[0]
            ai = a_ref[1]
            a2r = ar * ar - ai * ai
            a2i = 2.0 * ar * ai

            def step(tt, carry):
                cr, ci = carry
                r1 = rowidx(2 * tt)
                r2 = rowidx(2 * tt + 1)
                x1r, x1i, x2r, x2i = xr_ref[r1], xi_ref[r1], xr_ref[r2], xi_ref[r2]
                s1r = ar * cr - ai * ci + x1r
                s1i = ar * ci + ai * cr + x1i
                kr = ar * x1r - ai * x1i + x2r
                ki = ar * x1i + ai * x1r + x2i
                s2r = a2r * cr - a2i * ci + kr
                s2i = a2r * ci + a2i * cr + ki
                sr_ref[r1] = s1r
                si_ref[r1] = s1i
                sr_ref[r2] = s2r
                si_ref[r2] = s2i
                return s2r, s2i
            cr, ci = lax.fori_loop(0, RB // 2, step, (c_ref[0], c_ref[1]), unroll=4)
            c_ref[0] = cr
            c_ref[1] = ci

        return pl.pallas_call(
            body, name=name, grid=(n,), in_specs=[aspec, xspec, xspec], out_specs=[xspec, xspec],
            out_shape=[plane, plane], scratch_shapes=[pltpu.VMEM(a.shape, f32)], compiler_params=_cparams(1),
        )(a, xr, xi)

    def body(a_ref, xr_ref, xi_ref, sr_ref, si_ref, gr_ref, gi_ref, da_ref, c_ref):
        i = pl.program_id(0)

        @pl.when(i == 0)
        def _():
            c_ref[...] = jnp.zeros_like(c_ref)
            da_ref[...] = jnp.zeros_like(da_ref)
        ar = a_ref[0]
        ai = a_ref[1]
        a2r = ar * ar - ai * ai
        a2i = 2.0 * ar * ai

        def step(tt, carry):
            gr, gi, dar, dai, dbr, dbi = carry
            r1 = rowidx(2 * tt)
            r2 = rowidx(2 * tt + 1)
            x1r, x1i, x2r, x2i = xr_ref[r1], xi_ref[r1], xr_ref[r2], xi_ref[r2]
            v1r, v1i, v2r, v2i = sr_ref[r1], si_ref[r1], sr_ref[r2], si_ref[r2]
            g1r = x1r + ar * gr + ai * gi
            g1i = x1i + ar * gi - ai * gr
            kr = x2r + ar * x1r + ai * x1i
            ki = x2i + ar * x1i - ai * x1r
            g2r = kr + a2r * gr + a2i * gi
            g2i = ki + a2r * gi - a2i * gr
            dar = dar + gr * v1r + gi * v1i
            dai = dai + gi * v1r - gr * v1i
            dbr = dbr + g1r * v2r + g1i * v2i
            dbi = dbi + g1i * v2r - g1r * v2i
            gr_ref[r1] = g1r
            gi_ref[r1] = g1i
            gr_ref[r2] = g2r
            gi_ref[r2] = g2i
            return g2r, g2i, dar, dai, dbr, dbi
        z = jnp.zeros(tile, f32)
        gr, gi, dar, dai, dbr, dbi = lax.fori_loop(0, RB // 2, step, (c_ref[0], c_ref[1], z, z, z, z), unroll=4)
        c_ref[0] = gr
        c_ref[1] = gi
        da_ref[0] += dar + dbr
        da_ref[1] += dai + dbi

    return pl.pallas_call(
        body, name=name, grid=(n,), in_specs=[aspec] + [xspec] * 4, out_specs=[xspec, xspec, aspec],
        out_shape=[plane, plane, jax.ShapeDtypeStruct(a.shape, f32)],
        scratch_shapes=[pltpu.VMEM(a.shape, f32)], compiler_params=_cparams(1),
    )(a, xr, xi, sr, si)


def s5_states(tag, u, bd, a0, a1):
    t = u.shape[0]
    tile = a0.shape[1:]

    def run(u, bd, a0, a1):
        bu = [b.reshape((t,) + tile) for b in _bd_call(tag + "s5_bu", [u], [bd], "nn")]
        s0 = _cscan_call(tag + "s5_scan0", "asc", True, a0, bu[0], bu[1])
        s1 = _cscan_call(tag + "s5_scan1", "d1", False, a1, bu[0], bu[1])
        return (s0[0], s0[1], s1[0], s1[1])

    @jax.custom_vjp
    def op(u, bd, a0, a1):
        return run(u, bd, a0, a1)

    def fwd(u, bd, a0, a1):
        s = run(u, bd, a0, a1)
        return s, (u, bd, a0, a1, s)

    def bwd(res, ds):
        u, bd, a0, a1, s = res
        g0r, g0i, da0 = _cscan_call(tag + "s5_scan0_bwd", "desc", False, a0, ds[0], ds[1], s[0], s[1])
        g1r, g1i, da1 = _cscan_call(tag + "s5_scan1_bwd", "d1r", True, a1, ds[2], ds[3], s[2], s[3])
        gs = [g.reshape(t, -1) for g in (g0r, g0i, g1r, g1i)]
        du, = _bd_call(tag + "s5_bu_da", gs, [[bd[0], bd[1], bd[0], bd[1]]], "nt")
        k = bd[0].shape[0]
        dbd = [_bd_call(tag + "s5_bu_db%d" % j, u, gs[j], "tn", k) + _bd_call(tag + "s5_bu_db%d" % (j + 2), u, gs[j + 2], "tn", k)
               for j in range(2)]
        return du, dbd, da0, da1

    op.defvjp(fwd, bwd)
    return op(u, list(bd), a0, a1)


def _bd_call(name, a, b, mode, k=None):
    if mode == "tn":
        t = a.shape[0]
        ck, cn = a.shape[1] // k, b.shape[1] // k

        def body(a_ref, b_ref, o_ref):
            o_ref[0] = _dot(a_ref[...], b_ref[...], _TN, _MAP_PREC)

        return pl.pallas_call(
            body, name=name, grid=(k,),
            in_specs=[pl.BlockSpec((t, ck), lambda j: (0, j)), pl.BlockSpec((t, cn), lambda j: (0, j))],
            out_specs=pl.BlockSpec((1, ck, cn), lambda j: (j, 0, 0)),
            out_shape=jax.ShapeDtypeStruct((k, ck, cn), f32), compiler_params=_cparams(1),
        )(a, b)
    k, ck, cn = b[0][0].shape
    n_i, n_o = len(b), len(b[0])
    n_x = len(a)
    t = a[0].shape[0]
    tm = _pick(t, 2176, _SUB)
    flat = [w for row in b for w in row]
    win, wout, n_out, dn = (ck, cn, n_o, _NN) if mode == "nn" else (cn, ck, n_i, _NT)

    def body(*refs):
        xv = [r[...] for r in refs[:n_x]]
        w_refs = refs[n_x:n_x + len(flat)]
        o_refs = refs[n_x + len(flat):]
        for q in range(n_out):
            acc = None
            for s in range(n_x):
                w = w_refs[s * n_o + q] if mode == "nn" else w_refs[q * n_o + s]
                part = _dot(xv[s], w[0], dn, _MAP_PREC)
                acc = part if acc is None else acc + part
            o_refs[q][...] = acc

    return pl.pallas_call(
        body, name=name, grid=(t // tm, k),
        in_specs=[pl.BlockSpec((tm, win), lambda i, j: (i, j))] * n_x
        + [pl.BlockSpec((1, ck, cn), lambda i, j: (j, 0, 0))] * len(flat),
        out_specs=[pl.BlockSpec((tm, wout), lambda i, j: (i, j))] * n_out,
        out_shape=[jax.ShapeDtypeStruct((t, k * wout), f32)] * n_out, compiler_params=_cparams(2),
    )(*a, *flat)


def bd_mm(name, xs, ws):
    k = ws[0][0].shape[0]

    @jax.custom_vjp
    def op(xs, ws):
        return tuple(_bd_call(name, xs, ws, "nn"))

    def fwd(xs, ws):
        return op(xs, ws), (xs, ws)

    def bwd(res, gs):
        xs, ws = res
        dws = [[_bd_call(name + "_db%d%d" % (i, o), x, g, "tn", k) for o, g in enumerate(gs)] for i, x in enumerate(xs)]
        return list(_bd_call(name + "_da", list(gs), ws, "nt")), dws

    op.defvjp(fwd, bwd)
    return op(list(xs), [list(row) for row in ws])


def _rscan_call(name, order, asc, a, x, hp=None):
    t = x.shape[0]
    n = t // RB
    cshape = (1, x.shape[1])
    xspec = pl.BlockSpec((RB, x.shape[1]), lambda i: (_blk(order, i, n), 0))

    def rd(ref, r):
        return ref[pl.ds(r, 1), :]

    def wr(ref, r, v):
        ref[pl.ds(r, 1), :] = v

    def rowidx(tt):
        return tt if asc else RB - 1 - tt

    if hp is None:
        def body(a_ref, x_ref, h_ref, hp_ref, c_ref):
            i = pl.program_id(0)

            @pl.when(i == 0)
            def _():
                c_ref[...] = jnp.zeros_like(c_ref)

            def step(tt, h):
                r1 = rowidx(2 * tt)
                r2 = rowidx(2 * tt + 1)
                a1, a2, x1, x2 = rd(a_ref, r1), rd(a_ref, r2), rd(x_ref, r1), rd(x_ref, r2)
                h1 = a1 * h + x1
                h2 = (a2 * a1) * h + (a2 * x1 + x2)
                wr(hp_ref, r1, h)
                wr(h_ref, r1, h1)
                wr(hp_ref, r2, h1)
                wr(h_ref, r2, h2)
                return h2
            c_ref[...] = lax.fori_loop(0, RB // 2, step, c_ref[...], unroll=4)

        return pl.pallas_call(
            body, name=name, grid=(n,), in_specs=[xspec, xspec], out_specs=[xspec, xspec],
            out_shape=[jax.ShapeDtypeStruct(x.shape, f32)] * 2, scratch_shapes=[pltpu.VMEM(cshape, f32)],
            compiler_params=_cparams(1),
        )(a, x)

    def body(a_ref, x_ref, hp_ref, da_ref, db_ref, c_ref):
        i = pl.program_id(0)

        @pl.when(i == 0)
        def _():
            c_ref[...] = jnp.zeros_like(c_ref)

        def step(tt, c):
            r1 = rowidx(2 * tt)
            r2 = rowidx(2 * tt + 1)
            a1, a2, x1, x2 = rd(a_ref, r1), rd(a_ref, r2), rd(x_ref, r1), rd(x_ref, r2)
            g1 = x1 + c
            k = x2 + a1 * x1
            g2 = k + a1 * c
            wr(db_ref, r1, g1)
            wr(da_ref, r1, g1 * rd(hp_ref, r1))
            wr(db_ref, r2, g2)
            wr(da_ref, r2, g2 * rd(hp_ref, r2))
            return a2 * k + (a2 * a1) * c
        c_ref[...] = lax.fori_loop(0, RB // 2, step, c_ref[...], unroll=4)

    return pl.pallas_call(
        body, name=name, grid=(n,), in_specs=[xspec, xspec, xspec], out_specs=[xspec, xspec],
        out_shape=[jax.ShapeDtypeStruct(x.shape, f32)] * 2, scratch_shapes=[pltpu.VMEM(cshape, f32)],
        compiler_params=_cparams(1),
    )(a, x, hp)


def rscan(name, d, a, x):
    order = "d1" if d else "asc"

    @jax.custom_vjp
    def op(a, x):
        return _rscan_call(name, order, d == 0, a, x)[0]

    def fwd(a, x):
        h, hp = _rscan_call(name, order, d == 0, a, x)
        return h, (a, hp)

    def bwd(res, dh):
        a, hp = res
        da, db = _rscan_call(name + "_bwd", _REV[order], d != 0, a, dh, hp)
        return da, db

    op.defvjp(fwd, bwd)
    return op(a, x)


def _wide_rows(t):
    return _pick(t, 544, 16)


def _mod_part(blk, rows, mod, bm, lo, hi):
    is_ctx = blk * rows + lax.broadcasted_iota(jnp.int32, (rows, 1), 0) < RB
    r = mod[:, lo:hi] + bm[:, lo:hi]
    return jnp.where(is_ctx, r[1:2], r[0:1])


def _f_silu(blk, p, x):
    return [_silu(x[0]).astype(bf16)]


def _f_normmod(blk, p, x):
    nw, mod, bm = p
    rows, d = x[0].shape
    shift = _mod_part(blk, rows, mod, bm, 0, d)
    scale = _mod_part(blk, rows, mod, bm, d, 2 * d)
    return [(_rms(x[0], nw) * (1.0 + scale) + shift).astype(bf16)]


def _f_resid(blk, p, x):
    mod, bm = p
    rows, d = x[0].shape
    return [x[0] + _mod_part(blk, rows, mod, bm, 2 * d, 3 * d) * x[1]]


def _f_mix(blk, p, x):
    bg, = p
    gp = x[0]
    d = x[1].shape[1]
    acc = None
    for k in range(4):
        t = jax.nn.sigmoid(gp[:, k * d:(k + 1) * d] + bg[:, k * d:(k + 1) * d]) * x[1 + k]
        acc = t if acc is None else acc + t
    return [acc.astype(bf16)]


def _tri(rev):
    row = lax.broadcasted_iota(jnp.int32, (CHUNK, CHUNK), 0)
    col = lax.broadcasted_iota(jnp.int32, (CHUNK, CHUNK), 1)
    return (col >= row) if rev else (col <= row)


def _chunk_ids(rev):
    ids = list(range(RB // CHUNK))
    return ids[::-1] if rev else ids


def _f_hg(rev):
    def f(blk, p, c, x):
        lb, = p
        st, = c
        qi, fr = x
        q = _silu(qi[:, :BR_W])
        v = qi[:, BR_W:]
        fg = lb + (1.0 - lb) * jax.nn.sigmoid(fr)
        logf = jnp.log(fg)
        k = 1.0 - fg
        m = _tri(rev)
        mf = m.astype(f32)
        outs = [None] * (RB // CHUNK)
        for ci in _chunk_ids(rev):
            sl = slice(CHUNK * ci, CHUNK * ci + CHUNK)
            lf = logf[sl]
            b = _dot(mf, lf, hi=True)
            bend = jnp.sum(lf, axis=0, keepdims=True)
            mid = 0.5 * bend
            qe = q[sl] * jnp.exp(b - mid)
            ke = k[sl] * jnp.exp(mid - b)
            kd = k[sl] * jnp.exp(bend - b)
            qb = q[sl] * jnp.exp(b)
            dec = jnp.exp(bend)
            vc = v[sl]
            oh, ns = [], []
            for hh in range(HG_HEADS):
                cs = slice(HG_DK * hh, HG_DK * hh + HG_DK)
                sth = st[cs]
                att = jnp.where(m, _dot(qe[:, cs], ke[:, cs], _NT), 0.0)
                oh.append(_dot(att, vc[:, cs]) + _dot(qb[:, cs], sth, _NT))
                ns.append(sth * dec[:, cs] + _dot(vc[:, cs], kd[:, cs], _TN))
            st = jnp.concatenate(ns, axis=0)
            outs[ci] = jnp.concatenate(oh, axis=1)
        return [st], [jnp.concatenate(outs, axis=0)]
    return f


def _both(f0, f1, n_p, n_x):
    def f(blk, p, c, x):
        c0, y0 = f0(blk, p[:n_p], c[:1], x[:n_x])
        c1, y1 = f1(blk, p[n_p:], c[1:], x[n_x:])
        return c0 + c1, y0 + y1
    return f


_BOTH_ORDERS = (["asc", "asc", "d1", "d1"], ["asc", "d1"])


def _f_hg_final(blk, p, x):
    nw, = p
    o = x[0] + x[1]
    parts = []
    for hh in range(HG_HEADS):
        cs = slice(HG_DK * hh, HG_DK * hh + HG_DK)
        parts.append(_rms(o[:, cs], nw[:, cs]))
    return [(jnp.concatenate(parts, axis=1) * _silu(x[2])).astype(bf16)]


def _conv(x, cw, cb, blk):
    rows = x.shape[0]
    r = lax.broadcasted_iota(jnp.int32, (rows, 1), 0)
    rm = jnp.where(blk == 0, r, r % CHUNK)
    seg = jnp.where(blk == 0, rows, CHUNK)

    def vmask(o):
        return ((rm + o >= 0) & (rm + o < seg)).astype(f32)

    def shifted(o):
        @jax.custom_vjp
        def sh(x, mo, mn):
            return pltpu.roll(x, (-o) % rows, 0) * mo

        def fwd(x, mo, mn):
            return sh(x, mo, mn), (mo, mn)

        def bwd(res, g):
            mo, mn = res
            return pltpu.roll(g, o % rows, 0) * mn, jnp.zeros_like(mo), jnp.zeros_like(mn)
        sh.defvjp(fwd, bwd)
        return sh(x, vmask(o), vmask(-o))

    lo = (CONV_W - 1) // 2
    out = cb
    for k in range(CONV_W):
        o = k - lo
        out = out + cw[k:k + 1] * (x if o == 0 else shifted(o))
    return out


def _f_lru_a(blk, p, x):
    cw, cb, wg, gb, lam = p
    xc = _conv(x[0], cw, cb, blk)
    n_chunks = BR_W // _LANE
    xk = [xc[:, _LANE * k:_LANE * (k + 1)] for k in range(n_chunks)]

    def gate(j):
        pre = jnp.concatenate([_dot(xk[k], wg[j * n_chunks + k], hi=_MAP_PREC) for k in range(n_chunks)], axis=1)
        return jax.nn.sigmoid(pre + gb[:, BR_W * j:BR_W * (j + 1)])

    outs = []
    for d in range(2):
        r = gate(2 * d)
        ig = gate(2 * d + 1)
        log_a = -LRU_C * r * _softplus(-lam[d:d + 1])
        outs.append(jnp.exp(log_a))
        outs.append(jnp.sqrt(_one_minus_exp(2.0 * log_a)) * (ig * xc))
    return outs


def _f_lru_c(blk, p, x):
    return [((x[0] + x[1]) * _silu(x[2])).astype(bf16)]


def _f_s5_c1(blk, p, x):
    dsk, = p
    return [jax.nn.gelu(x[0] + dsk * x[1])]


def _f_s5_c2(blk, p, x):
    bglu, = p
    return [(x[0] * jax.nn.sigmoid(x[1] + bglu) * _silu(x[2])).astype(bf16)]


def _f_m2_a(blk, p, x):
    cw, cb, dtb = p
    return [_silu(_conv(x[0], cw, cb, blk)), _softplus(x[1] + dtb)]


def _f_ssd(d):
    rev = d == 1
    hpg = M2_HEADS // M2_GROUPS

    def f(blk, p, c, x):
        alog, = p
        st, = c
        xbc, dtp = x
        a = -jnp.exp(alog[:, M2_HEADS * d:M2_HEADS * (d + 1)])
        dt = dtp[:, M2_HEADS * d:M2_HEADS * (d + 1)]
        xs = xbc[:, :BR_W]
        bm = xbc[:, BR_W:BR_W + M2_GROUPS * M2_STATE]
        cm = xbc[:, BR_W + M2_GROUPS * M2_STATE:]
        gw = hpg * M2_HEADDIM
        mf = _tri(rev).astype(f32)
        row = lax.broadcasted_iota(jnp.int32, (CHUNK, gw), 0)
        col = lax.broadcasted_iota(jnp.int32, (CHUNK, gw), 1)
        m4 = (col % CHUNK >= row) if rev else (col % CHUNK <= row)
        spread = (lax.broadcasted_iota(jnp.int32, (hpg, gw), 0)
                  == lax.div(lax.broadcasted_iota(jnp.int32, (hpg, gw), 1), M2_HEADDIM)).astype(f32)
        own = [lax.div(lax.broadcasted_iota(jnp.int32, (1, gw), 1), M2_HEADDIM) == r for r in range(hpg)]
        outs = [None] * (RB // CHUNK)
        for ci in _chunk_ids(rev):
            sl = slice(CHUNK * ci, CHUNK * ci + CHUNK)
            dtc = dt[sl]
            dta = dtc * a
            cum = _dot(mf, dta, hi=True)
            cum_t = cum.T
            dt_t = dtc.T
            ys, ns = [], []
            for g in range(M2_GROUPS):
                hs = slice(hpg * g, hpg * (g + 1))
                bmg = bm[sl, M2_STATE * g:M2_STATE * (g + 1)]
                cmg = cm[sl, M2_STATE * g:M2_STATE * (g + 1)]
                xg = xs[sl, gw * g:gw * (g + 1)]
                stg = st[M2_STATE * g:M2_STATE * (g + 1)]
                cum_i = _dot(cum[:, hs], spread, hi=True)
                cum_j = jnp.concatenate([cum_t[hpg * g + r:hpg * g + r + 1] for r in range(hpg)], axis=1)
                dt_j = jnp.concatenate([dt_t[hpg * g + r:hpg * g + r + 1] for r in range(hpg)], axis=1)
                dt_i = _dot(dtc[:, hs], spread, hi=True)
                cend_g = jnp.sum(_dot(dta[:, hs], spread, hi=True), axis=0, keepdims=True)
                decay = jnp.exp(jnp.where(m4, cum_i - cum_j, -1e30))
                scores = _dot(cmg, jnp.concatenate([bmg] * hpg, axis=0), _NT)
                w = scores * decay * dt_j
                xdiag = jnp.concatenate([jnp.where(own[r], xg, 0.0) for r in range(hpg)], axis=0)
                ys.append(_dot(w, xdiag) + _dot(cmg, stg) * jnp.exp(cum_i))
                wx = jnp.exp(cend_g - cum_i) * dt_i * xg
                ns.append(jnp.exp(cend_g) * stg + _dot(bmg, wx, _TN))
            st = jnp.concatenate(ns, axis=0)
            outs[ci] = jnp.concatenate(ys, axis=1)
        return [st], [jnp.concatenate(outs, axis=0)]
    return f


def _f_m2_c(blk, p, x):
    dsk, nw = p
    y = x[0] + x[1] + dsk * x[2][:, :BR_W]
    return [_rms(y * _silu(x[3]), nw).astype(bf16)]


def _f_loss(blk, p, x):
    fnw, = p
    err = _rms(x[0], fnw) - x[1]
    return [0.5 * jnp.mean(err * err, axis=-1, keepdims=True)]


def _blockdiag(w):
    g, a, b = w.shape
    return jnp.einsum("gab,gh->gahb", w, jnp.eye(g, dtype=w.dtype)).reshape(g * a, g * b)


def _s5_params(l, w):
    a_scan, cds = [], []
    per = _LANE // S5_GROUP

    def chunks(m):
        return jnp.stack([_blockdiag(m[k * per:(k + 1) * per]) for k in range(S5_GROUPS // per)])

    b_re = jnp.transpose(w["s5_b_re"][l], (0, 2, 1))
    b_im = jnp.transpose(w["s5_b_im"][l], (0, 2, 1))
    bd = [chunks(b_re), chunks(b_im)]
    c_re = jnp.transpose(w["s5_c_re"][l], (0, 2, 1))
    c_im = jnp.transpose(w["s5_c_im"][l], (0, 2, 1))
    for d in range(2):
        lam_re = w["s5_a_re"][l, d]
        lam_im = w["s5_a_im"][l, d]
        step = jnp.exp(w["s5_log_step"][l, d])[:, None]
        mag = jnp.exp(lam_re * step)
        ab_re = mag * jnp.cos(lam_im * step)
        ab_im = mag * jnp.sin(lam_im * step)
        den = lam_re * lam_re + lam_im * lam_im
        nr = ab_re - 1.0
        co_re = (nr * lam_re + ab_im * lam_im) / den
        co_im = (ab_im * lam_re - nr * lam_im) / den
        n_state = S5_GROUPS * S5_STATE
        a_scan.append(jnp.stack([ab_re.reshape(_SUB, n_state // _SUB), ab_im.reshape(_SUB, n_state // _SUB)]))
        cp_re = c_re * co_re[:, :, None] - c_im * co_im[:, :, None]
        cp_im = c_re * co_im[:, :, None] + c_im * co_re[:, :, None]
        cds.append([chunks(cp_re), -chunks(cp_im)])
    return a_scan, bd, cds


def _lru_gate(l, w):
    gw = w["lru_gate_w"][l]
    per = _LANE // (BR_W // LRU_BLOCKS)
    chunks = [_blockdiag(gw[d, g, k * per:(k + 1) * per])
              for d in range(2) for g in range(2) for k in range(LRU_BLOCKS // per)]
    return jnp.stack(chunks), w["lru_gate_b"][l].reshape(1, -1)


def _pad_cols(a, n):
    return jnp.pad(a, ((0, 0), (0, n - a.shape[1])))


IN_SIZES = (BR_W,) * 9 + (M2_XBC, 2 * M2_HEADS, BR_W)
IN_OFFS = tuple(sum(IN_SIZES[:i]) for i in range(len(IN_SIZES) + 1))
IN_GROUPS = (("hg_qi", 0, 2, 1024), ("hg_ff", 2, 1, 512), ("hg_fb", 3, 1, 512), ("hg_z", 4, 1, 512),
             ("s5_u", 5, 1, 512), ("s5_z", 6, 1, 512), ("lru_x", 7, 1, 512), ("lru_z", 8, 1, 512),
             ("m2_xbc", 9, 1, 768), ("m2_dt", 10, 1, 128), ("m2_z", 11, 1, 512))


def _new_slots(big):
    slots = {n: [jnp.zeros(w.shape, f32) for w in ws] for n, ws in big.items() if n not in ("w_in", "w_gate", "w_branch")}
    n_layers, d_model = len(big["w_in"]), big["w_in"][0].shape[0]
    slots["w_in"] = [{name: jnp.zeros((d_model, width), f32) for name, _, _, width in IN_GROUPS} for _ in range(n_layers)]
    slots["w_gate"] = [jnp.zeros((d_model, 4 * d_model), f32) for _ in range(n_layers)]
    slots["w_branch"] = [[jnp.zeros(w.shape[1:], f32) for _ in range(w.shape[0])] for w in big["w_branch"]]
    return slots


def _slot_grads(g):
    out = {n: jnp.stack(v) for n, v in g.items() if n not in ("w_in", "w_gate", "w_branch")}
    out["w_branch"] = jnp.stack([jnp.stack(gl) for gl in g["w_branch"]])
    out["w_in"] = jnp.stack([
        jnp.concatenate([gl[name][:, :IN_OFFS[s0 + ns] - IN_OFFS[s0]] for name, s0, ns, _ in IN_GROUPS], axis=1)
        for gl in g["w_in"]])
    d_model = g["w_gate"][0].shape[0]
    out["w_gate"] = jnp.stack([jnp.transpose(gl.reshape(d_model, 4, d_model), (1, 0, 2)) for gl in g["w_gate"]])
    return out


def _forward(p, big, slots, x, ctx, c, target):
    n_layers = p["norm_w"].shape[0]
    d_model = x.shape[-1]
    xa = jnp.concatenate([ctx, x], axis=0)
    t = xa.shape[0]
    cc = jnp.concatenate([c, p["c_ctx"][None], jnp.zeros((_SUB - 2, d_model), f32)], axis=0)
    lb_all = jnp.cumsum(jax.nn.softmax(p["hg_lb_logits"], axis=0), axis=0)
    scc, = blocked_op("silu_c", _f_silu, [], [cc], [(d_model, bf16)], rb=_SUB)
    wide = _wide_rows(t)

    for l in range(n_layers):
        tag = "l%d_" % l
        mod = mm(tag + "mod", scc, big["w_mod"][l], slots["w_mod"][l])
        bm = p["b_mod"][l][None]
        h, = blocked_op(tag + "normmod", _f_normmod, [p["norm_w"][l][None], mod, bm], [xa], [(d_model, bf16)], rb=wide)
        gnames = [g[0] for g in IN_GROUPS]
        wvs = [_pad_cols(big["w_in"][l][:, IN_OFFS[s0]:IN_OFFS[s0 + ns]], width) for _, s0, ns, width in IN_GROUPS]
        u = dict(zip(gnames, multi_mm(tag + "in_", gnames, h, wvs, [slots["w_in"][l][g] for g in gnames])))

        o_dirs = blocked_op(tag + "hg", _both(_f_hg(False), _f_hg(True), 1, 2), [lb_all[l, 0][None], lb_all[l, 1][None]],
                            [u["hg_qi"], u["hg_ff"], u["hg_qi"], u["hg_fb"]], [(BR_W, f32)] * 2,
                            order=_BOTH_ORDERS, carry_sds=[(BR_W, HG_DK)] * 2)
        y_hg, = blocked_op(tag + "hg_fin", _f_hg_final, [p["hg_norm"][l][None]], list(o_dirs) + [u["hg_z"]], [(BR_W, bf16)],
                           rb=wide)

        a_scan, bd, cds = _s5_params(l, p)
        n_state = S5_GROUPS * S5_STATE
        planes = [s.reshape(t, n_state) for s in s5_states(tag, u["s5_u"], bd, a_scan[0], a_scan[1])]
        ysum, = bd_mm(tag + "s5_c", planes, [[cds[d][part]] for d in range(2) for part in range(2)])
        g5, = blocked_op(tag + "s5_c1", _f_s5_c1, [p["s5_d"][l][None]], [ysum, u["s5_u"]], [(BR_W, f32)], rb=wide)
        gl = mm(tag + "s5_glu", g5, big["s5_w_glu"][l], slots["s5_w_glu"][l])
        y_s5, = blocked_op(tag + "s5_c2", _f_s5_c2, [p["s5_b_glu"][l][None]], [g5, gl, u["s5_z"]], [(BR_W, bf16)],
                           rb=wide)

        wg, gb = _lru_gate(l, p)
        ab = blocked_op(tag + "lru_a", _f_lru_a,
                        [p["lru_conv_w"][l], p["lru_conv_b"][l][None], wg, gb, p["lru_lam"][l]],
                        [u["lru_x"]], [(BR_W, f32)] * 4)
        hs = []
        for d in range(2):
            hs.append(rscan(tag + "lru_scan%d" % d, d, ab[2 * d], ab[2 * d + 1]))
        y_lru, = blocked_op(tag + "lru_c", _f_lru_c, [], hs + [u["lru_z"]], [(BR_W, bf16)], rb=wide)

        dtb = _pad_cols(p["m2_dt_bias"][l].reshape(1, -1), _LANE)
        xbc, dtp = blocked_op(tag + "m2_a", _f_m2_a, [p["m2_conv_w"][l], p["m2_conv_b"][l][None], dtb],
                              [u["m2_xbc"], u["m2_dt"]], [(M2_XBC, f32), (_LANE, f32)])
        alog = _pad_cols(p["m2_a_log"][l].reshape(1, -1), _LANE)
        y_dirs = blocked_op(tag + "ssd", _both(_f_ssd(0), _f_ssd(1), 1, 2), [alog, alog], [xbc, dtp, xbc, dtp],
                            [(BR_W, f32)] * 2, order=_BOTH_ORDERS,
                            carry_sds=[(M2_GROUPS * M2_STATE, BR_W // M2_GROUPS)] * 2)
        dsk = jnp.repeat(p["m2_d"][l], M2_HEADDIM)[None]
        y_m2, = blocked_op(tag + "m2_c", _f_m2_c, [dsk, p["m2_norm"][l][None]], list(y_dirs) + [xbc, u["m2_z"]], [(BR_W, bf16)],
                           rb=wide)

        wg_all = jnp.transpose(big["w_gate"][l], (1, 0, 2)).reshape(d_model, 4 * d_model)
        gp = mm(tag + "gate", h, wg_all, slots["w_gate"][l], out_dtype=bf16)
        bs = [mm(tag + "br%d" % k, yk, big["w_branch"][l][k], slots["w_branch"][l][k], out_dtype=bf16)
              for k, yk in enumerate((y_hg, y_s5, y_lru, y_m2))]
        mix, = blocked_op(tag + "mix", _f_mix, [p["b_gate"][l].reshape(1, -1)], [gp] + bs, [(d_model, bf16)])
        o = mm(tag + "out", mix, big["w_out"][l], slots["w_out"][l])
        xa, = blocked_op(tag + "resid", _f_resid, [mod, bm], [xa, o], [(d_model, f32)], rb=wide)

    rl, = blocked_op("loss", _f_loss, [p["final_norm"][None]], [xa[ctx.shape[0]:], target], [(1, f32)],
                     rb=_wide_rows(target.shape[0]))
    return jnp.sum(rl)


_MESH = pl.DeviceIdType.MESH
_ANY = pl.BlockSpec(memory_space=pl.ANY)
W_PACK = 1024


def _place():
    x, y, c = lax.axis_index("x"), lax.axis_index("y"), lax.axis_index("c")
    chips = [(x, 1 - y), (1 - x, y), (1 - x, 1 - y)]
    return x, y, c, chips


def _rcopy(src, dst, ssem, rsem, k, to):
    return pltpu.make_async_remote_copy(src_ref=src, dst_ref=dst, send_sem=ssem.at[k], recv_sem=rsem.at[k],
                                        device_id=to, device_id_type=_MESH)


def gather_shards(xs):
    n = len(xs)

    def body(*refs):
        x_refs, o_refs = refs[:n], refs[n:2 * n]
        ssem, rsem, lsem = refs[2 * n:]
        x, y, c, chips = _place()
        j = 2 * x + y
        sib = (x, y, 1 - c)
        mine = [pltpu.make_async_copy(x_refs[a], o_refs[a].at[j], lsem.at[a]) for a in range(n)]
        for cp in mine:
            cp.start()
        first = [_rcopy(x_refs[a].at[c], o_refs[a].at[j, c], ssem, rsem, 6 * a + r, (*chips[r], c))
                 for r in range(3) for a in range(n)]
        for cp in first:
            cp.start()
        passed = []
        for r in range(3):
            jr = j ^ (r + 1)
            for a in range(n):
                _rcopy(x_refs[a].at[c], o_refs[a].at[jr, c], ssem, rsem, 6 * a + r, sib).wait_recv()
                cp = _rcopy(o_refs[a].at[jr, c], o_refs[a].at[jr, c], ssem, rsem, 6 * a + 3 + r, sib)
                cp.start()
                passed.append(cp)
        for r in range(3):
            jr = j ^ (r + 1)
            for a in range(n):
                _rcopy(x_refs[a].at[c], o_refs[a].at[jr, 1 - c], ssem, rsem, 6 * a + 3 + r, sib).wait_recv()
        for cp in first + passed:
            cp.wait_send()
        for cp in mine:
            cp.wait()

    return pl.pallas_call(
        body, name="gather_shards", out_shape=[jax.ShapeDtypeStruct((4,) + x.shape, x.dtype) for x in xs],
        in_specs=[_ANY] * n, out_specs=[_ANY] * n,
        scratch_shapes=[pltpu.SemaphoreType.DMA((6 * n,)), pltpu.SemaphoreType.DMA((6 * n,)), pltpu.SemaphoreType.DMA((n,))],
    )(*xs)


def sibling_halves(gs):
    n = len(gs)

    def body(*refs):
        g_refs, o_refs = refs[:n], refs[n:2 * n]
        ssem, rsem = refs[2 * n:]
        x, y, c, _ = _place()
        sib = (x, y, 1 - c)
        cps = [_rcopy(g_refs[a].at[k, 1 - c], o_refs[a].at[k], ssem, rsem, 4 * a + k, sib)
               for k in range(4) for a in range(n)]
        for cp in cps:
            cp.start()
        for cp in cps:
            cp.wait()

    return pl.pallas_call(
        body, name="sibling_halves", out_shape=[jax.ShapeDtypeStruct((4,) + g.shape[2:], g.dtype) for g in gs],
        in_specs=[_ANY] * n, out_specs=[_ANY] * n,
        scratch_shapes=[pltpu.SemaphoreType.DMA((4 * n,)), pltpu.SemaphoreType.DMA((4 * n,))],
    )(*gs)


def scatter_chips(ps):
    n = len(ps)

    def body(*refs):
        p_refs, o_refs = refs[:n], refs[n:2 * n]
        ssem, rsem = refs[2 * n:]
        x, y, c, chips = _place()
        j = 2 * x + y
        cps = [_rcopy(p_refs[a].at[j ^ (r + 1)], o_refs[a].at[r], ssem, rsem, 3 * a + r, (*chips[r], c))
               for r in range(3) for a in range(n)]
        for cp in cps:
            cp.start()
        for cp in cps:
            cp.wait()

    return pl.pallas_call(
        body, name="scatter_chips", out_shape=[jax.ShapeDtypeStruct((3,) + p.shape[1:], p.dtype) for p in ps],
        in_specs=[_ANY] * n, out_specs=[_ANY] * n,
        scratch_shapes=[pltpu.SemaphoreType.DMA((3 * n,)), pltpu.SemaphoreType.DMA((3 * n,))],
    )(*ps)


def join_halves(qs):
    n = len(qs)

    def body(*refs):
        o_refs = refs[n:2 * n]
        ssem, rsem = refs[2 * n:]
        x, y, c, _ = _place()
        sib = (x, y, 1 - c)
        cps = [_rcopy(o_refs[a].at[c], o_refs[a].at[c], ssem, rsem, a, sib) for a in range(n)]
        for cp in cps:
            cp.start()
        for a in range(n):
            _rcopy(o_refs[a].at[c], o_refs[a].at[1 - c], ssem, rsem, a, sib).wait_recv()
        for cp in cps:
            cp.wait_send()

    return pl.pallas_call(
        body, name="join_halves", out_shape=[jax.ShapeDtypeStruct(q.shape, q.dtype) for q in qs],
        in_specs=[_ANY] * n, out_specs=[_ANY] * n, input_output_aliases={a: a for a in range(n)},
        scratch_shapes=[pltpu.SemaphoreType.DMA((n,)), pltpu.SemaphoreType.DMA((n,))],
    )(*qs)


def _rows_block(r):
    return _pick(r, 256, _SUB)


def add_sibling(tag, g, r1, place, out_dtype):
    _, _, rows, w = g.shape
    rb = _rows_block(rows)

    def body(pl_ref, g_ref, r_ref, o_ref):
        o_ref[...] = (g_ref[0] + r_ref[...]).astype(out_dtype)

    return pl.pallas_call(
        body, name="add_sibling_" + tag, out_shape=jax.ShapeDtypeStruct((4, rows, w), out_dtype),
        grid_spec=pltpu.PrefetchScalarGridSpec(
            num_scalar_prefetch=1, grid=(4, rows // rb),
            in_specs=[pl.BlockSpec((1, 1, rb, w), lambda k, i, s: (k, s[1], i, 0)),
                      pl.BlockSpec((1, rb, w), lambda k, i, s: (k, i, 0))],
            out_specs=pl.BlockSpec((1, rb, w), lambda k, i, s: (k, i, 0))),
        compiler_params=_cparams(2),
    )(place, g, r1)


def add_chips(tag, p, r2, place):
    _, rows, w = p.shape
    rb = _rows_block(rows)

    def body(pl_ref, p_ref, r_ref, o_ref):
        j = pl_ref[0]
        own = p_ref[0].astype(f32)
        others = [r_ref[0].astype(f32), r_ref[1].astype(f32), r_ref[2].astype(f32)]
        acc = None
        for k in range(4):
            rel = k ^ j
            t = jnp.where(rel == 0, own, jnp.where(rel == 1, others[0], jnp.where(rel == 2, others[1], others[2])))
            acc = t if acc is None else acc + t
        o_ref[0] = acc

    return pl.pallas_call(
        body, name="add_chips_" + tag, out_shape=jax.ShapeDtypeStruct((2, rows, w), f32),
        grid_spec=pltpu.PrefetchScalarGridSpec(
            num_scalar_prefetch=1, grid=(rows // rb,),
            in_specs=[pl.BlockSpec((1, rb, w), lambda i, s: (s[0], i, 0)),
                      pl.BlockSpec((3, rb, w), lambda i, s: (0, i, 0))],
            out_specs=pl.BlockSpec((1, rb, w), lambda i, s: (s[1], i, 0))),
        compiler_params=_cparams(1),
    )(place, p, r2)


def adamw(tag, g, w, m, v):
    rows, wd = g.shape
    rb = _rows_block(rows)

    def body(g_ref, w_ref, m_ref, v_ref, d_ref, nm_ref, nv_ref):
        gv = g_ref[...]
        nm = ADAM_B1 * m_ref[...] + (1.0 - ADAM_B1) * gv
        nv = ADAM_B2 * v_ref[...] + (1.0 - ADAM_B2) * (gv * gv)
        m_hat = nm / (1.0 - ADAM_B1 ** ADAM_STEP)
        v_hat = nv / (1.0 - ADAM_B2 ** ADAM_STEP)
        d_ref[...] = -ADAM_LR * (m_hat / (jnp.sqrt(v_hat) + ADAM_EPS) + ADAM_WD * w_ref[...])
        nm_ref[...] = nm
        nv_ref[...] = nv

    spec = pl.BlockSpec((rb, wd), lambda i: (i, 0))
    return pl.pallas_call(
        body, name="adamw_" + tag, grid=(rows // rb,), in_specs=[spec] * 4, out_specs=[spec] * 3,
        out_shape=[jax.ShapeDtypeStruct(g.shape, f32)] * 3, compiler_params=_cparams(1),
    )(g, w, m, v)


WEIGHTS = ("c_ctx", "norm_w", "w_mod", "b_mod", "w_in", "hg_lb_logits", "hg_norm", "s5_a_re", "s5_a_im", "s5_log_step",
           "s5_b_re", "s5_b_im", "s5_c_re", "s5_c_im", "s5_d", "s5_w_glu", "s5_b_glu", "lru_conv_w", "lru_conv_b",
           "lru_gate_w", "lru_gate_b", "lru_lam", "m2_conv_w", "m2_conv_b", "m2_dt_bias", "m2_a_log", "m2_d", "m2_norm",
           "w_branch", "w_gate", "b_gate", "w_out", "final_norm")
SHARD_AXIS = {"w_mod": 2, "w_in": 2, "hg_lb_logits": 2, "s5_w_glu": 1, "lru_conv_w": 2, "lru_lam": 2, "m2_conv_w": 2,
              "w_branch": 3, "w_gate": 2, "b_gate": 2, "w_out": 1}
BIG = ("w_mod", "w_in", "s5_w_glu", "w_branch", "w_gate", "w_out")
N_CHIPS = 4


def _to_rows(flat, row_unit):
    n = flat.shape[-1]
    per = 2 * row_unit * W_PACK
    total = -(-n // per) * per
    flat = jnp.pad(flat, [(0, 0)] * (flat.ndim - 1) + [(0, total - n)])
    return flat.reshape(flat.shape[:-1] + (2, total // (2 * W_PACK), W_PACK))


SMALL_SHARDED = tuple(n for n in WEIGHTS if n in SHARD_AXIS and n not in BIG)
SMALL_REPLICATED = tuple(n for n in WEIGHTS if n not in SHARD_AXIS)


def _chip_slices(a, axis):
    width = a.shape[axis] // N_CHIPS
    return jnp.stack([lax.slice_in_dim(a, k * width, (k + 1) * width, axis=axis) for k in range(N_CHIPS)])


def _gather_weights(local):
    small = jnp.concatenate([lax.bitcast_convert_type(local[n], bf16).reshape(-1) for n in SMALL_SHARDED])
    got = gather_shards([local[n].astype(bf16) for n in BIG] + [_to_rows(small, 16)])
    full = {}
    for n, g in zip(BIG, got):
        full[n] = [jnp.concatenate([g[j, l] for j in range(N_CHIPS)], axis=SHARD_AXIS[n] - 1) for l in range(g.shape[1])]
    flat, off = got[-1].reshape(N_CHIPS, -1), 0
    for n in SMALL_SHARDED:
        shp = local[n].shape
        size = 2 * math.prod(shp)
        part = lax.bitcast_convert_type(flat[:, off:off + size].reshape((N_CHIPS,) + shp + (2,)), f32)
        off += size
        full[n] = jnp.concatenate([part[j] for j in range(N_CHIPS)], axis=SHARD_AXIS[n])
    return full


def _whole_rows(v):
    n = v.shape[-1]
    return jnp.pad(v, [(0, 0)] * (v.ndim - 1) + [(0, -n % W_PACK)])


def _pack_small(vals, extra):
    return jnp.concatenate([_whole_rows(vals[n].reshape(-1)) for n in SMALL_SHARDED + SMALL_REPLICATED]
                           + [_whole_rows(extra.reshape(1))])


def _pack_small_grads(grads, loss):
    rep = [grads[n].reshape(-1) for n in SMALL_REPLICATED] + [loss.reshape(1)]
    sh = [_chip_slices(grads[n], SHARD_AXIS[n]).reshape(N_CHIPS, -1) for n in SMALL_SHARDED]
    return jnp.concatenate([_whole_rows(a) for a in sh]
                           + [_whole_rows(jnp.broadcast_to(r, (N_CHIPS,) + r.shape)) for r in rep], axis=1)


def _unpack_small(rows, like):
    out, r0 = {}, 0
    for n in SMALL_SHARDED + SMALL_REPLICATED:
        size = math.prod(like[n].shape)
        nr = -(-size // W_PACK)
        piece = lax.optimization_barrier(rows[r0:r0 + nr])
        out[n] = piece.reshape(-1)[:size].reshape(like[n].shape)
        r0 += nr
    return out, lax.optimization_barrier(rows[r0:r0 + 1])[0, 0]


def _reduce_grads(tags, gs):
    place = jnp.stack([2 * lax.axis_index("x") + lax.axis_index("y"), lax.axis_index("c")]).astype(jnp.int32)
    pairs = [add_sibling(t, g, r, place, bf16 if t in BIG else f32) for t, g, r in zip(tags, gs, sibling_halves(gs))]
    quads = [add_chips(t, p, r, place) for t, p, r in zip(tags, pairs, scatter_chips(pairs))]
    return join_halves(quads)


def kernel(x, c, ctx, c_ctx, norm_w, w_mod, b_mod, w_in, hg_lb_logits, hg_norm, s5_a_re, s5_a_im, s5_log_step, s5_b_re, s5_b_im, s5_c_re, s5_c_im, s5_d, s5_w_glu, s5_b_glu, lru_conv_w, lru_conv_b, lru_gate_w, lru_gate_b, lru_lam, m2_conv_w, m2_conv_b, m2_dt_bias, m2_a_log, m2_d, m2_norm, w_branch, w_gate, b_gate, w_out, final_norm, loss_target, m_c_ctx, m_norm_w, m_w_mod, m_b_mod, m_w_in, m_hg_lb_logits, m_hg_norm, m_s5_a_re, m_s5_a_im, m_s5_log_step, m_s5_b_re, m_s5_b_im, m_s5_c_re, m_s5_c_im, m_s5_d, m_s5_w_glu, m_s5_b_glu, m_lru_conv_w, m_lru_conv_b, m_lru_gate_w, m_lru_gate_b, m_lru_lam, m_m2_conv_w, m_m2_conv_b, m_m2_dt_bias, m_m2_a_log, m_m2_d, m_m2_norm, m_w_branch, m_w_gate, m_b_gate, m_w_out, m_final_norm, v_c_ctx, v_norm_w, v_w_mod, v_b_mod, v_w_in, v_hg_lb_logits, v_hg_norm, v_s5_a_re, v_s5_a_im, v_s5_log_step, v_s5_b_re, v_s5_b_im, v_s5_c_re, v_s5_c_im, v_s5_d, v_s5_w_glu, v_s5_b_glu, v_lru_conv_w, v_lru_conv_b, v_lru_gate_w, v_lru_gate_b, v_lru_lam, v_m2_conv_w, v_m2_conv_b, v_m2_dt_bias, v_m2_a_log, v_m2_d, v_m2_norm, v_w_branch, v_w_gate, v_b_gate, v_w_out, v_final_norm):
    given = dict(locals())
    w_loc = {n: given[n] for n in WEIGHTS}
    m_loc = {n: given["m_" + n] for n in WEIGHTS}
    v_loc = {n: given["v_" + n] for n in WEIGHTS}

    full = _gather_weights(w_loc)
    params = {n: (full[n] if n in SHARD_AXIS else w_loc[n]) for n in WEIGHTS if n not in BIG}
    big = {n: full[n] for n in BIG}
    def loss_fn(p, s, xx):
        return _forward(p, big, s, xx, ctx[0], c, loss_target[0])

    loss, (g_p, g_s, g_x) = jax.value_and_grad(loss_fn, argnums=(0, 1, 2))(params, _new_slots(big), x[0])
    grads = {**g_p, **_slot_grads(g_s)}

    def rows4(a):
        return a.reshape(a.shape[:2] + (-1, a.shape[-1]))

    g_big = [rows4(_chip_slices(grads[n], SHARD_AXIS[n])) for n in BIG]
    g_small = _to_rows(_pack_small_grads(grads, loss), 64)
    summed = _reduce_grads(list(BIG) + ["small"], g_big + [g_small])

    g_out, d_out, m_out, v_out = {}, {}, {}, {}
    for n, g in zip(BIG, summed):
        shp = w_loc[n].shape
        flat2 = lambda a: a.reshape(-1, shp[-1])
        g_out[n] = g.reshape(shp)
        d, nm, nv = adamw(n, flat2(g), flat2(w_loc[n]), flat2(m_loc[n]), flat2(v_loc[n]))
        d_out[n], m_out[n], v_out[n] = d.reshape(shp), nm.reshape(shp), nv.reshape(shp)
    zero = jnp.zeros((), f32)
    flat = lambda vals: _to_rows(_pack_small(vals, zero), 64).reshape(-1, W_PACK)
    gs = summed[-1].reshape(-1, W_PACK)
    d, nm, nv = adamw("small", gs, flat(w_loc), flat(m_loc), flat(v_loc))
    gsm, loss_out = _unpack_small(gs, w_loc)
    g_out.update(gsm)
    d_out.update(_unpack_small(d, w_loc)[0])
    m_out.update(_unpack_small(nm, w_loc)[0])
    v_out.update(_unpack_small(nv, w_loc)[0])
    outs = [loss_out, g_x[None]]
    for group in (g_out, d_out, m_out, v_out):
        outs += [group[n] for n in WEIGHTS]
    return tuple(outs)
```
